```python
import jax, jax.numpy as jnp
from jax import lax
import numpy as np

D_MODEL = 1024
BATCH = 2
SEQ = 8192
DEPTH = 1
DEC_BATCH = 8
DEC_SEQ = 32
PAST_LEN = 4096

CHUNK = 64
Q_BLOCK = 128
H_A = 8
NOPE = 64
ROPE = 32
DV = 64
Q_RANK = 384
KV_RANK = 256
ROPE_BASE = 10000.0
H_B = 8
D_HB = 64
D_FF = 2816
CONV_W = 3
EPS = 1e-6
NEG = -1e30
MLA_SCALE = (NOPE + ROPE) ** -0.5
SB_SCALE = D_HB ** -0.5
IN_SPLITS = [int(v) for v in np.cumsum([Q_RANK, KV_RANK, ROPE, H_B * D_HB, H_B * D_HB, H_B * D_HB, D_MODEL])]
N_IN = Q_RANK + KV_RANK + ROPE + 3 * H_B * D_HB + 2 * D_MODEL

kernel_name = "mla_stickbreaking_gated_hybrid_streaming_step"


def rms_norm(x, g):
    xf = x.astype(jnp.float32)
    y = xf * lax.rsqrt(jnp.mean(xf * xf, axis=-1, keepdims=True) + EPS)
    return (y * g.astype(jnp.float32)).astype(x.dtype)


def rope_tables(pos, dtype):
    inv = ROPE_BASE ** (-jnp.arange(0, ROPE, 2, dtype=jnp.float32) / ROPE)
    ang = pos.astype(jnp.float32)[:, None] * inv[None, :]
    return jnp.cos(ang).astype(dtype), jnp.sin(ang).astype(dtype)


def apply_rope(x, cos, sin):
    x1, x2 = x[..., :ROPE // 2], x[..., ROPE // 2:]
    return jnp.concatenate([x1 * cos - x2 * sin, x2 * cos + x1 * sin], axis=-1)


def mixer_inputs(h, pos, w_in, g_q_lat, w_uq, g_kv_lat):
    B, T, _ = h.shape
    p = h @ w_in
    q_lat, c_kv, k_r, q_b, k_b, v_b, g_a, g_b = jnp.split(p, IN_SPLITS, axis=-1)
    q = (rms_norm(q_lat, g_q_lat) @ w_uq).reshape(B, T, H_A, NOPE + ROPE)
    cos, sin = rope_tables(pos, h.dtype)
    q_nope = q[..., :NOPE]
    q_rope = apply_rope(q[..., NOPE:], cos[:, None, :], sin[:, None, :])
    k_rope = apply_rope(k_r, cos, sin)
    c_kv = rms_norm(c_kv, g_kv_lat)
    shp = (B, T, H_B, D_HB)
    return (q_nope, q_rope, c_kv, k_rope, q_b.reshape(shp), k_b.reshape(shp), v_b.reshape(shp),
            jax.nn.sigmoid(g_a), jax.nn.sigmoid(g_b))


def expand_latent(c_kv, w_uk, w_uv):
    B, S, _ = c_kv.shape
    return (c_kv @ w_uk).reshape(B, S, H_A, NOPE), (c_kv @ w_uv).reshape(B, S, H_A, DV)


def mla_attend(q_nope, q_rope, qpos, k_nope, k_rope, v, kpos):
    s = (jnp.einsum('bqhd,bshd->bhqs', q_nope, k_nope)
         + jnp.einsum('bqhr,bsr->bhqs', q_rope, k_rope)).astype(jnp.float32) * MLA_SCALE
    mask = (kpos[None, :] // CHUNK) <= (qpos[:, None] // CHUNK)
    p = jax.nn.softmax(jnp.where(mask, s, NEG), axis=-1)
    return jnp.einsum('bhqs,bshd->bqhd', p.astype(v.dtype), v)


def sb_attend(q, qpos, k, v, kpos):
    z = jnp.einsum('bqhd,bshd->bhqs', q, k).astype(jnp.float32) * SB_SCALE
    mask = kpos[None, :] < qpos[:, None]
    sp = jnp.where(mask, jax.nn.softplus(z), 0.0)
    later = lax.cumsum(sp, axis=3, reverse=True) - sp
    a = jnp.where(mask, jnp.exp(jax.nn.log_sigmoid(z) - later), 0.0)
    return jnp.einsum('bhqs,bshd->bqhd', a.astype(v.dtype), v)


def sweep_query_blocks(attend, qs, qpos, kvs, kpos):
    B, T = qs[0].shape[:2]
    nb = T // Q_BLOCK
    qs_b = tuple(jnp.moveaxis(q.reshape((B, nb, Q_BLOCK) + q.shape[2:]), 1, 0) for q in qs)
    qpos_b = qpos.reshape(nb, Q_BLOCK)

    def body(args):
        return attend(*args[:-1], args[-1], *kvs, kpos)

    out = lax.map(body, qs_b + (qpos_b,))
    return jnp.moveaxis(out, 0, 1).reshape((B, T) + out.shape[3:])


def conv_ffn(h, conv_state, w_up, conv_w, conv_b, w_down):
    T = h.shape[1]
    u = h @ w_up
    u_ext = jnp.concatenate([conv_state, u], axis=1)
    y = conv_b
    for i in range(CONV_W):
        y = y + conv_w[i] * u_ext[:, i:i + T]
    a, b = jnp.split(y, 2, axis=-1)
    return (jax.nn.gelu(a, approximate=True) * b) @ w_down, u_ext[:, T:]


def encoder_layer(x, c, pos, past, w_ada, b_ada, g_pre_mix, g_post_mix, g_pre_ffn, g_post_ffn,
                  w_in, g_q_lat, w_uq, g_kv_lat, w_uk, w_uv, w_proj_a, w_proj_b, w_out,
                  w_up, conv_w, conv_b, w_down):
    B, T, _ = x.shape
    ada = jax.nn.silu(c) @ w_ada + b_ada
    sh1, sc1, gt1, sh2, sc2, gt2 = [a[:, None, :] for a in jnp.split(ada, 6, axis=-1)]
    h = rms_norm(x, g_pre_mix) * (1 + sc1) + sh1
    q_nope, q_rope, c_kv, k_rope, q_b, k_b, v_b, g_a, g_b = mixer_inputs(h, pos, w_in, g_q_lat, w_uq, g_kv_lat)
    if past is None:
        k_nope, v_a = expand_latent(c_kv, w_uk, w_uv)
        o_a = sweep_query_blocks(mla_attend, (q_nope, q_rope), pos, (k_nope, k_rope, v_a), pos)
        o_b = sweep_query_blocks(sb_attend, (q_b,), pos, (k_b, v_b), pos)
        conv_state = jnp.zeros((B, CONV_W - 1, 2 * D_FF), x.dtype)
    else:
        past_ckv, past_krope, past_k, past_v, conv_state = past
        kpos = jnp.arange(past_ckv.shape[1] + T, dtype=jnp.int32)
        k_nope, v_a = expand_latent(jnp.concatenate([past_ckv, c_kv], axis=1), w_uk, w_uv)
        o_a = mla_attend(q_nope, q_rope, pos, k_nope, jnp.concatenate([past_krope, k_rope], axis=1), v_a, kpos)
        o_b = sb_attend(q_b, pos, jnp.concatenate([past_k, k_b], axis=1),
                        jnp.concatenate([past_v, v_b], axis=1), kpos)
    merged = (g_a * (o_a.reshape(B, T, H_A * DV) @ w_proj_a)
              + g_b * (o_b.reshape(B, T, H_B * D_HB) @ w_proj_b))
    x = x + gt1 * rms_norm(merged @ w_out, g_post_mix)
    h2 = rms_norm(x, g_pre_ffn) * (1 + sc2) + sh2
    f, new_conv = conv_ffn(h2, conv_state, w_up, conv_w, conv_b, w_down)
    x = x + gt2 * rms_norm(f, g_post_ffn)
    return x, (c_kv, k_rope, k_b, v_b, new_conv)


def setup_inputs(seed: int = 0) -> dict:
    key = jax.random.key(seed)
    ks = iter(jax.random.split(key, 40))
    nrm = lambda shape, s=1.0: jax.random.normal(next(ks), shape, jnp.float32) * s
    gain = lambda n: 1.0 + nrm((DEPTH, n), 0.05)
    L = DEPTH
    return {
        "x_prompt": nrm((BATCH, SEQ, D_MODEL)),
        "x_sample": nrm((DEC_BATCH, DEC_SEQ, D_MODEL)),
        "cache_mla_ckv": nrm((L, DEC_BATCH, PAST_LEN, KV_RANK)),
        "cache_mla_krope": nrm((L, DEC_BATCH, PAST_LEN, ROPE)),
        "cache_sb_k": nrm((L, DEC_BATCH, PAST_LEN, H_B, D_HB)),
        "cache_sb_v": nrm((L, DEC_BATCH, PAST_LEN, H_B, D_HB)),
        "state_ffn_conv": nrm((L, DEC_BATCH, CONV_W - 1, 2 * D_FF)),
        "c_prompt": nrm((BATCH, D_MODEL)),
        "c_sample": nrm((DEC_BATCH, D_MODEL)),
        "w_ada": nrm((L, D_MODEL, 6 * D_MODEL), D_MODEL ** -0.5),
        "b_ada": nrm((L, 6 * D_MODEL), 0.01),
        "g_pre_mix": gain(D_MODEL),
        "g_post_mix": gain(D_MODEL),
        "g_pre_ffn": gain(D_MODEL),
        "g_post_ffn": gain(D_MODEL),
        "w_in": nrm((L, D_MODEL, N_IN), D_MODEL ** -0.5),
        "g_q_lat": gain(Q_RANK),
        "w_uq": nrm((L, Q_RANK, H_A * (NOPE + ROPE)), Q_RANK ** -0.5),
        "g_kv_lat": gain(KV_RANK),
        "w_uk": nrm((L, KV_RANK, H_A * NOPE), KV_RANK ** -0.5),
        "w_uv": nrm((L, KV_RANK, H_A * DV), KV_RANK ** -0.5),
        "w_proj_a": nrm((L, H_A * DV, D_MODEL), (H_A * DV) ** -0.5),
        "w_proj_b": nrm((L, H_B * D_HB, D_MODEL), (H_B * D_HB) ** -0.5),
        "w_out": nrm((L, D_MODEL, D_MODEL), D_MODEL ** -0.5),
        "w_up": nrm((L, D_MODEL, 2 * D_FF), D_MODEL ** -0.5),
        "conv_w": nrm((L, CONV_W, 2 * D_FF), CONV_W ** -0.5),
        "conv_b": nrm((L, 2 * D_FF), 0.01),
        "w_down": nrm((L, D_FF, D_MODEL), D_FF ** -0.5),
    }


def reference(x_prompt, x_sample, cache_mla_ckv, cache_mla_krope, cache_sb_k, cache_sb_v, state_ffn_conv,
              c_prompt, c_sample, w_ada, b_ada, g_pre_mix, g_post_mix, g_pre_ffn, g_post_ffn,
              w_in, g_q_lat, w_uq, g_kv_lat, w_uk, w_uv, w_proj_a, w_proj_b, w_out,
              w_up, conv_w, conv_b, w_down):
    past_len = cache_mla_ckv.shape[2]
    pos_p = jnp.arange(x_prompt.shape[1], dtype=jnp.int32)
    pos_s = past_len + jnp.arange(x_sample.shape[1], dtype=jnp.int32)
    xp, xs = x_prompt, x_sample
    st_p = [[] for _ in range(5)]
    st_s = [[] for _ in range(5)]
    for l in range(DEPTH):
        w = (w_ada[l], b_ada[l], g_pre_mix[l], g_post_mix[l], g_pre_ffn[l], g_post_ffn[l],
             w_in[l], g_q_lat[l], w_uq[l], g_kv_lat[l], w_uk[l], w_uv[l], w_proj_a[l], w_proj_b[l], w_out[l],
             w_up[l], conv_w[l], conv_b[l], w_down[l])
        xp, sp = encoder_layer(xp, c_prompt, pos_p, None, *w)
        past = (cache_mla_ckv[l], cache_mla_krope[l], cache_sb_k[l], cache_sb_v[l], state_ffn_conv[l])
        xs, ss = encoder_layer(xs, c_sample, pos_s, past, *w)
        for i in range(5):
            st_p[i].append(sp[i])
            st_s[i].append(ss[i])
    p_ckv, p_kr, p_k, p_v, p_conv = [jnp.stack(a, axis=0) for a in st_p]
    s_ckv, s_kr, s_k, s_v, s_conv = [jnp.stack(a, axis=0) for a in st_s]
    return (xp, xs, p_ckv, p_kr, p_k, p_v, p_conv, s_ckv, s_kr, s_k, s_v, s_conv)
```

```python
import functools
import math

import numpy as np
import jax
import jax.numpy as jnp
from jax import lax
from jax.experimental import pallas as pl
from jax.experimental.pallas import tpu as pltpu

D_MODEL = 1024
CHUNK = 64
CHUNK_SHIFT = 6
H_A = 8
NOPE = 64
ROPE = 32
DV = 64
Q_RANK = 384
KV_RANK = 256
ROPE_BASE = 10000.0
H_B = 8
D_HB = 64
D_FF = 2816
CONV_W = 3
EPS = 1e-6
NEG = -1e30
MLA_SCALE = (NOPE + ROPE) ** -0.5
SB_SCALE = D_HB ** -0.5
LOG2E = math.log2(math.e)

LANES = 128
HEAD_PAD = LANES
SB_DEAD = 110.0
VMEM_LIMIT = 56 * 1024 * 1024

F32 = jnp.float32
BF16 = jnp.bfloat16


def _dot(a, b):
    return jnp.dot(a, b, preferred_element_type=F32)


def _dot_t(a, b):
    return lax.dot_general(a, b, (((1,), (1,)), ((), ())), preferred_element_type=F32)


def _rms(x, g):
    return x * lax.rsqrt(jnp.mean(x * x, axis=-1, keepdims=True) + EPS) * g


def _sigmoid(x):
    return 1.0 / (1.0 + jnp.exp(-x))


def _params(n_parallel, n_arbitrary=0):
    return pltpu.CompilerParams(
        dimension_semantics=("parallel",) * n_parallel + ("arbitrary",) * n_arbitrary,
        vmem_limit_bytes=VMEM_LIMIT)


def _const_spec(shape):
    n = len(shape)
    return pl.BlockSpec(shape, lambda *_: (0,) * n)


def _ada_kernel(c_ref, w_ref, b_ref, o_ref):
    c = c_ref[...]
    s = (c * _sigmoid(c)).astype(BF16)
    o_ref[...] = _dot(s, w_ref[...].astype(BF16)) + b_ref[...]


def _ada(c_all, w_ada, b_ada):
    n = c_all.shape[0]
    nchunk = 6
    return pl.pallas_call(
        _ada_kernel,
        grid=(nchunk,),
        in_specs=[pl.BlockSpec((n, D_MODEL), lambda j: (0, 0)),
                  pl.BlockSpec((D_MODEL, D_MODEL), lambda j: (0, j)),
                  pl.BlockSpec((1, D_MODEL), lambda j: (0, j))],
        out_specs=pl.BlockSpec((n, D_MODEL), lambda j: (0, j)),
        out_shape=jax.ShapeDtypeStruct((n, 6 * D_MODEL), F32),
        compiler_params=_params(1),
        name="ada",
    )(c_all, w_ada, b_ada.reshape(1, -1))


_C_QLAT = 0
_C_CKV = _C_QLAT + Q_RANK
_C_KR = _C_CKV + KV_RANK
_C_KRR = _C_KR + LANES
_C_QB = _C_KRR + LANES
_C_KB = _C_QB + H_B * D_HB
_C_VB = _C_KB + H_B * D_HB
_N_K1 = _C_VB + H_B * D_HB


def _premix(x, ada_ref, g):
    sh = ada_ref[0, 0:1, :]
    sc = ada_ref[0, 1:2, :]
    return _rms(x, g) * (1.0 + sc) + sh


def _mixer_kernel(x_ref, ada_ref, g_ref, win_ref, gq_ref, wuq_ref, gkv_ref, wuk_ref, wuv_ref,
                  cos_ref, sin_ref, *out_refs, expand_kv):
    if expand_kv:
        q_ref, ckv_ref, kr_ref, qb_ref, kb_ref, vb_ref, k_ref, v_ref = out_refs
    else:
        q_ref, ckv_ref, kr_ref, qb_ref, kb_ref, vb_ref = out_refs
    h = _premix(x_ref[0], ada_ref, g_ref[...]).astype(BF16)
    p = _dot(h, win_ref[...])
    cos = cos_ref[...]
    sin = sin_ref[...]

    q_lat = _rms(p[:, _C_QLAT:_C_QLAT + Q_RANK], gq_ref[...]).astype(BF16)
    q2 = _dot(q_lat, wuq_ref[...])
    nq = H_A * HEAD_PAD
    for hd in range(H_A):
        lo = hd * HEAD_PAD
        qh = q2[:, lo:lo + HEAD_PAD] * cos + q2[:, nq + lo:nq + lo + HEAD_PAD] * sin
        q_ref[0, :, lo:lo + HEAD_PAD] = (qh * (MLA_SCALE * LOG2E)).astype(BF16)

    c_kv = _rms(p[:, _C_CKV:_C_CKV + KV_RANK], gkv_ref[...])
    ckv_ref[0] = c_kv
    kr = p[:, _C_KR:_C_KR + LANES] * cos + p[:, _C_KRR:_C_KRR + LANES] * sin
    kr_ref[0] = kr[:, :ROPE]

    qb_ref[0] = (p[:, _C_QB:_C_QB + H_B * D_HB] * SB_SCALE).astype(BF16)
    kb_ref[0] = p[:, _C_KB:_C_KB + H_B * D_HB]
    vb_ref[0] = p[:, _C_VB:_C_VB + H_B * D_HB]

    if expand_kv:
        c_bf = c_kv.astype(BF16)
        k2 = _dot(c_bf, wuk_ref[...])
        for hd in range(H_A):
            lo = hd * HEAD_PAD
            k_ref[0, :, lo:lo + HEAD_PAD] = (k2[:, lo:lo + HEAD_PAD] + kr).astype(BF16)
        v_ref[0] = _dot(c_bf, wuv_ref[...]).astype(BF16)


def _mixer_inputs(x, ada, cos_t, sin_t, wts, *, tm, expand_kv):
    b, t, _ = x.shape
    nt = t // tm
    tok = lambda n: pl.BlockSpec((1, tm, n), lambda i, j: (i, j, 0))
    in_specs = [tok(D_MODEL),
                pl.BlockSpec((1, 6, D_MODEL), lambda i, j: (i, 0, 0)),
                _const_spec((1, D_MODEL)),
                _const_spec((D_MODEL, _N_K1)),
                _const_spec((1, Q_RANK)),
                _const_spec((Q_RANK, 2 * H_A * HEAD_PAD)),
                _const_spec((1, KV_RANK)),
                _const_spec((KV_RANK, H_A * HEAD_PAD)),
                _const_spec((KV_RANK, H_A * DV)),
                pl.BlockSpec((tm, LANES), lambda i, j: (j, 0)),
                pl.BlockSpec((tm, LANES), lambda i, j: (j, 0))]
    shapes = [((b, t, H_A * HEAD_PAD), BF16), ((b, t, KV_RANK), F32), ((b, t, ROPE), F32),
              ((b, t, H_B * D_HB), BF16), ((b, t, H_B * D_HB), F32), ((b, t, H_B * D_HB), F32)]
    if expand_kv:
        shapes += [((b, t, H_A * HEAD_PAD), BF16), ((b, t, H_A * DV), BF16)]
    return pl.pallas_call(
        functools.partial(_mixer_kernel, expand_kv=expand_kv),
        grid=(b, nt),
        in_specs=in_specs,
        out_specs=[tok(s[-1]) for s, _ in shapes],
        out_shape=[jax.ShapeDtypeStruct(s, d) for s, d in shapes],
        compiler_params=_params(2),
        name="mixer_in_kv" if expand_kv else "mixer_in",
    )(x, ada, wts["g_pre_mix"], wts["w_in_k1"], wts["g_q_lat"], wts["w_uq_ext"], wts["g_kv_lat"],
      wts["w_uk_pad"], wts["w_uv"], cos_t, sin_t)


def _mla_prompt_kernel(q_ref, k_ref, v_ref, o_ref, *, tq, tk):
    qi = pl.program_id(2)
    n_diag = tq // tk
    n_full = qi * n_diag
    outs = []
    for hh in range(2):
        q = q_ref[0, :, hh * HEAD_PAD:(hh + 1) * HEAD_PAD]

        def step(j, carry, masked):
            m, l, acc = carry
            start = pl.multiple_of(j * tk, tk)
            k = k_ref[0, pl.ds(start, tk), hh * HEAD_PAD:(hh + 1) * HEAD_PAD]
            v = v_ref[0, pl.ds(start, tk), :]
            s = _dot_t(q, k)
            if masked:
                qpos = qi * tq + lax.broadcasted_iota(jnp.int32, (tq, tk), 0)
                kpos = j * tk + lax.broadcasted_iota(jnp.int32, (tq, tk), 1)
                s = jnp.where((kpos >> CHUNK_SHIFT) <= (qpos >> CHUNK_SHIFT), s, NEG)
            m_new = jnp.maximum(m, jnp.max(s, axis=-1, keepdims=True))
            alpha = jnp.exp2(m - m_new)
            p = jnp.exp2(s - m_new)
            l = alpha * l + jnp.sum(p, axis=-1, keepdims=True)
            acc = alpha * acc + _dot(p.astype(BF16), v)
            return m_new, l, acc

        carry = (jnp.full((tq, 1), NEG, F32), jnp.zeros((tq, 1), F32), jnp.zeros((tq, LANES), F32))
        carry = lax.fori_loop(0, n_full, functools.partial(step, masked=False), carry)
        for d in range(n_diag):
            carry = step(n_full + d, carry, True)
        _, l, acc = carry
        outs.append(acc / l)
    lane = lax.broadcasted_iota(jnp.int32, (tq, LANES), 1)
    o_ref[0] = jnp.where(lane < DV, outs[0], outs[1]).astype(BF16)


def _mla_prompt(q, k, v, *, tq, tk):
    b, t, _ = q.shape
    return pl.pallas_call(
        functools.partial(_mla_prompt_kernel, tq=tq, tk=tk),
        grid=(b, H_A // 2, t // tq),
        in_specs=[pl.BlockSpec((1, tq, 2 * HEAD_PAD), lambda i, j, n: (i, n, j)),
                  pl.BlockSpec((1, t, 2 * HEAD_PAD), lambda i, j, n: (i, 0, j)),
                  pl.BlockSpec((1, t, 2 * DV), lambda i, j, n: (i, 0, j))],
        out_specs=pl.BlockSpec((1, tq, 2 * DV), lambda i, j, n: (i, n, j)),
        out_shape=jax.ShapeDtypeStruct((b, t, H_A * DV), BF16),
        compiler_params=_params(3),
        name="mla_prompt",
    )(q, k, v)


def _tri(n):
    r = lax.broadcasted_iota(jnp.int32, (n, n), 0)
    c = lax.broadcasted_iota(jnp.int32, (n, n), 1)
    return jnp.where(r > c, 1.0, 0.0).astype(BF16)


def _sb_block(qh, k, v, r_prev, tri, causal):
    tq, tk = qh.shape[0], k.shape[0]
    z = _dot_t(qh, k)
    sp = jnp.maximum(z, 0.0) + jnp.log(1.0 + jnp.exp(-jnp.abs(z)))
    if causal:
        row = lax.broadcasted_iota(jnp.int32, (tq, tk), 0)
        col = lax.broadcasted_iota(jnp.int32, (tq, tk), 1)
        keep = col < row
        sp = jnp.where(keep, sp, 0.0)
    sp_hi = sp.astype(BF16)
    sp_lo = (sp - sp_hi.astype(F32)).astype(BF16)
    later = _dot(sp_hi, tri) + _dot(sp_lo, tri)
    if tk >= LANES:
        r_b = jnp.concatenate([r_prev] * (tk // LANES), axis=1)
    else:
        r_b = r_prev[:, :tk]
    a = jnp.exp(z - sp - later - r_b)
    if causal:
        a = jnp.where(keep, a, 0.0)
    return _dot(a.astype(BF16), v), jnp.sum(sp, axis=-1, keepdims=True)


def _head_mask(x, hh):
    lane = lax.broadcasted_iota(jnp.int32, x.shape, 1)
    mine = jnp.logical_and(lane >= hh * D_HB, lane < (hh + 1) * D_HB)
    return jnp.where(mine, x, jnp.zeros_like(x))


def _sb_prompt_kernel(q_ref, k_ref, v_ref, o_ref, acc_ref, r_ref, *, tq):
    qi = pl.program_id(2)
    tri = _tri(tq)
    outs = []
    for hh in range(2):
        qh = _head_mask(q_ref[0], hh)

        def load(j):
            start = pl.multiple_of(j * tq, tq)
            return (k_ref[0, pl.ds(start, tq), :].astype(BF16),
                    v_ref[0, pl.ds(start, tq), :].astype(BF16))

        k, v = load(qi)
        o, rs = _sb_block(qh, k, v, jnp.zeros((tq, LANES), F32), tri, True)
        acc_ref[...] = o
        r_ref[...] = jnp.broadcast_to(rs, (tq, LANES))

        def cond(c):
            j, rmin = c
            return jnp.logical_and(j >= 0, rmin < SB_DEAD)

        def body(c):
            j, _ = c
            k, v = load(j)
            r_prev = r_ref[...]
            o, rs = _sb_block(qh, k, v, r_prev, tri, False)
            acc_ref[...] += o
            r_new = r_prev + rs
            r_ref[...] = r_new
            return j - 1, jnp.min(r_new)

        lax.while_loop(cond, body, (qi - 1, jnp.float32(0.0)))
        outs.append(acc_ref[...])
    lane = lax.broadcasted_iota(jnp.int32, (tq, LANES), 1)
    o_ref[0] = jnp.where(lane < D_HB, outs[0], outs[1]).astype(BF16)


def _sb_prompt(q, k, v, *, tq):
    b, t, _ = q.shape
    return pl.pallas_call(
        functools.partial(_sb_prompt_kernel, tq=tq),
        grid=(b, H_B // 2, t // tq),
        in_specs=[pl.BlockSpec((1, tq, LANES), lambda i, j, n: (i, n, j)),
                  pl.BlockSpec((1, t, LANES), lambda i, j, n: (i, 0, j)),
                  pl.BlockSpec((1, t, LANES), lambda i, j, n: (i, 0, j))],
        out_specs=pl.BlockSpec((1, tq, LANES), lambda i, j, n: (i, n, j)),
        out_shape=jax.ShapeDtypeStruct((b, t, H_B * D_HB), BF16),
        scratch_shapes=[pltpu.VMEM((tq, LANES), F32), pltpu.VMEM((tq, LANES), F32)],
        compiler_params=_params(3),
        name="sb_prompt",
    )(q, k, v)


def _sb_sample_kernel(q_ref, kn_ref, vn_ref, kc_ref, vc_ref, o_ref, acc_ref, r_ref, *, tk):
    tq = q_ref.shape[1]
    past = kc_ref.shape[1]
    nblk = past // tk
    tri_new = _tri(tq)
    tri = _tri(tk)
    outs = []
    for hh in range(2):
        qh = _head_mask(q_ref[0], hh)
        o, rs = _sb_block(qh, kn_ref[0].astype(BF16), vn_ref[0].astype(BF16),
                          jnp.zeros((tq, LANES), F32), tri_new, True)
        acc_ref[...] = o
        r_ref[...] = jnp.broadcast_to(rs, (tq, LANES))

        def cond(c):
            j, rmin = c
            return jnp.logical_and(j >= 0, rmin < SB_DEAD)

        def body(c):
            j, _ = c
            start = pl.multiple_of(j * tk, tk)
            k = kc_ref[0, pl.ds(start, tk), :].astype(BF16)
            v = vc_ref[0, pl.ds(start, tk), :].astype(BF16)
            r_prev = r_ref[...]
            o, rs = _sb_block(qh, k, v, r_prev, tri, False)
            acc_ref[...] += o
            r_new = r_prev + rs
            r_ref[...] = r_new
            return j - 1, jnp.min(r_new)

        lax.while_loop(cond, body, (jnp.int32(nblk - 1), jnp.float32(0.0)))
        outs.append(acc_ref[...])
    lane = lax.broadcasted_iota(jnp.int32, (tq, LANES), 1)
    o_ref[0] = jnp.where(lane < D_HB, outs[0], outs[1]).astype(BF16)


def _sb_sample(q, k_new, v_new, k_cache, v_cache, *, tk):
    b, t, _ = q.shape
    past = k_cache.shape[1]
    new = pl.BlockSpec((1, t, LANES), lambda i, j: (i, 0, j))
    old = pl.BlockSpec((1, past, LANES), lambda i, j: (i, 0, j))
    return pl.pallas_call(
        functools.partial(_sb_sample_kernel, tk=tk),
        grid=(b, H_B // 2),
        in_specs=[new, new, new, old, old],
        out_specs=new,
        out_shape=jax.ShapeDtypeStruct((b, t, H_B * D_HB), BF16),
        scratch_shapes=[pltpu.VMEM((t, LANES), F32), pltpu.VMEM((t, LANES), F32)],
        compiler_params=_params(2),
        name="sb_sample",
    )(q, k_new, v_new, k_cache, v_cache)


def _mla_sample_kernel(q_ref, cn_ref, rn_ref, cc_ref, rc_ref, wabs_ref, wuv_ref, o_ref, *, kc, past_len):
    t = q_ref.shape[1]
    past = cc_ref.shape[1]
    qa, qr = [], []
    for hd in range(H_A):
        qh = q_ref[0, :, hd * HEAD_PAD:(hd + 1) * HEAD_PAD]
        qa.append(_dot(qh, wabs_ref[hd]).astype(BF16))
        qr.append(qh[:, :ROPE])
    qa = jnp.concatenate(qa, axis=0)
    qr = jnp.concatenate(qr, axis=0)
    rows = H_A * t

    pieces = [(cc_ref[0, c * kc:(c + 1) * kc, :], rc_ref[0, c * kc:(c + 1) * kc, :], c * kc)
              for c in range(past // kc)]
    pieces.append((cn_ref[0], rn_ref[0], past_len))
    scores, lat = [], []
    for ckv, kr, k0 in pieces:
        ckv = ckv.astype(BF16)
        s = _dot_t(qa, ckv) + _dot_t(qr, kr.astype(BF16))
        n = ckv.shape[0]
        if (k0 + n - 1) // CHUNK > past_len // CHUNK:
            qpos = past_len + lax.broadcasted_iota(jnp.int32, (rows, n), 0) % t
            kpos = k0 + lax.broadcasted_iota(jnp.int32, (rows, n), 1)
            s = jnp.where((kpos >> CHUNK_SHIFT) <= (qpos >> CHUNK_SHIFT), s, NEG)
        scores.append(s)
        lat.append(ckv)
    m = functools.reduce(jnp.maximum, [jnp.max(s, axis=-1, keepdims=True) for s in scores])
    l = jnp.zeros((rows, 1), F32)
    o_lat = jnp.zeros((rows, KV_RANK), F32)
    for s, ckv in zip(scores, lat):
        p = jnp.exp2(s - m)
        l = l + jnp.sum(p, axis=-1, keepdims=True)
        o_lat = o_lat + _dot(p.astype(BF16), ckv)
    o_lat = (o_lat / l).astype(BF16)
    o = jnp.zeros((t, H_A * DV), F32)
    for hd in range(H_A):
        o = o + _dot(o_lat[hd * t:(hd + 1) * t], wuv_ref[hd])
    o_ref[0] = o.astype(BF16)


def _mla_sample(q, ckv_new, kr_new, ckv_cache, kr_cache, w_abs, w_uv_heads, *, past_len, kc):
    b, t, _ = q.shape
    past = ckv_cache.shape[1]
    row = lambda n, s: pl.BlockSpec((1, n, s), lambda i: (i, 0, 0))
    return pl.pallas_call(
        functools.partial(_mla_sample_kernel, kc=kc, past_len=past_len),
        grid=(b,),
        in_specs=[row(t, H_A * HEAD_PAD), row(t, KV_RANK), row(t, ROPE),
                  row(past, KV_RANK), row(past, ROPE),
                  _const_spec((H_A, HEAD_PAD, KV_RANK)), _const_spec((H_A, KV_RANK, H_A * DV))],
        out_specs=row(t, H_A * DV),
        out_shape=jax.ShapeDtypeStruct((b, t, H_A * DV), BF16),
        compiler_params=_params(1),
        name="mla_sample",
    )(q, ckv_new, kr_new, ckv_cache, kr_cache, w_abs, w_uv_heads)


def _merge_kernel(x_ref, ada_ref, g_ref, wg_ref, oa_ref, ob_ref, wpa_ref, wpb_ref, wo_ref, gpost_ref, o_ref):
    x = x_ref[0]
    h = _premix(x, ada_ref, g_ref[...]).astype(BF16)
    gates = _sigmoid(_dot(h, wg_ref[...]))
    merged = (gates[:, :D_MODEL] * _dot(oa_ref[0], wpa_ref[...])
              + gates[:, D_MODEL:] * _dot(ob_ref[0], wpb_ref[...]))
    mo = _dot(merged.astype(BF16), wo_ref[...])
    gt1 = ada_ref[0, 2:3, :]
    o_ref[0] = x + gt1 * _rms(mo, gpost_ref[...])


def _merge(x, ada, o_a, o_b, wts, *, tm):
    b, t, _ = x.shape
    tok = lambda n: pl.BlockSpec((1, tm, n), lambda i, j: (i, j, 0))
    return pl.pallas_call(
        _merge_kernel,
        grid=(b, t // tm),
        in_specs=[tok(D_MODEL),
                  pl.BlockSpec((1, 6, D_MODEL), lambda i, j: (i, 0, 0)),
                  _const_spec((1, D_MODEL)),
                  _const_spec((D_MODEL, 2 * D_MODEL)),
                  tok(H_A * DV), tok(H_B * D_HB),
                  _const_spec((H_A * DV, D_MODEL)), _const_spec((H_B * D_HB, D_MODEL)),
                  _const_spec((D_MODEL, D_MODEL)),
                  _const_spec((1, D_MODEL))],
        out_specs=tok(D_MODEL),
        out_shape=jax.ShapeDtypeStruct((b, t, D_MODEL), F32),
        compiler_params=_params(2),
        name="merge",
    )(x, ada, wts["g_pre_mix"], wts["w_gate"], o_a, o_b, wts["w_proj_a"], wts["w_proj_b"],
      wts["w_out"], wts["g_post_mix"])


FF_CHUNK = 256
HALO = 8


def _gelu_tanh(a):
    return 0.5 * a * (1.0 + jnp.tanh(math.sqrt(2.0 / math.pi) * (a + 0.044715 * (a * a * a))))


def _ffn_kernel(x_ref, ada_ref, g_ref, wup_ref, cw_ref, cb_ref, wdn_ref, gpost_ref, cs_ref,
                o_ref, nc_ref, halo_ref, *, tm):
    ti = pl.program_id(1)
    nt = pl.num_programs(1)

    @pl.when(ti == 0)
    def _():
        halo_ref[0:HALO - 2, :] = jnp.zeros((HALO - 2, 2 * D_FF), F32)
        halo_ref[HALO - 2:HALO, :] = cs_ref[0]

    x = x_ref[0]
    sh = ada_ref[0, 3:4, :]
    sc = ada_ref[0, 4:5, :]
    gt2 = ada_ref[0, 5:6, :]
    h2 = (_rms(x, g_ref[...]) * (1.0 + sc) + sh).astype(BF16)

    def conv(col):
        u = _dot(h2, wup_ref[:, col:col + FF_CHUNK])
        ext = jnp.concatenate([halo_ref[:, col:col + FF_CHUNK], u], axis=0)
        halo_ref[:, col:col + FF_CHUNK] = u[tm - HALO:, :]
        y = cb_ref[:, col:col + FF_CHUNK]
        for i in range(CONV_W):
            off = HALO - (CONV_W - 1) + i
            y = y + cw_ref[i:i + 1, col:col + FF_CHUNK] * ext[off:off + tm, :]
        return y

    acc = jnp.zeros((tm, D_MODEL), F32)
    for c in range(D_FF // FF_CHUNK):
        ya = conv(c * FF_CHUNK)
        yb = conv(D_FF + c * FF_CHUNK)
        g = (_gelu_tanh(ya) * yb).astype(BF16)
        acc = acc + _dot(g, wdn_ref[c * FF_CHUNK:(c + 1) * FF_CHUNK, :])
    o_ref[0] = x + gt2 * _rms(acc, gpost_ref[...])

    @pl.when(ti == nt - 1)
    def _():
        nc_ref[0] = halo_ref[HALO - (CONV_W - 1):HALO, :]


def _ffn(x, ada, conv_state, wts, *, tm):
    b, t, _ = x.shape
    assert tm >= HALO and t % tm == 0
    tok = pl.BlockSpec((1, tm, D_MODEL), lambda i, j: (i, j, 0))
    state = pl.BlockSpec((1, CONV_W - 1, 2 * D_FF), lambda i, j: (i, 0, 0))
    return pl.pallas_call(
        functools.partial(_ffn_kernel, tm=tm),
        grid=(b, t // tm),
        in_specs=[tok,
                  pl.BlockSpec((1, 6, D_MODEL), lambda i, j: (i, 0, 0)),
                  _const_spec((1, D_MODEL)),
                  _const_spec((D_MODEL, 2 * D_FF)),
                  _const_spec((CONV_W, 2 * D_FF)),
                  _const_spec((1, 2 * D_FF)),
                  _const_spec((D_FF, D_MODEL)),
                  _const_spec((1, D_MODEL)),
                  state],
        out_specs=[tok, state],
        out_shape=[jax.ShapeDtypeStruct((b, t, D_MODEL), F32),
                   jax.ShapeDtypeStruct((b, CONV_W - 1, 2 * D_FF), F32)],
        scratch_shapes=[pltpu.VMEM((HALO, 2 * D_FF), F32)],
        compiler_params=_params(1, 1),
        name="conv_ffn",
    )(x, ada, wts["g_pre_ffn"], wts["w_up"], wts["conv_w"], wts["conv_b"], wts["w_down"],
      wts["g_post_ffn"], conv_state)


def _rope_tables(pos):
    inv = ROPE_BASE ** (-jnp.arange(0, ROPE, 2, dtype=F32) / ROPE)
    ang = pos.astype(F32)[:, None] * inv[None, :]
    cos, sin = jnp.cos(ang), jnp.sin(ang)
    t = pos.shape[0]
    cos_t = jnp.concatenate([cos, cos, jnp.ones((t, NOPE), F32), jnp.zeros((t, HEAD_PAD - ROPE - NOPE), F32)], axis=1)
    sin_t = jnp.concatenate([-sin, sin, jnp.zeros((t, HEAD_PAD - ROPE), F32)], axis=1)
    return cos_t, sin_t


def _swap_halves(w):
    return jnp.concatenate([w[..., ROPE // 2:], w[..., :ROPE // 2]], axis=-1)


def _pad_lanes(w, n):
    return jnp.pad(w, [(0, 0)] * (w.ndim - 1) + [(0, n - w.shape[-1])])


def _layer_weights(l, g_pre_mix, g_post_mix, g_pre_ffn, g_post_ffn, w_in, g_q_lat, w_uq, g_kv_lat,
                   w_uk, w_uv, w_proj_a, w_proj_b, w_out, w_up, conv_w, conv_b, w_down):
    splits = np.cumsum([Q_RANK, KV_RANK, ROPE, H_B * D_HB, H_B * D_HB, H_B * D_HB, D_MODEL])
    wi = w_in[l]
    w_ql, w_ckv, w_kr, w_qb, w_kb, w_vb, w_ga, w_gb = jnp.split(wi, splits, axis=1)
    w_in_k1 = jnp.concatenate([w_ql, w_ckv, _pad_lanes(w_kr, LANES), _pad_lanes(_swap_halves(w_kr), LANES),
                               w_qb, w_kb, w_vb], axis=1).astype(BF16)
    wq = w_uq[l].reshape(Q_RANK, H_A, NOPE + ROPE)
    wq_nope, wq_rope = wq[..., :NOPE], wq[..., NOPE:]
    zeros = jnp.zeros((Q_RANK, H_A, HEAD_PAD - ROPE - NOPE), F32)
    q_main = jnp.concatenate([wq_rope, wq_nope, zeros], axis=-1).reshape(Q_RANK, H_A * HEAD_PAD)
    q_rot = _pad_lanes(_swap_halves(wq_rope), HEAD_PAD).reshape(Q_RANK, H_A * HEAD_PAD)
    wk = w_uk[l].reshape(KV_RANK, H_A, NOPE)
    w_uk_pad = jnp.pad(wk, [(0, 0), (0, 0), (ROPE, HEAD_PAD - ROPE - NOPE)])
    w_abs = jnp.transpose(w_uk_pad, (1, 2, 0))
    head_of_col = jnp.arange(H_A * DV) // DV
    w_uv_heads = jnp.where(head_of_col[None, None, :] == jnp.arange(H_A)[:, None, None], w_uv[l][None], 0.0)
    row = lambda g: g[l].reshape(1, -1)
    return {
        "g_pre_mix": row(g_pre_mix), "g_post_mix": row(g_post_mix),
        "g_pre_ffn": row(g_pre_ffn), "g_post_ffn": row(g_post_ffn),
        "g_q_lat": row(g_q_lat), "g_kv_lat": row(g_kv_lat),
        "w_in_k1": w_in_k1,
        "w_gate": jnp.concatenate([w_ga, w_gb], axis=1).astype(BF16),
        "w_uq_ext": jnp.concatenate([q_main, q_rot], axis=1).astype(BF16),
        "w_uk_pad": w_uk_pad.reshape(KV_RANK, H_A * HEAD_PAD).astype(BF16),
        "w_abs": w_abs.astype(BF16),
        "w_uv": w_uv[l].astype(BF16),
        "w_uv_heads": w_uv_heads.astype(BF16),
        "w_proj_a": w_proj_a[l].astype(BF16), "w_proj_b": w_proj_b[l].astype(BF16),
        "w_out": w_out[l].astype(BF16),
        "w_up": w_up[l].astype(BF16), "conv_w": conv_w[l], "conv_b": conv_b[l].reshape(1, -1),
        "w_down": w_down[l].astype(BF16),
    }


def _pick_tile(t, want):
    tm = min(t, want)
    assert t % tm == 0
    return tm


def _layer(x, ada, pos, past, wts):
    b, t, _ = x.shape
    cos_t, sin_t = _rope_tables(pos)
    tm = _pick_tile(t, 256)
    if past is None:
        q, ckv, kr, qb, kb, vb, k_mla, v_mla = _mixer_inputs(x, ada, cos_t, sin_t, wts, tm=tm, expand_kv=True)
        blk = _pick_tile(t, 256)
        o_a = _mla_prompt(q, k_mla, v_mla, tq=blk, tk=blk)
        o_b = _sb_prompt(qb, kb, vb, tq=blk)
        conv_state = jnp.zeros((b, CONV_W - 1, 2 * D_FF), F32)
    else:
        past_ckv, past_kr, past_k, past_v, conv_state = past
        past_len = past_ckv.shape[1]
        q, ckv, kr, qb, kb, vb = _mixer_inputs(x, ada, cos_t, sin_t, wts, tm=tm, expand_kv=False)
        o_a = _mla_sample(q, ckv, kr, past_ckv, past_kr, wts["w_abs"], wts["w_uv_heads"],
                          past_len=past_len, kc=_pick_tile(past_len, 1024))
        o_b = _sb_sample(qb, kb, vb, past_k.reshape(b, past_len, H_B * D_HB),
                         past_v.reshape(b, past_len, H_B * D_HB), tk=_pick_tile(past_len, 256))
    x1 = _merge(x, ada, o_a, o_b, wts, tm=tm)
    y, new_conv = _ffn(x1, ada, conv_state, wts, tm=tm)
    state = (ckv, kr, kb.reshape(b, t, H_B, D_HB), vb.reshape(b, t, H_B, D_HB), new_conv)
    return y, state


def kernel(x_prompt, x_sample, cache_mla_ckv, cache_mla_krope, cache_sb_k, cache_sb_v, state_ffn_conv,
           c_prompt, c_sample, w_ada, b_ada, g_pre_mix, g_post_mix, g_pre_ffn, g_post_ffn,
           w_in, g_q_lat, w_uq, g_kv_lat, w_uk, w_uv, w_proj_a, w_proj_b, w_out,
           w_up, conv_w, conv_b, w_down):
    depth = w_in.shape[0]
    nb_p = x_prompt.shape[0]
    past_len = cache_mla_ckv.shape[2]
    pos_p = jnp.arange(x_prompt.shape[1], dtype=jnp.int32)
    pos_s = past_len + jnp.arange(x_sample.shape[1], dtype=jnp.int32)
    xp, xs = x_prompt, x_sample
    c_all = jnp.concatenate([c_prompt, c_sample], axis=0)
    st_p = [[] for _ in range(5)]
    st_s = [[] for _ in range(5)]
    for l in range(depth):
        wts = _layer_weights(l, g_pre_mix, g_post_mix, g_pre_ffn, g_post_ffn, w_in, g_q_lat, w_uq, g_kv_lat,
                             w_uk, w_uv, w_proj_a, w_proj_b, w_out, w_up, conv_w, conv_b, w_down)
        ada = _ada(c_all, w_ada[l], b_ada[l]).reshape(-1, 6, D_MODEL)
        xp, sp = _layer(xp, ada[:nb_p], pos_p, None, wts)
        past = (cache_mla_ckv[l], cache_mla_krope[l], cache_sb_k[l], cache_sb_v[l], state_ffn_conv[l])
        xs, ss = _layer(xs, ada[nb_p:], pos_s, past, wts)
        for i in range(5):
            st_p[i].append(sp[i])
            st_s[i].append(ss[i])
    p_state = [jnp.stack(a, axis=0) for a in st_p]
    s_state = [jnp.stack(a, axis=0) for a in st_s]
    return (xp, xs, *p_state, *s_state)
```

```python
import functools
import math

import numpy as np
import jax
import jax.numpy as jnp
from jax import lax
from jax.experimental import pallas as pl
from jax.experimental.pallas import tpu as pltpu

D_MODEL = 1024
CHUNK = 64
CHUNK_SHIFT = 6
H_A = 8
NOPE = 64
ROPE = 32
DV = 64
Q_RANK = 384
KV_RANK = 256
ROPE_BASE = 10000.0
H_B = 8
D_HB = 64
D_FF = 2816
CONV_W = 3
EPS = 1e-6
NEG = -1e30
MLA_SCALE = (NOPE + ROPE) ** -0.5
SB_SCALE = D_HB ** -0.5
LOG2E = math.log2(math.e)

LANES = 128
HEAD_PAD = LANES
SB_DEAD = 110.0
VMEM_LIMIT = 56 * 1024 * 1024

F32 = jnp.float32
BF16 = jnp.bfloat16


def _dot(a, b):
    return jnp.dot(a, b, preferred_element_type=F32)


def _dot_t(a, b):
    return lax.dot_general(a, b, (((1,), (1,)), ((), ())), preferred_element_type=F32)


def _rms(x, g):
    return x * lax.rsqrt(jnp.mean(x * x, axis=-1, keepdims=True) + EPS) * g


def _sigmoid(x):
    return 1.0 / (1.0 + jnp.exp(-x))


def _params(n_parallel, n_arbitrary=0):
    return pltpu.CompilerParams(
        dimension_semantics=("parallel",) * n_parallel + ("arbitrary",) * n_arbitrary,
        vmem_limit_bytes=VMEM_LIMIT)


def _const_spec(shape):
    n = len(shape)
    return pl.BlockSpec(shape, lambda *_: (0,) * n)


def _ada_kernel(c_ref, w_ref, b_ref, o_ref):
    c = c_ref[...]
    s = (c * _sigmoid(c)).astype(BF16)
    o_ref[...] = _dot(s, w_ref[...].astype(BF16)) + b_ref[...]


def _ada(c_all, w_ada, b_ada):
    n = c_all.shape[0]
    nchunk = 6
    return pl.pallas_call(
        _ada_kernel,
        grid=(nchunk,),
        in_specs=[pl.BlockSpec((n, D_MODEL), lambda j: (0, 0)),
                  pl.BlockSpec((D_MODEL, D_MODEL), lambda j: (0, j)),
                  pl.BlockSpec((1, D_MODEL), lambda j: (0, j))],
        out_specs=pl.BlockSpec((n, D_MODEL), lambda j: (0, j)),
        out_shape=jax.ShapeDtypeStruct((n, 6 * D_MODEL), F32),
        compiler_params=_params(1),
        name="ada",
    )(c_all, w_ada, b_ada.reshape(1, -1))


_C_QLAT = 0
_C_CKV = _C_QLAT + Q_RANK
_C_KR = _C_CKV + KV_RANK
_C_KRR = _C_KR + LANES
_C_QB = _C_KRR + LANES
_C_KB = _C_QB + H_B * D_HB
_C_VB = _C_KB + H_B * D_HB
_N_K1 = _C_VB + H_B * D_HB


def _premix(x, ada_ref, g):
    sh = ada_ref[0, 0:1, :]
    sc = ada_ref[0, 1:2, :]
    return _rms(x, g) * (1.0 + sc) + sh


def _mixer_kernel(x_ref, ada_ref, g_ref, win_ref, gq_ref, wuq_ref, gkv_ref, wuk_ref, wuv_ref, vone_ref,
                  cos_ref, sin_ref, *out_refs, expand_kv):
    if expand_kv:
        q_ref, ckv_ref, kr_ref, qb_ref, kb_ref, vb_ref, k_ref, v_ref = out_refs
    else:
        q_ref, ckv_ref, kr_ref, qb_ref, kb_ref, vb_ref = out_refs
    h = _premix(x_ref[0], ada_ref, g_ref[...]).astype(BF16)
    p = _dot(h, win_ref[...])
    cos = cos_ref[...]
    sin = sin_ref[...]

    q_lat = _rms(p[:, _C_QLAT:_C_QLAT + Q_RANK], gq_ref[...]).astype(BF16)
    q2 = _dot(q_lat, wuq_ref[...])
    nq = H_A * HEAD_PAD
    for hd in range(H_A):
        lo = hd * HEAD_PAD
        qh = q2[:, lo:lo + HEAD_PAD] * cos + q2[:, nq + lo:nq + lo + HEAD_PAD] * sin
        q_ref[0, :, lo:lo + HEAD_PAD] = (qh * (MLA_SCALE * LOG2E)).astype(BF16)

    c_kv = _rms(p[:, _C_CKV:_C_CKV + KV_RANK], gkv_ref[...])
    ckv_ref[0] = c_kv
    kr = p[:, _C_KR:_C_KR + LANES] * cos + p[:, _C_KRR:_C_KRR + LANES] * sin
    kr_ref[0] = kr[:, :ROPE]

    qb_ref[0] = (p[:, _C_QB:_C_QB + H_B * D_HB] * SB_SCALE).astype(BF16)
    kb_ref[0] = p[:, _C_KB:_C_KB + H_B * D_HB]
    vb_ref[0] = p[:, _C_VB:_C_VB + H_B * D_HB]

    if expand_kv:
        c_bf = c_kv.astype(BF16)
        k2 = _dot(c_bf, wuk_ref[...])
        for hd in range(H_A):
            lo = hd * HEAD_PAD
            k_ref[0, :, lo:lo + HEAD_PAD] = (k2[:, lo:lo + HEAD_PAD] + kr).astype(BF16)
        v_ref[0] = (_dot(c_bf, wuv_ref[...]) + vone_ref[...]).astype(BF16)


def _mixer_inputs(x, ada, cos_t, sin_t, wts, *, tm, expand_kv):
    b, t, _ = x.shape
    nt = t // tm
    tok = lambda n: pl.BlockSpec((1, tm, n), lambda i, j: (i, j, 0))
    in_specs = [tok(D_MODEL),
                pl.BlockSpec((1, 6, D_MODEL), lambda i, j: (i, 0, 0)),
                _const_spec((1, D_MODEL)),
                _const_spec((D_MODEL, _N_K1)),
                _const_spec((1, Q_RANK)),
                _const_spec((Q_RANK, 2 * H_A * HEAD_PAD)),
                _const_spec((1, KV_RANK)),
                _const_spec((KV_RANK, H_A * HEAD_PAD)),
                _const_spec((KV_RANK, H_A * HEAD_PAD)),
                _const_spec((1, H_A * HEAD_PAD)),
                pl.BlockSpec((tm, LANES), lambda i, j: (j, 0)),
                pl.BlockSpec((tm, LANES), lambda i, j: (j, 0))]
    shapes = [((b, t, H_A * HEAD_PAD), BF16), ((b, t, KV_RANK), F32), ((b, t, ROPE), F32),
              ((b, t, H_B * D_HB), BF16), ((b, t, H_B * D_HB), F32), ((b, t, H_B * D_HB), F32)]
    if expand_kv:
        shapes += [((b, t, H_A * HEAD_PAD), BF16), ((b, t, H_A * HEAD_PAD), BF16)]
    return pl.pallas_call(
        functools.partial(_mixer_kernel, expand_kv=expand_kv),
        grid=(b, nt),
        in_specs=in_specs,
        out_specs=[tok(s[-1]) for s, _ in shapes],
        out_shape=[jax.ShapeDtypeStruct(s, d) for s, d in shapes],
        compiler_params=_params(2),
        name="mixer_in_kv" if expand_kv else "mixer_in",
    )(x, ada, wts["g_pre_mix"], wts["w_in_k1"], wts["g_q_lat"], wts["w_uq_ext"], wts["g_kv_lat"],
      wts["w_uk_pad"], wts["w_uv_pad"], wts["v_ones"], cos_t, sin_t)


def _mla_prompt_kernel(q_ref, k_ref, v_ref, o_ref, m_ref, acc_ref, *, tq, tk):
    qi = pl.program_id(2)
    n_diag = tq // tk
    n_full = qi * n_diag
    m_ref[...] = jnp.full(m_ref.shape, NEG, F32)
    acc_ref[...] = jnp.zeros(acc_ref.shape, F32)

    def step(j, masked):
        start = pl.multiple_of(j * tk, tk)
        for hh in range(2):
            lanes = slice(hh * HEAD_PAD, (hh + 1) * HEAD_PAD)
            s = _dot_t(q_ref[0, :, lanes], k_ref[0, pl.ds(start, tk), lanes])
            if masked:
                qpos = qi * tq + lax.broadcasted_iota(jnp.int32, (tq, tk), 0)
                kpos = j * tk + lax.broadcasted_iota(jnp.int32, (tq, tk), 1)
                s = jnp.where((kpos >> CHUNK_SHIFT) <= (qpos >> CHUNK_SHIFT), s, NEG)
            m_old = m_ref[hh]
            m_new = jnp.maximum(m_old, jnp.max(s, axis=-1, keepdims=True))
            alpha = jnp.exp2(m_old - m_new)
            p = jnp.exp2(s - jnp.concatenate([m_new] * (tk // LANES), axis=1))
            acc_ref[hh] = alpha * acc_ref[hh] + _dot(p.astype(BF16), v_ref[0, pl.ds(start, tk), lanes])
            m_ref[hh] = m_new

    def body(j, c):
        step(j, False)
        return c

    lax.fori_loop(0, n_full, body, 0)
    for d in range(n_diag):
        step(n_full + d, True)
    acc_e, acc_o = acc_ref[0], acc_ref[1]
    lane = lax.broadcasted_iota(jnp.int32, (tq, LANES), 1)
    o_ref[0] = jnp.where(lane < DV, acc_e / acc_e[:, DV:DV + 1], acc_o / acc_o[:, 0:1]).astype(BF16)


def _mla_prompt(q, k, v, *, tq, tk):
    b, t, _ = q.shape
    return pl.pallas_call(
        functools.partial(_mla_prompt_kernel, tq=tq, tk=tk),
        grid=(b, H_A // 2, t // tq),
        in_specs=[pl.BlockSpec((1, tq, 2 * HEAD_PAD), lambda i, j, n: (i, n, j)),
                  pl.BlockSpec((1, t, 2 * HEAD_PAD), lambda i, j, n: (i, 0, j)),
                  pl.BlockSpec((1, t, 2 * HEAD_PAD), lambda i, j, n: (i, 0, j))],
        out_specs=pl.BlockSpec((1, tq, 2 * DV), lambda i, j, n: (i, n, j)),
        out_shape=jax.ShapeDtypeStruct((b, t, H_A * DV), BF16),
        scratch_shapes=[pltpu.VMEM((2, tq, LANES), F32), pltpu.VMEM((2, tq, LANES), F32)],
        compiler_params=_params(3),
        name="mla_prompt",
    )(q, k, v)


def _tri(n):
    r = lax.broadcasted_iota(jnp.int32, (n, n), 0)
    c = lax.broadcasted_iota(jnp.int32, (n, n), 1)
    return jnp.where(r > c, 1.0, 0.0).astype(BF16)


def _sb_block(qh, k, v, r_prev, tri, causal):
    tq, tk = qh.shape[0], k.shape[0]
    z = _dot_t(qh, k)
    sp = jnp.maximum(z, 0.0) + jnp.log(1.0 + jnp.exp(-jnp.abs(z)))
    if causal:
        row = lax.broadcasted_iota(jnp.int32, (tq, tk), 0)
        col = lax.broadcasted_iota(jnp.int32, (tq, tk), 1)
        keep = col < row
        sp = jnp.where(keep, sp, 0.0)
    sp_hi = sp.astype(BF16)
    sp_lo = (sp - sp_hi.astype(F32)).astype(BF16)
    later = _dot(sp_hi, tri) + _dot(sp_lo, tri)
    if tk >= LANES:
        r_b = jnp.concatenate([r_prev] * (tk // LANES), axis=1)
    else:
        r_b = r_prev[:, :tk]
    a = jnp.exp(z - sp - later - r_b)
    if causal:
        a = jnp.where(keep, a, 0.0)
    return _dot(a.astype(BF16), v), jnp.sum(sp, axis=-1, keepdims=True)


def _head_mask(x, hh):
    lane = lax.broadcasted_iota(jnp.int32, x.shape, 1)
    mine = jnp.logical_and(lane >= hh * D_HB, lane < (hh + 1) * D_HB)
    return jnp.where(mine, x, jnp.zeros_like(x))


def _sb_prompt_kernel(q_ref, k_ref, v_ref, o_ref, acc_ref, r_ref, *, tq):
    qi = pl.program_id(2)
    tri = _tri(tq)
    outs = []
    for hh in range(2):
        qh = _head_mask(q_ref[0], hh)

        def load(j):
            start = pl.multiple_of(j * tq, tq)
            return (k_ref[0, pl.ds(start, tq), :].astype(BF16),
                    v_ref[0, pl.ds(start, tq), :].astype(BF16))

        k, v = load(qi)
        o, rs = _sb_block(qh, k, v, jnp.zeros((tq, LANES), F32), tri, True)
        acc_ref[...] = o
        r_ref[...] = jnp.broadcast_to(rs, (tq, LANES))

        def cond(c):
            j, rmin = c
            return jnp.logical_and(j >= 0, rmin < SB_DEAD)

        def body(c):
            j, _ = c
            k, v = load(j)
            r_prev = r_ref[...]
            o, rs = _sb_block(qh, k, v, r_prev, tri, False)
            acc_ref[...] += o
            r_new = r_prev + rs
            r_ref[...] = r_new
            return j - 1, jnp.min(r_new)

        lax.while_loop(cond, body, (qi - 1, jnp.float32(0.0)))
        outs.append(acc_ref[...])
    lane = lax.broadcasted_iota(jnp.int32, (tq, LANES), 1)
    o_ref[0] = jnp.where(lane < D_HB, outs[0], outs[1]).astype(BF16)


def _sb_prompt(q, k, v, *, tq):
    b, t, _ = q.shape
    return pl.pallas_call(
        functools.partial(_sb_prompt_kernel, tq=tq),
        grid=(b, H_B // 2, t // tq),
        in_specs=[pl.BlockSpec((1, tq, LANES), lambda i, j, n: (i, n, j)),
                  pl.BlockSpec((1, t, LANES), lambda i, j, n: (i, 0, j)),
                  pl.BlockSpec((1, t, LANES), lambda i, j, n: (i, 0, j))],
        out_specs=pl.BlockSpec((1, tq, LANES), lambda i, j, n: (i, n, j)),
        out_shape=jax.ShapeDtypeStruct((b, t, H_B * D_HB), BF16),
        scratch_shapes=[pltpu.VMEM((tq, LANES), F32), pltpu.VMEM((tq, LANES), F32)],
        compiler_params=_params(3),
        name="sb_prompt",
    )(q, k, v)


def _sb_sample_kernel(q_ref, kn_ref, vn_ref, kc_ref, vc_ref, o_ref, acc_ref, r_ref, *, tk):
    tq = q_ref.shape[1]
    past = kc_ref.shape[1]
    nblk = past // tk
    tri_new = _tri(tq)
    tri = _tri(tk)
    outs = []
    for hh in range(2):
        qh = _head_mask(q_ref[0], hh)
        o, rs = _sb_block(qh, kn_ref[0].astype(BF16), vn_ref[0].astype(BF16),
                          jnp.zeros((tq, LANES), F32), tri_new, True)
        acc_ref[...] = o
        r_ref[...] = jnp.broadcast_to(rs, (tq, LANES))

        def cond(c):
            j, rmin = c
            return jnp.logical_and(j >= 0, rmin < SB_DEAD)

        def body(c):
            j, _ = c
            start = pl.multiple_of(j * tk, tk)
            k = kc_ref[0, pl.ds(start, tk), :].astype(BF16)
            v = vc_ref[0, pl.ds(start, tk), :].astype(BF16)
            r_prev = r_ref[...]
            o, rs = _sb_block(qh, k, v, r_prev, tri, False)
            acc_ref[...] += o
            r_new = r_prev + rs
            r_ref[...] = r_new
            return j - 1, jnp.min(r_new)

        lax.while_loop(cond, body, (jnp.int32(nblk - 1), jnp.float32(0.0)))
        outs.append(acc_ref[...])
    lane = lax.broadcasted_iota(jnp.int32, (tq, LANES), 1)
    o_ref[0] = jnp.where(lane < D_HB, outs[0], outs[1]).astype(BF16)


def _sb_sample(q, k_new, v_new, k_cache, v_cache, *, tk):
    b, t, _ = q.shape
    past = k_cache.shape[1]
    new = pl.BlockSpec((1, t, LANES), lambda i, j: (i, 0, j))
    old = pl.BlockSpec((1, past, LANES), lambda i, j: (i, 0, j))
    return pl.pallas_call(
        functools.partial(_sb_sample_kernel, tk=tk),
        grid=(b, H_B // 2),
        in_specs=[new, new, new, old, old],
        out_specs=new,
        out_shape=jax.ShapeDtypeStruct((b, t, H_B * D_HB), BF16),
        scratch_shapes=[pltpu.VMEM((t, LANES), F32), pltpu.VMEM((t, LANES), F32)],
        compiler_params=_params(2),
        name="sb_sample",
    )(q, k_new, v_new, k_cache, v_cache)


def _mla_sample_kernel(q_ref, cn_ref, rn_ref, cc_ref, rc_ref, wabs_ref, wuv_ref, o_ref, *, kc, past_len):
    t = q_ref.shape[1]
    past = cc_ref.shape[1]
    qa, qr = [], []
    for hd in range(H_A):
        qh = q_ref[0, :, hd * HEAD_PAD:(hd + 1) * HEAD_PAD]
        qa.append(_dot(qh, wabs_ref[hd]).astype(BF16))
        qr.append(qh[:, :ROPE])
    qa = jnp.concatenate(qa, axis=0)
    qr = jnp.concatenate(qr, axis=0)
    rows = H_A * t

    pieces = [(cc_ref[0, c * kc:(c + 1) * kc, :], rc_ref[0, c * kc:(c + 1) * kc, :], c * kc)
              for c in range(past // kc)]
    pieces.append((cn_ref[0], rn_ref[0], past_len))
    scores, lat = [], []
    for ckv, kr, k0 in pieces:
        ckv = ckv.astype(BF16)
        s = _dot_t(qa, ckv) + _dot_t(qr, kr.astype(BF16))
        n = ckv.shape[0]
        if (k0 + n - 1) // CHUNK > past_len // CHUNK:
            qpos = past_len + lax.broadcasted_iota(jnp.int32, (rows, n), 0) % t
            kpos = k0 + lax.broadcasted_iota(jnp.int32, (rows, n), 1)
            s = jnp.where((kpos >> CHUNK_SHIFT) <= (qpos >> CHUNK_SHIFT), s, NEG)
        scores.append(s)
        lat.append(ckv)
    m = functools.reduce(jnp.maximum, [jnp.max(s, axis=-1, keepdims=True) for s in scores])
    l = jnp.zeros((rows, 1), F32)
    o_lat = jnp.zeros((rows, KV_RANK), F32)
    for s, ckv in zip(scores, lat):
        p = jnp.exp2(s - m)
        l = l + jnp.sum(p, axis=-1, keepdims=True)
        o_lat = o_lat + _dot(p.astype(BF16), ckv)
    o_lat = (o_lat / l).astype(BF16)
    o = jnp.zeros((t, H_A * DV), F32)
    for hd in range(H_A):
        o = o + _dot(o_lat[hd * t:(hd + 1) * t], wuv_ref[hd])
    o_ref[0] = o.astype(BF16)


def _mla_sample(q, ckv_new, kr_new, ckv_cache, kr_cache, w_abs, w_uv_heads, *, past_len, kc):
    b, t, _ = q.shape
    past = ckv_cache.shape[1]
    row = lambda n, s: pl.BlockSpec((1, n, s), lambda i: (i, 0, 0))
    return pl.pallas_call(
        functools.partial(_mla_sample_kernel, kc=kc, past_len=past_len),
        grid=(b,),
        in_specs=[row(t, H_A * HEAD_PAD), row(t, KV_RANK), row(t, ROPE),
                  row(past, KV_RANK), row(past, ROPE),
                  _const_spec((H_A, HEAD_PAD, KV_RANK)), _const_spec((H_A, KV_RANK, H_A * DV))],
        out_specs=row(t, H_A * DV),
        out_shape=jax.ShapeDtypeStruct((b, t, H_A * DV), BF16),
        compiler_params=_params(1),
        name="mla_sample",
    )(q, ckv_new, kr_new, ckv_cache, kr_cache, w_abs, w_uv_heads)


def _merge_kernel(x_ref, ada_ref, g_ref, wg_ref, oa_ref, ob_ref, wpa_ref, wpb_ref, wo_ref, gpost_ref, o_ref):
    x = x_ref[0]
    h = _premix(x, ada_ref, g_ref[...]).astype(BF16)
    gates = _sigmoid(_dot(h, wg_ref[...]))
    merged = (gates[:, :D_MODEL] * _dot(oa_ref[0], wpa_ref[...])
              + gates[:, D_MODEL:] * _dot(ob_ref[0], wpb_ref[...]))
    mo = _dot(merged.astype(BF16), wo_ref[...])
    gt1 = ada_ref[0, 2:3, :]
    o_ref[0] = x + gt1 * _rms(mo, gpost_ref[...])


def _merge(x, ada, o_a, o_b, wts, *, tm):
    b, t, _ = x.shape
    tok = lambda n: pl.BlockSpec((1, tm, n), lambda i, j: (i, j, 0))
    return pl.pallas_call(
        _merge_kernel,
        grid=(b, t // tm),
        in_specs=[tok(D_MODEL),
                  pl.BlockSpec((1, 6, D_MODEL), lambda i, j: (i, 0, 0)),
                  _const_spec((1, D_MODEL)),
                  _const_spec((D_MODEL, 2 * D_MODEL)),
                  tok(H_A * DV), tok(H_B * D_HB),
                  _const_spec((H_A * DV, D_MODEL)), _const_spec((H_B * D_HB, D_MODEL)),
                  _const_spec((D_MODEL, D_MODEL)),
                  _const_spec((1, D_MODEL))],
        out_specs=tok(D_MODEL),
        out_shape=jax.ShapeDtypeStruct((b, t, D_MODEL), F32),
        compiler_params=_params(2),
        name="merge",
    )(x, ada, wts["g_pre_mix"], wts["w_gate"], o_a, o_b, wts["w_proj_a"], wts["w_proj_b"],
      wts["w_out"], wts["g_post_mix"])


FF_CHUNK = 256
HALO = 8


def _gelu_tanh(a):
    return 0.5 * a * (1.0 + jnp.tanh(math.sqrt(2.0 / math.pi) * (a + 0.044715 * (a * a * a))))


def _ffn_kernel(x_ref, ada_ref, g_ref, wup_ref, cw_ref, cb_ref, wdn_ref, gpost_ref, cs_ref,
                o_ref, nc_ref, halo_ref, *, tm):
    ti = pl.program_id(1)
    nt = pl.num_programs(1)

    @pl.when(ti == 0)
    def _():
        halo_ref[0:HALO - 2, :] = jnp.zeros((HALO - 2, 2 * D_FF), F32)
        halo_ref[HALO - 2:HALO, :] = cs_ref[0]

    x = x_ref[0]
    sh = ada_ref[0, 3:4, :]
    sc = ada_ref[0, 4:5, :]
    gt2 = ada_ref[0, 5:6, :]
    h2 = (_rms(x, g_ref[...]) * (1.0 + sc) + sh).astype(BF16)

    def conv(col):
        u = _dot(h2, wup_ref[:, col:col + FF_CHUNK])
        ext = jnp.concatenate([halo_ref[:, col:col + FF_CHUNK], u], axis=0)
        halo_ref[:, col:col + FF_CHUNK] = u[tm - HALO:, :]
        y = cb_ref[:, col:col + FF_CHUNK]
        for i in range(CONV_W):
            off = HALO - (CONV_W - 1) + i
            y = y + cw_ref[i:i + 1, col:col + FF_CHUNK] * ext[off:off + tm, :]
        return y

    acc = jnp.zeros((tm, D_MODEL), F32)
    for c in range(D_FF // FF_CHUNK):
        ya = conv(c * FF_CHUNK)
        yb = conv(D_FF + c * FF_CHUNK)
        g = (_gelu_tanh(ya) * yb).astype(BF16)
        acc = acc + _dot(g, wdn_ref[c * FF_CHUNK:(c + 1) * FF_CHUNK, :])
    o_ref[0] = x + gt2 * _rms(acc, gpost_ref[...])

    @pl.when(ti == nt - 1)
    def _():
        nc_ref[0] = halo_ref[HALO - (CONV_W - 1):HALO, :]


def _ffn(x, ada, conv_state, wts, *, tm):
    b, t, _ = x.shape
    assert tm >= HALO and t % tm == 0
    tok = pl.BlockSpec((1, tm, D_MODEL), lambda i, j: (i, j, 0))
    state = pl.BlockSpec((1, CONV_W - 1, 2 * D_FF), lambda i, j: (i, 0, 0))
    return pl.pallas_call(
        functools.partial(_ffn_kernel, tm=tm),
        grid=(b, t // tm),
        in_specs=[tok,
                  pl.BlockSpec((1, 6, D_MODEL), lambda i, j: (i, 0, 0)),
                  _const_spec((1, D_MODEL)),
                  _const_spec((D_MODEL, 2 * D_FF)),
                  _const_spec((CONV_W, 2 * D_FF)),
                  _const_spec((1, 2 * D_FF)),
                  _const_spec((D_FF, D_MODEL)),
                  _const_spec((1, D_MODEL)),
                  state],
        out_specs=[tok, state],
        out_shape=[jax.ShapeDtypeStruct((b, t, D_MODEL), F32),
                   jax.ShapeDtypeStruct((b, CONV_W - 1, 2 * D_FF), F32)],
        scratch_shapes=[pltpu.VMEM((HALO, 2 * D_FF), F32)],
        compiler_params=_params(1, 1),
        name="conv_ffn",
    )(x, ada, wts["g_pre_ffn"], wts["w_up"], wts["conv_w"], wts["conv_b"], wts["w_down"],
      wts["g_post_ffn"], conv_state)


def _rope_tables(pos):
    inv = ROPE_BASE ** (-jnp.arange(0, ROPE, 2, dtype=F32) / ROPE)
    ang = pos.astype(F32)[:, None] * inv[None, :]
    cos, sin = jnp.cos(ang), jnp.sin(ang)
    t = pos.shape[0]
    cos_t = jnp.concatenate([cos, cos, jnp.ones((t, NOPE), F32), jnp.zeros((t, HEAD_PAD - ROPE - NOPE), F32)], axis=1)
    sin_t = jnp.concatenate([-sin, sin, jnp.zeros((t, HEAD_PAD - ROPE), F32)], axis=1)
    return cos_t, sin_t


def _swap_halves(w):
    return jnp.concatenate([w[..., ROPE // 2:], w[..., :ROPE // 2]], axis=-1)


def _pad_lanes(w, n):
    return jnp.pad(w, [(0, 0)] * (w.ndim - 1) + [(0, n - w.shape[-1])])


def _layer_weights(l, g_pre_mix, g_post_mix, g_pre_ffn, g_post_ffn, w_in, g_q_lat, w_uq, g_kv_lat,
                   w_uk, w_uv, w_proj_a, w_proj_b, w_out, w_up, conv_w, conv_b, w_down):
    splits = np.cumsum([Q_RANK, KV_RANK, ROPE, H_B * D_HB, H_B * D_HB, H_B * D_HB, D_MODEL])
    wi = w_in[l]
    w_ql, w_ckv, w_kr, w_qb, w_kb, w_vb, w_ga, w_gb = jnp.split(wi, splits, axis=1)
    w_in_k1 = jnp.concatenate([w_ql, w_ckv, _pad_lanes(w_kr, LANES), _pad_lanes(_swap_halves(w_kr), LANES),
                               w_qb, w_kb, w_vb], axis=1).astype(BF16)
    wq = w_uq[l].reshape(Q_RANK, H_A, NOPE + ROPE)
    wq_nope, wq_rope = wq[..., :NOPE], wq[..., NOPE:]
    zeros = jnp.zeros((Q_RANK, H_A, HEAD_PAD - ROPE - NOPE), F32)
    q_main = jnp.concatenate([wq_rope, wq_nope, zeros], axis=-1).reshape(Q_RANK, H_A * HEAD_PAD)
    q_rot = _pad_lanes(_swap_halves(wq_rope), HEAD_PAD).reshape(Q_RANK, H_A * HEAD_PAD)
    wk = w_uk[l].reshape(KV_RANK, H_A, NOPE)
    w_uk_pad = jnp.pad(wk, [(0, 0), (0, 0), (ROPE, HEAD_PAD - ROPE - NOPE)])
    w_abs = jnp.transpose(w_uk_pad, (1, 2, 0))
    wv = w_uv[l].reshape(KV_RANK, H_A, DV)
    is_even = (np.arange(H_A) % 2 == 0)[None, :, None]
    w_uv_pad = jnp.where(is_even, jnp.pad(wv, [(0, 0), (0, 0), (0, HEAD_PAD - DV)]),
                         jnp.pad(wv, [(0, 0), (0, 0), (HEAD_PAD - DV, 0)]))
    v_ones = np.zeros((H_A, HEAD_PAD), np.float32)
    v_ones[0::2, DV] = 1.0
    v_ones[1::2, 0] = 1.0
    head_of_col = jnp.arange(H_A * DV) // DV
    w_uv_heads = jnp.where(head_of_col[None, None, :] == jnp.arange(H_A)[:, None, None], w_uv[l][None], 0.0)
    row = lambda g: g[l].reshape(1, -1)
    return {
        "g_pre_mix": row(g_pre_mix), "g_post_mix": row(g_post_mix),
        "g_pre_ffn": row(g_pre_ffn), "g_post_ffn": row(g_post_ffn),
        "g_q_lat": row(g_q_lat), "g_kv_lat": row(g_kv_lat),
        "w_in_k1": w_in_k1,
        "w_gate": jnp.concatenate([w_ga, w_gb], axis=1).astype(BF16),
        "w_uq_ext": jnp.concatenate([q_main, q_rot], axis=1).astype(BF16),
        "w_uk_pad": w_uk_pad.reshape(KV_RANK, H_A * HEAD_PAD).astype(BF16),
        "w_abs": w_abs.astype(BF16),
        "w_uv_pad": w_uv_pad.reshape(KV_RANK, H_A * HEAD_PAD).astype(BF16),
        "v_ones": jnp.asarray(v_ones.reshape(1, H_A * HEAD_PAD)),
        "w_uv_heads": w_uv_heads.astype(BF16),
        "w_proj_a": w_proj_a[l].astype(BF16), "w_proj_b": w_proj_b[l].astype(BF16),
        "w_out": w_out[l].astype(BF16),
        "w_up": w_up[l].astype(BF16), "conv_w": conv_w[l], "conv_b": conv_b[l].reshape(1, -1),
        "w_down": w_down[l].astype(BF16),
    }


def _pick_tile(t, want):
    tm = min(t, want)
    assert t % tm == 0
    return tm


def _layer(x, ada, pos, past, wts):
    b, t, _ = x.shape
    cos_t, sin_t = _rope_tables(pos)
    tm = _pick_tile(t, 256)
    if past is None:
        q, ckv, kr, qb, kb, vb, k_mla, v_mla = _mixer_inputs(x, ada, cos_t, sin_t, wts, tm=tm, expand_kv=True)
        blk_a = _pick_tile(t, 512)
        o_a = _mla_prompt(q, k_mla, v_mla, tq=blk_a, tk=blk_a)
        o_b = _sb_prompt(qb, kb, vb, tq=_pick_tile(t, 256))
        conv_state = jnp.zeros((b, CONV_W - 1, 2 * D_FF), F32)
    else:
        past_ckv, past_kr, past_k, past_v, conv_state = past
        past_len = past_ckv.shape[1]
        q, ckv, kr, qb, kb, vb = _mixer_inputs(x, ada, cos_t, sin_t, wts, tm=tm, expand_kv=False)
        o_a = _mla_sample(q, ckv, kr, past_ckv, past_kr, wts["w_abs"], wts["w_uv_heads"],
                          past_len=past_len, kc=_pick_tile(past_len, 1024))
        o_b = _sb_sample(qb, kb, vb, past_k.reshape(b, past_len, H_B * D_HB),
                         past_v.reshape(b, past_len, H_B * D_HB), tk=_pick_tile(past_len, 256))
    x1 = _merge(x, ada, o_a, o_b, wts, tm=tm)
    y, new_conv = _ffn(x1, ada, conv_state, wts, tm=tm)
    state = (ckv, kr, kb.reshape(b, t, H_B, D_HB), vb.reshape(b, t, H_B, D_HB), new_conv)
    return y, state


def kernel(x_prompt, x_sample, cache_mla_ckv, cache_mla_krope, cache_sb_k, cache_sb_v, state_ffn_conv,
           c_prompt, c_sample, w_ada, b_ada, g_pre_mix, g_post_mix, g_pre_ffn, g_post_ffn,
           w_in, g_q_lat, w_uq, g_kv_lat, w_uk, w_uv, w_proj_a, w_proj_b, w_out,
           w_up, conv_w, conv_b, w_down):
    depth = w_in.shape[0]
    nb_p = x_prompt.shape[0]
    past_len = cache_mla_ckv.shape[2]
    pos_p = jnp.arange(x_prompt.shape[1], dtype=jnp.int32)
    pos_s = past_len + jnp.arange(x_sample.shape[1], dtype=jnp.int32)
    xp, xs = x_prompt, x_sample
    c_all = jnp.concatenate([c_prompt, c_sample], axis=0)
    st_p = [[] for _ in range(5)]
    st_s = [[] for _ in range(5)]
    for l in range(depth):
        wts = _layer_weights(l, g_pre_mix, g_post_mix, g_pre_ffn, g_post_ffn, w_in, g_q_lat, w_uq, g_kv_lat,
                             w_uk, w_uv, w_proj_a, w_proj_b, w_out, w_up, conv_w, conv_b, w_down)
        ada = _ada(c_all, w_ada[l], b_ada[l]).reshape(-1, 6, D_MODEL)
        xp, sp = _layer(xp, ada[:nb_p], pos_p, None, wts)
        past = (cache_mla_ckv[l], cache_mla_krope[l], cache_sb_k[l], cache_sb_v[l], state_ffn_conv[l])
        xs, ss = _layer(xs, ada[nb_p:], pos_s, past, wts)
        for i in range(5):
            st_p[i].append(sp[i])
            st_s[i].append(ss[i])
    p_state = [jnp.stack(a, axis=0) for a in st_p]
    s_state = [jnp.stack(a, axis=0) for a in st_s]
    return (xp, xs, *p_state, *s_state)
```

```python
import functools
import math

import numpy as np
import jax
import jax.numpy as jnp
from jax import lax
from jax.experimental import pallas as pl
from jax.experimental.pallas import tpu as pltpu

D_MODEL = 1024
CHUNK = 64
CHUNK_SHIFT = 6
H_A = 8
NOPE = 64
ROPE = 32
DV = 64
Q_RANK = 384
KV_RANK = 256
ROPE_BASE = 10000.0
H_B = 8
D_HB = 64
D_FF = 2816
CONV_W = 3
EPS = 1e-6
NEG = -1e30
MLA_SCALE = (NOPE + ROPE) ** -0.5
SB_SCALE = D_HB ** -0.5
LOG2E = math.log2(math.e)

LANES = 128
HEAD_PAD = LANES
SB_DEAD = 110.0
VMEM_LIMIT = 56 * 1024 * 1024

F32 = jnp.float32
BF16 = jnp.bfloat16


def _dot(a, b):
    return jnp.dot(a, b, preferred_element_type=F32)


def _dot_t(a, b):
    return lax.dot_general(a, b, (((1,), (1,)), ((), ())), preferred_element_type=F32)


def _rms(x, g):
    return x * lax.rsqrt(jnp.mean(x * x, axis=-1, keepdims=True) + EPS) * g


def _sigmoid(x):
    return 1.0 / (1.0 + jnp.exp(-x))


def _params(n_parallel, n_arbitrary=0):
    return pltpu.CompilerParams(
        dimension_semantics=("parallel",) * n_parallel + ("arbitrary",) * n_arbitrary,
        vmem_limit_bytes=VMEM_LIMIT)


def _const_spec(shape):
    n = len(shape)
    return pl.BlockSpec(shape, lambda *_: (0,) * n)


def _ada_kernel(c_ref, w_ref, b_ref, o_ref):
    c = c_ref[...]
    s = (c * _sigmoid(c)).astype(BF16)
    o_ref[...] = _dot(s, w_ref[...].astype(BF16)) + b_ref[...]


def _ada(c_all, w_ada, b_ada):
    n = c_all.shape[0]
    nchunk = 6
    return pl.pallas_call(
        _ada_kernel,
        grid=(nchunk,),
        in_specs=[pl.BlockSpec((n, D_MODEL), lambda j: (0, 0)),
                  pl.BlockSpec((D_MODEL, D_MODEL), lambda j: (0, j)),
                  pl.BlockSpec((1, D_MODEL), lambda j: (0, j))],
        out_specs=pl.BlockSpec((n, D_MODEL), lambda j: (0, j)),
        out_shape=jax.ShapeDtypeStruct((n, 6 * D_MODEL), F32),
        compiler_params=_params(1),
        name="ada",
    )(c_all, w_ada, b_ada.reshape(1, -1))


_C_QLAT = 0
_C_CKV = _C_QLAT + Q_RANK
_C_KR = _C_CKV + KV_RANK
_C_KRR = _C_KR + LANES
_C_QB = _C_KRR + LANES
_C_KB = _C_QB + H_B * D_HB
_C_VB = _C_KB + H_B * D_HB
_N_K1 = _C_VB + H_B * D_HB


def _premix(x, ada_ref, g):
    sh = ada_ref[0, 0:1, :]
    sc = ada_ref[0, 1:2, :]
    return _rms(x, g) * (1.0 + sc) + sh


def _mixer_kernel(x_ref, ada_ref, g_ref, win_ref, gq_ref, wuq_ref, gkv_ref, wuk_ref, wuv_ref, vone_ref,
                  cos_ref, sin_ref, *out_refs, expand_kv):
    if expand_kv:
        q_ref, ckv_ref, kr_ref, qb_ref, kb_ref, vb_ref, k_ref, v_ref = out_refs
    else:
        q_ref, ckv_ref, kr_ref, qb_ref, kb_ref, vb_ref = out_refs
    h = _premix(x_ref[0], ada_ref, g_ref[...]).astype(BF16)
    p = _dot(h, win_ref[...])
    cos = cos_ref[...]
    sin = sin_ref[...]

    q_lat = _rms(p[:, _C_QLAT:_C_QLAT + Q_RANK], gq_ref[...]).astype(BF16)
    q2 = _dot(q_lat, wuq_ref[...])
    nq = H_A * HEAD_PAD
    for hd in range(H_A):
        lo = hd * HEAD_PAD
        qh = q2[:, lo:lo + HEAD_PAD] * cos + q2[:, nq + lo:nq + lo + HEAD_PAD] * sin
        q_ref[0, :, lo:lo + HEAD_PAD] = (qh * (MLA_SCALE * LOG2E)).astype(BF16)

    c_kv = _rms(p[:, _C_CKV:_C_CKV + KV_RANK], gkv_ref[...])
    ckv_ref[0] = c_kv
    kr = p[:, _C_KR:_C_KR + LANES] * cos + p[:, _C_KRR:_C_KRR + LANES] * sin
    kr_ref[0] = kr[:, :ROPE]

    qb_ref[0] = (p[:, _C_QB:_C_QB + H_B * D_HB] * SB_SCALE).astype(BF16)
    kb_ref[0] = p[:, _C_KB:_C_KB + H_B * D_HB]
    vb_ref[0] = p[:, _C_VB:_C_VB + H_B * D_HB]

    if expand_kv:
        c_bf = c_kv.astype(BF16)
        k2 = _dot(c_bf, wuk_ref[...])
        for hd in range(H_A):
            lo = hd * HEAD_PAD
            k_ref[0, :, lo:lo + HEAD_PAD] = (k2[:, lo:lo + HEAD_PAD] + kr).astype(BF16)
        v_ref[0] = (_dot(c_bf, wuv_ref[...]) + vone_ref[...]).astype(BF16)


def _mixer_inputs(x, ada, cos_t, sin_t, wts, *, tm, expand_kv):
    b, t, _ = x.shape
    nt = t // tm
    tok = lambda n: pl.BlockSpec((1, tm, n), lambda i, j: (i, j, 0))
    in_specs = [tok(D_MODEL),
                pl.BlockSpec((1, 6, D_MODEL), lambda i, j: (i, 0, 0)),
                _const_spec((1, D_MODEL)),
                _const_spec((D_MODEL, _N_K1)),
                _const_spec((1, Q_RANK)),
                _const_spec((Q_RANK, 2 * H_A * HEAD_PAD)),
                _const_spec((1, KV_RANK)),
                _const_spec((KV_RANK, H_A * HEAD_PAD)),
                _const_spec((KV_RANK, H_A * HEAD_PAD)),
                _const_spec((1, H_A * HEAD_PAD)),
                pl.BlockSpec((tm, LANES), lambda i, j: (j, 0)),
                pl.BlockSpec((tm, LANES), lambda i, j: (j, 0))]
    shapes = [((b, t, H_A * HEAD_PAD), BF16), ((b, t, KV_RANK), F32), ((b, t, ROPE), F32),
              ((b, t, H_B * D_HB), BF16), ((b, t, H_B * D_HB), F32), ((b, t, H_B * D_HB), F32)]
    if expand_kv:
        shapes += [((b, t, H_A * HEAD_PAD), BF16), ((b, t, H_A * HEAD_PAD), BF16)]
    return pl.pallas_call(
        functools.partial(_mixer_kernel, expand_kv=expand_kv),
        grid=(b, nt),
        in_specs=in_specs,
        out_specs=[tok(s[-1]) for s, _ in shapes],
        out_shape=[jax.ShapeDtypeStruct(s, d) for s, d in shapes],
        compiler_params=_params(2),
        name="mixer_in_kv" if expand_kv else "mixer_in",
    )(x, ada, wts["g_pre_mix"], wts["w_in_k1"], wts["g_q_lat"], wts["w_uq_ext"], wts["g_kv_lat"],
      wts["w_uk_pad"], wts["w_uv_pad"], wts["v_ones"], cos_t, sin_t)


def _mla_prompt_kernel(q_ref, k_ref, v_ref, o_ref, m_ref, acc_ref, *, tq, tk):
    qi = pl.program_id(2)
    n_diag = tq // tk
    n_full = qi * n_diag
    m_ref[...] = jnp.full(m_ref.shape, NEG, F32)
    acc_ref[...] = jnp.zeros(acc_ref.shape, F32)

    def step(j, masked):
        start = pl.multiple_of(j * tk, tk)
        lanes = [slice(hh * HEAD_PAD, (hh + 1) * HEAD_PAD) for hh in range(2)]
        scores = [_dot_t(q_ref[0, :, ln], k_ref[0, pl.ds(start, tk), ln]) for ln in lanes]
        for hh, ln in enumerate(lanes):
            s = scores[hh]
            if masked:
                qpos = qi * tq + lax.broadcasted_iota(jnp.int32, (tq, tk), 0)
                kpos = j * tk + lax.broadcasted_iota(jnp.int32, (tq, tk), 1)
                s = jnp.where((kpos >> CHUNK_SHIFT) <= (qpos >> CHUNK_SHIFT), s, NEG)
            m_old = m_ref[hh]
            m_new = jnp.maximum(m_old, jnp.max(s, axis=-1, keepdims=True))
            alpha = jnp.exp2(m_old - m_new)
            p = jnp.exp2(s - jnp.concatenate([m_new] * (tk // LANES), axis=1))
            acc_ref[hh] = alpha * acc_ref[hh] + _dot(p.astype(BF16), v_ref[0, pl.ds(start, tk), ln])
            m_ref[hh] = m_new

    def body(j, c):
        step(j, False)
        return c

    lax.fori_loop(0, n_full, body, 0)
    for d in range(n_diag):
        step(n_full + d, True)
    acc_e, acc_o = acc_ref[0], acc_ref[1]
    lane = lax.broadcasted_iota(jnp.int32, (tq, LANES), 1)
    o_ref[0] = jnp.where(lane < DV, acc_e / acc_e[:, DV:DV + 1], acc_o / acc_o[:, 0:1]).astype(BF16)


def _mla_prompt(q, k, v, *, tq, tk):
    b, t, _ = q.shape
    return pl.pallas_call(
        functools.partial(_mla_prompt_kernel, tq=tq, tk=tk),
        grid=(b, H_A // 2, t // tq),
        in_specs=[pl.BlockSpec((1, tq, 2 * HEAD_PAD), lambda i, j, n: (i, n, j)),
                  pl.BlockSpec((1, t, 2 * HEAD_PAD), lambda i, j, n: (i, 0, j)),
                  pl.BlockSpec((1, t, 2 * HEAD_PAD), lambda i, j, n: (i, 0, j))],
        out_specs=pl.BlockSpec((1, tq, 2 * DV), lambda i, j, n: (i, n, j)),
        out_shape=jax.ShapeDtypeStruct((b, t, H_A * DV), BF16),
        scratch_shapes=[pltpu.VMEM((2, tq, LANES), F32), pltpu.VMEM((2, tq, LANES), F32)],
        compiler_params=_params(3),
        name="mla_prompt",
    )(q, k, v)


def _tri(n):
    r = lax.broadcasted_iota(jnp.int32, (n, n), 0)
    c = lax.broadcasted_iota(jnp.int32, (n, n), 1)
    return jnp.where(r > c, 1.0, 0.0).astype(BF16)


def _sb_blocks(qs, blocks, r_prev):
    zs = [_dot_t(qs, k) for k, _, _, _ in blocks]
    sps = []
    for z, (_, _, _, keep) in zip(zs, blocks):
        sp = jnp.maximum(z, 0.0) + jnp.log(1.0 + jnp.exp2(jnp.abs(z) * (-LOG2E)))
        sps.append(sp if keep is None else jnp.where(keep, sp, 0.0))
    laters = [_dot(sp.astype(BF16), tri) for sp, (_, _, tri, _) in zip(sps, blocks)]
    out = None
    for z, sp, later, (k, v, _, keep) in zip(zs, sps, laters, blocks):
        tk = k.shape[0]
        w = z - sp - later
        if r_prev is not None:
            r_b = r_prev
            if r_b.shape[1] == LANES and tk > LANES:
                r_b = jnp.concatenate([r_b] * (tk // LANES), axis=1)
            elif r_b.shape[1] == LANES and tk < LANES:
                r_b = r_b[:, :tk]
            w = w - r_b
        a = jnp.exp2(w * LOG2E)
        if keep is not None:
            a = jnp.where(keep, a, 0.0)
        o = _dot(a.astype(BF16), v)
        out = o if out is None else out + o
        rs = jnp.sum(sp, axis=-1, keepdims=True)
        r_prev = rs if r_prev is None else r_prev + rs
    return out, r_prev


def _head_mask(x, hh):
    lane = lax.broadcasted_iota(jnp.int32, x.shape, 1)
    mine = jnp.logical_and(lane >= hh * D_HB, lane < (hh + 1) * D_HB)
    return jnp.where(mine, x, jnp.zeros_like(x))


def _stack_heads(q):
    return jnp.concatenate([_head_mask(q, 0), _head_mask(q, 1)], axis=0)


def _causal_pair(t, tk):
    row = lax.broadcasted_iota(jnp.int32, (t, tk), 0)
    col = lax.broadcasted_iota(jnp.int32, (t, tk), 1)
    keep = col < row
    return jnp.concatenate([keep, keep], axis=0)


def _sb_tail(qs, load, tri, acc_ref, r_ref, j_start):
    def cond(c):
        j, rmin = c
        return jnp.logical_and(j >= 0, rmin < SB_DEAD)

    def body(c):
        j, _ = c
        k, v = load(j)
        o, r_new = _sb_blocks(qs, [(k, v, tri, None)], r_ref[...])
        acc_ref[...] += o
        r_ref[...] = r_new
        return j - 1, jnp.min(r_new)

    lax.while_loop(cond, body, (j_start, jnp.min(r_ref[...])))


def _unstack_heads(acc, t):
    lane = lax.broadcasted_iota(jnp.int32, (t, LANES), 1)
    return jnp.where(lane < D_HB, acc[:t], acc[t:])


def _sb_prompt_kernel(q_ref, k_ref, v_ref, o_ref, acc_ref, r_ref, *, tq):
    qi = pl.program_id(2)
    qs = _stack_heads(q_ref[0])
    tri = _tri(tq)

    def load(j):
        start = pl.multiple_of(j * tq, tq)
        return (k_ref[0, pl.ds(start, tq), :].astype(BF16),
                v_ref[0, pl.ds(start, tq), :].astype(BF16))

    k_d, v_d = load(qi)
    k_p, v_p = load(jnp.maximum(qi - 1, 0))
    o, r = _sb_blocks(qs, [(k_d, v_d, tri, _causal_pair(tq, tq)), (k_p, v_p, tri, qi > 0)], None)
    acc_ref[...] = o
    r_ref[...] = jnp.broadcast_to(r, (2 * tq, LANES))
    _sb_tail(qs, load, tri, acc_ref, r_ref, qi - 2)
    o_ref[0] = _unstack_heads(acc_ref[...], tq).astype(BF16)


def _sb_prompt(q, k, v, *, tq):
    b, t, _ = q.shape
    return pl.pallas_call(
        functools.partial(_sb_prompt_kernel, tq=tq),
        grid=(b, H_B // 2, t // tq),
        in_specs=[pl.BlockSpec((1, tq, LANES), lambda i, j, n: (i, n, j)),
                  pl.BlockSpec((1, t, LANES), lambda i, j, n: (i, 0, j)),
                  pl.BlockSpec((1, t, LANES), lambda i, j, n: (i, 0, j))],
        out_specs=pl.BlockSpec((1, tq, LANES), lambda i, j, n: (i, n, j)),
        out_shape=jax.ShapeDtypeStruct((b, t, H_B * D_HB), BF16),
        scratch_shapes=[pltpu.VMEM((2 * tq, LANES), F32), pltpu.VMEM((2 * tq, LANES), F32)],
        compiler_params=_params(3),
        name="sb_prompt",
    )(q, k, v)


def _sb_sample_kernel(q_ref, kn_ref, vn_ref, kc_ref, vc_ref, o_ref, acc_ref, r_ref, *, tk):
    tq = q_ref.shape[1]
    nblk = kc_ref.shape[1] // tk
    qs = _stack_heads(q_ref[0])
    tri = _tri(tk)

    def load(j):
        start = pl.multiple_of(j * tk, tk)
        return (kc_ref[0, pl.ds(start, tk), :].astype(BF16),
                vc_ref[0, pl.ds(start, tk), :].astype(BF16))

    k_p, v_p = load(nblk - 1)
    o, r = _sb_blocks(qs, [(kn_ref[0].astype(BF16), vn_ref[0].astype(BF16), _tri(tq), _causal_pair(tq, tq)),
                           (k_p, v_p, tri, None)], None)
    acc_ref[...] = o
    r_ref[...] = jnp.broadcast_to(r, (2 * tq, LANES))
    _sb_tail(qs, load, tri, acc_ref, r_ref, jnp.int32(nblk - 2))
    o_ref[0] = _unstack_heads(acc_ref[...], tq).astype(BF16)


def _sb_sample(q, k_new, v_new, k_cache, v_cache, *, tk):
    b, t, _ = q.shape
    past = k_cache.shape[1]
    new = pl.BlockSpec((1, t, LANES), lambda i, j: (i, 0, j))
    old = pl.BlockSpec((1, past, LANES), lambda i, j: (i, 0, j))
    return pl.pallas_call(
        functools.partial(_sb_sample_kernel, tk=tk),
        grid=(b, H_B // 2),
        in_specs=[new, new, new, old, old],
        out_specs=new,
        out_shape=jax.ShapeDtypeStruct((b, t, H_B * D_HB), BF16),
        scratch_shapes=[pltpu.VMEM((2 * t, LANES), F32), pltpu.VMEM((2 * t, LANES), F32)],
        compiler_params=_params(2),
        name="sb_sample",
    )(q, k_new, v_new, k_cache, v_cache)


def _mla_sample_kernel(q_ref, cn_ref, rn_ref, cc_ref, rc_ref, wabs_ref, wuv_ref, o_ref, *, kc, past_len):
    t = q_ref.shape[1]
    past = cc_ref.shape[1]
    qa, qr = [], []
    for hd in range(H_A):
        qh = q_ref[0, :, hd * HEAD_PAD:(hd + 1) * HEAD_PAD]
        qa.append(_dot(qh, wabs_ref[hd]).astype(BF16))
        qr.append(qh[:, :ROPE])
    qa = jnp.concatenate(qa, axis=0)
    qr = jnp.concatenate(qr, axis=0)
    rows = H_A * t

    pieces = [(cc_ref[0, c * kc:(c + 1) * kc, :], rc_ref[0, c * kc:(c + 1) * kc, :], c * kc)
              for c in range(past // kc)]
    pieces.append((cn_ref[0], rn_ref[0], past_len))
    scores, lat = [], []
    for ckv, kr, k0 in pieces:
        ckv = ckv.astype(BF16)
        s = _dot_t(qa, ckv) + _dot_t(qr, kr.astype(BF16))
        n = ckv.shape[0]
        if (k0 + n - 1) // CHUNK > past_len // CHUNK:
            qpos = past_len + lax.broadcasted_iota(jnp.int32, (rows, n), 0) % t
            kpos = k0 + lax.broadcasted_iota(jnp.int32, (rows, n), 1)
            s = jnp.where((kpos >> CHUNK_SHIFT) <= (qpos >> CHUNK_SHIFT), s, NEG)
        scores.append(s)
        lat.append(ckv)
    m = functools.reduce(jnp.maximum, [jnp.max(s, axis=-1, keepdims=True) for s in scores])
    l = jnp.zeros((rows, 1), F32)
    o_lat = jnp.zeros((rows, KV_RANK), F32)
    for s, ckv in zip(scores, lat):
        p = jnp.exp2(s - m)
        l = l + jnp.sum(p, axis=-1, keepdims=True)
        o_lat = o_lat + _dot(p.astype(BF16), ckv)
    o_lat = (o_lat / l).astype(BF16)
    o = jnp.zeros((t, H_A * DV), F32)
    for hd in range(H_A):
        o = o + _dot(o_lat[hd * t:(hd + 1) * t], wuv_ref[hd])
    o_ref[0] = o.astype(BF16)


def _mla_sample(q, ckv_new, kr_new, ckv_cache, kr_cache, w_abs, w_uv_heads, *, past_len, kc):
    b, t, _ = q.shape
    past = ckv_cache.shape[1]
    row = lambda n, s: pl.BlockSpec((1, n, s), lambda i: (i, 0, 0))
    return pl.pallas_call(
        functools.partial(_mla_sample_kernel, kc=kc, past_len=past_len),
        grid=(b,),
        in_specs=[row(t, H_A * HEAD_PAD), row(t, KV_RANK), row(t, ROPE),
                  row(past, KV_RANK), row(past, ROPE),
                  _const_spec((H_A, HEAD_PAD, KV_RANK)), _const_spec((H_A, KV_RANK, H_A * DV))],
        out_specs=row(t, H_A * DV),
        out_shape=jax.ShapeDtypeStruct((b, t, H_A * DV), BF16),
        compiler_params=_params(1),
        name="mla_sample",
    )(q, ckv_new, kr_new, ckv_cache, kr_cache, w_abs, w_uv_heads)


def _merge_kernel(x_ref, ada_ref, g_ref, wg_ref, oa_ref, ob_ref, wpa_ref, wpb_ref, wo_ref, gpost_ref, o_ref):
    x = x_ref[0]
    h = _premix(x, ada_ref, g_ref[...]).astype(BF16)
    gates = _sigmoid(_dot(h, wg_ref[...]))
    merged = (gates[:, :D_MODEL] * _dot(oa_ref[0], wpa_ref[...])
              + gates[:, D_MODEL:] * _dot(ob_ref[0], wpb_ref[...]))
    mo = _dot(merged.astype(BF16), wo_ref[...])
    gt1 = ada_ref[0, 2:3, :]
    o_ref[0] = x + gt1 * _rms(mo, gpost_ref[...])


def _merge(x, ada, o_a, o_b, wts, *, tm):
    b, t, _ = x.shape
    tok = lambda n: pl.BlockSpec((1, tm, n), lambda i, j: (i, j, 0))
    return pl.pallas_call(
        _merge_kernel,
        grid=(b, t // tm),
        in_specs=[tok(D_MODEL),
                  pl.BlockSpec((1, 6, D_MODEL), lambda i, j: (i, 0, 0)),
                  _const_spec((1, D_MODEL)),
                  _const_spec((D_MODEL, 2 * D_MODEL)),
                  tok(H_A * DV), tok(H_B * D_HB),
                  _const_spec((H_A * DV, D_MODEL)), _const_spec((H_B * D_HB, D_MODEL)),
                  _const_spec((D_MODEL, D_MODEL)),
                  _const_spec((1, D_MODEL))],
        out_specs=tok(D_MODEL),
        out_shape=jax.ShapeDtypeStruct((b, t, D_MODEL), F32),
        compiler_params=_params(2),
        name="merge",
    )(x, ada, wts["g_pre_mix"], wts["w_gate"], o_a, o_b, wts["w_proj_a"], wts["w_proj_b"],
      wts["w_out"], wts["g_post_mix"])


FF_CHUNK = 256
SUBLANES = 8
SUB_GROUPS = 4
SUB_ROWS = SUB_GROUPS * SUBLANES
HALO = 2 * SUBLANES


def _gelu_tanh(a):
    return 0.5 * a * (1.0 + jnp.tanh(math.sqrt(2.0 / math.pi) * (a + 0.044715 * (a * a * a))))


def _ffn_kernel(x_ref, ada_ref, g_ref, wup_ref, cw_ref, cb_ref, wdn_ref, gpost_ref, cs_ref,
                o_ref, nc_ref, halo_ref, perm_ref, *, tm):
    ti = pl.program_id(1)
    nt = pl.num_programs(1)
    nsub = tm // SUB_ROWS

    @pl.when(ti == 0)
    def _():
        halo_ref[...] = jnp.zeros((HALO, 2 * D_FF), F32)
        halo_ref[SUBLANES - 1:SUBLANES, :] = cs_ref[0, 0:1, :]
        halo_ref[HALO - 1:HALO, :] = cs_ref[0, 1:2, :]

    nc = D_MODEL // LANES
    for c in range(nc):
        perm_ref[c] = x_ref[0, :, c * LANES:(c + 1) * LANES]
    x = jnp.concatenate(
        [jnp.concatenate([perm_ref[c, pl.ds(j * SUB_ROWS + i, SUBLANES, stride=SUB_GROUPS), :] for c in range(nc)],
                         axis=1)
         for j in range(nsub) for i in range(SUB_GROUPS)], axis=0)
    sh = ada_ref[0, 3:4, :]
    sc = ada_ref[0, 4:5, :]
    gt2 = ada_ref[0, 5:6, :]
    h2 = (_rms(x, g_ref[...]) * (1.0 + sc) + sh).astype(BF16)
    first = lax.broadcasted_iota(jnp.int32, (SUBLANES, FF_CHUNK), 0) == 0

    def up(col):
        cols = slice(col, col + FF_CHUNK)
        u = _dot(h2, wup_ref[:, cols])
        grp = [u[g * SUBLANES:(g + 1) * SUBLANES] for g in range(tm // SUBLANES)]
        rot2 = [pltpu.roll(halo_ref[0:SUBLANES, cols], 1, axis=0)]
        rot3 = [pltpu.roll(halo_ref[SUBLANES:HALO, cols], 1, axis=0)]
        for j in range(nsub):
            rot2.append(pltpu.roll(grp[j * SUB_GROUPS + SUB_GROUPS - 2], 1, axis=0))
            rot3.append(pltpu.roll(grp[j * SUB_GROUPS + SUB_GROUPS - 1], 1, axis=0))
        halo_ref[:, cols] = u[tm - HALO:]
        u1, u2 = [], []
        for j in range(nsub):
            f1 = jnp.where(first, rot3[j], rot3[j + 1])
            f2 = jnp.where(first, rot2[j], rot2[j + 1])
            g0 = j * SUB_GROUPS
            u1 += [f1] + grp[g0:g0 + SUB_GROUPS - 1]
            u2 += [f2, f1] + grp[g0:g0 + SUB_GROUPS - 2]
        u1 = jnp.concatenate(u1, axis=0)
        u2 = jnp.concatenate(u2, axis=0)
        return (cb_ref[:, cols] + cw_ref[0:1, cols] * u2 + cw_ref[1:2, cols] * u1 + cw_ref[2:3, cols] * u)

    nchunk = D_FF // FF_CHUNK
    acc = jnp.zeros((tm, D_MODEL), F32)
    ya, yb = up(0), up(D_FF)
    for c in range(nchunk):
        if c + 1 < nchunk:
            ya_next, yb_next = up((c + 1) * FF_CHUNK), up(D_FF + (c + 1) * FF_CHUNK)
        g = (_gelu_tanh(ya) * yb).astype(BF16)
        acc = acc + _dot(g, wdn_ref[c * FF_CHUNK:(c + 1) * FF_CHUNK, :])
        if c + 1 < nchunk:
            ya, yb = ya_next, yb_next
    y = x + gt2 * _rms(acc, gpost_ref[...])
    for j in range(nsub):
        for i in range(SUB_GROUPS):
            g0 = (j * SUB_GROUPS + i) * SUBLANES
            for c in range(nc):
                perm_ref[c, pl.ds(j * SUB_ROWS + i, SUBLANES, stride=SUB_GROUPS), :] = (
                    y[g0:g0 + SUBLANES, c * LANES:(c + 1) * LANES])
    for c in range(nc):
        o_ref[0, :, c * LANES:(c + 1) * LANES] = perm_ref[c]

    @pl.when(ti == nt - 1)
    def _():
        nc_ref[0, 0:1, :] = halo_ref[SUBLANES - 1:SUBLANES, :]
        nc_ref[0, 1:2, :] = halo_ref[HALO - 1:HALO, :]


def _ffn(x, ada, conv_state, wts, *, tm):
    b, t, _ = x.shape
    assert tm % SUB_ROWS == 0 and t % tm == 0
    tok = pl.BlockSpec((1, tm, D_MODEL), lambda i, j: (i, j, 0))
    state = pl.BlockSpec((1, CONV_W - 1, 2 * D_FF), lambda i, j: (i, 0, 0))
    return pl.pallas_call(
        functools.partial(_ffn_kernel, tm=tm),
        grid=(b, t // tm),
        in_specs=[tok,
                  pl.BlockSpec((1, 6, D_MODEL), lambda i, j: (i, 0, 0)),
                  _const_spec((1, D_MODEL)),
                  _const_spec((D_MODEL, 2 * D_FF)),
                  _const_spec((CONV_W, 2 * D_FF)),
                  _const_spec((1, 2 * D_FF)),
                  _const_spec((D_FF, D_MODEL)),
                  _const_spec((1, D_MODEL)),
                  state],
        out_specs=[tok, state],
        out_shape=[jax.ShapeDtypeStruct((b, t, D_MODEL), F32),
                   jax.ShapeDtypeStruct((b, CONV_W - 1, 2 * D_FF), F32)],
        scratch_shapes=[pltpu.VMEM((HALO, 2 * D_FF), F32), pltpu.VMEM((D_MODEL // LANES, tm, LANES), F32)],
        compiler_params=_params(1, 1),
        name="conv_ffn",
    )(x, ada, wts["g_pre_ffn"], wts["w_up"], wts["conv_w"], wts["conv_b"], wts["w_down"],
      wts["g_post_ffn"], conv_state)


def _rope_tables(pos):
    inv = ROPE_BASE ** (-jnp.arange(0, ROPE, 2, dtype=F32) / ROPE)
    ang = pos.astype(F32)[:, None] * inv[None, :]
    cos, sin = jnp.cos(ang), jnp.sin(ang)
    t = pos.shape[0]
    cos_t = jnp.concatenate([cos, cos, jnp.ones((t, NOPE), F32), jnp.zeros((t, HEAD_PAD - ROPE - NOPE), F32)], axis=1)
    sin_t = jnp.concatenate([-sin, sin, jnp.zeros((t, HEAD_PAD - ROPE), F32)], axis=1)
    return cos_t, sin_t


def _swap_halves(w):
    return jnp.concatenate([w[..., ROPE // 2:], w[..., :ROPE // 2]], axis=-1)


def _pad_lanes(w, n):
    return jnp.pad(w, [(0, 0)] * (w.ndim - 1) + [(0, n - w.shape[-1])])


def _layer_weights(l, g_pre_mix, g_post_mix, g_pre_ffn, g_post_ffn, w_in, g_q_lat, w_uq, g_kv_lat,
                   w_uk, w_uv, w_proj_a, w_proj_b, w_out, w_up, conv_w, conv_b, w_down):
    splits = np.cumsum([Q_RANK, KV_RANK, ROPE, H_B * D_HB, H_B * D_HB, H_B * D_HB, D_MODEL])
    wi = w_in[l]
    w_ql, w_ckv, w_kr, w_qb, w_kb, w_vb, w_ga, w_gb = jnp.split(wi, splits, axis=1)
    w_in_k1 = jnp.concatenate([w_ql, w_ckv, _pad_lanes(w_kr, LANES), _pad_lanes(_swap_halves(w_kr), LANES),
                               w_qb, w_kb, w_vb], axis=1).astype(BF16)
    wq = w_uq[l].reshape(Q_RANK, H_A, NOPE + ROPE)
    wq_nope, wq_rope = wq[..., :NOPE], wq[..., NOPE:]
    zeros = jnp.zeros((Q_RANK, H_A, HEAD_PAD - ROPE - NOPE), F32)
    q_main = jnp.concatenate([wq_rope, wq_nope, zeros], axis=-1).reshape(Q_RANK, H_A * HEAD_PAD)
    q_rot = _pad_lanes(_swap_halves(wq_rope), HEAD_PAD).reshape(Q_RANK, H_A * HEAD_PAD)
    wk = w_uk[l].reshape(KV_RANK, H_A, NOPE)
    w_uk_pad = jnp.pad(wk, [(0, 0), (0, 0), (ROPE, HEAD_PAD - ROPE - NOPE)])
    w_abs = jnp.transpose(w_uk_pad, (1, 2, 0))
    wv = w_uv[l].reshape(KV_RANK, H_A, DV)
    is_even = (np.arange(H_A) % 2 == 0)[None, :, None]
    w_uv_pad = jnp.where(is_even, jnp.pad(wv, [(0, 0), (0, 0), (0, HEAD_PAD - DV)]),
                         jnp.pad(wv, [(0, 0), (0, 0), (HEAD_PAD - DV, 0)]))
    v_ones = np.zeros((H_A, HEAD_PAD), np.float32)
    v_ones[0::2, DV] = 1.0
    v_ones[1::2, 0] = 1.0
    head_of_col = jnp.arange(H_A * DV) // DV
    w_uv_heads = jnp.where(head_of_col[None, None, :] == jnp.arange(H_A)[:, None, None], w_uv[l][None], 0.0)
    row = lambda g: g[l].reshape(1, -1)
    return {
        "g_pre_mix": row(g_pre_mix), "g_post_mix": row(g_post_mix),
        "g_pre_ffn": row(g_pre_ffn), "g_post_ffn": row(g_post_ffn),
        "g_q_lat": row(g_q_lat), "g_kv_lat": row(g_kv_lat),
        "w_in_k1": w_in_k1,
        "w_gate": jnp.concatenate([w_ga, w_gb], axis=1).astype(BF16),
        "w_uq_ext": jnp.concatenate([q_main, q_rot], axis=1).astype(BF16),
        "w_uk_pad": w_uk_pad.reshape(KV_RANK, H_A * HEAD_PAD).astype(BF16),
        "w_abs": w_abs.astype(BF16),
        "w_uv_pad": w_uv_pad.reshape(KV_RANK, H_A * HEAD_PAD).astype(BF16),
        "v_ones": jnp.asarray(v_ones.reshape(1, H_A * HEAD_PAD)),
        "w_uv_heads": w_uv_heads.astype(BF16),
        "w_proj_a": w_proj_a[l].astype(BF16), "w_proj_b": w_proj_b[l].astype(BF16),
        "w_out": w_out[l].astype(BF16),
        "w_up": w_up[l].astype(BF16), "conv_w": conv_w[l], "conv_b": conv_b[l].reshape(1, -1),
        "w_down": w_down[l].astype(BF16),
    }


def _pick_tile(t, want):
    tm = min(t, want)
    assert t % tm == 0
    return tm


def _layer(x, ada, pos, past, wts):
    b, t, _ = x.shape
    cos_t, sin_t = _rope_tables(pos)
    tm = _pick_tile(t, 256)
    if past is None:
        q, ckv, kr, qb, kb, vb, k_mla, v_mla = _mixer_inputs(x, ada, cos_t, sin_t, wts, tm=tm, expand_kv=True)
        blk_a = _pick_tile(t, 512)
        o_a = _mla_prompt(q, k_mla, v_mla, tq=blk_a, tk=blk_a)
        o_b = _sb_prompt(qb, kb, vb, tq=_pick_tile(t, 256))
        conv_state = jnp.zeros((b, CONV_W - 1, 2 * D_FF), F32)
    else:
        past_ckv, past_kr, past_k, past_v, conv_state = past
        past_len = past_ckv.shape[1]
        q, ckv, kr, qb, kb, vb = _mixer_inputs(x, ada, cos_t, sin_t, wts, tm=tm, expand_kv=False)
        o_a = _mla_sample(q, ckv, kr, past_ckv, past_kr, wts["w_abs"], wts["w_uv_heads"],
                          past_len=past_len, kc=_pick_tile(past_len, 1024))
        o_b = _sb_sample(qb, kb, vb, past_k.reshape(b, past_len, H_B * D_HB),
                         past_v.reshape(b, past_len, H_B * D_HB), tk=_pick_tile(past_len, 256))
    x1 = _merge(x, ada, o_a, o_b, wts, tm=tm)
    y, new_conv = _ffn(x1, ada, conv_state, wts, tm=tm)
    state = (ckv, kr, kb.reshape(b, t, H_B, D_HB), vb.reshape(b, t, H_B, D_HB), new_conv)
    return y, state


def kernel(x_prompt, x_sample, cache_mla_ckv, cache_mla_krope, cache_sb_k, cache_sb_v, state_ffn_conv,
           c_prompt, c_sample, w_ada, b_ada, g_pre_mix, g_post_mix, g_pre_ffn, g_post_ffn,
           w_in, g_q_lat, w_uq, g_kv_lat, w_uk, w_uv, w_proj_a, w_proj_b, w_out,
           w_up, conv_w, conv_b, w_down):
    depth = w_in.shape[0]
    nb_p = x_prompt.shape[0]
    past_len = cache_mla_ckv.shape[2]
    pos_p = jnp.arange(x_prompt.shape[1], dtype=jnp.int32)
    pos_s = past_len + jnp.arange(x_sample.shape[1], dtype=jnp.int32)
    xp, xs = x_prompt, x_sample
    c_all = jnp.concatenate([c_prompt, c_sample], axis=0)
    st_p = [[] for _ in range(5)]
    st_s = [[] for _ in range(5)]
    for l in range(depth):
        wts = _layer_weights(l, g_pre_mix, g_post_mix, g_pre_ffn, g_post_ffn, w_in, g_q_lat, w_uq, g_kv_lat,
                             w_uk, w_uv, w_proj_a, w_proj_b, w_out, w_up, conv_w, conv_b, w_down)
        ada = _ada(c_all, w_ada[l], b_ada[l]).reshape(-1, 6, D_MODEL)
        xp, sp = _layer(xp, ada[:nb_p], pos_p, None, wts)
        past = (cache_mla_ckv[l], cache_mla_krope[l], cache_sb_k[l], cache_sb_v[l], state_ffn_conv[l])
        xs, ss = _layer(xs, ada[nb_p:], pos_s, past, wts)
        for i in range(5):
            st_p[i].append(sp[i])
            st_s[i].append(ss[i])
    p_state = [jnp.stack(a, axis=0) for a in st_p]
    s_state = [jnp.stack(a, axis=0) for a in st_s]
    return (xp, xs, *p_state, *s_state)
```

```python
import functools
import math

import numpy as np
import jax
import jax.numpy as jnp
from jax import lax
from jax.experimental import pallas as pl
from jax.experimental.pallas import tpu as pltpu

D_MODEL = 1024
CHUNK = 64
CHUNK_SHIFT = 6
H_A = 8
NOPE = 64
ROPE = 32
DV = 64
Q_RANK = 384
KV_RANK = 256
ROPE_BASE = 10000.0
H_B = 8
D_HB = 64
D_FF = 2816
CONV_W = 3
EPS = 1e-6
NEG = -1e30
MLA_SCALE = (NOPE + ROPE) ** -0.5
SB_SCALE = D_HB ** -0.5
LOG2E = math.log2(math.e)

LANES = 128
SUBLANES = 8
HEAD_PAD = LANES
SB_DEAD = 110.0
VMEM_LIMIT = 56 * 1024 * 1024

F32 = jnp.float32
BF16 = jnp.bfloat16


def _dot(a, b):
    return jnp.dot(a, b, preferred_element_type=F32)


def _dot_t(a, b):
    return lax.dot_general(a, b, (((1,), (1,)), ((), ())), preferred_element_type=F32)


def _rms(x, g):
    return x * lax.rsqrt(jnp.mean(x * x, axis=-1, keepdims=True) + EPS) * g


def _sigmoid(x):
    return 1.0 / (1.0 + jnp.exp(-x))


def _params(n_parallel, n_arbitrary=0):
    return pltpu.CompilerParams(
        dimension_semantics=("parallel",) * n_parallel + ("arbitrary",) * n_arbitrary,
        vmem_limit_bytes=VMEM_LIMIT)


def _const_spec(shape):
    n = len(shape)
    return pl.BlockSpec(shape, lambda *_: (0,) * n, pipeline_mode=pl.Buffered(1))


def _ada_kernel(c_ref, w_ref, b_ref, o_ref):
    c = c_ref[...]
    s = (c * _sigmoid(c)).astype(BF16)
    o_ref[...] = _dot(s, w_ref[...].astype(BF16)) + b_ref[...]


def _ada(c_all, w_ada, b_ada):
    n = c_all.shape[0]
    nchunk = 6
    return pl.pallas_call(
        _ada_kernel,
        grid=(nchunk,),
        in_specs=[pl.BlockSpec((n, D_MODEL), lambda j: (0, 0)),
                  pl.BlockSpec((D_MODEL, D_MODEL), lambda j: (0, j)),
                  pl.BlockSpec((1, D_MODEL), lambda j: (0, j))],
        out_specs=pl.BlockSpec((n, D_MODEL), lambda j: (0, j)),
        out_shape=jax.ShapeDtypeStruct((n, 6 * D_MODEL), F32),
        compiler_params=_params(1),
        name="ada",
    )(c_all, w_ada, b_ada.reshape(1, -1))


_C_QLAT = 0
_C_CKV = _C_QLAT + Q_RANK
_C_KR = _C_CKV + KV_RANK
_C_KRR = _C_KR + LANES
_C_QB = _C_KRR + LANES
_C_KB = _C_QB + H_B * D_HB
_C_VB = _C_KB + H_B * D_HB
_N_K1 = _C_VB + H_B * D_HB


def _premix(x, ada_ref, g):
    sh = ada_ref[0, 0:1, :]
    sc = ada_ref[0, 1:2, :]
    return _rms(x, g) * (1.0 + sc) + sh


def _row_parts(tm):
    n = 2 if tm % (2 * 128) == 0 else 1
    return [slice(i * (tm // n), (i + 1) * (tm // n)) for i in range(n)]


def _mixer_kernel(x_ref, ada_ref, g_ref, win_ref, gq_ref, wuq_ref, gkv_ref, wuk_ref, wuv_ref, vone_ref,
                  cos_ref, sin_ref, *out_refs, expand_kv):
    if expand_kv:
        q_ref, ckv_ref, kr_ref, qb_ref, kb_ref, vb_ref, k_ref, v_ref = out_refs
    else:
        q_ref, ckv_ref, kr_ref, qb_ref, kb_ref, vb_ref = out_refs
    parts = _row_parts(x_ref.shape[1])
    nq = H_A * HEAD_PAD
    ps = [_dot(_premix(x_ref[0, rows, :], ada_ref, g_ref[...]).astype(BF16), win_ref[...]) for rows in parts]
    q_lats = [_rms(p[:, _C_QLAT:_C_QLAT + Q_RANK], gq_ref[...]).astype(BF16) for p in ps]
    c_kvs = [_rms(p[:, _C_CKV:_C_CKV + KV_RANK], gkv_ref[...]) for p in ps]
    q2s = [_dot(q_lat, wuq_ref[...]) for q_lat in q_lats]
    if expand_kv:
        c_bfs = [c_kv.astype(BF16) for c_kv in c_kvs]
        k2s = [_dot(c_bf, wuk_ref[...]) for c_bf in c_bfs]
        v2s = [_dot(c_bf, wuv_ref[...]) for c_bf in c_bfs]
    for i, rows in enumerate(parts):
        p, q2 = ps[i], q2s[i]
        cos = cos_ref[rows, :]
        sin = sin_ref[rows, :]
        for hd in range(H_A):
            lo = hd * HEAD_PAD
            qh = q2[:, lo:lo + HEAD_PAD] * cos + q2[:, nq + lo:nq + lo + HEAD_PAD] * sin
            q_ref[0, rows, lo:lo + HEAD_PAD] = (qh * (MLA_SCALE * LOG2E)).astype(BF16)
        ckv_ref[0, rows, :] = c_kvs[i]
        kr = p[:, _C_KR:_C_KR + LANES] * cos + p[:, _C_KRR:_C_KRR + LANES] * sin
        kr_ref[0, rows, :] = kr[:, :ROPE]
        qb_ref[0, rows, :] = (p[:, _C_QB:_C_QB + H_B * D_HB] * SB_SCALE).astype(BF16)
        kb_ref[0, rows, :] = p[:, _C_KB:_C_KB + H_B * D_HB]
        vb_ref[0, rows, :] = p[:, _C_VB:_C_VB + H_B * D_HB]
        if expand_kv:
            for hd in range(H_A):
                lo = hd * HEAD_PAD
                k_ref[0, rows, lo:lo + HEAD_PAD] = (k2s[i][:, lo:lo + HEAD_PAD] + kr).astype(BF16)
            v_ref[0, rows, :] = (v2s[i] + vone_ref[...]).astype(BF16)


def _mixer_inputs(x, ada, cos_t, sin_t, wts, *, tm, expand_kv):
    b, t, _ = x.shape
    nt = t // tm
    tok = lambda n: pl.BlockSpec((1, tm, n), lambda i, j: (i, j, 0))
    in_specs = [tok(D_MODEL),
                pl.BlockSpec((1, 6, D_MODEL), lambda i, j: (i, 0, 0)),
                _const_spec((1, D_MODEL)),
                _const_spec((D_MODEL, _N_K1)),
                _const_spec((1, Q_RANK)),
                _const_spec((Q_RANK, 2 * H_A * HEAD_PAD)),
                _const_spec((1, KV_RANK)),
                _const_spec((KV_RANK, H_A * HEAD_PAD)),
                _const_spec((KV_RANK, H_A * HEAD_PAD)),
                _const_spec((1, H_A * HEAD_PAD)),
                pl.BlockSpec((tm, LANES), lambda i, j: (j, 0)),
                pl.BlockSpec((tm, LANES), lambda i, j: (j, 0))]
    shapes = [((b, t, H_A * HEAD_PAD), BF16), ((b, t, KV_RANK), F32), ((b, t, ROPE), F32),
              ((b, t, H_B * D_HB), BF16), ((b, t, H_B * D_HB), F32), ((b, t, H_B * D_HB), F32)]
    if expand_kv:
        shapes += [((b, t, H_A * HEAD_PAD), BF16), ((b, t, H_A * HEAD_PAD), BF16)]
    return pl.pallas_call(
        functools.partial(_mixer_kernel, expand_kv=expand_kv),
        grid=(b, nt),
        in_specs=in_specs,
        out_specs=[tok(s[-1]) for s, _ in shapes],
        out_shape=[jax.ShapeDtypeStruct(s, d) for s, d in shapes],
        compiler_params=_params(2),
        name="mixer_in_kv" if expand_kv else "mixer_in",
    )(x, ada, wts["g_pre_mix"], wts["w_in_k1"], wts["g_q_lat"], wts["w_uq_ext"], wts["g_kv_lat"],
      wts["w_uk_pad"], wts["w_uv_pad"], wts["v_ones"], cos_t, sin_t)


def _mla_prompt_kernel(q_ref, k_ref, v_ref, o_ref, m_ref, acc_ref, *, tq, tk):
    qi = pl.program_id(2)
    n_diag = tq // tk
    n_full = qi * n_diag
    m_ref[...] = jnp.full(m_ref.shape, NEG, F32)
    acc_ref[...] = jnp.zeros(acc_ref.shape, F32)

    def step(j, masked):
        start = pl.multiple_of(j * tk, tk)
        lanes = [slice(hh * HEAD_PAD, (hh + 1) * HEAD_PAD) for hh in range(2)]
        scores = [_dot_t(q_ref[0, :, ln], k_ref[0, pl.ds(start, tk), ln]) for ln in lanes]
        for hh, ln in enumerate(lanes):
            s = scores[hh]
            if masked:
                qpos = qi * tq + lax.broadcasted_iota(jnp.int32, (tq, tk), 0)
                kpos = j * tk + lax.broadcasted_iota(jnp.int32, (tq, tk), 1)
                s = jnp.where((kpos >> CHUNK_SHIFT) <= (qpos >> CHUNK_SHIFT), s, NEG)
            m_old = m_ref[hh]
            m_new = jnp.maximum(m_old, jnp.max(s, axis=-1, keepdims=True))
            alpha = jnp.exp2(m_old - m_new)
            p = jnp.exp2(s - jnp.concatenate([m_new] * (tk // LANES), axis=1))
            acc_ref[hh] = alpha * acc_ref[hh] + _dot(p.astype(BF16), v_ref[0, pl.ds(start, tk), ln])
            m_ref[hh] = m_new

    def body(j, c):
        step(j, False)
        return c

    lax.fori_loop(0, n_full, body, 0)
    for d in range(n_diag):
        step(n_full + d, True)
    acc_e, acc_o = acc_ref[0], acc_ref[1]
    lane = lax.broadcasted_iota(jnp.int32, (tq, LANES), 1)
    o_ref[0] = jnp.where(lane < DV, acc_e / acc_e[:, DV:DV + 1], acc_o / acc_o[:, 0:1]).astype(BF16)


def _mla_prompt(q, k, v, *, tq, tk):
    b, t, _ = q.shape
    return pl.pallas_call(
        functools.partial(_mla_prompt_kernel, tq=tq, tk=tk),
        grid=(b, H_A // 2, t // tq),
        in_specs=[pl.BlockSpec((1, tq, 2 * HEAD_PAD), lambda i, j, n: (i, n, j)),
                  pl.BlockSpec((1, t, 2 * HEAD_PAD), lambda i, j, n: (i, 0, j)),
                  pl.BlockSpec((1, t, 2 * HEAD_PAD), lambda i, j, n: (i, 0, j))],
        out_specs=pl.BlockSpec((1, tq, 2 * DV), lambda i, j, n: (i, n, j)),
        out_shape=jax.ShapeDtypeStruct((b, t, H_A * DV), BF16),
        scratch_shapes=[pltpu.VMEM((2, tq, LANES), F32), pltpu.VMEM((2, tq, LANES), F32)],
        compiler_params=_params(3),
        name="mla_prompt",
    )(q, k, v)


def _tri(n):
    r = lax.broadcasted_iota(jnp.int32, (n, n), 0)
    c = lax.broadcasted_iota(jnp.int32, (n, n), 1)
    return jnp.where(r > c, 1.0, 0.0).astype(BF16)


def _sb_blocks(qs, blocks, r_prev):
    zs = [_dot_t(qs, k) for k, _, _, _ in blocks]
    sps = []
    for z, (_, _, _, keep) in zip(zs, blocks):
        sp = jnp.maximum(z, 0.0) + jnp.log(1.0 + jnp.exp2(jnp.abs(z) * (-LOG2E)))
        sps.append(sp if keep is None else jnp.where(keep, sp, 0.0))
    laters = [_dot(sp.astype(BF16), tri) for sp, (_, _, tri, _) in zip(sps, blocks)]
    out = None
    for z, sp, later, (k, v, _, keep) in zip(zs, sps, laters, blocks):
        tk = k.shape[0]
        w = z - sp - later
        if r_prev is not None:
            r_b = r_prev
            if r_b.shape[1] == LANES and tk > LANES:
                r_b = jnp.concatenate([r_b] * (tk // LANES), axis=1)
            elif r_b.shape[1] == LANES and tk < LANES:
                r_b = r_b[:, :tk]
            w = w - r_b
        a = jnp.exp2(w * LOG2E)
        if keep is not None:
            a = jnp.where(keep, a, 0.0)
        o = _dot(a.astype(BF16), v)
        out = o if out is None else out + o
        rs = jnp.sum(sp, axis=-1, keepdims=True)
        r_prev = rs if r_prev is None else r_prev + rs
    return out, r_prev


def _head_mask(x, hh):
    lane = lax.broadcasted_iota(jnp.int32, x.shape, 1)
    mine = jnp.logical_and(lane >= hh * D_HB, lane < (hh + 1) * D_HB)
    return jnp.where(mine, x, jnp.zeros_like(x))


def _stack_heads(q):
    return jnp.concatenate([_head_mask(q, 0), _head_mask(q, 1)], axis=0)


def _causal_pair(t, tk):
    row = lax.broadcasted_iota(jnp.int32, (t, tk), 0)
    col = lax.broadcasted_iota(jnp.int32, (t, tk), 1)
    keep = col < row
    return jnp.concatenate([keep, keep], axis=0)


def _sb_tail(qs, load, tri, acc_ref, r_ref, j_start):
    def cond(c):
        j, rmin = c
        return jnp.logical_and(j >= 0, rmin < SB_DEAD)

    def body(c):
        j, _ = c
        k, v = load(j)
        o, r_new = _sb_blocks(qs, [(k, v, tri, None)], r_ref[...])
        acc_ref[...] += o
        r_ref[...] = r_new
        return j - 1, jnp.min(r_new)

    lax.while_loop(cond, body, (j_start, jnp.min(r_ref[...])))


def _unstack_heads(acc, t):
    lane = lax.broadcasted_iota(jnp.int32, (t, LANES), 1)
    return jnp.where(lane < D_HB, acc[:t], acc[t:])


def _sb_prompt_kernel(q_ref, k_ref, v_ref, o_ref, acc_ref, r_ref, *, tq, nsub):
    qi = pl.program_id(2)
    tri = _tri(tq)
    causal = _causal_pair(tq, tq)

    def load(j):
        start = pl.multiple_of(j * tq, tq)
        return (k_ref[0, pl.ds(start, tq), :].astype(BF16),
                v_ref[0, pl.ds(start, tq), :].astype(BF16))

    qss = []
    for sub in range(nsub):
        blk = qi * nsub + sub
        qs = _stack_heads(q_ref[0, sub * tq:(sub + 1) * tq, :])
        k_d, v_d = load(blk)
        k_p, v_p = load(jnp.maximum(blk - 1, 0))
        o, r = _sb_blocks(qs, [(k_d, v_d, tri, causal), (k_p, v_p, tri, blk > 0)], None)
        acc_ref[sub] = o
        r_ref[sub] = jnp.broadcast_to(r, (2 * tq, LANES))
        qss.append(qs)
    for sub in range(nsub):
        _sb_tail(qss[sub], load, tri, acc_ref.at[sub], r_ref.at[sub], qi * nsub + sub - 2)
        o_ref[0, sub * tq:(sub + 1) * tq, :] = _unstack_heads(acc_ref[sub], tq).astype(BF16)


def _sb_prompt(q, k, v, *, tq, nsub):
    b, t, _ = q.shape
    rows = tq * nsub
    return pl.pallas_call(
        functools.partial(_sb_prompt_kernel, tq=tq, nsub=nsub),
        grid=(b, H_B // 2, t // rows),
        in_specs=[pl.BlockSpec((1, rows, LANES), lambda i, j, n: (i, n, j)),
                  pl.BlockSpec((1, t, LANES), lambda i, j, n: (i, 0, j)),
                  pl.BlockSpec((1, t, LANES), lambda i, j, n: (i, 0, j))],
        out_specs=pl.BlockSpec((1, rows, LANES), lambda i, j, n: (i, n, j)),
        out_shape=jax.ShapeDtypeStruct((b, t, H_B * D_HB), BF16),
        scratch_shapes=[pltpu.VMEM((nsub, 2 * tq, LANES), F32), pltpu.VMEM((nsub, 2 * tq, LANES), F32)],
        compiler_params=_params(3),
        name="sb_prompt",
    )(q, k, v)


def _sb_sample_kernel(q_ref, kn_ref, vn_ref, kc_ref, vc_ref, o_ref, left_ref, acc_ref, r_ref, *, tk):
    tq = q_ref.shape[1]
    nblk = kc_ref.shape[1] // tk
    qs = _stack_heads(q_ref[0])
    tri = _tri(tk)

    def load(j):
        start = pl.multiple_of(j * tk, tk)
        return (kc_ref[0, pl.ds(start, tk), :].astype(BF16),
                vc_ref[0, pl.ds(start, tk), :].astype(BF16))

    k_p, v_p = load(nblk - 1)
    o, r = _sb_blocks(qs, [(kn_ref[0].astype(BF16), vn_ref[0].astype(BF16), _tri(tq), _causal_pair(tq, tq)),
                           (k_p, v_p, tri, None)], None)
    acc_ref[...] = o
    r_ref[...] = jnp.broadcast_to(r, (2 * tq, LANES))
    _sb_tail(qs, load, tri, acc_ref, r_ref, jnp.int32(nblk - 2))
    o_ref[0] = _unstack_heads(acc_ref[...], tq).astype(BF16)
    left_ref[...] = jnp.full(left_ref.shape, jnp.min(r_ref[...]), F32)


def _sb_sample_call(q, k_new, v_new, k_cache, v_cache, *, tk):
    b, t, _ = q.shape
    past = k_cache.shape[1]
    new = pl.BlockSpec((1, t, LANES), lambda i, j: (i, 0, j))
    old = pl.BlockSpec((1, past, LANES), lambda i, j: (i, 0, j))
    return pl.pallas_call(
        functools.partial(_sb_sample_kernel, tk=tk),
        grid=(b, H_B // 2),
        in_specs=[new, new, new, old, old],
        out_specs=[new, pl.BlockSpec((1, 1, SUBLANES, LANES), lambda i, j: (i, j, 0, 0))],
        out_shape=[jax.ShapeDtypeStruct((b, t, H_B * D_HB), BF16),
                   jax.ShapeDtypeStruct((b, H_B // 2, SUBLANES, LANES), F32)],
        scratch_shapes=[pltpu.VMEM((2 * t, LANES), F32), pltpu.VMEM((2 * t, LANES), F32)],
        compiler_params=_params(2),
        name="sb_sample",
    )(q, k_new, v_new, k_cache, v_cache)


def _sb_sample(q, k_new, v_new, k_cache, v_cache, *, tk, recent):
    b, past = k_cache.shape[:2]
    flat = lambda c: c.reshape(b, c.shape[1], H_B * D_HB)
    recent = min(recent, past)
    o, left = _sb_sample_call(q, k_new, v_new, flat(k_cache[:, past - recent:]), flat(v_cache[:, past - recent:]),
                              tk=tk)
    if recent == past:
        return o
    return lax.cond(jnp.min(left) < SB_DEAD,
                    lambda: _sb_sample_call(q, k_new, v_new, flat(k_cache), flat(v_cache), tk=tk)[0],
                    lambda: o)


def _mla_sample_kernel(q_ref, cn_ref, rn_ref, cc_ref, rc_ref, wabs_ref, wuv_ref, o_ref, *, kc, past_len):
    t = q_ref.shape[1]
    past = cc_ref.shape[1]
    qa, qr = [], []
    for hd in range(H_A):
        qh = q_ref[0, :, hd * HEAD_PAD:(hd + 1) * HEAD_PAD]
        qa.append(_dot(qh, wabs_ref[hd]).astype(BF16))
        qr.append(qh[:, :ROPE])
    qa = jnp.concatenate(qa, axis=0)
    qr = jnp.concatenate(qr, axis=0)
    rows = H_A * t

    pieces = [(cc_ref[0, c * kc:(c + 1) * kc, :], rc_ref[0, c * kc:(c + 1) * kc, :], c * kc)
              for c in range(past // kc)]
    pieces.append((cn_ref[0], rn_ref[0], past_len))
    scores, lat = [], []
    for ckv, kr, k0 in pieces:
        ckv = ckv.astype(BF16)
        s = _dot_t(qa, ckv) + _dot_t(qr, kr.astype(BF16))
        n = ckv.shape[0]
        if (k0 + n - 1) // CHUNK > past_len // CHUNK:
            qpos = past_len + lax.broadcasted_iota(jnp.int32, (rows, n), 0) % t
            kpos = k0 + lax.broadcasted_iota(jnp.int32, (rows, n), 1)
            s = jnp.where((kpos >> CHUNK_SHIFT) <= (qpos >> CHUNK_SHIFT), s, NEG)
        scores.append(s)
        lat.append(ckv)
    m = functools.reduce(jnp.maximum, [jnp.max(s, axis=-1, keepdims=True) for s in scores])
    l = jnp.zeros((rows, 1), F32)
    o_lat = jnp.zeros((rows, KV_RANK), F32)
    for s, ckv in zip(scores, lat):
        p = jnp.exp2(s - m)
        l = l + jnp.sum(p, axis=-1, keepdims=True)
        o_lat = o_lat + _dot(p.astype(BF16), ckv)
    o_lat = (o_lat / l).astype(BF16)
    o = jnp.zeros((t, H_A * DV), F32)
    for hd in range(H_A):
        o = o + _dot(o_lat[hd * t:(hd + 1) * t], wuv_ref[hd])
    o_ref[0] = o.astype(BF16)


def _mla_sample(q, ckv_new, kr_new, ckv_cache, kr_cache, w_abs, w_uv_heads, *, past_len, kc):
    b, t, _ = q.shape
    past = ckv_cache.shape[1]
    row = lambda n, s: pl.BlockSpec((1, n, s), lambda i: (i, 0, 0))
    return pl.pallas_call(
        functools.partial(_mla_sample_kernel, kc=kc, past_len=past_len),
        grid=(b,),
        in_specs=[row(t, H_A * HEAD_PAD), row(t, KV_RANK), row(t, ROPE),
                  row(past, KV_RANK), row(past, ROPE),
                  _const_spec((H_A, HEAD_PAD, KV_RANK)), _const_spec((H_A, KV_RANK, H_A * DV))],
        out_specs=row(t, H_A * DV),
        out_shape=jax.ShapeDtypeStruct((b, t, H_A * DV), BF16),
        compiler_params=_params(1),
        name="mla_sample",
    )(q, ckv_new, kr_new, ckv_cache, kr_cache, w_abs, w_uv_heads)


def _merge_kernel(x_ref, ada_ref, g_ref, wg_ref, oa_ref, ob_ref, wpa_ref, wpb_ref, wo_ref, gpost_ref, o_ref):
    parts = _row_parts(x_ref.shape[1])
    xs = [x_ref[0, rows, :] for rows in parts]
    gate_in = [_dot(_premix(x, ada_ref, g_ref[...]).astype(BF16), wg_ref[...]) for x in xs]
    pas = [_dot(oa_ref[0, rows, :], wpa_ref[...]) for rows in parts]
    pbs = [_dot(ob_ref[0, rows, :], wpb_ref[...]) for rows in parts]
    merged = [(_sigmoid(gi[:, :D_MODEL]) * pa + _sigmoid(gi[:, D_MODEL:]) * pb).astype(BF16)
              for gi, pa, pb in zip(gate_in, pas, pbs)]
    mos = [_dot(m, wo_ref[...]) for m in merged]
    gt1 = ada_ref[0, 2:3, :]
    for rows, x, mo in zip(parts, xs, mos):
        o_ref[0, rows, :] = x + gt1 * _rms(mo, gpost_ref[...])


def _merge(x, ada, o_a, o_b, wts, *, tm):
    b, t, _ = x.shape
    tok = lambda n: pl.BlockSpec((1, tm, n), lambda i, j: (i, j, 0))
    return pl.pallas_call(
        _merge_kernel,
        grid=(b, t // tm),
        in_specs=[tok(D_MODEL),
                  pl.BlockSpec((1, 6, D_MODEL), lambda i, j: (i, 0, 0)),
                  _const_spec((1, D_MODEL)),
                  _const_spec((D_MODEL, 2 * D_MODEL)),
                  tok(H_A * DV), tok(H_B * D_HB),
                  _const_spec((H_A * DV, D_MODEL)), _const_spec((H_B * D_HB, D_MODEL)),
                  _const_spec((D_MODEL, D_MODEL)),
                  _const_spec((1, D_MODEL))],
        out_specs=tok(D_MODEL),
        out_shape=jax.ShapeDtypeStruct((b, t, D_MODEL), F32),
        compiler_params=_params(2),
        name="merge",
    )(x, ada, wts["g_pre_mix"], wts["w_gate"], o_a, o_b, wts["w_proj_a"], wts["w_proj_b"],
      wts["w_out"], wts["g_post_mix"])


FF_CHUNK = 256
SUB_GROUPS = 4
SUB_ROWS = SUB_GROUPS * SUBLANES
HALO = 2 * SUBLANES


def _gelu_tanh(a):
    return 0.5 * a * (1.0 + jnp.tanh(math.sqrt(2.0 / math.pi) * (a + 0.044715 * (a * a * a))))


def _ffn_kernel(x_ref, ada_ref, g_ref, wup_ref, cw_ref, cb_ref, wdn_ref, gpost_ref, cs_ref,
                o_ref, nc_ref, halo_ref, perm_ref, *, tm):
    ti = pl.program_id(1)
    nt = pl.num_programs(1)
    nsub = tm // SUB_ROWS

    @pl.when(ti == 0)
    def _():
        halo_ref[...] = jnp.zeros((HALO, 2 * D_FF), F32)
        halo_ref[SUBLANES - 1:SUBLANES, :] = cs_ref[0, 0:1, :]
        halo_ref[HALO - 1:HALO, :] = cs_ref[0, 1:2, :]

    nc = D_MODEL // LANES
    for c in range(nc):
        perm_ref[c] = x_ref[0, :, c * LANES:(c + 1) * LANES]
    x = jnp.concatenate(
        [jnp.concatenate([perm_ref[c, pl.ds(j * SUB_ROWS + i, SUBLANES, stride=SUB_GROUPS), :] for c in range(nc)],
                         axis=1)
         for j in range(nsub) for i in range(SUB_GROUPS)], axis=0)
    sh = ada_ref[0, 3:4, :]
    sc = ada_ref[0, 4:5, :]
    gt2 = ada_ref[0, 5:6, :]
    h2 = (_rms(x, g_ref[...]) * (1.0 + sc) + sh).astype(BF16)
    first = lax.broadcasted_iota(jnp.int32, (SUBLANES, FF_CHUNK), 0) == 0

    def up(col):
        cols = slice(col, col + FF_CHUNK)
        u = _dot(h2, wup_ref[:, cols])
        grp = [u[g * SUBLANES:(g + 1) * SUBLANES] for g in range(tm // SUBLANES)]
        rot2 = [pltpu.roll(halo_ref[0:SUBLANES, cols], 1, axis=0)]
        rot3 = [pltpu.roll(halo_ref[SUBLANES:HALO, cols], 1, axis=0)]
        for j in range(nsub):
            rot2.append(pltpu.roll(grp[j * SUB_GROUPS + SUB_GROUPS - 2], 1, axis=0))
            rot3.append(pltpu.roll(grp[j * SUB_GROUPS + SUB_GROUPS - 1], 1, axis=0))
        halo_ref[:, cols] = u[tm - HALO:]
        u1, u2 = [], []
        for j in range(nsub):
            f1 = jnp.where(first, rot3[j], rot3[j + 1])
            f2 = jnp.where(first, rot2[j], rot2[j + 1])
            g0 = j * SUB_GROUPS
            u1 += [f1] + grp[g0:g0 + SUB_GROUPS - 1]
            u2 += [f2, f1] + grp[g0:g0 + SUB_GROUPS - 2]
        u1 = jnp.concatenate(u1, axis=0)
        u2 = jnp.concatenate(u2, axis=0)
        return (cb_ref[:, cols] + cw_ref[0:1, cols] * u2 + cw_ref[1:2, cols] * u1 + cw_ref[2:3, cols] * u)

    nchunk = D_FF // FF_CHUNK
    acc = jnp.zeros((tm, D_MODEL), F32)
    ya, yb = up(0), up(D_FF)
    for c in range(nchunk):
        if c + 1 < nchunk:
            ya_next, yb_next = up((c + 1) * FF_CHUNK), up(D_FF + (c + 1) * FF_CHUNK)
        g = (_gelu_tanh(ya) * yb).astype(BF16)
        acc = acc + _dot(g, wdn_ref[c * FF_CHUNK:(c + 1) * FF_CHUNK, :])
        if c + 1 < nchunk:
            ya, yb = ya_next, yb_next
    y = x + gt2 * _rms(acc, gpost_ref[...])
    for j in range(nsub):
        for i in range(SUB_GROUPS):
            g0 = (j * SUB_GROUPS + i) * SUBLANES
            for c in range(nc):
                perm_ref[c, pl.ds(j * SUB_ROWS + i, SUBLANES, stride=SUB_GROUPS), :] = (
                    y[g0:g0 + SUBLANES, c * LANES:(c + 1) * LANES])
    for c in range(nc):
        o_ref[0, :, c * LANES:(c + 1) * LANES] = perm_ref[c]

    @pl.when(ti == nt - 1)
    def _():
        nc_ref[0, 0:1, :] = halo_ref[SUBLANES - 1:SUBLANES, :]
        nc_ref[0, 1:2, :] = halo_ref[HALO - 1:HALO, :]


def _ffn(x, ada, conv_state, wts, *, tm):
    b, t, _ = x.shape
    assert tm % SUB_ROWS == 0 and t % tm == 0
    tok = pl.BlockSpec((1, tm, D_MODEL), lambda i, j: (i, j, 0))
    state = pl.BlockSpec((1, CONV_W - 1, 2 * D_FF), lambda i, j: (i, 0, 0))
    return pl.pallas_call(
        functools.partial(_ffn_kernel, tm=tm),
        grid=(b, t // tm),
        in_specs=[tok,
                  pl.BlockSpec((1, 6, D_MODEL), lambda i, j: (i, 0, 0)),
                  _const_spec((1, D_MODEL)),
                  _const_spec((D_MODEL, 2 * D_FF)),
                  _const_spec((CONV_W, 2 * D_FF)),
                  _const_spec((1, 2 * D_FF)),
                  _const_spec((D_FF, D_MODEL)),
                  _const_spec((1, D_MODEL)),
                  state],
        out_specs=[tok, state],
        out_shape=[jax.ShapeDtypeStruct((b, t, D_MODEL), F32),
                   jax.ShapeDtypeStruct((b, CONV_W - 1, 2 * D_FF), F32)],
        scratch_shapes=[pltpu.VMEM((HALO, 2 * D_FF), F32), pltpu.VMEM((D_MODEL // LANES, tm, LANES), F32)],
        compiler_params=_params(1, 1),
        name="conv_ffn",
    )(x, ada, wts["g_pre_ffn"], wts["w_up"], wts["conv_w"], wts["conv_b"], wts["w_down"],
      wts["g_post_ffn"], conv_state)


def _rope_tables(pos):
    inv = ROPE_BASE ** (-jnp.arange(0, ROPE, 2, dtype=F32) / ROPE)
    ang = pos.astype(F32)[:, None] * inv[None, :]
    cos, sin = jnp.cos(ang), jnp.sin(ang)
    t = pos.shape[0]
    cos_t = jnp.concatenate([cos, cos, jnp.ones((t, NOPE), F32), jnp.zeros((t, HEAD_PAD - ROPE - NOPE), F32)], axis=1)
    sin_t = jnp.concatenate([-sin, sin, jnp.zeros((t, HEAD_PAD - ROPE), F32)], axis=1)
    return cos_t, sin_t


def _swap_halves(w):
    return jnp.concatenate([w[..., ROPE // 2:], w[..., :ROPE // 2]], axis=-1)


def _pad_lanes(w, n):
    return jnp.pad(w, [(0, 0)] * (w.ndim - 1) + [(0, n - w.shape[-1])])


def _layer_weights(l, g_pre_mix, g_post_mix, g_pre_ffn, g_post_ffn, w_in, g_q_lat, w_uq, g_kv_lat,
                   w_uk, w_uv, w_proj_a, w_proj_b, w_out, w_up, conv_w, conv_b, w_down):
    splits = np.cumsum([Q_RANK, KV_RANK, ROPE, H_B * D_HB, H_B * D_HB, H_B * D_HB, D_MODEL])
    wi = w_in[l]
    w_ql, w_ckv, w_kr, w_qb, w_kb, w_vb, w_ga, w_gb = jnp.split(wi, splits, axis=1)
    w_in_k1 = jnp.concatenate([w_ql, w_ckv, _pad_lanes(w_kr, LANES), _pad_lanes(_swap_halves(w_kr), LANES),
                               w_qb, w_kb, w_vb], axis=1).astype(BF16)
    wq = w_uq[l].reshape(Q_RANK, H_A, NOPE + ROPE)
    wq_nope, wq_rope = wq[..., :NOPE], wq[..., NOPE:]
    zeros = jnp.zeros((Q_RANK, H_A, HEAD_PAD - ROPE - NOPE), F32)
    q_main = jnp.concatenate([wq_rope, wq_nope, zeros], axis=-1).reshape(Q_RANK, H_A * HEAD_PAD)
    q_rot = _pad_lanes(_swap_halves(wq_rope), HEAD_PAD).reshape(Q_RANK, H_A * HEAD_PAD)
    wk = w_uk[l].reshape(KV_RANK, H_A, NOPE)
    w_uk_pad = jnp.pad(wk, [(0, 0), (0, 0), (ROPE, HEAD_PAD - ROPE - NOPE)])
    w_abs = jnp.transpose(w_uk_pad, (1, 2, 0))
    wv = w_uv[l].reshape(KV_RANK, H_A, DV)
    is_even = (np.arange(H_A) % 2 == 0)[None, :, None]
    w_uv_pad = jnp.where(is_even, jnp.pad(wv, [(0, 0), (0, 0), (0, HEAD_PAD - DV)]),
                         jnp.pad(wv, [(0, 0), (0, 0), (HEAD_PAD - DV, 0)]))
    v_ones = np.zeros((H_A, HEAD_PAD), np.float32)
    v_ones[0::2, DV] = 1.0
    v_ones[1::2, 0] = 1.0
    head_of_col = jnp.arange(H_A * DV) // DV
    w_uv_heads = jnp.where(head_of_col[None, None, :] == jnp.arange(H_A)[:, None, None], w_uv[l][None], 0.0)
    row = lambda g: g[l].reshape(1, -1)
    return {
        "g_pre_mix": row(g_pre_mix), "g_post_mix": row(g_post_mix),
        "g_pre_ffn": row(g_pre_ffn), "g_post_ffn": row(g_post_ffn),
        "g_q_lat": row(g_q_lat), "g_kv_lat": row(g_kv_lat),
        "w_in_k1": w_in_k1,
        "w_gate": jnp.concatenate([w_ga, w_gb], axis=1).astype(BF16),
        "w_uq_ext": jnp.concatenate([q_main, q_rot], axis=1).astype(BF16),
        "w_uk_pad": w_uk_pad.reshape(KV_RANK, H_A * HEAD_PAD).astype(BF16),
        "w_abs": w_abs.astype(BF16),
        "w_uv_pad": w_uv_pad.reshape(KV_RANK, H_A * HEAD_PAD).astype(BF16),
        "v_ones": jnp.asarray(v_ones.reshape(1, H_A * HEAD_PAD)),
        "w_uv_heads": w_uv_heads.astype(BF16),
        "w_proj_a": w_proj_a[l].astype(BF16), "w_proj_b": w_proj_b[l].astype(BF16),
        "w_out": w_out[l].astype(BF16),
        "w_up": w_up[l].astype(BF16), "conv_w": conv_w[l], "conv_b": conv_b[l].reshape(1, -1),
        "w_down": w_down[l].astype(BF16),
    }


def _pick_tile(t, want):
    tm = min(t, want)
    assert t % tm == 0
    return tm


def _layer(x, ada, pos, past, wts):
    b, t, _ = x.shape
    cos_t, sin_t = _rope_tables(pos)
    tm = _pick_tile(t, 512)
    if past is None:
        q, ckv, kr, qb, kb, vb, k_mla, v_mla = _mixer_inputs(x, ada, cos_t, sin_t, wts, tm=tm, expand_kv=True)
        blk_a = _pick_tile(t, 512)
        o_a = _mla_prompt(q, k_mla, v_mla, tq=blk_a, tk=blk_a)
        o_b = _sb_prompt(qb, kb, vb, tq=_pick_tile(t, 256), nsub=2 if t % 512 == 0 else 1)
        conv_state = jnp.zeros((b, CONV_W - 1, 2 * D_FF), F32)
    else:
        past_ckv, past_kr, past_k, past_v, conv_state = past
        past_len = past_ckv.shape[1]
        q, ckv, kr, qb, kb, vb = _mixer_inputs(x, ada, cos_t, sin_t, wts, tm=tm, expand_kv=False)
        o_a = _mla_sample(q, ckv, kr, past_ckv, past_kr, wts["w_abs"], wts["w_uv_heads"],
                          past_len=past_len, kc=_pick_tile(past_len, 1024))
        o_b = _sb_sample(qb, kb, vb, past_k, past_v, tk=_pick_tile(past_len, 256), recent=512)
    x1 = _merge(x, ada, o_a, o_b, wts, tm=tm)
    y, new_conv = _ffn(x1, ada, conv_state, wts, tm=_pick_tile(t, 256))
    state = (ckv, kr, kb.reshape(b, t, H_B, D_HB), vb.reshape(b, t, H_B, D_HB), new_conv)
    return y, state


def kernel(x_prompt, x_sample, cache_mla_ckv, cache_mla_krope, cache_sb_k, cache_sb_v, state_ffn_conv,
           c_prompt, c_sample, w_ada, b_ada, g_pre_mix, g_post_mix, g_pre_ffn, g_post_ffn,
           w_in, g_q_lat, w_uq, g_kv_lat, w_uk, w_uv, w_proj_a, w_proj_b, w_out,
           w_up, conv_w, conv_b, w_down):
    depth = w_in.shape[0]
    nb_p = x_prompt.shape[0]
    past_len = cache_mla_ckv.shape[2]
    pos_p = jnp.arange(x_prompt.shape[1], dtype=jnp.int32)
    pos_s = past_len + jnp.arange(x_sample.shape[1], dtype=jnp.int32)
    xp, xs = x_prompt, x_sample
    c_all = jnp.concatenate([c_prompt, c_sample], axis=0)
    st_p = [[] for _ in range(5)]
    st_s = [[] for _ in range(5)]
    for l in range(depth):
        wts = _layer_weights(l, g_pre_mix, g_post_mix, g_pre_ffn, g_post_ffn, w_in, g_q_lat, w_uq, g_kv_lat,
                             w_uk, w_uv, w_proj_a, w_proj_b, w_out, w_up, conv_w, conv_b, w_down)
        ada = _ada(c_all, w_ada[l], b_ada[l]).reshape(-1, 6, D_MODEL)
        xp, sp = _layer(xp, ada[:nb_p], pos_p, None, wts)
        past = (cache_mla_ckv[l], cache_mla_krope[l], cache_sb_k[l], cache_sb_v[l], state_ffn_conv[l])
        xs, ss = _layer(xs, ada[nb_p:], pos_s, past, wts)
        for i in range(5):
            st_p[i].append(sp[i])
            st_s[i].append(ss[i])
    p_state = [jnp.stack(a, axis=0) for a in st_p]
    s_state = [jnp.stack(a, axis=0) for a in st_s]
    return (xp, xs, *p_state, *s_state)
```

```python
import functools
import math

import numpy as np
import jax
import jax.numpy as jnp
from jax import lax
from jax.experimental import pallas as pl
from jax.experimental.pallas import tpu as pltpu

D_MODEL = 1024
CHUNK = 64
CHUNK_SHIFT = 6
H_A = 8
NOPE = 64
ROPE = 32
DV = 64
Q_RANK = 384
KV_RANK = 256
ROPE_BASE = 10000.0
H_B = 8
D_HB = 64
D_FF = 2816
CONV_W = 3
EPS = 1e-6
NEG = -1e30
MLA_SCALE = (NOPE + ROPE) ** -0.5
SB_SCALE = D_HB ** -0.5
LOG2E = math.log2(math.e)

LANES = 128
SUBLANES = 8
HEAD_PAD = LANES
VT_ROWS = 80
SB_DEAD = 110.0
VMEM_LIMIT = 56 * 1024 * 1024

F32 = jnp.float32
BF16 = jnp.bfloat16


def _dot(a, b):
    return jnp.dot(a, b, preferred_element_type=F32)


def _dot_t(a, b):
    return lax.dot_general(a, b, (((1,), (1,)), ((), ())), preferred_element_type=F32)


def _rms(x, g):
    return x * lax.rsqrt(jnp.mean(x * x, axis=-1, keepdims=True) + EPS) * g


def _sigmoid(x):
    return 1.0 / (1.0 + jnp.exp(-x))


def _params(n_parallel, n_arbitrary=0):
    return pltpu.CompilerParams(
        dimension_semantics=("parallel",) * n_parallel + ("arbitrary",) * n_arbitrary,
        vmem_limit_bytes=VMEM_LIMIT)


def _const_spec(shape):
    n = len(shape)
    return pl.BlockSpec(shape, lambda *_: (0,) * n, pipeline_mode=pl.Buffered(1))


def _ada_kernel(c_ref, w_ref, b_ref, o_ref):
    c = c_ref[...]
    s = (c * _sigmoid(c)).astype(BF16)
    o_ref[...] = _dot(s, w_ref[...].astype(BF16)) + b_ref[...]


def _ada(c_all, w_ada, b_ada):
    n = c_all.shape[0]
    nchunk = 6
    return pl.pallas_call(
        _ada_kernel,
        grid=(nchunk,),
        in_specs=[pl.BlockSpec((n, D_MODEL), lambda j: (0, 0)),
                  pl.BlockSpec((D_MODEL, D_MODEL), lambda j: (0, j)),
                  pl.BlockSpec((1, D_MODEL), lambda j: (0, j))],
        out_specs=pl.BlockSpec((n, D_MODEL), lambda j: (0, j)),
        out_shape=jax.ShapeDtypeStruct((n, 6 * D_MODEL), F32),
        compiler_params=_params(1),
        name="ada",
    )(c_all, w_ada, b_ada.reshape(1, -1))


_C_QLAT = 0
_C_CKV = _C_QLAT + Q_RANK
_C_KR = _C_CKV + KV_RANK
_C_KRR = _C_KR + LANES
_C_QB = _C_KRR + LANES
_C_KB = _C_QB + H_B * D_HB
_C_VB = _C_KB + H_B * D_HB
_N_K1 = _C_VB + H_B * D_HB


def _premix(x, ada_ref, g):
    sh = ada_ref[0, 0:1, :]
    sc = ada_ref[0, 1:2, :]
    return _rms(x, g) * (1.0 + sc) + sh


def _row_parts(tm):
    n = 2 if tm % (2 * 128) == 0 else 1
    return [slice(i * (tm // n), (i + 1) * (tm // n)) for i in range(n)]


def _mixer_kernel(x_ref, ada_ref, g_ref, win_ref, gq_ref, wuq_ref, gkv_ref, wuk_ref, wuv_ref, vone_ref,
                  cos_ref, sin_ref, *out_refs, expand_kv):
    if expand_kv:
        q_ref, ckv_ref, kr_ref, qb_ref, kb_ref, vb_ref, k_ref, vt_ref = out_refs
    else:
        q_ref, ckv_ref, kr_ref, qb_ref, kb_ref, vb_ref = out_refs
    parts = _row_parts(x_ref.shape[1])
    nq = H_A * HEAD_PAD
    ps = [_dot(_premix(x_ref[0, rows, :], ada_ref, g_ref[...]).astype(BF16), win_ref[...]) for rows in parts]
    q_lats = [_rms(p[:, _C_QLAT:_C_QLAT + Q_RANK], gq_ref[...]).astype(BF16) for p in ps]
    c_kvs = [_rms(p[:, _C_CKV:_C_CKV + KV_RANK], gkv_ref[...]) for p in ps]
    q2s = [_dot(q_lat, wuq_ref[...]) for q_lat in q_lats]
    if expand_kv:
        c_bfs = [c_kv.astype(BF16) for c_kv in c_kvs]
        k2s = [_dot(c_bf, wuk_ref[...]) for c_bf in c_bfs]
        v2s = [_dot(c_bf, wuv_ref[...]) for c_bf in c_bfs]
    for i, rows in enumerate(parts):
        p, q2 = ps[i], q2s[i]
        cos = cos_ref[rows, :]
        sin = sin_ref[rows, :]
        for hd in range(H_A):
            lo = hd * HEAD_PAD
            qh = q2[:, lo:lo + HEAD_PAD] * cos + q2[:, nq + lo:nq + lo + HEAD_PAD] * sin
            q_ref[0, rows, lo:lo + HEAD_PAD] = (qh * (MLA_SCALE * LOG2E)).astype(BF16)
        ckv_ref[0, rows, :] = c_kvs[i]
        kr = p[:, _C_KR:_C_KR + LANES] * cos + p[:, _C_KRR:_C_KRR + LANES] * sin
        kr_ref[0, rows, :] = kr[:, :ROPE]
        qb_ref[0, rows, :] = (p[:, _C_QB:_C_QB + H_B * D_HB] * SB_SCALE).astype(BF16)
        kb_ref[0, rows, :] = p[:, _C_KB:_C_KB + H_B * D_HB]
        vb_ref[0, rows, :] = p[:, _C_VB:_C_VB + H_B * D_HB]
        if expand_kv:
            for hd in range(H_A):
                lo = hd * HEAD_PAD
                k_ref[0, rows, lo:lo + HEAD_PAD] = (k2s[i][:, lo:lo + HEAD_PAD] + kr).astype(BF16)
            vt_ref[0, :, rows] = (v2s[i] + vone_ref[...]).T.astype(BF16)


def _mixer_inputs(x, ada, cos_t, sin_t, wts, *, tm, expand_kv):
    b, t, _ = x.shape
    nt = t // tm
    tok = lambda n: pl.BlockSpec((1, tm, n), lambda i, j: (i, j, 0))
    in_specs = [tok(D_MODEL),
                pl.BlockSpec((1, 6, D_MODEL), lambda i, j: (i, 0, 0)),
                _const_spec((1, D_MODEL)),
                _const_spec((D_MODEL, _N_K1)),
                _const_spec((1, Q_RANK)),
                _const_spec((Q_RANK, 2 * H_A * HEAD_PAD)),
                _const_spec((1, KV_RANK)),
                _const_spec((KV_RANK, H_A * HEAD_PAD)),
                _const_spec((KV_RANK, H_A * VT_ROWS)),
                _const_spec((1, H_A * VT_ROWS)),
                pl.BlockSpec((tm, LANES), lambda i, j: (j, 0)),
                pl.BlockSpec((tm, LANES), lambda i, j: (j, 0))]
    shapes = [((b, t, H_A * HEAD_PAD), BF16), ((b, t, KV_RANK), F32), ((b, t, ROPE), F32),
              ((b, t, H_B * D_HB), BF16), ((b, t, H_B * D_HB), F32), ((b, t, H_B * D_HB), F32)]
    if expand_kv:
        shapes += [((b, t, H_A * HEAD_PAD), BF16)]
    out_specs = [tok(s[-1]) for s, _ in shapes]
    if expand_kv:
        shapes += [((b, H_A * VT_ROWS, t), BF16)]
        out_specs += [pl.BlockSpec((1, H_A * VT_ROWS, tm), lambda i, j: (i, 0, j))]
    return pl.pallas_call(
        functools.partial(_mixer_kernel, expand_kv=expand_kv),
        grid=(b, nt),
        in_specs=in_specs,
        out_specs=out_specs,
        out_shape=[jax.ShapeDtypeStruct(s, d) for s, d in shapes],
        compiler_params=_params(2),
        name="mixer_in_kv" if expand_kv else "mixer_in",
    )(x, ada, wts["g_pre_mix"], wts["w_in_k1"], wts["g_q_lat"], wts["w_uq_ext"], wts["g_kv_lat"],
      wts["w_uk_pad"], wts["w_uv_t"], wts["v_ones"], cos_t, sin_t)


MLA_PAIR = 2


def _mla_prompt_kernel(q_ref, k_ref, vt_ref, o_ref, sa_ref, sb_ref, m_ref, acc_ref, *, blk, nblk):
    qi = pl.program_id(2)
    npair = (qi + MLA_PAIR) // MLA_PAIR
    lanes = [slice(hh * HEAD_PAD, (hh + 1) * HEAD_PAD) for hh in range(2)]
    m_ref[...] = jnp.full(m_ref.shape, NEG, F32)
    acc_ref[...] = jnp.zeros(acc_ref.shape, F32)

    def start_of(j):
        return pl.multiple_of(jnp.minimum(j, nblk - 1) * blk, blk)

    def scores(s_ref, g):
        for hh, ln in enumerate(lanes):
            for b in range(MLA_PAIR):
                k = k_ref[0, pl.ds(start_of(MLA_PAIR * g + b), blk), ln]
                s_ref[hh, b] = _dot_t(k, q_ref[0, :, ln])

    def softmax_pv(s_ref, g, masked):
        for hh in range(2):
            ss = []
            for b in range(MLA_PAIR):
                s = s_ref[hh, b]
                if masked:
                    kpos = (MLA_PAIR * g + b) * blk + lax.broadcasted_iota(jnp.int32, (blk, blk), 0)
                    qpos = qi * blk + lax.broadcasted_iota(jnp.int32, (blk, blk), 1)
                    s = jnp.where((kpos >> CHUNK_SHIFT) <= (qpos >> CHUNK_SHIFT), s, NEG)
                ss.append(s)
            m_old = m_ref[hh]
            m_new = m_old
            for s in ss:
                m_new = jnp.maximum(m_new, jnp.max(s, axis=0, keepdims=True))
            acc = jnp.exp2(m_old - m_new) * acc_ref[hh]
            for b, s in enumerate(ss):
                p = jnp.exp2(s - m_new).astype(BF16)
                vt = vt_ref[0, hh * VT_ROWS:(hh + 1) * VT_ROWS, pl.ds(start_of(MLA_PAIR * g + b), blk)]
                acc = acc + _dot(vt, p)
            acc_ref[hh] = acc
            m_ref[hh] = m_new

    scores(sa_ref, 0)

    def body(h, c):
        scores(sb_ref, 2 * h + 1)
        softmax_pv(sa_ref, 2 * h, False)
        scores(sa_ref, 2 * h + 2)
        softmax_pv(sb_ref, 2 * h + 1, False)
        return c

    done = (npair - 1) // 2 * 2
    lax.fori_loop(0, (npair - 1) // 2, body, 0)

    @pl.when(done == npair - 1)
    def _():
        softmax_pv(sa_ref, done, True)

    @pl.when(done != npair - 1)
    def _():
        scores(sb_ref, done + 1)
        softmax_pv(sa_ref, done, False)
        softmax_pv(sb_ref, done + 1, True)

    outs = [acc_ref[hh][:DV] / acc_ref[hh][DV:DV + 1] for hh in range(2)]
    o_ref[0] = jnp.concatenate(outs, axis=0).T.astype(BF16)


def _mla_prompt(q, k, vt, *, blk):
    b, t, _ = q.shape
    nblk = t // blk
    return pl.pallas_call(
        functools.partial(_mla_prompt_kernel, blk=blk, nblk=nblk),
        grid=(b, H_A // 2, nblk),
        in_specs=[pl.BlockSpec((1, blk, 2 * HEAD_PAD), lambda i, j, n: (i, n, j)),
                  pl.BlockSpec((1, t, 2 * HEAD_PAD), lambda i, j, n: (i, 0, j)),
                  pl.BlockSpec((1, 2 * VT_ROWS, t), lambda i, j, n: (i, j, 0))],
        out_specs=pl.BlockSpec((1, blk, 2 * DV), lambda i, j, n: (i, n, j)),
        out_shape=jax.ShapeDtypeStruct((b, t, H_A * DV), BF16),
        scratch_shapes=[pltpu.VMEM((2, MLA_PAIR, blk, blk), F32), pltpu.VMEM((2, MLA_PAIR, blk, blk), F32),
                        pltpu.VMEM((2, 1, blk), F32), pltpu.VMEM((2, VT_ROWS, blk), F32)],
        compiler_params=_params(3),
        name="mla_prompt",
    )(q, k, vt)


def _tri(n):
    r = lax.broadcasted_iota(jnp.int32, (n, n), 0)
    c = lax.broadcasted_iota(jnp.int32, (n, n), 1)
    return jnp.where(r > c, 1.0, 0.0).astype(BF16)


def _sb_blocks(qs, blocks, r_prev):
    zs = [_dot_t(qs, k) for k, _, _, _ in blocks]
    sps = []
    for z, (_, _, _, keep) in zip(zs, blocks):
        sp = jnp.maximum(z, 0.0) + jnp.log(1.0 + jnp.exp2(jnp.abs(z) * (-LOG2E)))
        sps.append(sp if keep is None else jnp.where(keep, sp, 0.0))
    laters = [_dot(sp.astype(BF16), tri) for sp, (_, _, tri, _) in zip(sps, blocks)]
    out = None
    for z, sp, later, (k, v, _, keep) in zip(zs, sps, laters, blocks):
        tk = k.shape[0]
        w = z - sp - later
        if r_prev is not None:
            r_b = r_prev
            if r_b.shape[1] == LANES and tk > LANES:
                r_b = jnp.concatenate([r_b] * (tk // LANES), axis=1)
            elif r_b.shape[1] == LANES and tk < LANES:
                r_b = r_b[:, :tk]
            w = w - r_b
        a = jnp.exp2(w * LOG2E)
        if keep is not None:
            a = jnp.where(keep, a, 0.0)
        o = _dot(a.astype(BF16), v)
        out = o if out is None else out + o
        rs = jnp.sum(sp, axis=-1, keepdims=True)
        r_prev = rs if r_prev is None else r_prev + rs
    return out, r_prev


def _head_mask(x, hh):
    lane = lax.broadcasted_iota(jnp.int32, x.shape, 1)
    mine = jnp.logical_and(lane >= hh * D_HB, lane < (hh + 1) * D_HB)
    return jnp.where(mine, x, jnp.zeros_like(x))


def _stack_heads(q):
    return jnp.concatenate([_head_mask(q, 0), _head_mask(q, 1)], axis=0)


def _causal_pair(t, tk):
    row = lax.broadcasted_iota(jnp.int32, (t, tk), 0)
    col = lax.broadcasted_iota(jnp.int32, (t, tk), 1)
    keep = col < row
    return jnp.concatenate([keep, keep], axis=0)


def _sb_tail(qs, load, tri, acc_ref, r_ref, j_start):
    def cond(c):
        j, rmin = c
        return jnp.logical_and(j >= 0, rmin < SB_DEAD)

    def body(c):
        j, _ = c
        k, v = load(j)
        o, r_new = _sb_blocks(qs, [(k, v, tri, None)], r_ref[...])
        acc_ref[...] += o
        r_ref[...] = r_new
        return j - 1, jnp.min(r_new)

    lax.while_loop(cond, body, (j_start, jnp.min(r_ref[...])))


def _unstack_heads(acc, t):
    lane = lax.broadcasted_iota(jnp.int32, (t, LANES), 1)
    return jnp.where(lane < D_HB, acc[:t], acc[t:])


def _sb_prompt_kernel(q_ref, k_ref, v_ref, o_ref, acc_ref, r_ref, *, tq, nsub):
    qi = pl.program_id(2)
    tri = _tri(tq)
    causal = _causal_pair(tq, tq)

    def load(j):
        start = pl.multiple_of(j * tq, tq)
        return (k_ref[0, pl.ds(start, tq), :].astype(BF16),
                v_ref[0, pl.ds(start, tq), :].astype(BF16))

    qss = []
    for sub in range(nsub):
        blk = qi * nsub + sub
        qs = _stack_heads(q_ref[0, sub * tq:(sub + 1) * tq, :])
        k_d, v_d = load(blk)
        k_p, v_p = load(jnp.maximum(blk - 1, 0))
        o, r = _sb_blocks(qs, [(k_d, v_d, tri, causal), (k_p, v_p, tri, blk > 0)], None)
        acc_ref[sub] = o
        r_ref[sub] = jnp.broadcast_to(r, (2 * tq, LANES))
        qss.append(qs)
    for sub in range(nsub):
        _sb_tail(qss[sub], load, tri, acc_ref.at[sub], r_ref.at[sub], qi * nsub + sub - 2)
        o_ref[0, sub * tq:(sub + 1) * tq, :] = _unstack_heads(acc_ref[sub], tq).astype(BF16)


def _sb_prompt(q, k, v, *, tq, nsub):
    b, t, _ = q.shape
    rows = tq * nsub
    return pl.pallas_call(
        functools.partial(_sb_prompt_kernel, tq=tq, nsub=nsub),
        grid=(b, H_B // 2, t // rows),
        in_specs=[pl.BlockSpec((1, rows, LANES), lambda i, j, n: (i, n, j)),
                  pl.BlockSpec((1, t, LANES), lambda i, j, n: (i, 0, j)),
                  pl.BlockSpec((1, t, LANES), lambda i, j, n: (i, 0, j))],
        out_specs=pl.BlockSpec((1, rows, LANES), lambda i, j, n: (i, n, j)),
        out_shape=jax.ShapeDtypeStruct((b, t, H_B * D_HB), BF16),
        scratch_shapes=[pltpu.VMEM((nsub, 2 * tq, LANES), F32), pltpu.VMEM((nsub, 2 * tq, LANES), F32)],
        compiler_params=_params(3),
        name="sb_prompt",
    )(q, k, v)


def _sb_sample_kernel(q_ref, kn_ref, vn_ref, kc_ref, vc_ref, o_ref, left_ref, acc_ref, r_ref, *, tk):
    tq = q_ref.shape[1]
    nblk = kc_ref.shape[1] // tk
    qs = _stack_heads(q_ref[0])
    tri = _tri(tk)

    def load(j):
        start = pl.multiple_of(j * tk, tk)
        return (kc_ref[0, pl.ds(start, tk), :].astype(BF16),
                vc_ref[0, pl.ds(start, tk), :].astype(BF16))

    k_p, v_p = load(nblk - 1)
    o, r = _sb_blocks(qs, [(kn_ref[0].astype(BF16), vn_ref[0].astype(BF16), _tri(tq), _causal_pair(tq, tq)),
                           (k_p, v_p, tri, None)], None)
    acc_ref[...] = o
    r_ref[...] = jnp.broadcast_to(r, (2 * tq, LANES))
    _sb_tail(qs, load, tri, acc_ref, r_ref, jnp.int32(nblk - 2))
    o_ref[0] = _unstack_heads(acc_ref[...], tq).astype(BF16)
    left_ref[...] = jnp.full(left_ref.shape, jnp.min(r_ref[...]), F32)


def _sb_sample_call(q, k_new, v_new, k_cache, v_cache, *, tk):
    b, t, _ = q.shape
    past = k_cache.shape[1]
    new = pl.BlockSpec((1, t, LANES), lambda i, j: (i, 0, j))
    old = pl.BlockSpec((1, past, LANES), lambda i, j: (i, 0, j))
    return pl.pallas_call(
        functools.partial(_sb_sample_kernel, tk=tk),
        grid=(b, H_B // 2),
        in_specs=[new, new, new, old, old],
        out_specs=[new, pl.BlockSpec((1, 1, SUBLANES, LANES), lambda i, j: (i, j, 0, 0))],
        out_shape=[jax.ShapeDtypeStruct((b, t, H_B * D_HB), BF16),
                   jax.ShapeDtypeStruct((b, H_B // 2, SUBLANES, LANES), F32)],
        scratch_shapes=[pltpu.VMEM((2 * t, LANES), F32), pltpu.VMEM((2 * t, LANES), F32)],
        compiler_params=_params(2),
        name="sb_sample",
    )(q, k_new, v_new, k_cache, v_cache)


def _sb_sample(q, k_new, v_new, k_cache, v_cache, *, tk, recent):
    b, past = k_cache.shape[:2]
    flat = lambda c: c.reshape(b, c.shape[1], H_B * D_HB)
    recent = min(recent, past)
    o, left = _sb_sample_call(q, k_new, v_new, flat(k_cache[:, past - recent:]), flat(v_cache[:, past - recent:]),
                              tk=tk)
    if recent == past:
        return o
    return lax.cond(jnp.min(left) < SB_DEAD,
                    lambda: _sb_sample_call(q, k_new, v_new, flat(k_cache), flat(v_cache), tk=tk)[0],
                    lambda: o)


def _mla_sample_kernel(q_ref, cn_ref, rn_ref, cc_ref, rc_ref, wabs_ref, wuv_ref, o_ref, *, kc, past_len):
    t = q_ref.shape[1]
    past = cc_ref.shape[1]
    qa, qr = [], []
    for hd in range(H_A):
        qh = q_ref[0, :, hd * HEAD_PAD:(hd + 1) * HEAD_PAD]
        qa.append(_dot(qh, wabs_ref[hd]).astype(BF16))
        qr.append(qh[:, :ROPE])
    qa = jnp.concatenate(qa, axis=0)
    qr = jnp.concatenate(qr, axis=0)
    rows = H_A * t

    pieces = [(cc_ref[0, c * kc:(c + 1) * kc, :], rc_ref[0, c * kc:(c + 1) * kc, :], c * kc)
              for c in range(past // kc)]
    pieces.append((cn_ref[0], rn_ref[0], past_len))
    scores, lat = [], []
    for ckv, kr, k0 in pieces:
        ckv = ckv.astype(BF16)
        s = _dot_t(qa, ckv) + _dot_t(qr, kr.astype(BF16))
        n = ckv.shape[0]
        if (k0 + n - 1) // CHUNK > past_len // CHUNK:
            qpos = past_len + lax.broadcasted_iota(jnp.int32, (rows, n), 0) % t
            kpos = k0 + lax.broadcasted_iota(jnp.int32, (rows, n), 1)
            s = jnp.where((kpos >> CHUNK_SHIFT) <= (qpos >> CHUNK_SHIFT), s, NEG)
        scores.append(s)
        lat.append(ckv)
    m = functools.reduce(jnp.maximum, [jnp.max(s, axis=-1, keepdims=True) for s in scores])
    l = jnp.zeros((rows, 1), F32)
    o_lat = jnp.zeros((rows, KV_RANK), F32)
    for s, ckv in zip(scores, lat):
        p = jnp.exp2(s - m)
        l = l + jnp.sum(p, axis=-1, keepdims=True)
        o_lat = o_lat + _dot(p.astype(BF16), ckv)
    o_lat = (o_lat / l).astype(BF16)
    o = jnp.zeros((t, H_A * DV), F32)
    for hd in range(H_A):
        o = o + _dot(o_lat[hd * t:(hd + 1) * t], wuv_ref[hd])
    o_ref[0] = o.astype(BF16)


def _mla_sample(q, ckv_new, kr_new, ckv_cache, kr_cache, w_abs, w_uv_heads, *, past_len, kc):
    b, t, _ = q.shape
    past = ckv_cache.shape[1]
    row = lambda n, s: pl.BlockSpec((1, n, s), lambda i: (i, 0, 0))
    return pl.pallas_call(
        functools.partial(_mla_sample_kernel, kc=kc, past_len=past_len),
        grid=(b,),
        in_specs=[row(t, H_A * HEAD_PAD), row(t, KV_RANK), row(t, ROPE),
                  row(past, KV_RANK), row(past, ROPE),
                  _const_spec((H_A, HEAD_PAD, KV_RANK)), _const_spec((H_A, KV_RANK, H_A * DV))],
        out_specs=row(t, H_A * DV),
        out_shape=jax.ShapeDtypeStruct((b, t, H_A * DV), BF16),
        compiler_params=_params(1),
        name="mla_sample",
    )(q, ckv_new, kr_new, ckv_cache, kr_cache, w_abs, w_uv_heads)


def _merge_kernel(x_ref, ada_ref, g_ref, wg_ref, oa_ref, ob_ref, wpa_ref, wpb_ref, wo_ref, gpost_ref, o_ref):
    parts = _row_parts(x_ref.shape[1])
    xs = [x_ref[0, rows, :] for rows in parts]
    gate_in = [_dot(_premix(x, ada_ref, g_ref[...]).astype(BF16), wg_ref[...]) for x in xs]
    pas = [_dot(oa_ref[0, rows, :], wpa_ref[...]) for rows in parts]
    pbs = [_dot(ob_ref[0, rows, :], wpb_ref[...]) for rows in parts]
    merged = [(_sigmoid(gi[:, :D_MODEL]) * pa + _sigmoid(gi[:, D_MODEL:]) * pb).astype(BF16)
              for gi, pa, pb in zip(gate_in, pas, pbs)]
    mos = [_dot(m, wo_ref[...]) for m in merged]
    gt1 = ada_ref[0, 2:3, :]
    for rows, x, mo in zip(parts, xs, mos):
        o_ref[0, rows, :] = x + gt1 * _rms(mo, gpost_ref[...])


def _merge(x, ada, o_a, o_b, wts, *, tm):
    b, t, _ = x.shape
    tok = lambda n: pl.BlockSpec((1, tm, n), lambda i, j: (i, j, 0))
    return pl.pallas_call(
        _merge_kernel,
        grid=(b, t // tm),
        in_specs=[tok(D_MODEL),
                  pl.BlockSpec((1, 6, D_MODEL), lambda i, j: (i, 0, 0)),
                  _const_spec((1, D_MODEL)),
                  _const_spec((D_MODEL, 2 * D_MODEL)),
                  tok(H_A * DV), tok(H_B * D_HB),
                  _const_spec((H_A * DV, D_MODEL)), _const_spec((H_B * D_HB, D_MODEL)),
                  _const_spec((D_MODEL, D_MODEL)),
                  _const_spec((1, D_MODEL))],
        out_specs=tok(D_MODEL),
        out_shape=jax.ShapeDtypeStruct((b, t, D_MODEL), F32),
        compiler_params=_params(2),
        name="merge",
    )(x, ada, wts["g_pre_mix"], wts["w_gate"], o_a, o_b, wts["w_proj_a"], wts["w_proj_b"],
      wts["w_out"], wts["g_post_mix"])


FF_CHUNK = 256
SUB_GROUPS = 4
SUB_ROWS = SUB_GROUPS * SUBLANES
HALO = 2 * SUBLANES


def _gelu_tanh(a):
    return 0.5 * a * (1.0 + jnp.tanh(math.sqrt(2.0 / math.pi) * (a + 0.044715 * (a * a * a))))


def _ffn_kernel(x_ref, ada_ref, g_ref, wup_ref, cw_ref, cb_ref, wdn_ref, gpost_ref, cs_ref,
                o_ref, nc_ref, halo_ref, perm_ref, *, tm):
    ti = pl.program_id(1)
    nt = pl.num_programs(1)
    nsub = tm // SUB_ROWS

    @pl.when(ti == 0)
    def _():
        halo_ref[...] = jnp.zeros((HALO, 2 * D_FF), F32)
        halo_ref[SUBLANES - 1:SUBLANES, :] = cs_ref[0, 0:1, :]
        halo_ref[HALO - 1:HALO, :] = cs_ref[0, 1:2, :]

    nc = D_MODEL // LANES
    for c in range(nc):
        perm_ref[c] = x_ref[0, :, c * LANES:(c + 1) * LANES]
    x = jnp.concatenate(
        [jnp.concatenate([perm_ref[c, pl.ds(j * SUB_ROWS + i, SUBLANES, stride=SUB_GROUPS), :] for c in range(nc)],
                         axis=1)
         for j in range(nsub) for i in range(SUB_GROUPS)], axis=0)
    sh = ada_ref[0, 3:4, :]
    sc = ada_ref[0, 4:5, :]
    gt2 = ada_ref[0, 5:6, :]
    h2 = (_rms(x, g_ref[...]) * (1.0 + sc) + sh).astype(BF16)
    first = lax.broadcasted_iota(jnp.int32, (SUBLANES, FF_CHUNK), 0) == 0

    def up(col):
        cols = slice(col, col + FF_CHUNK)
        u = _dot(h2, wup_ref[:, cols])
        grp = [u[g * SUBLANES:(g + 1) * SUBLANES] for g in range(tm // SUBLANES)]
        rot2 = [pltpu.roll(halo_ref[0:SUBLANES, cols], 1, axis=0)]
        rot3 = [pltpu.roll(halo_ref[SUBLANES:HALO, cols], 1, axis=0)]
        for j in range(nsub):
            rot2.append(pltpu.roll(grp[j * SUB_GROUPS + SUB_GROUPS - 2], 1, axis=0))
            rot3.append(pltpu.roll(grp[j * SUB_GROUPS + SUB_GROUPS - 1], 1, axis=0))
        halo_ref[:, cols] = u[tm - HALO:]
        u1, u2 = [], []
        for j in range(nsub):
            f1 = jnp.where(first, rot3[j], rot3[j + 1])
            f2 = jnp.where(first, rot2[j], rot2[j + 1])
            g0 = j * SUB_GROUPS
            u1 += [f1] + grp[g0:g0 + SUB_GROUPS - 1]
            u2 += [f2, f1] + grp[g0:g0 + SUB_GROUPS - 2]
        u1 = jnp.concatenate(u1, axis=0)
        u2 = jnp.concatenate(u2, axis=0)
        return (cb_ref[:, cols] + cw_ref[0:1, cols] * u2 + cw_ref[1:2, cols] * u1 + cw_ref[2:3, cols] * u)

    nchunk = D_FF // FF_CHUNK
    acc = jnp.zeros((tm, D_MODEL), F32)
    ya, yb = up(0), up(D_FF)
    for c in range(nchunk):
        if c + 1 < nchunk:
            ya_next, yb_next = up((c + 1) * FF_CHUNK), up(D_FF + (c + 1) * FF_CHUNK)
        g = (_gelu_tanh(ya) * yb).astype(BF16)
        acc = acc + _dot(g, wdn_ref[c * FF_CHUNK:(c + 1) * FF_CHUNK, :])
        if c + 1 < nchunk:
            ya, yb = ya_next, yb_next
    y = x + gt2 * _rms(acc, gpost_ref[...])
    for j in range(nsub):
        for i in range(SUB_GROUPS):
            g0 = (j * SUB_GROUPS + i) * SUBLANES
            for c in range(nc):
                perm_ref[c, pl.ds(j * SUB_ROWS + i, SUBLANES, stride=SUB_GROUPS), :] = (
                    y[g0:g0 + SUBLANES, c * LANES:(c + 1) * LANES])
    for c in range(nc):
        o_ref[0, :, c * LANES:(c + 1) * LANES] = perm_ref[c]

    @pl.when(ti == nt - 1)
    def _():
        nc_ref[0, 0:1, :] = halo_ref[SUBLANES - 1:SUBLANES, :]
        nc_ref[0, 1:2, :] = halo_ref[HALO - 1:HALO, :]


def _ffn(x, ada, conv_state, wts, *, tm):
    b, t, _ = x.shape
    assert tm % SUB_ROWS == 0 and t % tm == 0
    tok = pl.BlockSpec((1, tm, D_MODEL), lambda i, j: (i, j, 0))
    state = pl.BlockSpec((1, CONV_W - 1, 2 * D_FF), lambda i, j: (i, 0, 0))
    return pl.pallas_call(
        functools.partial(_ffn_kernel, tm=tm),
        grid=(b, t // tm),
        in_specs=[tok,
                  pl.BlockSpec((1, 6, D_MODEL), lambda i, j: (i, 0, 0)),
                  _const_spec((1, D_MODEL)),
                  _const_spec((D_MODEL, 2 * D_FF)),
                  _const_spec((CONV_W, 2 * D_FF)),
                  _const_spec((1, 2 * D_FF)),
                  _const_spec((D_FF, D_MODEL)),
                  _const_spec((1, D_MODEL)),
                  state],
        out_specs=[tok, state],
        out_shape=[jax.ShapeDtypeStruct((b, t, D_MODEL), F32),
                   jax.ShapeDtypeStruct((b, CONV_W - 1, 2 * D_FF), F32)],
        scratch_shapes=[pltpu.VMEM((HALO, 2 * D_FF), F32), pltpu.VMEM((D_MODEL // LANES, tm, LANES), F32)],
        compiler_params=_params(1, 1),
        name="conv_ffn",
    )(x, ada, wts["g_pre_ffn"], wts["w_up"], wts["conv_w"], wts["conv_b"], wts["w_down"],
      wts["g_post_ffn"], conv_state)


def _rope_tables(pos):
    inv = ROPE_BASE ** (-jnp.arange(0, ROPE, 2, dtype=F32) / ROPE)
    ang = pos.astype(F32)[:, None] * inv[None, :]
    cos, sin = jnp.cos(ang), jnp.sin(ang)
    t = pos.shape[0]
    cos_t = jnp.concatenate([cos, cos, jnp.ones((t, NOPE), F32), jnp.zeros((t, HEAD_PAD - ROPE - NOPE), F32)], axis=1)
    sin_t = jnp.concatenate([-sin, sin, jnp.zeros((t, HEAD_PAD - ROPE), F32)], axis=1)
    return cos_t, sin_t


def _swap_halves(w):
    return jnp.concatenate([w[..., ROPE // 2:], w[..., :ROPE // 2]], axis=-1)


def _pad_lanes(w, n):
    return jnp.pad(w, [(0, 0)] * (w.ndim - 1) + [(0, n - w.shape[-1])])


def _layer_weights(l, g_pre_mix, g_post_mix, g_pre_ffn, g_post_ffn, w_in, g_q_lat, w_uq, g_kv_lat,
                   w_uk, w_uv, w_proj_a, w_proj_b, w_out, w_up, conv_w, conv_b, w_down):
    splits = np.cumsum([Q_RANK, KV_RANK, ROPE, H_B * D_HB, H_B * D_HB, H_B * D_HB, D_MODEL])
    wi = w_in[l]
    w_ql, w_ckv, w_kr, w_qb, w_kb, w_vb, w_ga, w_gb = jnp.split(wi, splits, axis=1)
    w_in_k1 = jnp.concatenate([w_ql, w_ckv, _pad_lanes(w_kr, LANES), _pad_lanes(_swap_halves(w_kr), LANES),
                               w_qb, w_kb, w_vb], axis=1).astype(BF16)
    wq = w_uq[l].reshape(Q_RANK, H_A, NOPE + ROPE)
    wq_nope, wq_rope = wq[..., :NOPE], wq[..., NOPE:]
    zeros = jnp.zeros((Q_RANK, H_A, HEAD_PAD - ROPE - NOPE), F32)
    q_main = jnp.concatenate([wq_rope, wq_nope, zeros], axis=-1).reshape(Q_RANK, H_A * HEAD_PAD)
    q_rot = _pad_lanes(_swap_halves(wq_rope), HEAD_PAD).reshape(Q_RANK, H_A * HEAD_PAD)
    wk = w_uk[l].reshape(KV_RANK, H_A, NOPE)
    w_uk_pad = jnp.pad(wk, [(0, 0), (0, 0), (ROPE, HEAD_PAD - ROPE - NOPE)])
    w_abs = jnp.transpose(w_uk_pad, (1, 2, 0))
    w_uv_t = _pad_lanes(w_uv[l].reshape(KV_RANK, H_A, DV), VT_ROWS)
    v_ones = np.zeros((H_A, VT_ROWS), np.float32)
    v_ones[:, DV] = 1.0
    head_of_col = jnp.arange(H_A * DV) // DV
    w_uv_heads = jnp.where(head_of_col[None, None, :] == jnp.arange(H_A)[:, None, None], w_uv[l][None], 0.0)
    row = lambda g: g[l].reshape(1, -1)
    return {
        "g_pre_mix": row(g_pre_mix), "g_post_mix": row(g_post_mix),
        "g_pre_ffn": row(g_pre_ffn), "g_post_ffn": row(g_post_ffn),
        "g_q_lat": row(g_q_lat), "g_kv_lat": row(g_kv_lat),
        "w_in_k1": w_in_k1,
        "w_gate": jnp.concatenate([w_ga, w_gb], axis=1).astype(BF16),
        "w_uq_ext": jnp.concatenate([q_main, q_rot], axis=1).astype(BF16),
        "w_uk_pad": w_uk_pad.reshape(KV_RANK, H_A * HEAD_PAD).astype(BF16),
        "w_abs": w_abs.astype(BF16),
        "w_uv_t": w_uv_t.reshape(KV_RANK, H_A * VT_ROWS).astype(BF16),
        "v_ones": jnp.asarray(v_ones.reshape(1, H_A * VT_ROWS)),
        "w_uv_heads": w_uv_heads.astype(BF16),
        "w_proj_a": w_proj_a[l].astype(BF16), "w_proj_b": w_proj_b[l].astype(BF16),
        "w_out": w_out[l].astype(BF16),
        "w_up": w_up[l].astype(BF16), "conv_w": conv_w[l], "conv_b": conv_b[l].reshape(1, -1),
        "w_down": w_down[l].astype(BF16),
    }


def _pick_tile(t, want):
    tm = min(t, want)
    assert t % tm == 0
    return tm


def _layer(x, ada, pos, past, wts):
    b, t, _ = x.shape
    cos_t, sin_t = _rope_tables(pos)
    tm = _pick_tile(t, 512)
    if past is None:
        q, ckv, kr, qb, kb, vb, k_mla, vt_mla = _mixer_inputs(x, ada, cos_t, sin_t, wts, tm=tm, expand_kv=True)
        o_a = _mla_prompt(q, k_mla, vt_mla, blk=_pick_tile(t, 512))
        o_b = _sb_prompt(qb, kb, vb, tq=_pick_tile(t, 256), nsub=2 if t % 512 == 0 else 1)
        conv_state = jnp.zeros((b, CONV_W - 1, 2 * D_FF), F32)
    else:
        past_ckv, past_kr, past_k, past_v, conv_state = past
        past_len = past_ckv.shape[1]
        q, ckv, kr, qb, kb, vb = _mixer_inputs(x, ada, cos_t, sin_t, wts, tm=tm, expand_kv=False)
        o_a = _mla_sample(q, ckv, kr, past_ckv, past_kr, wts["w_abs"], wts["w_uv_heads"],
                          past_len=past_len, kc=_pick_tile(past_len, 1024))
        o_b = _sb_sample(qb, kb, vb, past_k, past_v, tk=_pick_tile(past_len, 256), recent=512)
    x1 = _merge(x, ada, o_a, o_b, wts, tm=tm)
    y, new_conv = _ffn(x1, ada, conv_state, wts, tm=_pick_tile(t, 256))
    state = (ckv, kr, kb.reshape(b, t, H_B, D_HB), vb.reshape(b, t, H_B, D_HB), new_conv)
    return y, state


def kernel(x_prompt, x_sample, cache_mla_ckv, cache_mla_krope, cache_sb_k, cache_sb_v, state_ffn_conv,
           c_prompt, c_sample, w_ada, b_ada, g_pre_mix, g_post_mix, g_pre_ffn, g_post_ffn,
           w_in, g_q_lat, w_uq, g_kv_lat, w_uk, w_uv, w_proj_a, w_proj_b, w_out,
           w_up, conv_w, conv_b, w_down):
    depth = w_in.shape[0]
    nb_p = x_prompt.shape[0]
    past_len = cache_mla_ckv.shape[2]
    pos_p = jnp.arange(x_prompt.shape[1], dtype=jnp.int32)
    pos_s = past_len + jnp.arange(x_sample.shape[1], dtype=jnp.int32)
    xp, xs = x_prompt, x_sample
    c_all = jnp.concatenate([c_prompt, c_sample], axis=0)
    st_p = [[] for _ in range(5)]
    st_s = [[] for _ in range(5)]
    for l in range(depth):
        wts = _layer_weights(l, g_pre_mix, g_post_mix, g_pre_ffn, g_post_ffn, w_in, g_q_lat, w_uq, g_kv_lat,
                             w_uk, w_uv, w_proj_a, w_proj_b, w_out, w_up, conv_w, conv_b, w_down)
        ada = _ada(c_all, w_ada[l], b_ada[l]).reshape(-1, 6, D_MODEL)
        xp, sp = _layer(xp, ada[:nb_p], pos_p, None, wts)
        past = (cache_mla_ckv[l], cache_mla_krope[l], cache_sb_k[l], cache_sb_v[l], state_ffn_conv[l])
        xs, ss = _layer(xs, ada[nb_p:], pos_s, past, wts)
        for i in range(5):
            st_p[i].append(sp[i])
            st_s[i].append(ss[i])
    p_state = [jnp.stack(a, axis=0) for a in st_p]
    s_state = [jnp.stack(a, axis=0) for a in st_s]
    return (xp, xs, *p_state, *s_state)
```

```python
import functools
import math

import numpy as np
import jax
import jax.numpy as jnp
from jax import lax
from jax.experimental import pallas as pl
from jax.experimental.pallas import tpu as pltpu

D_MODEL = 1024
CHUNK = 64
CHUNK_SHIFT = 6
H_A = 8
NOPE = 64
ROPE = 32
DV = 64
Q_RANK = 384
KV_RANK = 256
ROPE_BASE = 10000.0
H_B = 8
D_HB = 64
D_FF = 2816
CONV_W = 3
EPS = 1e-6
NEG = -1e30
MLA_SCALE = (NOPE + ROPE) ** -0.5
SB_SCALE = D_HB ** -0.5
LOG2E = math.log2(math.e)

LANES = 128
SUBLANES = 8
HEAD_PAD = LANES
VT_ROWS = 80
SB_DEAD = 110.0
SB_AHEAD = 2
VMEM_LIMIT = 56 * 1024 * 1024

F32 = jnp.float32
BF16 = jnp.bfloat16


def _dot(a, b):
    return jnp.dot(a, b, preferred_element_type=F32)


def _dot_t(a, b):
    return lax.dot_general(a, b, (((1,), (1,)), ((), ())), preferred_element_type=F32)


def _rms(x, g):
    return x * lax.rsqrt(jnp.mean(x * x, axis=-1, keepdims=True) + EPS) * g


def _sigmoid(x):
    return 1.0 / (1.0 + jnp.exp(-x))


def _params(n_parallel, n_arbitrary=0):
    return pltpu.CompilerParams(
        dimension_semantics=("parallel",) * n_parallel + ("arbitrary",) * n_arbitrary,
        vmem_limit_bytes=VMEM_LIMIT)


def _const_spec(shape):
    n = len(shape)
    return pl.BlockSpec(shape, lambda *_: (0,) * n, pipeline_mode=pl.Buffered(1))


def _ada_kernel(c_ref, w_ref, b_ref, o_ref):
    c = c_ref[...]
    s = (c * _sigmoid(c)).astype(BF16)
    o_ref[...] = _dot(s, w_ref[...].astype(BF16)) + b_ref[...]


def _ada(c_all, w_ada, b_ada):
    n = c_all.shape[0]
    nchunk = 6
    return pl.pallas_call(
        _ada_kernel,
        grid=(nchunk,),
        in_specs=[pl.BlockSpec((n, D_MODEL), lambda j: (0, 0)),
                  pl.BlockSpec((D_MODEL, D_MODEL), lambda j: (0, j)),
                  pl.BlockSpec((1, D_MODEL), lambda j: (0, j))],
        out_specs=pl.BlockSpec((n, D_MODEL), lambda j: (0, j)),
        out_shape=jax.ShapeDtypeStruct((n, 6 * D_MODEL), F32),
        compiler_params=_params(1),
        name="ada",
    )(c_all, w_ada, b_ada.reshape(1, -1))


_C_QLAT = 0
_C_CKV = _C_QLAT + Q_RANK
_C_KR = _C_CKV + KV_RANK
_C_KRR = _C_KR + LANES
_C_QB = _C_KRR + LANES
_C_KB = _C_QB + H_B * D_HB
_C_VB = _C_KB + H_B * D_HB
_N_K1 = _C_VB + H_B * D_HB


def _premix(x, ada_ref, g):
    sh = ada_ref[0, 0:1, :]
    sc = ada_ref[0, 1:2, :]
    return _rms(x, g) * (1.0 + sc) + sh


def _row_parts(tm):
    n = 2 if tm % (2 * 128) == 0 else 1
    return [slice(i * (tm // n), (i + 1) * (tm // n)) for i in range(n)]


def _mixer_kernel(x_ref, ada_ref, g_ref, win_ref, gq_ref, wuq_ref, gkv_ref, wuk_ref, wuv_ref, vone_ref,
                  cos_ref, sin_ref, *out_refs, expand_kv):
    if expand_kv:
        q_ref, ckv_ref, kr_ref, qb_ref, kb_ref, vb_ref, k_ref, vt_ref = out_refs
    else:
        q_ref, ckv_ref, kr_ref, qb_ref, kb_ref, vb_ref = out_refs
    parts = _row_parts(x_ref.shape[1])
    nq = H_A * HEAD_PAD
    ps = [_dot(_premix(x_ref[0, rows, :], ada_ref, g_ref[...]).astype(BF16), win_ref[...]) for rows in parts]
    q_lats = [_rms(p[:, _C_QLAT:_C_QLAT + Q_RANK], gq_ref[...]).astype(BF16) for p in ps]
    c_kvs = [_rms(p[:, _C_CKV:_C_CKV + KV_RANK], gkv_ref[...]) for p in ps]
    q2s = [_dot(q_lat, wuq_ref[...]) for q_lat in q_lats]
    if expand_kv:
        c_bfs = [c_kv.astype(BF16) for c_kv in c_kvs]
        k2s = [_dot(c_bf, wuk_ref[...]) for c_bf in c_bfs]
        v2s = [_dot(c_bf, wuv_ref[...]) for c_bf in c_bfs]
    for i, rows in enumerate(parts):
        p, q2 = ps[i], q2s[i]
        cos = cos_ref[rows, :]
        sin = sin_ref[rows, :]
        for hd in range(H_A):
            lo = hd * HEAD_PAD
            qh = q2[:, lo:lo + HEAD_PAD] * cos + q2[:, nq + lo:nq + lo + HEAD_PAD] * sin
            q_ref[0, rows, lo:lo + HEAD_PAD] = (qh * (MLA_SCALE * LOG2E)).astype(BF16)
        ckv_ref[0, rows, :] = c_kvs[i]
        kr = p[:, _C_KR:_C_KR + LANES] * cos + p[:, _C_KRR:_C_KRR + LANES] * sin
        kr_ref[0, rows, :] = kr[:, :ROPE]
        qb_ref[0, rows, :] = (p[:, _C_QB:_C_QB + H_B * D_HB] * SB_SCALE).astype(BF16)
        kb_ref[0, rows, :] = p[:, _C_KB:_C_KB + H_B * D_HB]
        vb_ref[0, rows, :] = p[:, _C_VB:_C_VB + H_B * D_HB]
        if expand_kv:
            for hd in range(H_A):
                lo = hd * HEAD_PAD
                k_ref[0, rows, lo:lo + HEAD_PAD] = (k2s[i][:, lo:lo + HEAD_PAD] + kr).astype(BF16)
            vt_ref[0, :, rows] = (v2s[i] + vone_ref[...]).T.astype(BF16)


def _mixer_inputs(x, ada, cos_t, sin_t, wts, *, tm, expand_kv):
    b, t, _ = x.shape
    nt = t // tm
    tok = lambda n: pl.BlockSpec((1, tm, n), lambda i, j: (i, j, 0))
    in_specs = [tok(D_MODEL),
                pl.BlockSpec((1, 6, D_MODEL), lambda i, j: (i, 0, 0)),
                _const_spec((1, D_MODEL)),
                _const_spec((D_MODEL, _N_K1)),
                _const_spec((1, Q_RANK)),
                _const_spec((Q_RANK, 2 * H_A * HEAD_PAD)),
                _const_spec((1, KV_RANK)),
                _const_spec((KV_RANK, H_A * HEAD_PAD)),
                _const_spec((KV_RANK, H_A * VT_ROWS)),
                _const_spec((1, H_A * VT_ROWS)),
                pl.BlockSpec((tm, LANES), lambda i, j: (j, 0)),
                pl.BlockSpec((tm, LANES), lambda i, j: (j, 0))]
    shapes = [((b, t, H_A * HEAD_PAD), BF16), ((b, t, KV_RANK), F32), ((b, t, ROPE), F32),
              ((b, t, H_B * D_HB), BF16), ((b, t, H_B * D_HB), F32), ((b, t, H_B * D_HB), F32)]
    if expand_kv:
        shapes += [((b, t, H_A * HEAD_PAD), BF16)]
    out_specs = [tok(s[-1]) for s, _ in shapes]
    if expand_kv:
        shapes += [((b, H_A * VT_ROWS, t), BF16)]
        out_specs += [pl.BlockSpec((1, H_A * VT_ROWS, tm), lambda i, j: (i, 0, j))]
    return pl.pallas_call(
        functools.partial(_mixer_kernel, expand_kv=expand_kv),
        grid=(b, nt),
        in_specs=in_specs,
        out_specs=out_specs,
        out_shape=[jax.ShapeDtypeStruct(s, d) for s, d in shapes],
        compiler_params=_params(2),
        name="mixer_in_kv" if expand_kv else "mixer_in",
    )(x, ada, wts["g_pre_mix"], wts["w_in_k1"], wts["g_q_lat"], wts["w_uq_ext"], wts["g_kv_lat"],
      wts["w_uk_pad"], wts["w_uv_t"], wts["v_ones"], cos_t, sin_t)


MLA_PAIR = 2


def _mla_prompt_kernel(q_ref, k_ref, vt_ref, bias_ref, o_ref, sa_ref, sb_ref, m_ref, acc_ref, *, blk):
    qi = pl.program_id(2)
    lanes = [slice(hh * HEAD_PAD, (hh + 1) * HEAD_PAD) for hh in range(2)]
    m_ref[...] = jnp.full(m_ref.shape, NEG, F32)
    acc_ref[...] = jnp.zeros(acc_ref.shape, F32)

    def start_of(j):
        return pl.multiple_of(j * blk, blk)

    def scores(s_ref, g):
        for hh, ln in enumerate(lanes):
            for b in range(MLA_PAIR):
                k = k_ref[0, pl.ds(start_of(MLA_PAIR * g + b), blk), ln]
                s_ref[hh, b] = _dot_t(k, q_ref[0, :, ln])

    def softmax_pv(s_ref, g, masked):
        for hh in range(2):
            ss = []
            for b in range(MLA_PAIR):
                s = s_ref[hh, b]
                ss.append(s + bias_ref[b] if masked else s)
            m_old = m_ref[hh]
            m_new = m_old
            for s in ss:
                m_new = jnp.maximum(m_new, jnp.max(s, axis=0, keepdims=True))
            acc = jnp.exp2(m_old - m_new) * acc_ref[hh]
            for b, s in enumerate(ss):
                p = jnp.exp2(s - m_new).astype(BF16)
                vt = vt_ref[0, hh * VT_ROWS:(hh + 1) * VT_ROWS, pl.ds(start_of(MLA_PAIR * g + b), blk)]
                acc = acc + _dot(vt, p)
            acc_ref[hh] = acc
            m_ref[hh] = m_new

    scores(sa_ref, 0)

    def body(h, c):
        scores(sb_ref, 2 * h + 1)
        softmax_pv(sa_ref, 2 * h, False)
        scores(sa_ref, 2 * h + 2)
        softmax_pv(sb_ref, 2 * h + 1, False)
        return c

    done = qi // 2 * 2
    lax.fori_loop(0, qi // 2, body, 0)

    @pl.when(done == qi)
    def _():
        softmax_pv(sa_ref, done, True)

    @pl.when(done != qi)
    def _():
        scores(sb_ref, done + 1)
        softmax_pv(sa_ref, done, False)
        softmax_pv(sb_ref, done + 1, True)

    outs = [acc_ref[hh][:DV] / acc_ref[hh][DV:DV + 1] for hh in range(2)]
    o_ref[0] = jnp.concatenate(outs, axis=0).T.astype(BF16)


def _mla_diag_bias(blk, tq):
    kpos = np.arange(MLA_PAIR * blk).reshape(MLA_PAIR, blk, 1)
    qpos = np.arange(tq).reshape(1, 1, tq)
    return np.where(kpos // CHUNK <= qpos // CHUNK, 0.0, NEG).astype(np.float32)


def _mla_prompt(q, k, vt, *, blk):
    b, t, _ = q.shape
    tq = MLA_PAIR * blk
    assert t % tq == 0
    return pl.pallas_call(
        functools.partial(_mla_prompt_kernel, blk=blk),
        grid=(b, H_A // 2, t // tq),
        in_specs=[pl.BlockSpec((1, tq, 2 * HEAD_PAD), lambda i, j, n: (i, n, j)),
                  pl.BlockSpec((1, t, 2 * HEAD_PAD), lambda i, j, n: (i, 0, j)),
                  pl.BlockSpec((1, 2 * VT_ROWS, t), lambda i, j, n: (i, j, 0)),
                  _const_spec((MLA_PAIR, blk, tq))],
        out_specs=pl.BlockSpec((1, tq, 2 * DV), lambda i, j, n: (i, n, j)),
        out_shape=jax.ShapeDtypeStruct((b, t, H_A * DV), BF16),
        scratch_shapes=[pltpu.VMEM((2, MLA_PAIR, blk, tq), F32), pltpu.VMEM((2, MLA_PAIR, blk, tq), F32),
                        pltpu.VMEM((2, 1, tq), F32), pltpu.VMEM((2, VT_ROWS, tq), F32)],
        compiler_params=_params(3),
        name="mla_prompt",
    )(q, k, vt, jnp.asarray(_mla_diag_bias(blk, tq)))


def _tri(n):
    r = lax.broadcasted_iota(jnp.int32, (n, n), 0)
    c = lax.broadcasted_iota(jnp.int32, (n, n), 1)
    return jnp.where(r > c, 1.0, 0.0).astype(BF16)


def _sb_blocks(qs, blocks, r_prev):
    zs = [_dot_t(qs, k) for k, _, _, _ in blocks]
    sps = []
    for z, (_, _, _, keep) in zip(zs, blocks):
        sp = jnp.maximum(z, 0.0) + jnp.log(1.0 + jnp.exp2(jnp.abs(z) * (-LOG2E)))
        sps.append(sp if keep is None else jnp.where(keep, sp, 0.0))
    laters = [_dot(sp.astype(BF16), tri) for sp, (_, _, tri, _) in zip(sps, blocks)]
    out = None
    for z, sp, later, (k, v, _, keep) in zip(zs, sps, laters, blocks):
        tk = k.shape[0]
        w = z - sp - later
        if r_prev is not None:
            r_b = r_prev
            if r_b.shape[1] == LANES and tk > LANES:
                r_b = jnp.concatenate([r_b] * (tk // LANES), axis=1)
            elif r_b.shape[1] == LANES and tk < LANES:
                r_b = r_b[:, :tk]
            w = w - r_b
        a = jnp.exp2(w * LOG2E)
        if keep is not None:
            a = jnp.where(keep, a, 0.0)
        o = _dot(a.astype(BF16), v)
        out = o if out is None else out + o
        rs = jnp.sum(sp, axis=-1, keepdims=True)
        r_prev = rs if r_prev is None else r_prev + rs
    return out, r_prev


def _head_mask(x, hh):
    lane = lax.broadcasted_iota(jnp.int32, x.shape, 1)
    mine = jnp.logical_and(lane >= hh * D_HB, lane < (hh + 1) * D_HB)
    return jnp.where(mine, x, jnp.zeros_like(x))


def _stack_heads(q):
    return jnp.concatenate([_head_mask(q, 0), _head_mask(q, 1)], axis=0)


def _causal_pair(t, tk):
    row = lax.broadcasted_iota(jnp.int32, (t, tk), 0)
    col = lax.broadcasted_iota(jnp.int32, (t, tk), 1)
    keep = col < row
    return jnp.concatenate([keep, keep], axis=0)


def _sb_tail(qs, load, tri, acc_ref, r_ref, j_start):
    def cond(c):
        j, rmin = c
        return jnp.logical_and(j >= 0, rmin < SB_DEAD)

    def body(c):
        j, _ = c
        k, v = load(j)
        o, r_new = _sb_blocks(qs, [(k, v, tri, None)], r_ref[...])
        acc_ref[...] += o
        r_ref[...] = r_new
        return j - 1, jnp.min(r_new)

    lax.while_loop(cond, body, (j_start, jnp.min(r_ref[...])))


def _unstack_heads(acc, t):
    lane = lax.broadcasted_iota(jnp.int32, (t, LANES), 1)
    return jnp.where(lane < D_HB, acc[:t], acc[t:])


def _sb_prompt_kernel(q_ref, k_ref, v_ref, o_ref, acc_ref, r_ref, *, tq, nsub):
    qi = pl.program_id(2)
    tri = _tri(tq)
    causal = _causal_pair(tq, tq)

    def load(j):
        start = pl.multiple_of(j * tq, tq)
        return (k_ref[0, pl.ds(start, tq), :].astype(BF16),
                v_ref[0, pl.ds(start, tq), :].astype(BF16))

    qss = []
    for sub in range(nsub):
        blk = qi * nsub + sub
        qs = _stack_heads(q_ref[0, sub * tq:(sub + 1) * tq, :])
        blocks = [load(blk) + (tri, causal)]
        for back in range(1, SB_AHEAD + 1):
            blocks.append(load(jnp.maximum(blk - back, 0)) + (tri, blk >= back))
        o, r = _sb_blocks(qs, blocks, None)
        acc_ref[sub] = o
        r_ref[sub] = jnp.broadcast_to(r, (2 * tq, LANES))
        qss.append(qs)
    for sub in range(nsub):
        _sb_tail(qss[sub], load, tri, acc_ref.at[sub], r_ref.at[sub], qi * nsub + sub - SB_AHEAD - 1)
        o_ref[0, sub * tq:(sub + 1) * tq, :] = _unstack_heads(acc_ref[sub], tq).astype(BF16)


def _sb_prompt(q, k, v, *, tq, nsub):
    b, t, _ = q.shape
    rows = tq * nsub
    return pl.pallas_call(
        functools.partial(_sb_prompt_kernel, tq=tq, nsub=nsub),
        grid=(b, H_B // 2, t // rows),
        in_specs=[pl.BlockSpec((1, rows, LANES), lambda i, j, n: (i, n, j)),
                  pl.BlockSpec((1, t, LANES), lambda i, j, n: (i, 0, j)),
                  pl.BlockSpec((1, t, LANES), lambda i, j, n: (i, 0, j))],
        out_specs=pl.BlockSpec((1, rows, LANES), lambda i, j, n: (i, n, j)),
        out_shape=jax.ShapeDtypeStruct((b, t, H_B * D_HB), BF16),
        scratch_shapes=[pltpu.VMEM((nsub, 2 * tq, LANES), F32), pltpu.VMEM((nsub, 2 * tq, LANES), F32)],
        compiler_params=_params(3),
        name="sb_prompt",
    )(q, k, v)


def _sb_sample_kernel(q_ref, kn_ref, vn_ref, kc_ref, vc_ref, o_ref, left_ref, acc_ref, r_ref, *, tk):
    tq = q_ref.shape[1]
    nblk = kc_ref.shape[1] // tk
    qs = _stack_heads(q_ref[0])
    tri = _tri(tk)

    def load(j):
        start = pl.multiple_of(j * tk, tk)
        return (kc_ref[0, pl.ds(start, tk), :].astype(BF16),
                vc_ref[0, pl.ds(start, tk), :].astype(BF16))

    k_p, v_p = load(nblk - 1)
    o, r = _sb_blocks(qs, [(kn_ref[0].astype(BF16), vn_ref[0].astype(BF16), _tri(tq), _causal_pair(tq, tq)),
                           (k_p, v_p, tri, None)], None)
    acc_ref[...] = o
    r_ref[...] = jnp.broadcast_to(r, (2 * tq, LANES))
    _sb_tail(qs, load, tri, acc_ref, r_ref, jnp.int32(nblk - 2))
    o_ref[0] = _unstack_heads(acc_ref[...], tq).astype(BF16)
    left_ref[...] = jnp.full(left_ref.shape, jnp.min(r_ref[...]), F32)


def _sb_sample_call(q, k_new, v_new, k_cache, v_cache, *, tk):
    b, t, _ = q.shape
    past = k_cache.shape[1]
    new = pl.BlockSpec((1, t, LANES), lambda i, j: (i, 0, j))
    old = pl.BlockSpec((1, past, LANES), lambda i, j: (i, 0, j))
    return pl.pallas_call(
        functools.partial(_sb_sample_kernel, tk=tk),
        grid=(b, H_B // 2),
        in_specs=[new, new, new, old, old],
        out_specs=[new, pl.BlockSpec((1, 1, SUBLANES, LANES), lambda i, j: (i, j, 0, 0))],
        out_shape=[jax.ShapeDtypeStruct((b, t, H_B * D_HB), BF16),
                   jax.ShapeDtypeStruct((b, H_B // 2, SUBLANES, LANES), F32)],
        scratch_shapes=[pltpu.VMEM((2 * t, LANES), F32), pltpu.VMEM((2 * t, LANES), F32)],
        compiler_params=_params(2),
        name="sb_sample",
    )(q, k_new, v_new, k_cache, v_cache)


def _sb_sample(q, k_new, v_new, k_cache, v_cache, *, tk, recent):
    b, past = k_cache.shape[:2]
    flat = lambda c: c.reshape(b, c.shape[1], H_B * D_HB)
    recent = min(recent, past)
    o, left = _sb_sample_call(q, k_new, v_new, flat(k_cache[:, past - recent:]), flat(v_cache[:, past - recent:]),
                              tk=tk)
    if recent == past:
        return o
    return lax.cond(jnp.min(left) < SB_DEAD,
                    lambda: _sb_sample_call(q, k_new, v_new, flat(k_cache), flat(v_cache), tk=tk)[0],
                    lambda: o)


def _mla_sample_kernel(q_ref, cn_ref, rn_ref, cc_ref, rc_ref, wabs_ref, wuv_ref, o_ref, *, kc, past_len):
    t = q_ref.shape[1]
    past = cc_ref.shape[1]
    qa, qr = [], []
    for hd in range(H_A):
        qh = q_ref[0, :, hd * HEAD_PAD:(hd + 1) * HEAD_PAD]
        qa.append(_dot(qh, wabs_ref[hd]).astype(BF16))
        qr.append(qh[:, :ROPE])
    qa = jnp.concatenate(qa, axis=0)
    qr = jnp.concatenate(qr, axis=0)
    rows = H_A * t

    pieces = [(cc_ref[0, c * kc:(c + 1) * kc, :], rc_ref[0, c * kc:(c + 1) * kc, :], c * kc)
              for c in range(past // kc)]
    pieces.append((cn_ref[0], rn_ref[0], past_len))
    scores, lat = [], []
    for ckv, kr, k0 in pieces:
        ckv = ckv.astype(BF16)
        s = _dot_t(qa, ckv) + _dot_t(qr, kr.astype(BF16))
        n = ckv.shape[0]
        if (k0 + n - 1) // CHUNK > past_len // CHUNK:
            qpos = past_len + lax.broadcasted_iota(jnp.int32, (rows, n), 0) % t
            kpos = k0 + lax.broadcasted_iota(jnp.int32, (rows, n), 1)
            s = jnp.where((kpos >> CHUNK_SHIFT) <= (qpos >> CHUNK_SHIFT), s, NEG)
        scores.append(s)
        lat.append(ckv)
    m = functools.reduce(jnp.maximum, [jnp.max(s, axis=-1, keepdims=True) for s in scores])
    l = jnp.zeros((rows, 1), F32)
    o_lat = jnp.zeros((rows, KV_RANK), F32)
    for s, ckv in zip(scores, lat):
        p = jnp.exp2(s - m)
        l = l + jnp.sum(p, axis=-1, keepdims=True)
        o_lat = o_lat + _dot(p.astype(BF16), ckv)
    o_lat = (o_lat / l).astype(BF16)
    o = jnp.zeros((t, H_A * DV), F32)
    for hd in range(H_A):
        o = o + _dot(o_lat[hd * t:(hd + 1) * t], wuv_ref[hd])
    o_ref[0] = o.astype(BF16)


def _mla_sample(q, ckv_new, kr_new, ckv_cache, kr_cache, w_abs, w_uv_heads, *, past_len, kc):
    b, t, _ = q.shape
    past = ckv_cache.shape[1]
    row = lambda n, s: pl.BlockSpec((1, n, s), lambda i: (i, 0, 0))
    return pl.pallas_call(
        functools.partial(_mla_sample_kernel, kc=kc, past_len=past_len),
        grid=(b,),
        in_specs=[row(t, H_A * HEAD_PAD), row(t, KV_RANK), row(t, ROPE),
                  row(past, KV_RANK), row(past, ROPE),
                  _const_spec((H_A, HEAD_PAD, KV_RANK)), _const_spec((H_A, KV_RANK, H_A * DV))],
        out_specs=row(t, H_A * DV),
        out_shape=jax.ShapeDtypeStruct((b, t, H_A * DV), BF16),
        compiler_params=_params(1),
        name="mla_sample",
    )(q, ckv_new, kr_new, ckv_cache, kr_cache, w_abs, w_uv_heads)


def _merge_kernel(x_ref, ada_ref, g_ref, wg_ref, oa_ref, ob_ref, wpa_ref, wpb_ref, wo_ref, gpost_ref, o_ref):
    parts = _row_parts(x_ref.shape[1])
    xs = [x_ref[0, rows, :] for rows in parts]
    gate_in = [_dot(_premix(x, ada_ref, g_ref[...]).astype(BF16), wg_ref[...]) for x in xs]
    pas = [_dot(oa_ref[0, rows, :], wpa_ref[...]) for rows in parts]
    pbs = [_dot(ob_ref[0, rows, :], wpb_ref[...]) for rows in parts]
    merged = [(_sigmoid(gi[:, :D_MODEL]) * pa + _sigmoid(gi[:, D_MODEL:]) * pb).astype(BF16)
              for gi, pa, pb in zip(gate_in, pas, pbs)]
    mos = [_dot(m, wo_ref[...]) for m in merged]
    gt1 = ada_ref[0, 2:3, :]
    for rows, x, mo in zip(parts, xs, mos):
        o_ref[0, rows, :] = x + gt1 * _rms(mo, gpost_ref[...])


def _merge(x, ada, o_a, o_b, wts, *, tm):
    b, t, _ = x.shape
    tok = lambda n: pl.BlockSpec((1, tm, n), lambda i, j: (i, j, 0))
    return pl.pallas_call(
        _merge_kernel,
        grid=(b, t // tm),
        in_specs=[tok(D_MODEL),
                  pl.BlockSpec((1, 6, D_MODEL), lambda i, j: (i, 0, 0)),
                  _const_spec((1, D_MODEL)),
                  _const_spec((D_MODEL, 2 * D_MODEL)),
                  tok(H_A * DV), tok(H_B * D_HB),
                  _const_spec((H_A * DV, D_MODEL)), _const_spec((H_B * D_HB, D_MODEL)),
                  _const_spec((D_MODEL, D_MODEL)),
                  _const_spec((1, D_MODEL))],
        out_specs=tok(D_MODEL),
        out_shape=jax.ShapeDtypeStruct((b, t, D_MODEL), F32),
        compiler_params=_params(2),
        name="merge",
    )(x, ada, wts["g_pre_mix"], wts["w_gate"], o_a, o_b, wts["w_proj_a"], wts["w_proj_b"],
      wts["w_out"], wts["g_post_mix"])


FF_CHUNK = 256
SUB_GROUPS = 4
SUB_ROWS = SUB_GROUPS * SUBLANES
HALO = 2 * SUBLANES


def _gelu_tanh(a):
    return 0.5 * a * (1.0 + jnp.tanh(math.sqrt(2.0 / math.pi) * (a + 0.044715 * (a * a * a))))


def _ffn_kernel(x_ref, ada_ref, g_ref, wup_ref, cw_ref, cb_ref, wdn_ref, gpost_ref, cs_ref,
                o_ref, nc_ref, halo_ref, perm_ref, *, tm):
    ti = pl.program_id(1)
    nt = pl.num_programs(1)
    nsub = tm // SUB_ROWS

    @pl.when(ti == 0)
    def _():
        halo_ref[...] = jnp.zeros((HALO, 2 * D_FF), F32)
        halo_ref[SUBLANES - 1:SUBLANES, :] = cs_ref[0, 0:1, :]
        halo_ref[HALO - 1:HALO, :] = cs_ref[0, 1:2, :]

    nc = D_MODEL // LANES
    for c in range(nc):
        perm_ref[c] = x_ref[0, :, c * LANES:(c + 1) * LANES]
    x = jnp.concatenate(
        [jnp.concatenate([perm_ref[c, pl.ds(j * SUB_ROWS + i, SUBLANES, stride=SUB_GROUPS), :] for c in range(nc)],
                         axis=1)
         for j in range(nsub) for i in range(SUB_GROUPS)], axis=0)
    sh = ada_ref[0, 3:4, :]
    sc = ada_ref[0, 4:5, :]
    gt2 = ada_ref[0, 5:6, :]
    h2 = (_rms(x, g_ref[...]) * (1.0 + sc) + sh).astype(BF16)
    first = lax.broadcasted_iota(jnp.int32, (SUBLANES, FF_CHUNK), 0) == 0

    def up(col):
        cols = slice(col, col + FF_CHUNK)
        u = _dot(h2, wup_ref[:, cols])
        grp = [u[g * SUBLANES:(g + 1) * SUBLANES] for g in range(tm // SUBLANES)]
        rot2 = [pltpu.roll(halo_ref[0:SUBLANES, cols], 1, axis=0)]
        rot3 = [pltpu.roll(halo_ref[SUBLANES:HALO, cols], 1, axis=0)]
        for j in range(nsub):
            rot2.append(pltpu.roll(grp[j * SUB_GROUPS + SUB_GROUPS - 2], 1, axis=0))
            rot3.append(pltpu.roll(grp[j * SUB_GROUPS + SUB_GROUPS - 1], 1, axis=0))
        halo_ref[:, cols] = u[tm - HALO:]
        u1, u2 = [], []
        for j in range(nsub):
            f1 = jnp.where(first, rot3[j], rot3[j + 1])
            f2 = jnp.where(first, rot2[j], rot2[j + 1])
            g0 = j * SUB_GROUPS
            u1 += [f1] + grp[g0:g0 + SUB_GROUPS - 1]
            u2 += [f2, f1] + grp[g0:g0 + SUB_GROUPS - 2]
        u1 = jnp.concatenate(u1, axis=0)
        u2 = jnp.concatenate(u2, axis=0)
        return (cb_ref[:, cols] + cw_ref[0:1, cols] * u2 + cw_ref[1:2, cols] * u1 + cw_ref[2:3, cols] * u)

    nchunk = D_FF // FF_CHUNK
    acc = jnp.zeros((tm, D_MODEL), F32)
    ya, yb = up(0), up(D_FF)
    for c in range(nchunk):
        if c + 1 < nchunk:
            ya_next, yb_next = up((c + 1) * FF_CHUNK), up(D_FF + (c + 1) * FF_CHUNK)
        g = (_gelu_tanh(ya) * yb).astype(BF16)
        acc = acc + _dot(g, wdn_ref[c * FF_CHUNK:(c + 1) * FF_CHUNK, :])
        if c + 1 < nchunk:
            ya, yb = ya_next, yb_next
    y = x + gt2 * _rms(acc, gpost_ref[...])
    for j in range(nsub):
        for i in range(SUB_GROUPS):
            g0 = (j * SUB_GROUPS + i) * SUBLANES
            for c in range(nc):
                perm_ref[c, pl.ds(j * SUB_ROWS + i, SUBLANES, stride=SUB_GROUPS), :] = (
                    y[g0:g0 + SUBLANES, c * LANES:(c + 1) * LANES])
    for c in range(nc):
        o_ref[0, :, c * LANES:(c + 1) * LANES] = perm_ref[c]

    @pl.when(ti == nt - 1)
    def _():
        nc_ref[0, 0:1, :] = halo_ref[SUBLANES - 1:SUBLANES, :]
        nc_ref[0, 1:2, :] = halo_ref[HALO - 1:HALO, :]


def _ffn(x, ada, conv_state, wts, *, tm):
    b, t, _ = x.shape
    assert tm % SUB_ROWS == 0 and t % tm == 0
    tok = pl.BlockSpec((1, tm, D_MODEL), lambda i, j: (i, j, 0))
    state = pl.BlockSpec((1, CONV_W - 1, 2 * D_FF), lambda i, j: (i, 0, 0))
    return pl.pallas_call(
        functools.partial(_ffn_kernel, tm=tm),
        grid=(b, t // tm),
        in_specs=[tok,
                  pl.BlockSpec((1, 6, D_MODEL), lambda i, j: (i, 0, 0)),
                  _const_spec((1, D_MODEL)),
                  _const_spec((D_MODEL, 2 * D_FF)),
                  _const_spec((CONV_W, 2 * D_FF)),
                  _const_spec((1, 2 * D_FF)),
                  _const_spec((D_FF, D_MODEL)),
                  _const_spec((1, D_MODEL)),
                  state],
        out_specs=[tok, state],
        out_shape=[jax.ShapeDtypeStruct((b, t, D_MODEL), F32),
                   jax.ShapeDtypeStruct((b, CONV_W - 1, 2 * D_FF), F32)],
        scratch_shapes=[pltpu.VMEM((HALO, 2 * D_FF), F32), pltpu.VMEM((D_MODEL // LANES, tm, LANES), F32)],
        compiler_params=_params(1, 1),
        name="conv_ffn",
    )(x, ada, wts["g_pre_ffn"], wts["w_up"], wts["conv_w"], wts["conv_b"], wts["w_down"],
      wts["g_post_ffn"], conv_state)


def _rope_tables(pos):
    inv = ROPE_BASE ** (-jnp.arange(0, ROPE, 2, dtype=F32) / ROPE)
    ang = pos.astype(F32)[:, None] * inv[None, :]
    cos, sin = jnp.cos(ang), jnp.sin(ang)
    t = pos.shape[0]
    cos_t = jnp.concatenate([cos, cos, jnp.ones((t, NOPE), F32), jnp.zeros((t, HEAD_PAD - ROPE - NOPE), F32)], axis=1)
    sin_t = jnp.concatenate([-sin, sin, jnp.zeros((t, HEAD_PAD - ROPE), F32)], axis=1)
    return cos_t, sin_t


def _swap_halves(w):
    return jnp.concatenate([w[..., ROPE // 2:], w[..., :ROPE // 2]], axis=-1)


def _pad_lanes(w, n):
    return jnp.pad(w, [(0, 0)] * (w.ndim - 1) + [(0, n - w.shape[-1])])


def _layer_weights(l, g_pre_mix, g_post_mix, g_pre_ffn, g_post_ffn, w_in, g_q_lat, w_uq, g_kv_lat,
                   w_uk, w_uv, w_proj_a, w_proj_b, w_out, w_up, conv_w, conv_b, w_down):
    splits = np.cumsum([Q_RANK, KV_RANK, ROPE, H_B * D_HB, H_B * D_HB, H_B * D_HB, D_MODEL])
    wi = w_in[l]
    w_ql, w_ckv, w_kr, w_qb, w_kb, w_vb, w_ga, w_gb = jnp.split(wi, splits, axis=1)
    w_in_k1 = jnp.concatenate([w_ql, w_ckv, _pad_lanes(w_kr, LANES), _pad_lanes(_swap_halves(w_kr), LANES),
                               w_qb, w_kb, w_vb], axis=1).astype(BF16)
    wq = w_uq[l].reshape(Q_RANK, H_A, NOPE + ROPE)
    wq_nope, wq_rope = wq[..., :NOPE], wq[..., NOPE:]
    zeros = jnp.zeros((Q_RANK, H_A, HEAD_PAD - ROPE - NOPE), F32)
    q_main = jnp.concatenate([wq_rope, wq_nope, zeros], axis=-1).reshape(Q_RANK, H_A * HEAD_PAD)
    q_rot = _pad_lanes(_swap_halves(wq_rope), HEAD_PAD).reshape(Q_RANK, H_A * HEAD_PAD)
    wk = w_uk[l].reshape(KV_RANK, H_A, NOPE)
    w_uk_pad = jnp.pad(wk, [(0, 0), (0, 0), (ROPE, HEAD_PAD - ROPE - NOPE)])
    w_abs = jnp.transpose(w_uk_pad, (1, 2, 0))
    w_uv_t = _pad_lanes(w_uv[l].reshape(KV_RANK, H_A, DV), VT_ROWS)
    v_ones = np.zeros((H_A, VT_ROWS), np.float32)
    v_ones[:, DV] = 1.0
    head_of_col = jnp.arange(H_A * DV) // DV
    w_uv_heads = jnp.where(head_of_col[None, None, :] == jnp.arange(H_A)[:, None, None], w_uv[l][None], 0.0)
    row = lambda g: g[l].reshape(1, -1)
    return {
        "g_pre_mix": row(g_pre_mix), "g_post_mix": row(g_post_mix),
        "g_pre_ffn": row(g_pre_ffn), "g_post_ffn": row(g_post_ffn),
        "g_q_lat": row(g_q_lat), "g_kv_lat": row(g_kv_lat),
        "w_in_k1": w_in_k1,
        "w_gate": jnp.concatenate([w_ga, w_gb], axis=1).astype(BF16),
        "w_uq_ext": jnp.concatenate([q_main, q_rot], axis=1).astype(BF16),
        "w_uk_pad": w_uk_pad.reshape(KV_RANK, H_A * HEAD_PAD).astype(BF16),
        "w_abs": w_abs.astype(BF16),
        "w_uv_t": w_uv_t.reshape(KV_RANK, H_A * VT_ROWS).astype(BF16),
        "v_ones": jnp.asarray(v_ones.reshape(1, H_A * VT_ROWS)),
        "w_uv_heads": w_uv_heads.astype(BF16),
        "w_proj_a": w_proj_a[l].astype(BF16), "w_proj_b": w_proj_b[l].astype(BF16),
        "w_out": w_out[l].astype(BF16),
        "w_up": w_up[l].astype(BF16), "conv_w": conv_w[l], "conv_b": conv_b[l].reshape(1, -1),
        "w_down": w_down[l].astype(BF16),
    }


def _pick_tile(t, want):
    tm = min(t, want)
    assert t % tm == 0
    return tm


def _layer(x, ada, pos, past, wts):
    b, t, _ = x.shape
    cos_t, sin_t = _rope_tables(pos)
    tm = _pick_tile(t, 512)
    if past is None:
        q, ckv, kr, qb, kb, vb, k_mla, vt_mla = _mixer_inputs(x, ada, cos_t, sin_t, wts, tm=tm, expand_kv=True)
        o_a = _mla_prompt(q, k_mla, vt_mla, blk=_pick_tile(t, 512))
        o_b = _sb_prompt(qb, kb, vb, tq=_pick_tile(t, 256), nsub=2 if t % 512 == 0 else 1)
        conv_state = jnp.zeros((b, CONV_W - 1, 2 * D_FF), F32)
    else:
        past_ckv, past_kr, past_k, past_v, conv_state = past
        past_len = past_ckv.shape[1]
        q, ckv, kr, qb, kb, vb = _mixer_inputs(x, ada, cos_t, sin_t, wts, tm=tm, expand_kv=False)
        o_a = _mla_sample(q, ckv, kr, past_ckv, past_kr, wts["w_abs"], wts["w_uv_heads"],
                          past_len=past_len, kc=_pick_tile(past_len, 1024))
        o_b = _sb_sample(qb, kb, vb, past_k, past_v, tk=_pick_tile(past_len, 256), recent=512)
    x1 = _merge(x, ada, o_a, o_b, wts, tm=tm)
    y, new_conv = _ffn(x1, ada, conv_state, wts, tm=_pick_tile(t, 256))
    state = (ckv, kr, kb.reshape(b, t, H_B, D_HB), vb.reshape(b, t, H_B, D_HB), new_conv)
    return y, state


def kernel(x_prompt, x_sample, cache_mla_ckv, cache_mla_krope, cache_sb_k, cache_sb_v, state_ffn_conv,
           c_prompt, c_sample, w_ada, b_ada, g_pre_mix, g_post_mix, g_pre_ffn, g_post_ffn,
           w_in, g_q_lat, w_uq, g_kv_lat, w_uk, w_uv, w_proj_a, w_proj_b, w_out,
           w_up, conv_w, conv_b, w_down):
    depth = w_in.shape[0]
    nb_p = x_prompt.shape[0]
    past_len = cache_mla_ckv.shape[2]
    pos_p = jnp.arange(x_prompt.shape[1], dtype=jnp.int32)
    pos_s = past_len + jnp.arange(x_sample.shape[1], dtype=jnp.int32)
    xp, xs = x_prompt, x_sample
    c_all = jnp.concatenate([c_prompt, c_sample], axis=0)
    st_p = [[] for _ in range(5)]
    st_s = [[] for _ in range(5)]
    for l in range(depth):
        wts = _layer_weights(l, g_pre_mix, g_post_mix, g_pre_ffn, g_post_ffn, w_in, g_q_lat, w_uq, g_kv_lat,
                             w_uk, w_uv, w_proj_a, w_proj_b, w_out, w_up, conv_w, conv_b, w_down)
        ada = _ada(c_all, w_ada[l], b_ada[l]).reshape(-1, 6, D_MODEL)
        xp, sp = _layer(xp, ada[:nb_p], pos_p, None, wts)
        past = (cache_mla_ckv[l], cache_mla_krope[l], cache_sb_k[l], cache_sb_v[l], state_ffn_conv[l])
        xs, ss = _layer(xs, ada[nb_p:], pos_s, past, wts)
        for i in range(5):
            st_p[i].append(sp[i])
            st_s[i].append(ss[i])
    p_state = [jnp.stack(a, axis=0) for a in st_p]
    s_state = [jnp.stack(a, axis=0) for a in st_s]
    return (xp, xs, *p_state, *s_state)
```

```python
import functools
import math

import numpy as np
import jax
import jax.numpy as jnp
from jax import lax
from jax.experimental import pallas as pl
from jax.experimental.pallas import tpu as pltpu

D_MODEL = 1024
CHUNK = 64
CHUNK_SHIFT = 6
H_A = 8
NOPE = 64
ROPE = 32
DV = 64
Q_RANK = 384
KV_RANK = 256
ROPE_BASE = 10000.0
H_B = 8
D_HB = 64
D_FF = 2816
CONV_W = 3
EPS = 1e-6
NEG = -1e30
MLA_SCALE = (NOPE + ROPE) ** -0.5
SB_SCALE = D_HB ** -0.5
LOG2E = math.log2(math.e)

LANES = 128
SUBLANES = 8
HEAD_PAD = LANES
VT_ROWS = 80
SB_DEAD = 110.0
SB_AHEAD = 2
VMEM_LIMIT = 56 * 1024 * 1024

F32 = jnp.float32
BF16 = jnp.bfloat16


def _dot(a, b):
    return jnp.dot(a, b, preferred_element_type=F32)


def _dot_t(a, b):
    return lax.dot_general(a, b, (((1,), (1,)), ((), ())), preferred_element_type=F32)


def _rms(x, g):
    return x * lax.rsqrt(jnp.mean(x * x, axis=-1, keepdims=True) + EPS) * g


def _sigmoid(x):
    return 1.0 / (1.0 + jnp.exp(-x))


def _params(n_parallel, n_arbitrary=0):
    return pltpu.CompilerParams(
        dimension_semantics=("parallel",) * n_parallel + ("arbitrary",) * n_arbitrary,
        vmem_limit_bytes=VMEM_LIMIT)


def _const_spec(shape):
    n = len(shape)
    return pl.BlockSpec(shape, lambda *_: (0,) * n, pipeline_mode=pl.Buffered(1))


def _ada_kernel(c_ref, w_ref, b_ref, o_ref):
    c = c_ref[...]
    s = (c * _sigmoid(c)).astype(BF16)
    o_ref[...] = _dot(s, w_ref[...].astype(BF16)) + b_ref[...]


def _ada(c_all, w_ada, b_ada):
    n = c_all.shape[0]
    nchunk = 6
    return pl.pallas_call(
        _ada_kernel,
        grid=(nchunk,),
        in_specs=[pl.BlockSpec((n, D_MODEL), lambda j: (0, 0)),
                  pl.BlockSpec((D_MODEL, D_MODEL), lambda j: (0, j)),
                  pl.BlockSpec((1, D_MODEL), lambda j: (0, j))],
        out_specs=pl.BlockSpec((n, D_MODEL), lambda j: (0, j)),
        out_shape=jax.ShapeDtypeStruct((n, 6 * D_MODEL), F32),
        compiler_params=_params(1),
        name="ada",
    )(c_all, w_ada, b_ada.reshape(1, -1))


_C_QLAT = 0
_C_CKV = _C_QLAT + Q_RANK
_C_KR = _C_CKV + KV_RANK
_C_KRR = _C_KR + LANES
_C_QB = _C_KRR + LANES
_C_KB = _C_QB + H_B * D_HB
_C_VB = _C_KB + H_B * D_HB
_N_K1 = _C_VB + H_B * D_HB


def _mod(ada_ref, comp, rows, tm):
    nseg = ada_ref.shape[0]
    if nseg == 1:
        return ada_ref[0, comp:comp + 1, :]
    seg = tm // nseg
    lo, hi = rows.start // seg, rows.stop // seg
    assert rows.start % seg == 0 and rows.stop % seg == 0
    return jnp.concatenate([jnp.broadcast_to(ada_ref[i, comp:comp + 1, :], (seg, ada_ref.shape[2]))
                            for i in range(lo, hi)], axis=0)


def _premix(x, ada_ref, g, rows, tm):
    return _rms(x, g) * (1.0 + _mod(ada_ref, 1, rows, tm)) + _mod(ada_ref, 0, rows, tm)


def _row_parts(tm):
    n = 2 if tm % (2 * 128) == 0 else 1
    return [slice(i * (tm // n), (i + 1) * (tm // n)) for i in range(n)]


def _mixer_kernel(x_ref, ada_ref, g_ref, win_ref, gq_ref, wuq_ref, gkv_ref, wuk_ref, wuv_ref, vone_ref,
                  cos_ref, sin_ref, *out_refs, expand_kv):
    if expand_kv:
        q_ref, ckv_ref, kr_ref, qb_ref, kb_ref, vb_ref, k_ref, vt_ref = out_refs
    else:
        q_ref, ckv_ref, kr_ref, qb_ref, kb_ref, vb_ref = out_refs
    tm = x_ref.shape[1]
    parts = _row_parts(tm)
    nq = H_A * HEAD_PAD
    ps = [_dot(_premix(x_ref[0, rows, :], ada_ref, g_ref[...], rows, tm).astype(BF16), win_ref[...])
          for rows in parts]
    q_lats = [_rms(p[:, _C_QLAT:_C_QLAT + Q_RANK], gq_ref[...]).astype(BF16) for p in ps]
    c_kvs = [_rms(p[:, _C_CKV:_C_CKV + KV_RANK], gkv_ref[...]) for p in ps]
    q2s = [_dot(q_lat, wuq_ref[...]) for q_lat in q_lats]
    if expand_kv:
        c_bfs = [c_kv.astype(BF16) for c_kv in c_kvs]
        k2s = [_dot(c_bf, wuk_ref[...]) for c_bf in c_bfs]
        v2s = [_dot(c_bf, wuv_ref[...]) for c_bf in c_bfs]
    for i, rows in enumerate(parts):
        p, q2 = ps[i], q2s[i]
        cos = cos_ref[rows, :]
        sin = sin_ref[rows, :]
        for hd in range(H_A):
            lo = hd * HEAD_PAD
            qh = q2[:, lo:lo + HEAD_PAD] * cos + q2[:, nq + lo:nq + lo + HEAD_PAD] * sin
            q_ref[0, rows, lo:lo + HEAD_PAD] = (qh * (MLA_SCALE * LOG2E)).astype(BF16)
        ckv_ref[0, rows, :] = c_kvs[i]
        kr = p[:, _C_KR:_C_KR + LANES] * cos + p[:, _C_KRR:_C_KRR + LANES] * sin
        kr_ref[0, rows, :] = kr[:, :ROPE]
        qb_ref[0, rows, :] = (p[:, _C_QB:_C_QB + H_B * D_HB] * SB_SCALE).astype(BF16)
        kb_ref[0, rows, :] = p[:, _C_KB:_C_KB + H_B * D_HB]
        vb_ref[0, rows, :] = p[:, _C_VB:_C_VB + H_B * D_HB]
        if expand_kv:
            for hd in range(H_A):
                lo = hd * HEAD_PAD
                k_ref[0, rows, lo:lo + HEAD_PAD] = (k2s[i][:, lo:lo + HEAD_PAD] + kr).astype(BF16)
            vt_ref[0, :, rows] = (v2s[i] + vone_ref[...]).T.astype(BF16)


def _mixer_inputs(x, ada, cos_t, sin_t, wts, *, tm, expand_kv):
    b, t, _ = x.shape
    nt = t // tm
    nseg = ada.shape[0] // b
    tok = lambda n: pl.BlockSpec((1, tm, n), lambda i, j: (i, j, 0))
    in_specs = [tok(D_MODEL),
                pl.BlockSpec((nseg, 6, D_MODEL), lambda i, j: (i, 0, 0)),
                _const_spec((1, D_MODEL)),
                _const_spec((D_MODEL, _N_K1)),
                _const_spec((1, Q_RANK)),
                _const_spec((Q_RANK, 2 * H_A * HEAD_PAD)),
                _const_spec((1, KV_RANK)),
                _const_spec((KV_RANK, H_A * HEAD_PAD)),
                _const_spec((KV_RANK, H_A * VT_ROWS)),
                _const_spec((1, H_A * VT_ROWS)),
                pl.BlockSpec((tm, LANES), lambda i, j: (j, 0)),
                pl.BlockSpec((tm, LANES), lambda i, j: (j, 0))]
    shapes = [((b, t, H_A * HEAD_PAD), BF16), ((b, t, KV_RANK), F32), ((b, t, ROPE), F32),
              ((b, t, H_B * D_HB), BF16), ((b, t, H_B * D_HB), F32), ((b, t, H_B * D_HB), F32)]
    if expand_kv:
        shapes += [((b, t, H_A * HEAD_PAD), BF16)]
    out_specs = [tok(s[-1]) for s, _ in shapes]
    if expand_kv:
        shapes += [((b, H_A * VT_ROWS, t), BF16)]
        out_specs += [pl.BlockSpec((1, H_A * VT_ROWS, tm), lambda i, j: (i, 0, j))]
    return pl.pallas_call(
        functools.partial(_mixer_kernel, expand_kv=expand_kv),
        grid=(b, nt),
        in_specs=in_specs,
        out_specs=out_specs,
        out_shape=[jax.ShapeDtypeStruct(s, d) for s, d in shapes],
        compiler_params=_params(2),
        name="mixer_in_kv" if expand_kv else "mixer_in",
    )(x, ada, wts["g_pre_mix"], wts["w_in_k1"], wts["g_q_lat"], wts["w_uq_ext"], wts["g_kv_lat"],
      wts["w_uk_pad"], wts["w_uv_t"], wts["v_ones"], cos_t, sin_t)


MLA_PAIR = 2


def _mla_prompt_kernel(q_ref, k_ref, vt_ref, bias_ref, o_ref, sa_ref, sb_ref, m_ref, acc_ref, *, blk):
    qi = pl.program_id(2)
    lanes = [slice(hh * HEAD_PAD, (hh + 1) * HEAD_PAD) for hh in range(2)]
    m_ref[...] = jnp.full(m_ref.shape, NEG, F32)
    acc_ref[...] = jnp.zeros(acc_ref.shape, F32)

    def start_of(j):
        return pl.multiple_of(j * blk, blk)

    def scores(s_ref, g):
        for hh, ln in enumerate(lanes):
            for b in range(MLA_PAIR):
                k = k_ref[0, pl.ds(start_of(MLA_PAIR * g + b), blk), ln]
                s_ref[hh, b] = _dot_t(k, q_ref[0, :, ln])

    def softmax_pv(s_ref, g, masked):
        for hh in range(2):
            ss = []
            for b in range(MLA_PAIR):
                s = s_ref[hh, b]
                ss.append(s + bias_ref[b] if masked else s)
            m_old = m_ref[hh]
            m_new = m_old
            for s in ss:
                m_new = jnp.maximum(m_new, jnp.max(s, axis=0, keepdims=True))
            acc = jnp.exp2(m_old - m_new) * acc_ref[hh]
            for b, s in enumerate(ss):
                p = jnp.exp2(s - m_new).astype(BF16)
                vt = vt_ref[0, hh * VT_ROWS:(hh + 1) * VT_ROWS, pl.ds(start_of(MLA_PAIR * g + b), blk)]
                acc = acc + _dot(vt, p)
            acc_ref[hh] = acc
            m_ref[hh] = m_new

    scores(sa_ref, 0)

    def body(h, c):
        scores(sb_ref, 2 * h + 1)
        softmax_pv(sa_ref, 2 * h, False)
        scores(sa_ref, 2 * h + 2)
        softmax_pv(sb_ref, 2 * h + 1, False)
        return c

    done = qi // 2 * 2
    lax.fori_loop(0, qi // 2, body, 0)

    @pl.when(done == qi)
    def _():
        softmax_pv(sa_ref, done, True)

    @pl.when(done != qi)
    def _():
        scores(sb_ref, done + 1)
        softmax_pv(sa_ref, done, False)
        softmax_pv(sb_ref, done + 1, True)

    outs = [acc_ref[hh][:DV] / acc_ref[hh][DV:DV + 1] for hh in range(2)]
    o_ref[0] = jnp.concatenate(outs, axis=0).T.astype(BF16)


def _mla_diag_bias(blk, tq):
    kpos = np.arange(MLA_PAIR * blk).reshape(MLA_PAIR, blk, 1)
    qpos = np.arange(tq).reshape(1, 1, tq)
    return np.where(kpos // CHUNK <= qpos // CHUNK, 0.0, NEG).astype(np.float32)


def _mla_prompt(q, k, vt, *, blk):
    b, t, _ = q.shape
    tq = MLA_PAIR * blk
    assert t % tq == 0
    return pl.pallas_call(
        functools.partial(_mla_prompt_kernel, blk=blk),
        grid=(b, H_A // 2, t // tq),
        in_specs=[pl.BlockSpec((1, tq, 2 * HEAD_PAD), lambda i, j, n: (i, n, j)),
                  pl.BlockSpec((1, t, 2 * HEAD_PAD), lambda i, j, n: (i, 0, j)),
                  pl.BlockSpec((1, 2 * VT_ROWS, t), lambda i, j, n: (i, j, 0)),
                  _const_spec((MLA_PAIR, blk, tq))],
        out_specs=pl.BlockSpec((1, tq, 2 * DV), lambda i, j, n: (i, n, j)),
        out_shape=jax.ShapeDtypeStruct((b, t, H_A * DV), BF16),
        scratch_shapes=[pltpu.VMEM((2, MLA_PAIR, blk, tq), F32), pltpu.VMEM((2, MLA_PAIR, blk, tq), F32),
                        pltpu.VMEM((2, 1, tq), F32), pltpu.VMEM((2, VT_ROWS, tq), F32)],
        compiler_params=_params(3),
        name="mla_prompt",
    )(q, k, vt, jnp.asarray(_mla_diag_bias(blk, tq)))


def _tri(n):
    r = lax.broadcasted_iota(jnp.int32, (n, n), 0)
    c = lax.broadcasted_iota(jnp.int32, (n, n), 1)
    return jnp.where(r > c, 1.0, 0.0).astype(BF16)


def _sb_chains(chains, r_init):
    n = len(chains)
    z, sp, sp_bf, later, a = ([None] * n for _ in range(5))
    r_before = [None] * n
    r = dict(r_init)
    out = {}

    def scores(c):
        _, qs, k, _, _, _ = chains[c]
        z[c] = _dot_t(qs, k)

    def softplus(c):
        g, _, _, _, _, keep = chains[c]
        zb = z[c].astype(BF16)
        sp_bf[c] = jnp.maximum(zb, 0.0) + jnp.log(1.0 + jnp.exp2(jnp.abs(zb) * (-LOG2E)))
        sp[c] = sp_bf[c].astype(F32)
        if keep is not None:
            sp[c] = jnp.where(keep, sp[c], 0.0)
            sp_bf[c] = sp[c].astype(BF16)
        rs = jnp.sum(sp[c], axis=-1, keepdims=True)
        r_before[c] = r[g]
        r[g] = rs if r[g] is None else r[g] + rs

    def suffix(c):
        later[c] = _dot(sp_bf[c], chains[c][4])

    def weights(c):
        _, _, k, _, _, keep = chains[c]
        tk = k.shape[0]
        w = z[c] - sp[c] - later[c]
        if r_before[c] is not None:
            r_b = r_before[c]
            if r_b.shape[1] == LANES and tk > LANES:
                r_b = jnp.concatenate([r_b] * (tk // LANES), axis=1)
            elif r_b.shape[1] == LANES and tk < LANES:
                r_b = r_b[:, :tk]
            w = w - r_b
        w = jnp.exp2(w * LOG2E)
        a[c] = (w if keep is None else jnp.where(keep, w, 0.0)).astype(BF16)

    def values(c):
        g, _, _, v, _, _ = chains[c]
        o = _dot(a[c], v)
        out[g] = o if g not in out else out[g] + o

    for stage in (scores, softplus, suffix, weights, values):
        for c in range(n):
            stage(c)
    return out, r


def _head_mask(x, hh):
    lane = lax.broadcasted_iota(jnp.int32, x.shape, 1)
    mine = jnp.logical_and(lane >= hh * D_HB, lane < (hh + 1) * D_HB)
    return jnp.where(mine, x, jnp.zeros_like(x))


def _stack_heads(q):
    return jnp.concatenate([_head_mask(q, 0), _head_mask(q, 1)], axis=0)


def _causal_pair(t, tk):
    row = lax.broadcasted_iota(jnp.int32, (t, tk), 0)
    col = lax.broadcasted_iota(jnp.int32, (t, tk), 1)
    keep = col < row
    return jnp.concatenate([keep, keep], axis=0)


def _sb_tail(qs, load, tri, acc_ref, r_ref, j_start):
    def cond(c):
        j, rmin = c
        return jnp.logical_and(j >= 0, rmin < SB_DEAD)

    def body(c):
        j, _ = c
        k, v = load(j)
        o, r_new = _sb_chains([(0, qs, k, v, tri, None)], {0: r_ref[...]})
        acc_ref[...] += o[0]
        r_ref[...] = r_new[0]
        return j - 1, jnp.min(r_new[0])

    lax.while_loop(cond, body, (j_start, jnp.min(r_ref[...])))


def _unstack_heads(acc, t):
    lane = lax.broadcasted_iota(jnp.int32, (t, LANES), 1)
    return jnp.where(lane < D_HB, acc[:t], acc[t:])


def _sb_prompt_kernel(q_ref, k_ref, v_ref, o_ref, acc_ref, r_ref, *, tq, nsub):
    qi = pl.program_id(2)
    tri = _tri(tq)
    causal = _causal_pair(tq, tq)

    def load(j):
        start = pl.multiple_of(j * tq, tq)
        return (k_ref[0, pl.ds(start, tq), :].astype(BF16),
                v_ref[0, pl.ds(start, tq), :].astype(BF16))

    qss = [_stack_heads(q_ref[0, sub * tq:(sub + 1) * tq, :]) for sub in range(nsub)]
    chains = []
    for back in range(SB_AHEAD + 1):
        for sub in range(nsub):
            blk = qi * nsub + sub
            keep = causal if back == 0 else blk >= back
            chains.append((sub, qss[sub]) + load(jnp.maximum(blk - back, 0)) + (tri, keep))
    o, r = _sb_chains(chains, {sub: None for sub in range(nsub)})
    for sub in range(nsub):
        acc_ref[sub] = o[sub]
        r_ref[sub] = jnp.broadcast_to(r[sub], (2 * tq, LANES))
    for sub in range(nsub):
        _sb_tail(qss[sub], load, tri, acc_ref.at[sub], r_ref.at[sub], qi * nsub + sub - SB_AHEAD - 1)
        o_ref[0, sub * tq:(sub + 1) * tq, :] = _unstack_heads(acc_ref[sub], tq).astype(BF16)


def _sb_prompt(q, k, v, *, tq, nsub):
    b, t, _ = q.shape
    rows = tq * nsub
    return pl.pallas_call(
        functools.partial(_sb_prompt_kernel, tq=tq, nsub=nsub),
        grid=(b, H_B // 2, t // rows),
        in_specs=[pl.BlockSpec((1, rows, LANES), lambda i, j, n: (i, n, j)),
                  pl.BlockSpec((1, t, LANES), lambda i, j, n: (i, 0, j)),
                  pl.BlockSpec((1, t, LANES), lambda i, j, n: (i, 0, j))],
        out_specs=pl.BlockSpec((1, rows, LANES), lambda i, j, n: (i, n, j)),
        out_shape=jax.ShapeDtypeStruct((b, t, H_B * D_HB), BF16),
        scratch_shapes=[pltpu.VMEM((nsub, 2 * tq, LANES), F32), pltpu.VMEM((nsub, 2 * tq, LANES), F32)],
        compiler_params=_params(3),
        name="sb_prompt",
    )(q, k, v)


def _sb_sample_kernel(q_ref, kn_ref, vn_ref, kc_ref, vc_ref, o_ref, left_ref, acc_ref, r_ref, *, tk):
    tq = q_ref.shape[1]
    nblk = kc_ref.shape[1] // tk
    qs = _stack_heads(q_ref[0])
    tri = _tri(tk)

    def load(j):
        start = pl.multiple_of(j * tk, tk)
        return (kc_ref[0, pl.ds(start, tk), :].astype(BF16),
                vc_ref[0, pl.ds(start, tk), :].astype(BF16))

    o, r = _sb_chains([(0, qs, kn_ref[0].astype(BF16), vn_ref[0].astype(BF16), _tri(tq), _causal_pair(tq, tq)),
                       (0, qs) + load(nblk - 1) + (tri, None)], {0: None})
    acc_ref[...] = o[0]
    r_ref[...] = jnp.broadcast_to(r[0], (2 * tq, LANES))
    _sb_tail(qs, load, tri, acc_ref, r_ref, jnp.int32(nblk - 2))
    o_ref[0] = _unstack_heads(acc_ref[...], tq).astype(BF16)
    left_ref[...] = jnp.full(left_ref.shape, jnp.min(r_ref[...]), F32)


def _sb_sample_call(q, k_new, v_new, k_cache, v_cache, *, tk):
    b, t, _ = q.shape
    past = k_cache.shape[1]
    new = pl.BlockSpec((1, t, LANES), lambda i, j: (i, 0, j))
    old = pl.BlockSpec((1, past, LANES), lambda i, j: (i, 0, j))
    return pl.pallas_call(
        functools.partial(_sb_sample_kernel, tk=tk),
        grid=(b, H_B // 2),
        in_specs=[new, new, new, old, old],
        out_specs=[new, pl.BlockSpec((1, 1, SUBLANES, LANES), lambda i, j: (i, j, 0, 0))],
        out_shape=[jax.ShapeDtypeStruct((b, t, H_B * D_HB), BF16),
                   jax.ShapeDtypeStruct((b, H_B // 2, SUBLANES, LANES), F32)],
        scratch_shapes=[pltpu.VMEM((2 * t, LANES), F32), pltpu.VMEM((2 * t, LANES), F32)],
        compiler_params=_params(2),
        name="sb_sample",
    )(q, k_new, v_new, k_cache, v_cache)


def _sb_sample(q, k_new, v_new, k_cache, v_cache, *, tk, recent):
    b, past = k_cache.shape[:2]
    flat = lambda c: c.reshape(b, c.shape[1], H_B * D_HB)
    recent = min(recent, past)
    o, left = _sb_sample_call(q, k_new, v_new, flat(k_cache[:, past - recent:]), flat(v_cache[:, past - recent:]),
                              tk=tk)
    if recent == past:
        return o
    return lax.cond(jnp.min(left) < SB_DEAD,
                    lambda: _sb_sample_call(q, k_new, v_new, flat(k_cache), flat(v_cache), tk=tk)[0],
                    lambda: o)


def _mla_sample_kernel(q_ref, cn_ref, rn_ref, cc_ref, rc_ref, wabs_ref, wuv_ref, o_ref, *, kc, past_len):
    t = q_ref.shape[1]
    past = cc_ref.shape[1]
    qa, qr = [], []
    for hd in range(H_A):
        qh = q_ref[0, :, hd * HEAD_PAD:(hd + 1) * HEAD_PAD]
        qa.append(_dot(qh, wabs_ref[hd]).astype(BF16))
        qr.append(qh[:, :ROPE])
    qa = jnp.concatenate(qa, axis=0)
    qr = jnp.concatenate(qr, axis=0)
    rows = H_A * t

    pieces = [(cc_ref[0, c * kc:(c + 1) * kc, :], rc_ref[0, c * kc:(c + 1) * kc, :], c * kc)
              for c in range(past // kc)]
    pieces.append((cn_ref[0], rn_ref[0], past_len))
    scores, lat = [], []
    for ckv, kr, k0 in pieces:
        ckv = ckv.astype(BF16)
        s = _dot_t(qa, ckv) + _dot_t(qr, kr.astype(BF16))
        n = ckv.shape[0]
        if (k0 + n - 1) // CHUNK > past_len // CHUNK:
            qpos = past_len + lax.broadcasted_iota(jnp.int32, (rows, n), 0) % t
            kpos = k0 + lax.broadcasted_iota(jnp.int32, (rows, n), 1)
            s = jnp.where((kpos >> CHUNK_SHIFT) <= (qpos >> CHUNK_SHIFT), s, NEG)
        scores.append(s)
        lat.append(ckv)
    m = functools.reduce(jnp.maximum, [jnp.max(s, axis=-1, keepdims=True) for s in scores])
    l = jnp.zeros((rows, 1), F32)
    o_lat = jnp.zeros((rows, KV_RANK), F32)
    for s, ckv in zip(scores, lat):
        p = jnp.exp2(s - m)
        l = l + jnp.sum(p, axis=-1, keepdims=True)
        o_lat = o_lat + _dot(p.astype(BF16), ckv)
    o_lat = (o_lat / l).astype(BF16)
    o = jnp.zeros((t, H_A * DV), F32)
    for hd in range(H_A):
        o = o + _dot(o_lat[hd * t:(hd + 1) * t], wuv_ref[hd])
    o_ref[0] = o.astype(BF16)


def _mla_sample(q, ckv_new, kr_new, ckv_cache, kr_cache, w_abs, w_uv_heads, *, past_len, kc):
    b, t, _ = q.shape
    past = ckv_cache.shape[1]
    row = lambda n, s: pl.BlockSpec((1, n, s), lambda i: (i, 0, 0))
    return pl.pallas_call(
        functools.partial(_mla_sample_kernel, kc=kc, past_len=past_len),
        grid=(b,),
        in_specs=[row(t, H_A * HEAD_PAD), row(t, KV_RANK), row(t, ROPE),
                  row(past, KV_RANK), row(past, ROPE),
                  _const_spec((H_A, HEAD_PAD, KV_RANK)), _const_spec((H_A, KV_RANK, H_A * DV))],
        out_specs=row(t, H_A * DV),
        out_shape=jax.ShapeDtypeStruct((b, t, H_A * DV), BF16),
        compiler_params=_params(1),
        name="mla_sample",
    )(q, ckv_new, kr_new, ckv_cache, kr_cache, w_abs, w_uv_heads)


def _merge_kernel(x_ref, ada_ref, g_ref, wg_ref, oa_ref, ob_ref, wpa_ref, wpb_ref, wo_ref, gpost_ref, o_ref):
    tm = x_ref.shape[1]
    parts = _row_parts(tm)
    xs = [x_ref[0, rows, :] for rows in parts]
    gate_in = [_dot(_premix(x, ada_ref, g_ref[...], rows, tm).astype(BF16), wg_ref[...])
               for x, rows in zip(xs, parts)]
    pas = [_dot(oa_ref[0, rows, :], wpa_ref[...]) for rows in parts]
    pbs = [_dot(ob_ref[0, rows, :], wpb_ref[...]) for rows in parts]
    merged = [(_sigmoid(gi[:, :D_MODEL]) * pa + _sigmoid(gi[:, D_MODEL:]) * pb).astype(BF16)
              for gi, pa, pb in zip(gate_in, pas, pbs)]
    mos = [_dot(m, wo_ref[...]) for m in merged]
    for rows, x, mo in zip(parts, xs, mos):
        o_ref[0, rows, :] = x + _mod(ada_ref, 2, rows, tm) * _rms(mo, gpost_ref[...])


def _merge(x, ada, o_a, o_b, wts, *, tm):
    b, t, _ = x.shape
    nseg = ada.shape[0] // b
    tok = lambda n: pl.BlockSpec((1, tm, n), lambda i, j: (i, j, 0))
    return pl.pallas_call(
        _merge_kernel,
        grid=(b, t // tm),
        in_specs=[tok(D_MODEL),
                  pl.BlockSpec((nseg, 6, D_MODEL), lambda i, j: (i, 0, 0)),
                  _const_spec((1, D_MODEL)),
                  _const_spec((D_MODEL, 2 * D_MODEL)),
                  tok(H_A * DV), tok(H_B * D_HB),
                  _const_spec((H_A * DV, D_MODEL)), _const_spec((H_B * D_HB, D_MODEL)),
                  _const_spec((D_MODEL, D_MODEL)),
                  _const_spec((1, D_MODEL))],
        out_specs=tok(D_MODEL),
        out_shape=jax.ShapeDtypeStruct((b, t, D_MODEL), F32),
        compiler_params=_params(2),
        name="merge",
    )(x, ada, wts["g_pre_mix"], wts["w_gate"], o_a, o_b, wts["w_proj_a"], wts["w_proj_b"],
      wts["w_out"], wts["g_post_mix"])


FF_CHUNK = 256
SUB_GROUPS = 4
SUB_ROWS = SUB_GROUPS * SUBLANES
HALO = 2 * SUBLANES


def _gelu_tanh(a):
    return 0.5 * a * (1.0 + jnp.tanh(math.sqrt(2.0 / math.pi) * (a + 0.044715 * (a * a * a))))


def _ffn_kernel(x_ref, ada_ref, g_ref, wup_ref, cw_ref, cb_ref, wdn_ref, gpost_ref, cs_ref,
                o_ref, nc_ref, halo_ref, perm_ref, *, tm):
    ti = pl.program_id(1)
    nt = pl.num_programs(1)
    nsub = tm // SUB_ROWS
    segmented = cs_ref.shape[0] > 1
    assert not segmented or cs_ref.shape[0] == nsub
    all_rows = slice(0, tm)

    if not segmented:
        @pl.when(ti == 0)
        def _():
            halo_ref[...] = jnp.zeros((HALO, 2 * D_FF), F32)
            halo_ref[SUBLANES - 1:SUBLANES, :] = cs_ref[0, 0:1, :]
            halo_ref[HALO - 1:HALO, :] = cs_ref[0, 1:2, :]

    nc = D_MODEL // LANES
    for c in range(nc):
        perm_ref[c] = x_ref[0, :, c * LANES:(c + 1) * LANES]
    x = jnp.concatenate(
        [jnp.concatenate([perm_ref[c, pl.ds(j * SUB_ROWS + i, SUBLANES, stride=SUB_GROUPS), :] for c in range(nc)],
                         axis=1)
         for j in range(nsub) for i in range(SUB_GROUPS)], axis=0)
    h2 = (_rms(x, g_ref[...]) * (1.0 + _mod(ada_ref, 4, all_rows, tm)) + _mod(ada_ref, 3, all_rows, tm)).astype(BF16)
    first = lax.broadcasted_iota(jnp.int32, (SUBLANES, FF_CHUNK), 0) == 0

    def up(col):
        cols = slice(col, col + FF_CHUNK)
        u = _dot(h2, wup_ref[:, cols])
        grp = [u[g * SUBLANES:(g + 1) * SUBLANES] for g in range(tm // SUBLANES)]
        own2 = [pltpu.roll(grp[j * SUB_GROUPS + SUB_GROUPS - 2], 1, axis=0) for j in range(nsub)]
        own3 = [pltpu.roll(grp[j * SUB_GROUPS + SUB_GROUPS - 1], 1, axis=0) for j in range(nsub)]
        if segmented:
            pred2 = [jnp.broadcast_to(cs_ref[j, 0:1, cols], (SUBLANES, FF_CHUNK)) for j in range(nsub)]
            pred3 = [jnp.broadcast_to(cs_ref[j, 1:2, cols], (SUBLANES, FF_CHUNK)) for j in range(nsub)]
            for j in range(nsub):
                last = (j + 1) * SUB_ROWS
                nc_ref[j, 0:1, cols] = u[last - SUBLANES - 1:last - SUBLANES]
                nc_ref[j, 1:2, cols] = u[last - 1:last]
        else:
            pred2 = [pltpu.roll(halo_ref[0:SUBLANES, cols], 1, axis=0)] + own2[:-1]
            pred3 = [pltpu.roll(halo_ref[SUBLANES:HALO, cols], 1, axis=0)] + own3[:-1]
            halo_ref[:, cols] = u[tm - HALO:]
        u1, u2 = [], []
        for j in range(nsub):
            f1 = jnp.where(first, pred3[j], own3[j])
            f2 = jnp.where(first, pred2[j], own2[j])
            g0 = j * SUB_GROUPS
            u1 += [f1] + grp[g0:g0 + SUB_GROUPS - 1]
            u2 += [f2, f1] + grp[g0:g0 + SUB_GROUPS - 2]
        u1 = jnp.concatenate(u1, axis=0)
        u2 = jnp.concatenate(u2, axis=0)
        return (cb_ref[:, cols] + cw_ref[0:1, cols] * u2 + cw_ref[1:2, cols] * u1 + cw_ref[2:3, cols] * u)

    nchunk = D_FF // FF_CHUNK
    acc = jnp.zeros((tm, D_MODEL), F32)
    ya, yb = up(0), up(D_FF)
    for c in range(nchunk):
        if c + 1 < nchunk:
            ya_next, yb_next = up((c + 1) * FF_CHUNK), up(D_FF + (c + 1) * FF_CHUNK)
        g = (_gelu_tanh(ya) * yb).astype(BF16)
        acc = acc + _dot(g, wdn_ref[c * FF_CHUNK:(c + 1) * FF_CHUNK, :])
        if c + 1 < nchunk:
            ya, yb = ya_next, yb_next
    y = x + _mod(ada_ref, 5, all_rows, tm) * _rms(acc, gpost_ref[...])
    for j in range(nsub):
        for i in range(SUB_GROUPS):
            g0 = (j * SUB_GROUPS + i) * SUBLANES
            for c in range(nc):
                perm_ref[c, pl.ds(j * SUB_ROWS + i, SUBLANES, stride=SUB_GROUPS), :] = (
                    y[g0:g0 + SUBLANES, c * LANES:(c + 1) * LANES])
    for c in range(nc):
        o_ref[0, :, c * LANES:(c + 1) * LANES] = perm_ref[c]

    if not segmented:
        @pl.when(ti == nt - 1)
        def _():
            nc_ref[0, 0:1, :] = halo_ref[SUBLANES - 1:SUBLANES, :]
            nc_ref[0, 1:2, :] = halo_ref[HALO - 1:HALO, :]


def _ffn(x, ada, conv_state, wts, *, tm):
    b, t, _ = x.shape
    nseg = ada.shape[0] // b
    assert tm % SUB_ROWS == 0 and t % tm == 0 and (nseg == 1 or (t == tm and t == nseg * SUB_ROWS))
    tok = pl.BlockSpec((1, tm, D_MODEL), lambda i, j: (i, j, 0))
    state = pl.BlockSpec((nseg, CONV_W - 1, 2 * D_FF), lambda i, j: (i, 0, 0))
    return pl.pallas_call(
        functools.partial(_ffn_kernel, tm=tm),
        grid=(b, t // tm),
        in_specs=[tok,
                  pl.BlockSpec((nseg, 6, D_MODEL), lambda i, j: (i, 0, 0)),
                  _const_spec((1, D_MODEL)),
                  _const_spec((D_MODEL, 2 * D_FF)),
                  _const_spec((CONV_W, 2 * D_FF)),
                  _const_spec((1, 2 * D_FF)),
                  _const_spec((D_FF, D_MODEL)),
                  _const_spec((1, D_MODEL)),
                  state],
        out_specs=[tok, state],
        out_shape=[jax.ShapeDtypeStruct((b, t, D_MODEL), F32),
                   jax.ShapeDtypeStruct((b * nseg, CONV_W - 1, 2 * D_FF), F32)],
        scratch_shapes=[pltpu.VMEM((HALO, 2 * D_FF), F32), pltpu.VMEM((D_MODEL // LANES, tm, LANES), F32)],
        compiler_params=_params(1, 1),
        name="conv_ffn",
    )(x, ada, wts["g_pre_ffn"], wts["w_up"], wts["conv_w"], wts["conv_b"], wts["w_down"],
      wts["g_post_ffn"], conv_state)


def _rope_tables(pos):
    inv = ROPE_BASE ** (-jnp.arange(0, ROPE, 2, dtype=F32) / ROPE)
    ang = pos.astype(F32)[:, None] * inv[None, :]
    cos, sin = jnp.cos(ang), jnp.sin(ang)
    t = pos.shape[0]
    cos_t = jnp.concatenate([cos, cos, jnp.ones((t, NOPE), F32), jnp.zeros((t, HEAD_PAD - ROPE - NOPE), F32)], axis=1)
    sin_t = jnp.concatenate([-sin, sin, jnp.zeros((t, HEAD_PAD - ROPE), F32)], axis=1)
    return cos_t, sin_t


def _swap_halves(w):
    return jnp.concatenate([w[..., ROPE // 2:], w[..., :ROPE // 2]], axis=-1)


def _pad_lanes(w, n):
    return jnp.pad(w, [(0, 0)] * (w.ndim - 1) + [(0, n - w.shape[-1])])


def _layer_weights(l, g_pre_mix, g_post_mix, g_pre_ffn, g_post_ffn, w_in, g_q_lat, w_uq, g_kv_lat,
                   w_uk, w_uv, w_proj_a, w_proj_b, w_out, w_up, conv_w, conv_b, w_down):
    splits = np.cumsum([Q_RANK, KV_RANK, ROPE, H_B * D_HB, H_B * D_HB, H_B * D_HB, D_MODEL])
    wi = w_in[l]
    w_ql, w_ckv, w_kr, w_qb, w_kb, w_vb, w_ga, w_gb = jnp.split(wi, splits, axis=1)
    w_in_k1 = jnp.concatenate([w_ql, w_ckv, _pad_lanes(w_kr, LANES), _pad_lanes(_swap_halves(w_kr), LANES),
                               w_qb, w_kb, w_vb], axis=1).astype(BF16)
    wq = w_uq[l].reshape(Q_RANK, H_A, NOPE + ROPE)
    wq_nope, wq_rope = wq[..., :NOPE], wq[..., NOPE:]
    zeros = jnp.zeros((Q_RANK, H_A, HEAD_PAD - ROPE - NOPE), F32)
    q_main = jnp.concatenate([wq_rope, wq_nope, zeros], axis=-1).reshape(Q_RANK, H_A * HEAD_PAD)
    q_rot = _pad_lanes(_swap_halves(wq_rope), HEAD_PAD).reshape(Q_RANK, H_A * HEAD_PAD)
    wk = w_uk[l].reshape(KV_RANK, H_A, NOPE)
    w_uk_pad = jnp.pad(wk, [(0, 0), (0, 0), (ROPE, HEAD_PAD - ROPE - NOPE)])
    w_abs = jnp.transpose(w_uk_pad, (1, 2, 0))
    w_uv_t = _pad_lanes(w_uv[l].reshape(KV_RANK, H_A, DV), VT_ROWS)
    v_ones = np.zeros((H_A, VT_ROWS), np.float32)
    v_ones[:, DV] = 1.0
    head_of_col = jnp.arange(H_A * DV) // DV
    w_uv_heads = jnp.where(head_of_col[None, None, :] == jnp.arange(H_A)[:, None, None], w_uv[l][None], 0.0)
    row = lambda g: g[l].reshape(1, -1)
    return {
        "g_pre_mix": row(g_pre_mix), "g_post_mix": row(g_post_mix),
        "g_pre_ffn": row(g_pre_ffn), "g_post_ffn": row(g_post_ffn),
        "g_q_lat": row(g_q_lat), "g_kv_lat": row(g_kv_lat),
        "w_in_k1": w_in_k1,
        "w_gate": jnp.concatenate([w_ga, w_gb], axis=1).astype(BF16),
        "w_uq_ext": jnp.concatenate([q_main, q_rot], axis=1).astype(BF16),
        "w_uk_pad": w_uk_pad.reshape(KV_RANK, H_A * HEAD_PAD).astype(BF16),
        "w_abs": w_abs.astype(BF16),
        "w_uv_t": w_uv_t.reshape(KV_RANK, H_A * VT_ROWS).astype(BF16),
        "v_ones": jnp.asarray(v_ones.reshape(1, H_A * VT_ROWS)),
        "w_uv_heads": w_uv_heads.astype(BF16),
        "w_proj_a": w_proj_a[l].astype(BF16), "w_proj_b": w_proj_b[l].astype(BF16),
        "w_out": w_out[l].astype(BF16),
        "w_up": w_up[l].astype(BF16), "conv_w": conv_w[l], "conv_b": conv_b[l].reshape(1, -1),
        "w_down": w_down[l].astype(BF16),
    }


def _pick_tile(t, want):
    tm = min(t, want)
    assert t % tm == 0
    return tm


def _layer(x, ada, pos, past, wts):
    b, t, _ = x.shape
    cos_t, sin_t = _rope_tables(pos)
    tm = _pick_tile(t, 512)
    if past is None:
        q, ckv, kr, qb, kb, vb, k_mla, vt_mla = _mixer_inputs(x, ada, cos_t, sin_t, wts, tm=tm, expand_kv=True)
        o_a = _mla_prompt(q, k_mla, vt_mla, blk=_pick_tile(t, 512))
        o_b = _sb_prompt(qb, kb, vb, tq=_pick_tile(t, 256), nsub=2 if t % 512 == 0 else 1)
        conv_state = jnp.zeros((b, CONV_W - 1, 2 * D_FF), F32)
    else:
        past_ckv, past_kr, past_k, past_v, conv_state = past
        past_len = past_ckv.shape[1]
        flat = t == SUB_ROWS and (b * t) % 256 == 0 and b * t <= 512
        if flat:
            x = x.reshape(1, b * t, D_MODEL)
            cos_t, sin_t = jnp.tile(cos_t, (b, 1)), jnp.tile(sin_t, (b, 1))
            tm = b * t
        per_batch = lambda a: a.reshape(b, t, a.shape[-1])
        q, ckv, kr, qb, kb, vb = map(per_batch, _mixer_inputs(x, ada, cos_t, sin_t, wts, tm=tm, expand_kv=False))
        o_a = _mla_sample(q, ckv, kr, past_ckv, past_kr, wts["w_abs"], wts["w_uv_heads"],
                          past_len=past_len, kc=_pick_tile(past_len, 1024))
        o_b = _sb_sample(qb, kb, vb, past_k, past_v, tk=_pick_tile(past_len, 256), recent=512)
        o_a, o_b = (o.reshape(x.shape[0], x.shape[1], o.shape[-1]) for o in (o_a, o_b))
    x1 = _merge(x, ada, o_a, o_b, wts, tm=tm)
    y, new_conv = _ffn(x1, ada, conv_state, wts, tm=_pick_tile(x.shape[1], 256))
    state = (ckv, kr, kb.reshape(b, t, H_B, D_HB), vb.reshape(b, t, H_B, D_HB), new_conv)
    return y.reshape(b, t, D_MODEL), state


def kernel(x_prompt, x_sample, cache_mla_ckv, cache_mla_krope, cache_sb_k, cache_sb_v, state_ffn_conv,
           c_prompt, c_sample, w_ada, b_ada, g_pre_mix, g_post_mix, g_pre_ffn, g_post_ffn,
           w_in, g_q_lat, w_uq, g_kv_lat, w_uk, w_uv, w_proj_a, w_proj_b, w_out,
           w_up, conv_w, conv_b, w_down):
    depth = w_in.shape[0]
    nb_p = x_prompt.shape[0]
    past_len = cache_mla_ckv.shape[2]
    pos_p = jnp.arange(x_prompt.shape[1], dtype=jnp.int32)
    pos_s = past_len + jnp.arange(x_sample.shape[1], dtype=jnp.int32)
    xp, xs = x_prompt, x_sample
    c_all = jnp.concatenate([c_prompt, c_sample], axis=0)
    st_p = [[] for _ in range(5)]
    st_s = [[] for _ in range(5)]
    for l in range(depth):
        wts = _layer_weights(l, g_pre_mix, g_post_mix, g_pre_ffn, g_post_ffn, w_in, g_q_lat, w_uq, g_kv_lat,
                             w_uk, w_uv, w_proj_a, w_proj_b, w_out, w_up, conv_w, conv_b, w_down)
        ada = _ada(c_all, w_ada[l], b_ada[l]).reshape(-1, 6, D_MODEL)
        xp, sp = _layer(xp, ada[:nb_p], pos_p, None, wts)
        past = (cache_mla_ckv[l], cache_mla_krope[l], cache_sb_k[l], cache_sb_v[l], state_ffn_conv[l])
        xs, ss = _layer(xs, ada[nb_p:], pos_s, past, wts)
        for i in range(5):
            st_p[i].append(sp[i])
            st_s[i].append(ss[i])
    p_state = [jnp.stack(a, axis=0) for a in st_p]
    s_state = [jnp.stack(a, axis=0) for a in st_s]
    return (xp, xs, *p_state, *s_state)
```

```python
import functools
import math

import numpy as np
import jax
import jax.numpy as jnp
from jax import lax
from jax.experimental import pallas as pl
from jax.experimental.pallas import tpu as pltpu

D_MODEL = 1024
CHUNK = 64
CHUNK_SHIFT = 6
H_A = 8
NOPE = 64
ROPE = 32
DV = 64
Q_RANK = 384
KV_RANK = 256
ROPE_BASE = 10000.0
H_B = 8
D_HB = 64
D_FF = 2816
CONV_W = 3
EPS = 1e-6
NEG = -1e30
MLA_SCALE = (NOPE + ROPE) ** -0.5
SB_SCALE = D_HB ** -0.5
LOG2E = math.log2(math.e)

LANES = 128
SUBLANES = 8
HEAD_PAD = LANES
VT_ROWS = 80
SB_DEAD = 110.0
SB_AHEAD = 2
VMEM_LIMIT = 56 * 1024 * 1024

F32 = jnp.float32
BF16 = jnp.bfloat16


def _dot(a, b):
    return jnp.dot(a, b, preferred_element_type=F32)


def _dot_t(a, b):
    return lax.dot_general(a, b, (((1,), (1,)), ((), ())), preferred_element_type=F32)


def _rms(x, g):
    return x * lax.rsqrt(jnp.mean(x * x, axis=-1, keepdims=True) + EPS) * g


def _sigmoid(x):
    return 1.0 / (1.0 + jnp.exp(-x))


def _params(n_parallel, n_arbitrary=0):
    return pltpu.CompilerParams(
        dimension_semantics=("parallel",) * n_parallel + ("arbitrary",) * n_arbitrary,
        vmem_limit_bytes=VMEM_LIMIT)


def _const_spec(shape):
    n = len(shape)
    return pl.BlockSpec(shape, lambda *_: (0,) * n, pipeline_mode=pl.Buffered(1))


def _ada_kernel(c_ref, w_ref, b_ref, o_ref):
    c = c_ref[...]
    s = (c * _sigmoid(c)).astype(BF16)
    o_ref[...] = _dot(s, w_ref[...].astype(BF16)) + b_ref[...]


def _ada(c_all, w_ada, b_ada):
    n = c_all.shape[0]
    nchunk = 6
    return pl.pallas_call(
        _ada_kernel,
        grid=(nchunk,),
        in_specs=[pl.BlockSpec((n, D_MODEL), lambda j: (0, 0)),
                  pl.BlockSpec((D_MODEL, D_MODEL), lambda j: (0, j)),
                  pl.BlockSpec((1, D_MODEL), lambda j: (0, j))],
        out_specs=pl.BlockSpec((n, D_MODEL), lambda j: (0, j)),
        out_shape=jax.ShapeDtypeStruct((n, 6 * D_MODEL), F32),
        compiler_params=_params(1),
        name="ada",
    )(c_all, w_ada, b_ada.reshape(1, -1))


_C_QLAT = 0
_C_CKV = _C_QLAT + Q_RANK
_C_KR = _C_CKV + KV_RANK
_C_KRR = _C_KR + LANES
_C_QB = _C_KRR + LANES
_C_KB = _C_QB + H_B * D_HB
_C_VB = _C_KB + H_B * D_HB
_N_K1 = _C_VB + H_B * D_HB


def _mod(ada_ref, comp, rows, tm):
    nseg = ada_ref.shape[0]
    if nseg == 1:
        return ada_ref[0, comp:comp + 1, :]
    seg = tm // nseg
    lo, hi = rows.start // seg, rows.stop // seg
    assert rows.start % seg == 0 and rows.stop % seg == 0
    return jnp.concatenate([jnp.broadcast_to(ada_ref[i, comp:comp + 1, :], (seg, ada_ref.shape[2]))
                            for i in range(lo, hi)], axis=0)


def _premix(x, ada_ref, g, rows, tm):
    return _rms(x, g) * (1.0 + _mod(ada_ref, 1, rows, tm)) + _mod(ada_ref, 0, rows, tm)


def _row_parts(tm):
    n = 2 if tm % (2 * 128) == 0 else 1
    return [slice(i * (tm // n), (i + 1) * (tm // n)) for i in range(n)]


def _mixer_kernel(x_ref, ada_ref, g_ref, win_ref, gq_ref, wuq_ref, gkv_ref, wuk_ref, wuv_ref, vone_ref,
                  cos_ref, sin_ref, *out_refs, expand_kv):
    if expand_kv:
        q_ref, ckv_ref, kr_ref, qb_ref, kb_ref, vb_ref, k_ref, vt_ref = out_refs
    else:
        q_ref, ckv_ref, kr_ref, qb_ref, kb_ref, vb_ref = out_refs
    tm = x_ref.shape[1]
    parts = _row_parts(tm)
    nq = H_A * HEAD_PAD
    ps = [_dot(_premix(x_ref[0, rows, :], ada_ref, g_ref[...], rows, tm).astype(BF16), win_ref[...])
          for rows in parts]
    q_lats = [_rms(p[:, _C_QLAT:_C_QLAT + Q_RANK], gq_ref[...]).astype(BF16) for p in ps]
    c_kvs = [_rms(p[:, _C_CKV:_C_CKV + KV_RANK], gkv_ref[...]) for p in ps]
    q2s = [_dot(q_lat, wuq_ref[...]) for q_lat in q_lats]
    if expand_kv:
        c_bfs = [c_kv.astype(BF16) for c_kv in c_kvs]
        k2s = [_dot(c_bf, wuk_ref[...]) for c_bf in c_bfs]
        v2s = [_dot(c_bf, wuv_ref[...]) for c_bf in c_bfs]
    for i, rows in enumerate(parts):
        p, q2 = ps[i], q2s[i]
        cos = cos_ref[rows, :]
        sin = sin_ref[rows, :]
        for hd in range(H_A):
            lo = hd * HEAD_PAD
            qh = q2[:, lo:lo + HEAD_PAD] * cos + q2[:, nq + lo:nq + lo + HEAD_PAD] * sin
            q_ref[0, rows, lo:lo + HEAD_PAD] = (qh * (MLA_SCALE * LOG2E)).astype(BF16)
        ckv_ref[0, rows, :] = c_kvs[i]
        kr = p[:, _C_KR:_C_KR + LANES] * cos + p[:, _C_KRR:_C_KRR + LANES] * sin
        kr_ref[0, rows, :] = kr[:, :ROPE]
        qb_ref[0, rows, :] = (p[:, _C_QB:_C_QB + H_B * D_HB] * SB_SCALE).astype(BF16)
        kb_ref[0, rows, :] = p[:, _C_KB:_C_KB + H_B * D_HB]
        vb_ref[0, rows, :] = p[:, _C_VB:_C_VB + H_B * D_HB]
        if expand_kv:
            for hd in range(H_A):
                lo = hd * HEAD_PAD
                k_ref[0, rows, lo:lo + HEAD_PAD] = (k2s[i][:, lo:lo + HEAD_PAD] + kr).astype(BF16)
            vt_ref[0, :, rows] = (v2s[i] + vone_ref[...]).T.astype(BF16)


def _mixer_inputs(x, ada, cos_t, sin_t, wts, *, tm, expand_kv):
    b, t, _ = x.shape
    nt = t // tm
    nseg = ada.shape[0] // b
    tok = lambda n: pl.BlockSpec((1, tm, n), lambda i, j: (i, j, 0))
    in_specs = [tok(D_MODEL),
                pl.BlockSpec((nseg, 6, D_MODEL), lambda i, j: (i, 0, 0)),
                _const_spec((1, D_MODEL)),
                _const_spec((D_MODEL, _N_K1)),
                _const_spec((1, Q_RANK)),
                _const_spec((Q_RANK, 2 * H_A * HEAD_PAD)),
                _const_spec((1, KV_RANK)),
                _const_spec((KV_RANK, H_A * HEAD_PAD)),
                _const_spec((KV_RANK, H_A * VT_ROWS)),
                _const_spec((1, H_A * VT_ROWS)),
                pl.BlockSpec((tm, LANES), lambda i, j: (j, 0)),
                pl.BlockSpec((tm, LANES), lambda i, j: (j, 0))]
    shapes = [((b, t, H_A * HEAD_PAD), BF16), ((b, t, KV_RANK), F32), ((b, t, ROPE), F32),
              ((b, t, H_B * D_HB), BF16), ((b, t, H_B * D_HB), F32), ((b, t, H_B * D_HB), F32)]
    if expand_kv:
        shapes += [((b, t, H_A * HEAD_PAD), BF16)]
    out_specs = [tok(s[-1]) for s, _ in shapes]
    if expand_kv:
        shapes += [((b, H_A * VT_ROWS, t), BF16)]
        out_specs += [pl.BlockSpec((1, H_A * VT_ROWS, tm), lambda i, j: (i, 0, j))]
    return pl.pallas_call(
        functools.partial(_mixer_kernel, expand_kv=expand_kv),
        grid=(b, nt),
        in_specs=in_specs,
        out_specs=out_specs,
        out_shape=[jax.ShapeDtypeStruct(s, d) for s, d in shapes],
        compiler_params=_params(2),
        name="mixer_in_kv" if expand_kv else "mixer_in",
    )(x, ada, wts["g_pre_mix"], wts["w_in_k1"], wts["g_q_lat"], wts["w_uq_ext"], wts["g_kv_lat"],
      wts["w_uk_pad"], wts["w_uv_t"], wts["v_ones"], cos_t, sin_t)


MLA_PAIR = 2


def _mla_prompt_kernel(q_ref, k_ref, vt_ref, bias_ref, o_ref, sa_ref, sb_ref, m_ref, acc_ref, *, blk):
    qi = pl.program_id(2)
    lanes = [slice(hh * HEAD_PAD, (hh + 1) * HEAD_PAD) for hh in range(2)]
    m_ref[...] = jnp.full(m_ref.shape, NEG, F32)
    acc_ref[...] = jnp.zeros(acc_ref.shape, F32)

    def start_of(j):
        return pl.multiple_of(j * blk, blk)

    def scores(s_ref, g):
        for hh, ln in enumerate(lanes):
            for b in range(MLA_PAIR):
                k = k_ref[0, pl.ds(start_of(MLA_PAIR * g + b), blk), ln]
                s_ref[hh, b] = _dot_t(k, q_ref[0, :, ln])

    def softmax_pv(s_ref, g, masked):
        for hh in range(2):
            ss = []
            for b in range(MLA_PAIR):
                s = s_ref[hh, b]
                ss.append(s + bias_ref[b] if masked else s)
            m_old = m_ref[hh]
            m_new = m_old
            for s in ss:
                m_new = jnp.maximum(m_new, jnp.max(s, axis=0, keepdims=True))
            acc = jnp.exp2(m_old - m_new) * acc_ref[hh]
            for b, s in enumerate(ss):
                p = jnp.exp2(s - m_new).astype(BF16)
                vt = vt_ref[0, hh * VT_ROWS:(hh + 1) * VT_ROWS, pl.ds(start_of(MLA_PAIR * g + b), blk)]
                acc = acc + _dot(vt, p)
            acc_ref[hh] = acc
            m_ref[hh] = m_new

    scores(sa_ref, 0)

    def body(h, c):
        scores(sb_ref, 2 * h + 1)
        softmax_pv(sa_ref, 2 * h, False)
        scores(sa_ref, 2 * h + 2)
        softmax_pv(sb_ref, 2 * h + 1, False)
        return c

    done = qi // 2 * 2
    lax.fori_loop(0, qi // 2, body, 0)

    @pl.when(done == qi)
    def _():
        softmax_pv(sa_ref, done, True)

    @pl.when(done != qi)
    def _():
        scores(sb_ref, done + 1)
        softmax_pv(sa_ref, done, False)
        softmax_pv(sb_ref, done + 1, True)

    outs = [acc_ref[hh][:DV] / acc_ref[hh][DV:DV + 1] for hh in range(2)]
    o_ref[0] = jnp.concatenate(outs, axis=0).T.astype(BF16)


def _mla_diag_bias(blk, tq):
    kpos = np.arange(MLA_PAIR * blk).reshape(MLA_PAIR, blk, 1)
    qpos = np.arange(tq).reshape(1, 1, tq)
    return np.where(kpos // CHUNK <= qpos // CHUNK, 0.0, NEG).astype(np.float32)


def _mla_prompt(q, k, vt, *, blk):
    b, t, _ = q.shape
    tq = MLA_PAIR * blk
    assert t % tq == 0
    return pl.pallas_call(
        functools.partial(_mla_prompt_kernel, blk=blk),
        grid=(b, H_A // 2, t // tq),
        in_specs=[pl.BlockSpec((1, tq, 2 * HEAD_PAD), lambda i, j, n: (i, n, j)),
                  pl.BlockSpec((1, t, 2 * HEAD_PAD), lambda i, j, n: (i, 0, j)),
                  pl.BlockSpec((1, 2 * VT_ROWS, t), lambda i, j, n: (i, j, 0)),
                  _const_spec((MLA_PAIR, blk, tq))],
        out_specs=pl.BlockSpec((1, tq, 2 * DV), lambda i, j, n: (i, n, j)),
        out_shape=jax.ShapeDtypeStruct((b, t, H_A * DV), BF16),
        scratch_shapes=[pltpu.VMEM((2, MLA_PAIR, blk, tq), F32), pltpu.VMEM((2, MLA_PAIR, blk, tq), F32),
                        pltpu.VMEM((2, 1, tq), F32), pltpu.VMEM((2, VT_ROWS, tq), F32)],
        compiler_params=_params(3),
        name="mla_prompt",
    )(q, k, vt, jnp.asarray(_mla_diag_bias(blk, tq)))


def _tri(n):
    r = lax.broadcasted_iota(jnp.int32, (n, n), 0)
    c = lax.broadcasted_iota(jnp.int32, (n, n), 1)
    return jnp.where(r > c, 1.0, 0.0).astype(BF16)


def _sb_chains(chains, r_init):
    n = len(chains)
    z, sp, sp_bf, later, a = ([None] * n for _ in range(5))
    r_before = [None] * n
    r = dict(r_init)
    out = {}

    def scores(c):
        _, qs, k, _, _, _ = chains[c]
        z[c] = _dot_t(qs, k)

    def softplus(c):
        g, _, _, _, _, keep = chains[c]
        zb = z[c].astype(BF16)
        sp_bf[c] = jnp.maximum(zb, 0.0) + jnp.log(1.0 + jnp.exp(-jnp.abs(zb)))
        sp[c] = sp_bf[c].astype(F32)
        if keep is not None:
            sp[c] = jnp.where(keep, sp[c], 0.0)
            sp_bf[c] = sp[c].astype(BF16)
        rs = jnp.sum(sp[c], axis=-1, keepdims=True)
        r_before[c] = r[g]
        r[g] = rs if r[g] is None else r[g] + rs

    def suffix(c):
        later[c] = _dot(sp_bf[c], chains[c][4])

    def weights(c):
        _, _, k, _, _, keep = chains[c]
        tk = k.shape[0]
        w = z[c] - sp[c] - later[c]
        if r_before[c] is not None:
            r_b = r_before[c]
            if r_b.shape[1] == LANES and tk > LANES:
                r_b = jnp.concatenate([r_b] * (tk // LANES), axis=1)
            elif r_b.shape[1] == LANES and tk < LANES:
                r_b = r_b[:, :tk]
            w = w - r_b
        w = jnp.exp2(w * LOG2E)
        a[c] = (w if keep is None else jnp.where(keep, w, 0.0)).astype(BF16)

    def values(c):
        g, _, _, v, _, _ = chains[c]
        o = _dot(a[c], v)
        out[g] = o if g not in out else out[g] + o

    for stage in (scores, softplus, suffix, weights, values):
        for c in range(n):
            stage(c)
    return out, r


def _head_mask(x, hh):
    lane = lax.broadcasted_iota(jnp.int32, x.shape, 1)
    mine = jnp.logical_and(lane >= hh * D_HB, lane < (hh + 1) * D_HB)
    return jnp.where(mine, x, jnp.zeros_like(x))


def _stack_heads(q):
    return jnp.concatenate([_head_mask(q, 0), _head_mask(q, 1)], axis=0)


def _causal_pair(t, tk):
    row = lax.broadcasted_iota(jnp.int32, (t, tk), 0)
    col = lax.broadcasted_iota(jnp.int32, (t, tk), 1)
    keep = col < row
    return jnp.concatenate([keep, keep], axis=0)


def _sb_tail(qs, load, tri, acc_ref, r_ref, j_start):
    def cond(c):
        j, rmin = c
        return jnp.logical_and(j >= 0, rmin < SB_DEAD)

    def body(c):
        j, _ = c
        k, v = load(j)
        o, r_new = _sb_chains([(0, qs, k, v, tri, None)], {0: r_ref[...]})
        acc_ref[...] += o[0]
        r_ref[...] = r_new[0]
        return j - 1, jnp.min(r_new[0])

    lax.while_loop(cond, body, (j_start, jnp.min(r_ref[...])))


def _unstack_heads(acc, t):
    lane = lax.broadcasted_iota(jnp.int32, (t, LANES), 1)
    return jnp.where(lane < D_HB, acc[:t], acc[t:])


def _sb_prompt_kernel(q_ref, k_ref, v_ref, o_ref, acc_ref, r_ref, *, tq, nsub):
    qi = pl.program_id(2)
    tri = _tri(tq)
    causal = _causal_pair(tq, tq)

    def load(j):
        start = pl.multiple_of(j * tq, tq)
        return (k_ref[0, pl.ds(start, tq), :].astype(BF16),
                v_ref[0, pl.ds(start, tq), :].astype(BF16))

    qss = [_stack_heads(q_ref[0, sub * tq:(sub + 1) * tq, :]) for sub in range(nsub)]
    chains = []
    for back in range(SB_AHEAD + 1):
        for sub in range(nsub):
            blk = qi * nsub + sub
            keep = causal if back == 0 else blk >= back
            chains.append((sub, qss[sub]) + load(jnp.maximum(blk - back, 0)) + (tri, keep))
    o, r = _sb_chains(chains, {sub: None for sub in range(nsub)})
    for sub in range(nsub):
        acc_ref[sub] = o[sub]
        r_ref[sub] = jnp.broadcast_to(r[sub], (2 * tq, LANES))
    for sub in range(nsub):
        _sb_tail(qss[sub], load, tri, acc_ref.at[sub], r_ref.at[sub], qi * nsub + sub - SB_AHEAD - 1)
        o_ref[0, sub * tq:(sub + 1) * tq, :] = _unstack_heads(acc_ref[sub], tq).astype(BF16)


def _sb_prompt(q, k, v, *, tq, nsub):
    b, t, _ = q.shape
    rows = tq * nsub
    return pl.pallas_call(
        functools.partial(_sb_prompt_kernel, tq=tq, nsub=nsub),
        grid=(b, H_B // 2, t // rows),
        in_specs=[pl.BlockSpec((1, rows, LANES), lambda i, j, n: (i, n, j)),
                  pl.BlockSpec((1, t, LANES), lambda i, j, n: (i, 0, j)),
                  pl.BlockSpec((1, t, LANES), lambda i, j, n: (i, 0, j))],
        out_specs=pl.BlockSpec((1, rows, LANES), lambda i, j, n: (i, n, j)),
        out_shape=jax.ShapeDtypeStruct((b, t, H_B * D_HB), BF16),
        scratch_shapes=[pltpu.VMEM((nsub, 2 * tq, LANES), F32), pltpu.VMEM((nsub, 2 * tq, LANES), F32)],
        compiler_params=_params(3),
        name="sb_prompt",
    )(q, k, v)


def _sb_sample_kernel(q_ref, kn_ref, vn_ref, kc_ref, vc_ref, o_ref, left_ref, acc_ref, r_ref, *, tk):
    tq = q_ref.shape[1]
    nblk = kc_ref.shape[1] // tk
    qs = _stack_heads(q_ref[0])
    tri = _tri(tk)

    def load(j):
        start = pl.multiple_of(j * tk, tk)
        return (kc_ref[0, pl.ds(start, tk), :].astype(BF16),
                vc_ref[0, pl.ds(start, tk), :].astype(BF16))

    o, r = _sb_chains([(0, qs, kn_ref[0].astype(BF16), vn_ref[0].astype(BF16), _tri(tq), _causal_pair(tq, tq)),
                       (0, qs) + load(nblk - 1) + (tri, None)], {0: None})
    acc_ref[...] = o[0]
    r_ref[...] = jnp.broadcast_to(r[0], (2 * tq, LANES))
    _sb_tail(qs, load, tri, acc_ref, r_ref, jnp.int32(nblk - 2))
    o_ref[0] = _unstack_heads(acc_ref[...], tq).astype(BF16)
    left_ref[...] = jnp.full(left_ref.shape, jnp.min(r_ref[...]), F32)


def _sb_sample_call(q, k_new, v_new, k_cache, v_cache, *, tk):
    b, t, _ = q.shape
    past = k_cache.shape[1]
    new = pl.BlockSpec((1, t, LANES), lambda i, j: (i, 0, j))
    old = pl.BlockSpec((1, past, LANES), lambda i, j: (i, 0, j))
    return pl.pallas_call(
        functools.partial(_sb_sample_kernel, tk=tk),
        grid=(b, H_B // 2),
        in_specs=[new, new, new, old, old],
        out_specs=[new, pl.BlockSpec((1, 1, SUBLANES, LANES), lambda i, j: (i, j, 0, 0))],
        out_shape=[jax.ShapeDtypeStruct((b, t, H_B * D_HB), BF16),
                   jax.ShapeDtypeStruct((b, H_B // 2, SUBLANES, LANES), F32)],
        scratch_shapes=[pltpu.VMEM((2 * t, LANES), F32), pltpu.VMEM((2 * t, LANES), F32)],
        compiler_params=_params(2),
        name="sb_sample",
    )(q, k_new, v_new, k_cache, v_cache)


def _sb_sample(q, k_new, v_new, k_cache, v_cache, *, tk, recent):
    b, past = k_cache.shape[:2]
    flat = lambda c: c.reshape(b, c.shape[1], H_B * D_HB)
    recent = min(recent, past)
    o, left = _sb_sample_call(q, k_new, v_new, flat(k_cache[:, past - recent:]), flat(v_cache[:, past - recent:]),
                              tk=tk)
    if recent == past:
        return o
    return lax.cond(jnp.min(left) < SB_DEAD,
                    lambda: _sb_sample_call(q, k_new, v_new, flat(k_cache), flat(v_cache), tk=tk)[0],
                    lambda: o)


def _mla_sample_kernel(q_ref, cn_ref, rn_ref, cc_ref, rc_ref, wabs_ref, wuv_ref, o_ref, *, kc, past_len):
    t = q_ref.shape[1]
    past = cc_ref.shape[1]
    qa, qr = [], []
    for hd in range(H_A):
        qh = q_ref[0, :, hd * HEAD_PAD:(hd + 1) * HEAD_PAD]
        qa.append(_dot(qh, wabs_ref[hd]).astype(BF16))
        qr.append(qh[:, :ROPE])
    qa = jnp.concatenate(qa, axis=0)
    qr = jnp.concatenate(qr, axis=0)
    rows = H_A * t

    pieces = [(cc_ref[0, c * kc:(c + 1) * kc, :], rc_ref[0, c * kc:(c + 1) * kc, :], c * kc)
              for c in range(past // kc)]
    pieces.append((cn_ref[0], rn_ref[0], past_len))
    scores, lat = [], []
    for ckv, kr, k0 in pieces:
        ckv = ckv.astype(BF16)
        s = _dot_t(qa, ckv) + _dot_t(qr, kr.astype(BF16))
        n = ckv.shape[0]
        if (k0 + n - 1) // CHUNK > past_len // CHUNK:
            qpos = past_len + lax.broadcasted_iota(jnp.int32, (rows, n), 0) % t
            kpos = k0 + lax.broadcasted_iota(jnp.int32, (rows, n), 1)
            s = jnp.where((kpos >> CHUNK_SHIFT) <= (qpos >> CHUNK_SHIFT), s, NEG)
        scores.append(s)
        lat.append(ckv)
    m = functools.reduce(jnp.maximum, [jnp.max(s, axis=-1, keepdims=True) for s in scores])
    l = jnp.zeros((rows, 1), F32)
    o_lat = jnp.zeros((rows, KV_RANK), F32)
    for s, ckv in zip(scores, lat):
        p = jnp.exp2(s - m)
        l = l + jnp.sum(p, axis=-1, keepdims=True)
        o_lat = o_lat + _dot(p.astype(BF16), ckv)
    o_lat = (o_lat / l).astype(BF16)
    o = jnp.zeros((t, H_A * DV), F32)
    for hd in range(H_A):
        o = o + _dot(o_lat[hd * t:(hd + 1) * t], wuv_ref[hd])
    o_ref[0] = o.astype(BF16)


def _mla_sample(q, ckv_new, kr_new, ckv_cache, kr_cache, w_abs, w_uv_heads, *, past_len, kc):
    b, t, _ = q.shape
    past = ckv_cache.shape[1]
    row = lambda n, s: pl.BlockSpec((1, n, s), lambda i: (i, 0, 0))
    return pl.pallas_call(
        functools.partial(_mla_sample_kernel, kc=kc, past_len=past_len),
        grid=(b,),
        in_specs=[row(t, H_A * HEAD_PAD), row(t, KV_RANK), row(t, ROPE),
                  row(past, KV_RANK), row(past, ROPE),
                  _const_spec((H_A, HEAD_PAD, KV_RANK)), _const_spec((H_A, KV_RANK, H_A * DV))],
        out_specs=row(t, H_A * DV),
        out_shape=jax.ShapeDtypeStruct((b, t, H_A * DV), BF16),
        compiler_params=_params(1),
        name="mla_sample",
    )(q, ckv_new, kr_new, ckv_cache, kr_cache, w_abs, w_uv_heads)


def _merge_kernel(x_ref, ada_ref, g_ref, wg_ref, oa_ref, ob_ref, wpa_ref, wpb_ref, wo_ref, gpost_ref, o_ref):
    tm = x_ref.shape[1]
    parts = _row_parts(tm)
    xs = [x_ref[0, rows, :] for rows in parts]
    gate_in = [_dot(_premix(x, ada_ref, g_ref[...], rows, tm).astype(BF16), wg_ref[...])
               for x, rows in zip(xs, parts)]
    pas = [_dot(oa_ref[0, rows, :], wpa_ref[...]) for rows in parts]
    pbs = [_dot(ob_ref[0, rows, :], wpb_ref[...]) for rows in parts]
    merged = [(_sigmoid(gi[:, :D_MODEL]) * pa + _sigmoid(gi[:, D_MODEL:]) * pb).astype(BF16)
              for gi, pa, pb in zip(gate_in, pas, pbs)]
    mos = [_dot(m, wo_ref[...]) for m in merged]
    for rows, x, mo in zip(parts, xs, mos):
        o_ref[0, rows, :] = x + _mod(ada_ref, 2, rows, tm) * _rms(mo, gpost_ref[...])


def _merge(x, ada, o_a, o_b, wts, *, tm):
    b, t, _ = x.shape
    nseg = ada.shape[0] // b
    tok = lambda n: pl.BlockSpec((1, tm, n), lambda i, j: (i, j, 0))
    return pl.pallas_call(
        _merge_kernel,
        grid=(b, t // tm),
        in_specs=[tok(D_MODEL),
                  pl.BlockSpec((nseg, 6, D_MODEL), lambda i, j: (i, 0, 0)),
                  _const_spec((1, D_MODEL)),
                  _const_spec((D_MODEL, 2 * D_MODEL)),
                  tok(H_A * DV), tok(H_B * D_HB),
                  _const_spec((H_A * DV, D_MODEL)), _const_spec((H_B * D_HB, D_MODEL)),
                  _const_spec((D_MODEL, D_MODEL)),
                  _const_spec((1, D_MODEL))],
        out_specs=tok(D_MODEL),
        out_shape=jax.ShapeDtypeStruct((b, t, D_MODEL), F32),
        compiler_params=_params(2),
        name="merge",
    )(x, ada, wts["g_pre_mix"], wts["w_gate"], o_a, o_b, wts["w_proj_a"], wts["w_proj_b"],
      wts["w_out"], wts["g_post_mix"])


FF_CHUNK = 256
SUB_GROUPS = 4
SUB_ROWS = SUB_GROUPS * SUBLANES
HALO = 2 * SUBLANES


def _gelu_tanh(a):
    return 0.5 * a * (1.0 + jnp.tanh(math.sqrt(2.0 / math.pi) * (a + 0.044715 * (a * a * a))))


def _ffn_kernel(x_ref, ada_ref, g_ref, wup_ref, cw_ref, cb_ref, wdn_ref, gpost_ref, cs_ref,
                o_ref, nc_ref, halo_ref, perm_ref, *, tm):
    ti = pl.program_id(1)
    nt = pl.num_programs(1)
    nsub = tm // SUB_ROWS
    segmented = cs_ref.shape[0] > 1
    assert not segmented or cs_ref.shape[0] == nsub
    all_rows = slice(0, tm)

    if not segmented:
        @pl.when(ti == 0)
        def _():
            halo_ref[...] = jnp.zeros((HALO, 2 * D_FF), F32)
            halo_ref[SUBLANES - 1:SUBLANES, :] = cs_ref[0, 0:1, :]
            halo_ref[HALO - 1:HALO, :] = cs_ref[0, 1:2, :]

    nc = D_MODEL // LANES
    for c in range(nc):
        perm_ref[c] = x_ref[0, :, c * LANES:(c + 1) * LANES]
    x = jnp.concatenate(
        [jnp.concatenate([perm_ref[c, pl.ds(j * SUB_ROWS + i, SUBLANES, stride=SUB_GROUPS), :] for c in range(nc)],
                         axis=1)
         for j in range(nsub) for i in range(SUB_GROUPS)], axis=0)
    h2 = (_rms(x, g_ref[...]) * (1.0 + _mod(ada_ref, 4, all_rows, tm)) + _mod(ada_ref, 3, all_rows, tm)).astype(BF16)
    first = lax.broadcasted_iota(jnp.int32, (SUBLANES, FF_CHUNK), 0) == 0

    def up(col):
        cols = slice(col, col + FF_CHUNK)
        u = _dot(h2, wup_ref[:, cols])
        grp = [u[g * SUBLANES:(g + 1) * SUBLANES] for g in range(tm // SUBLANES)]
        own2 = [pltpu.roll(grp[j * SUB_GROUPS + SUB_GROUPS - 2], 1, axis=0) for j in range(nsub)]
        own3 = [pltpu.roll(grp[j * SUB_GROUPS + SUB_GROUPS - 1], 1, axis=0) for j in range(nsub)]
        if segmented:
            pred2 = [jnp.broadcast_to(cs_ref[j, 0:1, cols], (SUBLANES, FF_CHUNK)) for j in range(nsub)]
            pred3 = [jnp.broadcast_to(cs_ref[j, 1:2, cols], (SUBLANES, FF_CHUNK)) for j in range(nsub)]
            for j in range(nsub):
                last = (j + 1) * SUB_ROWS
                nc_ref[j, 0:1, cols] = u[last - SUBLANES - 1:last - SUBLANES]
                nc_ref[j, 1:2, cols] = u[last - 1:last]
        else:
            pred2 = [pltpu.roll(halo_ref[0:SUBLANES, cols], 1, axis=0)] + own2[:-1]
            pred3 = [pltpu.roll(halo_ref[SUBLANES:HALO, cols], 1, axis=0)] + own3[:-1]
            halo_ref[:, cols] = u[tm - HALO:]
        u1, u2 = [], []
        for j in range(nsub):
            f1 = jnp.where(first, pred3[j], own3[j])
            f2 = jnp.where(first, pred2[j], own2[j])
            g0 = j * SUB_GROUPS
            u1 += [f1] + grp[g0:g0 + SUB_GROUPS - 1]
            u2 += [f2, f1] + grp[g0:g0 + SUB_GROUPS - 2]
        u1 = jnp.concatenate(u1, axis=0)
        u2 = jnp.concatenate(u2, axis=0)
        return (cb_ref[:, cols] + cw_ref[0:1, cols] * u2 + cw_ref[1:2, cols] * u1 + cw_ref[2:3, cols] * u)

    nchunk = D_FF // FF_CHUNK
    acc = jnp.zeros((tm, D_MODEL), F32)
    ya, yb = up(0), up(D_FF)
    for c in range(nchunk):
        if c + 1 < nchunk:
            ya_next, yb_next = up((c + 1) * FF_CHUNK), up(D_FF + (c + 1) * FF_CHUNK)
        g = (_gelu_tanh(ya) * yb).astype(BF16)
        acc = acc + _dot(g, wdn_ref[c * FF_CHUNK:(c + 1) * FF_CHUNK, :])
        if c + 1 < nchunk:
            ya, yb = ya_next, yb_next
    y = x + _mod(ada_ref, 5, all_rows, tm) * _rms(acc, gpost_ref[...])
    for j in range(nsub):
        for i in range(SUB_GROUPS):
            g0 = (j * SUB_GROUPS + i) * SUBLANES
            for c in range(nc):
                perm_ref[c, pl.ds(j * SUB_ROWS + i, SUBLANES, stride=SUB_GROUPS), :] = (
                    y[g0:g0 + SUBLANES, c * LANES:(c + 1) * LANES])
    for c in range(nc):
        o_ref[0, :, c * LANES:(c + 1) * LANES] = perm_ref[c]

    if not segmented:
        @pl.when(ti == nt - 1)
        def _():
            nc_ref[0, 0:1, :] = halo_ref[SUBLANES - 1:SUBLANES, :]
            nc_ref[0, 1:2, :] = halo_ref[HALO - 1:HALO, :]


def _ffn(x, ada, conv_state, wts, *, tm):
    b, t, _ = x.shape
    nseg = ada.shape[0] // b
    assert tm % SUB_ROWS == 0 and t % tm == 0 and (nseg == 1 or (t == tm and t == nseg * SUB_ROWS))
    tok = pl.BlockSpec((1, tm, D_MODEL), lambda i, j: (i, j, 0))
    state = pl.BlockSpec((nseg, CONV_W - 1, 2 * D_FF), lambda i, j: (i, 0, 0))
    return pl.pallas_call(
        functools.partial(_ffn_kernel, tm=tm),
        grid=(b, t // tm),
        in_specs=[tok,
                  pl.BlockSpec((nseg, 6, D_MODEL), lambda i, j: (i, 0, 0)),
                  _const_spec((1, D_MODEL)),
                  _const_spec((D_MODEL, 2 * D_FF)),
                  _const_spec((CONV_W, 2 * D_FF)),
                  _const_spec((1, 2 * D_FF)),
                  _const_spec((D_FF, D_MODEL)),
                  _const_spec((1, D_MODEL)),
                  state],
        out_specs=[tok, state],
        out_shape=[jax.ShapeDtypeStruct((b, t, D_MODEL), F32),
                   jax.ShapeDtypeStruct((b * nseg, CONV_W - 1, 2 * D_FF), F32)],
        scratch_shapes=[pltpu.VMEM((HALO, 2 * D_FF), F32), pltpu.VMEM((D_MODEL // LANES, tm, LANES), F32)],
        compiler_params=_params(1, 1),
        name="conv_ffn",
    )(x, ada, wts["g_pre_ffn"], wts["w_up"], wts["conv_w"], wts["conv_b"], wts["w_down"],
      wts["g_post_ffn"], conv_state)


def _rope_tables(first_pos, t):
    inv = ROPE_BASE ** (-np.arange(0, ROPE, 2, dtype=np.float64) / ROPE)
    ang = np.arange(first_pos, first_pos + t, dtype=np.float64)[:, None] * inv[None, :]
    cos, sin = jnp.asarray(np.cos(ang), F32), jnp.asarray(np.sin(ang), F32)
    cos_t = jnp.concatenate([cos, cos, jnp.ones((t, NOPE), F32), jnp.zeros((t, HEAD_PAD - ROPE - NOPE), F32)], axis=1)
    sin_t = jnp.concatenate([-sin, sin, jnp.zeros((t, HEAD_PAD - ROPE), F32)], axis=1)
    return cos_t, sin_t


def _swap_halves(w):
    return jnp.concatenate([w[..., ROPE // 2:], w[..., :ROPE // 2]], axis=-1)


def _pad_lanes(w, n):
    return jnp.pad(w, [(0, 0)] * (w.ndim - 1) + [(0, n - w.shape[-1])])


def _layer_weights(l, g_pre_mix, g_post_mix, g_pre_ffn, g_post_ffn, w_in, g_q_lat, w_uq, g_kv_lat,
                   w_uk, w_uv, w_proj_a, w_proj_b, w_out, w_up, conv_w, conv_b, w_down):
    splits = np.cumsum([Q_RANK, KV_RANK, ROPE, H_B * D_HB, H_B * D_HB, H_B * D_HB, D_MODEL])
    wi = w_in[l]
    w_ql, w_ckv, w_kr, w_qb, w_kb, w_vb, w_ga, w_gb = jnp.split(wi, splits, axis=1)
    w_in_k1 = jnp.concatenate([w_ql, w_ckv, _pad_lanes(w_kr, LANES), _pad_lanes(_swap_halves(w_kr), LANES),
                               w_qb, w_kb, w_vb], axis=1).astype(BF16)
    wq = w_uq[l].reshape(Q_RANK, H_A, NOPE + ROPE)
    wq_nope, wq_rope = wq[..., :NOPE], wq[..., NOPE:]
    zeros = jnp.zeros((Q_RANK, H_A, HEAD_PAD - ROPE - NOPE), F32)
    q_main = jnp.concatenate([wq_rope, wq_nope, zeros], axis=-1).reshape(Q_RANK, H_A * HEAD_PAD)
    q_rot = _pad_lanes(_swap_halves(wq_rope), HEAD_PAD).reshape(Q_RANK, H_A * HEAD_PAD)
    wk = w_uk[l].reshape(KV_RANK, H_A, NOPE)
    w_uk_pad = jnp.pad(wk, [(0, 0), (0, 0), (ROPE, HEAD_PAD - ROPE - NOPE)])
    w_abs = jnp.transpose(w_uk_pad, (1, 2, 0))
    w_uv_t = _pad_lanes(w_uv[l].reshape(KV_RANK, H_A, DV), VT_ROWS)
    v_ones = np.zeros((H_A, VT_ROWS), np.float32)
    v_ones[:, DV] = 1.0
    head_of_col = jnp.arange(H_A * DV) // DV
    w_uv_heads = jnp.where(head_of_col[None, None, :] == jnp.arange(H_A)[:, None, None], w_uv[l][None], 0.0)
    row = lambda g: g[l].reshape(1, -1)
    return {
        "g_pre_mix": row(g_pre_mix), "g_post_mix": row(g_post_mix),
        "g_pre_ffn": row(g_pre_ffn), "g_post_ffn": row(g_post_ffn),
        "g_q_lat": row(g_q_lat), "g_kv_lat": row(g_kv_lat),
        "w_in_k1": w_in_k1,
        "w_gate": jnp.concatenate([w_ga, w_gb], axis=1).astype(BF16),
        "w_uq_ext": jnp.concatenate([q_main, q_rot], axis=1).astype(BF16),
        "w_uk_pad": w_uk_pad.reshape(KV_RANK, H_A * HEAD_PAD).astype(BF16),
        "w_abs": w_abs.astype(BF16),
        "w_uv_t": w_uv_t.reshape(KV_RANK, H_A * VT_ROWS).astype(BF16),
        "v_ones": jnp.asarray(v_ones.reshape(1, H_A * VT_ROWS)),
        "w_uv_heads": w_uv_heads.astype(BF16),
        "w_proj_a": w_proj_a[l].astype(BF16), "w_proj_b": w_proj_b[l].astype(BF16),
        "w_out": w_out[l].astype(BF16),
        "w_up": w_up[l].astype(BF16), "conv_w": conv_w[l], "conv_b": conv_b[l].reshape(1, -1),
        "w_down": w_down[l].astype(BF16),
    }


def _pick_tile(t, want):
    tm = min(t, want)
    assert t % tm == 0
    return tm


def _layer(x, ada, first_pos, past, wts):
    b, t, _ = x.shape
    cos_t, sin_t = _rope_tables(first_pos, t)
    tm = _pick_tile(t, 512)
    if past is None:
        q, ckv, kr, qb, kb, vb, k_mla, vt_mla = _mixer_inputs(x, ada, cos_t, sin_t, wts, tm=tm, expand_kv=True)
        o_a = _mla_prompt(q, k_mla, vt_mla, blk=_pick_tile(t, 512))
        o_b = _sb_prompt(qb, kb, vb, tq=_pick_tile(t, 256), nsub=2 if t % 512 == 0 else 1)
        conv_state = jnp.zeros((b, CONV_W - 1, 2 * D_FF), F32)
    else:
        past_ckv, past_kr, past_k, past_v, conv_state = past
        past_len = past_ckv.shape[1]
        flat = t == SUB_ROWS and (b * t) % 256 == 0 and b * t <= 512
        if flat:
            x = x.reshape(1, b * t, D_MODEL)
            cos_t, sin_t = jnp.tile(cos_t, (b, 1)), jnp.tile(sin_t, (b, 1))
            tm = b * t
        per_batch = lambda a: a.reshape(b, t, a.shape[-1])
        q, ckv, kr, qb, kb, vb = map(per_batch, _mixer_inputs(x, ada, cos_t, sin_t, wts, tm=tm, expand_kv=False))
        o_a = _mla_sample(q, ckv, kr, past_ckv, past_kr, wts["w_abs"], wts["w_uv_heads"],
                          past_len=past_len, kc=_pick_tile(past_len, 1024))
        o_b = _sb_sample(qb, kb, vb, past_k, past_v, tk=_pick_tile(past_len, 256), recent=512)
        o_a, o_b = (o.reshape(x.shape[0], x.shape[1], o.shape[-1]) for o in (o_a, o_b))
    x1 = _merge(x, ada, o_a, o_b, wts, tm=tm)
    y, new_conv = _ffn(x1, ada, conv_state, wts, tm=_pick_tile(x.shape[1], 256))
    state = (ckv, kr, kb.reshape(b, t, H_B, D_HB), vb.reshape(b, t, H_B, D_HB), new_conv)
    return y.reshape(b, t, D_MODEL), state


def kernel(x_prompt, x_sample, cache_mla_ckv, cache_mla_krope, cache_sb_k, cache_sb_v, state_ffn_conv,
           c_prompt, c_sample, w_ada, b_ada, g_pre_mix, g_post_mix, g_pre_ffn, g_post_ffn,
           w_in, g_q_lat, w_uq, g_kv_lat, w_uk, w_uv, w_proj_a, w_proj_b, w_out,
           w_up, conv_w, conv_b, w_down):
    depth = w_in.shape[0]
    nb_p = x_prompt.shape[0]
    past_len = cache_mla_ckv.shape[2]
    xp, xs = x_prompt, x_sample
    c_all = jnp.concatenate([c_prompt, c_sample], axis=0)
    st_p = [[] for _ in range(5)]
    st_s = [[] for _ in range(5)]
    for l in range(depth):
        wts = _layer_weights(l, g_pre_mix, g_post_mix, g_pre_ffn, g_post_ffn, w_in, g_q_lat, w_uq, g_kv_lat,
                             w_uk, w_uv, w_proj_a, w_proj_b, w_out, w_up, conv_w, conv_b, w_down)
        ada = _ada(c_all, w_ada[l], b_ada[l]).reshape(-1, 6, D_MODEL)
        xp, sp = _layer(xp, ada[:nb_p], 0, None, wts)
        past = (cache_mla_ckv[l], cache_mla_krope[l], cache_sb_k[l], cache_sb_v[l], state_ffn_conv[l])
        xs, ss = _layer(xs, ada[nb_p:], past_len, past, wts)
        for i in range(5):
            st_p[i].append(sp[i])
            st_s[i].append(ss[i])
    p_state = [jnp.stack(a, axis=0) for a in st_p]
    s_state = [jnp.stack(a, axis=0) for a in st_s]
    return (xp, xs, *p_state, *s_state)
```

```python
import functools
import math

import numpy as np
import jax
import jax.numpy as jnp
from jax import lax
from jax.experimental import pallas as pl
from jax.experimental.pallas import tpu as pltpu

D_MODEL = 1024
CHUNK = 64
CHUNK_SHIFT = 6
H_A = 8
NOPE = 64
ROPE = 32
DV = 64
Q_RANK = 384
KV_RANK = 256
ROPE_BASE = 10000.0
H_B = 8
D_HB = 64
D_FF = 2816
CONV_W = 3
EPS = 1e-6
NEG = -1e30
MLA_SCALE = (NOPE + ROPE) ** -0.5
SB_SCALE = D_HB ** -0.5
LOG2E = math.log2(math.e)

LANES = 128
SUBLANES = 8
HEAD_PAD = LANES
VT_ROWS = 80
SB_DEAD = 110.0
SB_AHEAD = 2
VMEM_LIMIT = 56 * 1024 * 1024

F32 = jnp.float32
BF16 = jnp.bfloat16


def _dot(a, b):
    return jnp.dot(a, b, preferred_element_type=F32)


def _dot_t(a, b):
    return lax.dot_general(a, b, (((1,), (1,)), ((), ())), preferred_element_type=F32)


def _rms(x, g):
    return x * lax.rsqrt(jnp.mean(x * x, axis=-1, keepdims=True) + EPS) * g


def _sigmoid(x):
    return 1.0 / (1.0 + jnp.exp(-x))


def _params(n_parallel, n_arbitrary=0):
    return pltpu.CompilerParams(
        dimension_semantics=("parallel",) * n_parallel + ("arbitrary",) * n_arbitrary,
        vmem_limit_bytes=VMEM_LIMIT)


def _const_spec(shape):
    n = len(shape)
    return pl.BlockSpec(shape, lambda *_: (0,) * n, pipeline_mode=pl.Buffered(1))


def _ada_kernel(c_ref, w_ref, b_ref, o_ref):
    c = c_ref[...]
    s = (c * _sigmoid(c)).astype(BF16)
    o_ref[...] = _dot(s, w_ref[...].astype(BF16)) + b_ref[...]


def _ada(c_all, w_ada, b_ada):
    n = c_all.shape[0]
    nchunk = 6
    return pl.pallas_call(
        _ada_kernel,
        grid=(nchunk,),
        in_specs=[pl.BlockSpec((n, D_MODEL), lambda j: (0, 0)),
                  pl.BlockSpec((D_MODEL, D_MODEL), lambda j: (0, j)),
                  pl.BlockSpec((1, D_MODEL), lambda j: (0, j))],
        out_specs=pl.BlockSpec((n, D_MODEL), lambda j: (0, j)),
        out_shape=jax.ShapeDtypeStruct((n, 6 * D_MODEL), F32),
        compiler_params=_params(1),
        name="ada",
    )(c_all, w_ada, b_ada.reshape(1, -1))


_C_QLAT = 0
_C_CKV = _C_QLAT + Q_RANK
_C_KR = _C_CKV + KV_RANK
_C_KRR = _C_KR + LANES
_C_QB = _C_KRR + LANES
_C_KB = _C_QB + H_B * D_HB
_C_VB = _C_KB + H_B * D_HB
_N_K1 = _C_VB + H_B * D_HB


def _mod(ada_ref, comp, rows, tm):
    nseg = ada_ref.shape[0]
    if nseg == 1:
        return ada_ref[0, comp:comp + 1, :]
    seg = tm // nseg
    lo, hi = rows.start // seg, rows.stop // seg
    assert rows.start % seg == 0 and rows.stop % seg == 0
    return jnp.concatenate([jnp.broadcast_to(ada_ref[i, comp:comp + 1, :], (seg, ada_ref.shape[2]))
                            for i in range(lo, hi)], axis=0)


def _premix(x, ada_ref, g, rows, tm):
    return _rms(x, g) * (1.0 + _mod(ada_ref, 1, rows, tm)) + _mod(ada_ref, 0, rows, tm)


def _row_parts(tm):
    n = 2 if tm % (2 * 128) == 0 else 1
    return [slice(i * (tm // n), (i + 1) * (tm // n)) for i in range(n)]


def _mixer_kernel(x_ref, ada_ref, g_ref, win_ref, gq_ref, wuq_ref, gkv_ref, wuk_ref, wuv_ref, vone_ref,
                  cos_ref, sin_ref, *out_refs, expand_kv):
    if expand_kv:
        q_ref, ckv_ref, kr_ref, qb_ref, kb_ref, vb_ref, k_ref, vt_ref = out_refs
    else:
        q_ref, ckv_ref, kr_ref, qb_ref, kb_ref, vb_ref = out_refs
    tm = x_ref.shape[1]
    parts = _row_parts(tm)
    nq = H_A * HEAD_PAD
    ps = [_dot(_premix(x_ref[0, rows, :], ada_ref, g_ref[...], rows, tm).astype(BF16), win_ref[...])
          for rows in parts]
    q_lats = [_rms(p[:, _C_QLAT:_C_QLAT + Q_RANK], gq_ref[...]).astype(BF16) for p in ps]
    c_kvs = [_rms(p[:, _C_CKV:_C_CKV + KV_RANK], gkv_ref[...]) for p in ps]
    q2s = [_dot(q_lat, wuq_ref[...]) for q_lat in q_lats]
    if expand_kv:
        c_bfs = [c_kv.astype(BF16) for c_kv in c_kvs]
        k2s = [_dot(c_bf, wuk_ref[...]) for c_bf in c_bfs]
        v2s = [_dot(c_bf, wuv_ref[...]) for c_bf in c_bfs]
    for i, rows in enumerate(parts):
        p, q2 = ps[i], q2s[i]
        cos = cos_ref[rows, :]
        sin = sin_ref[rows, :]
        for hd in range(H_A):
            lo = hd * HEAD_PAD
            qh = q2[:, lo:lo + HEAD_PAD] * cos + q2[:, nq + lo:nq + lo + HEAD_PAD] * sin
            q_ref[0, rows, lo:lo + HEAD_PAD] = (qh * (MLA_SCALE * LOG2E)).astype(BF16)
        ckv_ref[0, rows, :] = c_kvs[i]
        kr = p[:, _C_KR:_C_KR + LANES] * cos + p[:, _C_KRR:_C_KRR + LANES] * sin
        kr_ref[0, rows, :] = kr[:, :ROPE]
        qb_ref[0, rows, :] = (p[:, _C_QB:_C_QB + H_B * D_HB] * SB_SCALE).astype(BF16)
        kb_ref[0, rows, :] = p[:, _C_KB:_C_KB + H_B * D_HB]
        vb_ref[0, rows, :] = p[:, _C_VB:_C_VB + H_B * D_HB]
        if expand_kv:
            for hd in range(H_A):
                lo = hd * HEAD_PAD
                k_ref[0, rows, lo:lo + HEAD_PAD] = (k2s[i][:, lo:lo + HEAD_PAD] + kr).astype(BF16)
            vt_ref[0, :, rows] = (v2s[i] + vone_ref[...]).T.astype(BF16)


def _mixer_inputs(x, ada, cos_t, sin_t, wts, *, tm, expand_kv):
    b, t, _ = x.shape
    nt = t // tm
    nseg = ada.shape[0] // b
    tok = lambda n: pl.BlockSpec((1, tm, n), lambda i, j: (i, j, 0))
    in_specs = [tok(D_MODEL),
                pl.BlockSpec((nseg, 6, D_MODEL), lambda i, j: (i, 0, 0)),
                _const_spec((1, D_MODEL)),
                _const_spec((D_MODEL, _N_K1)),
                _const_spec((1, Q_RANK)),
                _const_spec((Q_RANK, 2 * H_A * HEAD_PAD)),
                _const_spec((1, KV_RANK)),
                _const_spec((KV_RANK, H_A * HEAD_PAD)),
                _const_spec((KV_RANK, H_A * VT_ROWS)),
                _const_spec((1, H_A * VT_ROWS)),
                pl.BlockSpec((tm, LANES), lambda i, j: (j, 0)),
                pl.BlockSpec((tm, LANES), lambda i, j: (j, 0))]
    shapes = [((b, t, H_A * HEAD_PAD), BF16), ((b, t, KV_RANK), F32), ((b, t, ROPE), F32),
              ((b, t, H_B * D_HB), BF16), ((b, t, H_B * D_HB), F32), ((b, t, H_B * D_HB), F32)]
    if expand_kv:
        shapes += [((b, t, H_A * HEAD_PAD), BF16)]
    out_specs = [tok(s[-1]) for s, _ in shapes]
    if expand_kv:
        shapes += [((b, H_A * VT_ROWS, t), BF16)]
        out_specs += [pl.BlockSpec((1, H_A * VT_ROWS, tm), lambda i, j: (i, 0, j))]
    return pl.pallas_call(
        functools.partial(_mixer_kernel, expand_kv=expand_kv),
        grid=(b, nt),
        in_specs=in_specs,
        out_specs=out_specs,
        out_shape=[jax.ShapeDtypeStruct(s, d) for s, d in shapes],
        compiler_params=_params(2),
        name="mixer_in_kv" if expand_kv else "mixer_in",
    )(x, ada, wts["g_pre_mix"], wts["w_in_k1"], wts["g_q_lat"], wts["w_uq_ext"], wts["g_kv_lat"],
      wts["w_uk_pad"], wts["w_uv_t"], wts["v_ones"], cos_t, sin_t)


MLA_PAIR = 2


def _mla_prompt_kernel(q_ref, k_ref, vt_ref, bias_ref, o_ref, sa_ref, sb_ref, m_ref, acc_ref, *, blk):
    qi = pl.program_id(2)
    lanes = [slice(hh * HEAD_PAD, (hh + 1) * HEAD_PAD) for hh in range(2)]
    m_ref[...] = jnp.full(m_ref.shape, NEG, F32)
    acc_ref[...] = jnp.zeros(acc_ref.shape, F32)

    def start_of(j):
        return pl.multiple_of(j * blk, blk)

    def scores(s_ref, g):
        for hh, ln in enumerate(lanes):
            for b in range(MLA_PAIR):
                k = k_ref[0, pl.ds(start_of(MLA_PAIR * g + b), blk), ln]
                s_ref[hh, b] = _dot_t(k, q_ref[0, :, ln])

    def softmax_pv(s_ref, g, masked):
        for hh in range(2):
            ss = []
            for b in range(MLA_PAIR):
                s = s_ref[hh, b]
                ss.append(s + bias_ref[b] if masked else s)
            m_old = m_ref[hh]
            m_new = m_old
            for s in ss:
                m_new = jnp.maximum(m_new, jnp.max(s, axis=0, keepdims=True))
            acc = jnp.exp2(m_old - m_new) * acc_ref[hh]
            for b, s in enumerate(ss):
                p = jnp.exp2(s - m_new).astype(BF16)
                vt = vt_ref[0, hh * VT_ROWS:(hh + 1) * VT_ROWS, pl.ds(start_of(MLA_PAIR * g + b), blk)]
                acc = acc + _dot(vt, p)
            acc_ref[hh] = acc
            m_ref[hh] = m_new

    scores(sa_ref, 0)

    def body(h, c):
        scores(sb_ref, 2 * h + 1)
        softmax_pv(sa_ref, 2 * h, False)
        scores(sa_ref, 2 * h + 2)
        softmax_pv(sb_ref, 2 * h + 1, False)
        return c

    done = qi // 2 * 2
    lax.fori_loop(0, qi // 2, body, 0)

    @pl.when(done == qi)
    def _():
        softmax_pv(sa_ref, done, True)

    @pl.when(done != qi)
    def _():
        scores(sb_ref, done + 1)
        softmax_pv(sa_ref, done, False)
        softmax_pv(sb_ref, done + 1, True)

    outs = [acc_ref[hh][:DV] / acc_ref[hh][DV:DV + 1] for hh in range(2)]
    o_ref[0] = jnp.concatenate(outs, axis=0).T.astype(BF16)


def _mla_diag_bias(blk, tq):
    kpos = np.arange(MLA_PAIR * blk).reshape(MLA_PAIR, blk, 1)
    qpos = np.arange(tq).reshape(1, 1, tq)
    return np.where(kpos // CHUNK <= qpos // CHUNK, 0.0, NEG).astype(np.float32)


def _mla_prompt(q, k, vt, *, blk):
    b, t, _ = q.shape
    tq = MLA_PAIR * blk
    assert t % tq == 0
    return pl.pallas_call(
        functools.partial(_mla_prompt_kernel, blk=blk),
        grid=(b, H_A // 2, t // tq),
        in_specs=[pl.BlockSpec((1, tq, 2 * HEAD_PAD), lambda i, j, n: (i, n, j)),
                  pl.BlockSpec((1, t, 2 * HEAD_PAD), lambda i, j, n: (i, 0, j)),
                  pl.BlockSpec((1, 2 * VT_ROWS, t), lambda i, j, n: (i, j, 0)),
                  _const_spec((MLA_PAIR, blk, tq))],
        out_specs=pl.BlockSpec((1, tq, 2 * DV), lambda i, j, n: (i, n, j)),
        out_shape=jax.ShapeDtypeStruct((b, t, H_A * DV), BF16),
        scratch_shapes=[pltpu.VMEM((2, MLA_PAIR, blk, tq), F32), pltpu.VMEM((2, MLA_PAIR, blk, tq), F32),
                        pltpu.VMEM((2, 1, tq), F32), pltpu.VMEM((2, VT_ROWS, tq), F32)],
        compiler_params=_params(3),
        name="mla_prompt",
    )(q, k, vt, jnp.asarray(_mla_diag_bias(blk, tq)))


def _tri(n):
    r = lax.broadcasted_iota(jnp.int32, (n, n), 0)
    c = lax.broadcasted_iota(jnp.int32, (n, n), 1)
    return jnp.where(r > c, 1.0, 0.0).astype(BF16)


def _sb_chains(chains, r_init):
    n = len(chains)
    z, sp, sp_bf, later, a = ([None] * n for _ in range(5))
    r_before = [None] * n
    r = dict(r_init)
    out = {}

    def scores(c):
        _, qs, k, _, _, _ = chains[c]
        z[c] = _dot_t(qs, k)

    def softplus(c):
        g, _, _, _, _, keep = chains[c]
        zb = z[c].astype(BF16)
        sp_bf[c] = jnp.maximum(zb, 0.0) + jnp.log(1.0 + jnp.exp(-jnp.abs(zb)))
        sp[c] = sp_bf[c].astype(F32)
        if keep is not None:
            sp[c] = jnp.where(keep, sp[c], 0.0)
            sp_bf[c] = sp[c].astype(BF16)
        rs = jnp.sum(sp[c], axis=-1, keepdims=True)
        r_before[c] = r[g]
        r[g] = rs if r[g] is None else r[g] + rs

    def suffix(c):
        later[c] = _dot(sp_bf[c], chains[c][4])

    def weights(c):
        _, _, k, _, _, keep = chains[c]
        tk = k.shape[0]
        w = z[c] - sp[c] - later[c]
        if r_before[c] is not None:
            r_b = r_before[c]
            if r_b.shape[1] == LANES and tk > LANES:
                r_b = jnp.concatenate([r_b] * (tk // LANES), axis=1)
            elif r_b.shape[1] == LANES and tk < LANES:
                r_b = r_b[:, :tk]
            w = w - r_b
        w = jnp.exp2(w * LOG2E)
        a[c] = (w if keep is None else jnp.where(keep, w, 0.0)).astype(BF16)

    def values(c):
        g, _, _, v, _, _ = chains[c]
        o = _dot(a[c], v)
        out[g] = o if g not in out else out[g] + o

    for stage in (scores, softplus, suffix, weights, values):
        for c in range(n):
            stage(c)
    return out, r


def _head_mask(x, hh):
    lane = lax.broadcasted_iota(jnp.int32, x.shape, 1)
    mine = jnp.logical_and(lane >= hh * D_HB, lane < (hh + 1) * D_HB)
    return jnp.where(mine, x, jnp.zeros_like(x))


def _stack_heads(q):
    return jnp.concatenate([_head_mask(q, 0), _head_mask(q, 1)], axis=0)


def _causal_pair(t, tk):
    row = lax.broadcasted_iota(jnp.int32, (t, tk), 0)
    col = lax.broadcasted_iota(jnp.int32, (t, tk), 1)
    keep = col < row
    return jnp.concatenate([keep, keep], axis=0)


def _sb_tail(qs, load, tri, acc_ref, r_ref, j_start, keep=None):
    def cond(c):
        j, rmin = c
        return jnp.logical_and(j >= 0, rmin < SB_DEAD)

    def body(c):
        j, _ = c
        k, v = load(j)
        o, r_new = _sb_chains([(0, qs, k, v, tri, keep)], {0: r_ref[...]})
        acc_ref[...] += o[0]
        r_ref[...] = r_new[0]
        return j - 1, jnp.min(r_new[0])

    lax.while_loop(cond, body, (j_start, jnp.min(r_ref[...])))


def _unstack_heads(acc, t):
    lane = lax.broadcasted_iota(jnp.int32, (t, LANES), 1)
    return jnp.where(lane < D_HB, acc[:t], acc[t:])


def _sb_prompt_kernel(q_ref, k_ref, v_ref, o_ref, acc_ref, r_ref, *, tq, nsub):
    qi = pl.program_id(2)
    tri = _tri(tq)
    causal = _causal_pair(tq, tq)

    def load(j):
        start = pl.multiple_of(j * tq, tq)
        return (k_ref[0, pl.ds(start, tq), :].astype(BF16),
                v_ref[0, pl.ds(start, tq), :].astype(BF16))

    qss = [_stack_heads(q_ref[0, sub * tq:(sub + 1) * tq, :]) for sub in range(nsub)]
    chains = []
    for back in range(SB_AHEAD + 1):
        for sub in range(nsub):
            blk = qi * nsub + sub
            keep = causal if back == 0 else blk >= back
            chains.append((sub, qss[sub]) + load(jnp.maximum(blk - back, 0)) + (tri, keep))
    o, r = _sb_chains(chains, {sub: None for sub in range(nsub)})
    for sub in range(nsub):
        acc_ref[sub] = o[sub]
        r_ref[sub] = jnp.broadcast_to(r[sub], (2 * tq, LANES))
    for sub in range(nsub):
        _sb_tail(qss[sub], load, tri, acc_ref.at[sub], r_ref.at[sub], qi * nsub + sub - SB_AHEAD - 1)
        o_ref[0, sub * tq:(sub + 1) * tq, :] = _unstack_heads(acc_ref[sub], tq).astype(BF16)


def _sb_prompt(q, k, v, *, tq, nsub):
    b, t, _ = q.shape
    rows = tq * nsub
    return pl.pallas_call(
        functools.partial(_sb_prompt_kernel, tq=tq, nsub=nsub),
        grid=(b, H_B // 2, t // rows),
        in_specs=[pl.BlockSpec((1, rows, LANES), lambda i, j, n: (i, n, j)),
                  pl.BlockSpec((1, t, LANES), lambda i, j, n: (i, 0, j)),
                  pl.BlockSpec((1, t, LANES), lambda i, j, n: (i, 0, j))],
        out_specs=pl.BlockSpec((1, rows, LANES), lambda i, j, n: (i, n, j)),
        out_shape=jax.ShapeDtypeStruct((b, t, H_B * D_HB), BF16),
        scratch_shapes=[pltpu.VMEM((nsub, 2 * tq, LANES), F32), pltpu.VMEM((nsub, 2 * tq, LANES), F32)],
        compiler_params=_params(3),
        name="sb_prompt",
    )(q, k, v)


def _sb_sample_kernel(q_ref, kn_ref, vn_ref, kc_ref, vc_ref, o_ref, left_ref, acc_ref, r_ref, *, tk):
    tq = q_ref.shape[1]
    nblk = kc_ref.shape[1] // tk
    qs = _stack_heads(q_ref[0])
    tri = _tri(tk)

    def load(j):
        start = pl.multiple_of(j * tk, tk)
        return (kc_ref[0, pl.ds(start, tk), :].astype(BF16),
                vc_ref[0, pl.ds(start, tk), :].astype(BF16))

    o, r = _sb_chains([(0, qs, kn_ref[0].astype(BF16), vn_ref[0].astype(BF16), _tri(tq), _causal_pair(tq, tq)),
                       (0, qs) + load(nblk - 1) + (tri, None)], {0: None})
    acc_ref[...] = o[0]
    r_ref[...] = jnp.broadcast_to(r[0], (2 * tq, LANES))
    _sb_tail(qs, load, tri, acc_ref, r_ref, jnp.int32(nblk - 2))
    o_ref[0] = _unstack_heads(acc_ref[...], tq).astype(BF16)
    left_ref[...] = jnp.full(left_ref.shape, jnp.min(r_ref[...]), F32)


def _sb_sample_call(q, k_new, v_new, k_cache, v_cache, *, tk):
    b, t, _ = q.shape
    past = k_cache.shape[1]
    new = pl.BlockSpec((1, t, LANES), lambda i, j: (i, 0, j))
    old = pl.BlockSpec((1, past, LANES), lambda i, j: (i, 0, j))
    return pl.pallas_call(
        functools.partial(_sb_sample_kernel, tk=tk),
        grid=(b, H_B // 2),
        in_specs=[new, new, new, old, old],
        out_specs=[new, pl.BlockSpec((1, 1, SUBLANES, LANES), lambda i, j: (i, j, 0, 0))],
        out_shape=[jax.ShapeDtypeStruct((b, t, H_B * D_HB), BF16),
                   jax.ShapeDtypeStruct((b, H_B // 2, SUBLANES, LANES), F32)],
        scratch_shapes=[pltpu.VMEM((2 * t, LANES), F32), pltpu.VMEM((2 * t, LANES), F32)],
        compiler_params=_params(2),
        name="sb_sample",
    )(q, k_new, v_new, k_cache, v_cache)


SB_DECODE_KEYS = 32
SB_DECODE_AHEAD = 4


def _sb_decode_kernel(q_ref, kn_ref, vn_ref, kc_ref, vc_ref, o_ref, left_ref, acc_ref, r_ref):
    t = q_ref.shape[1]
    rows = H_B * t
    ncol = SB_DECODE_KEYS * H_B
    nstep = kc_ref.shape[1] // SB_DECODE_KEYS
    assert rows == ncol and t & (t - 1) == 0 and H_B & (H_B - 1) == 0
    t_shift, h_mask = t.bit_length() - 1, H_B - 1

    def heads_to_rows(x):
        return jnp.concatenate([x[:, h * D_HB:(h + 1) * D_HB] for h in range(H_B)], axis=0)

    r_i = lax.broadcasted_iota(jnp.int32, (rows, ncol), 0)
    c_i = lax.broadcasted_iota(jnp.int32, (rows, ncol), 1)
    keep_new = jnp.logical_and((c_i >> t_shift) == (r_i >> t_shift), (c_i & (t - 1)) < (r_i & (t - 1)))
    tri_new = jnp.where(jnp.logical_and((r_i >> t_shift) == (c_i >> t_shift), r_i > c_i), 1.0, 0.0).astype(BF16)
    keep_old = (c_i & h_mask) == (r_i >> t_shift)
    tri_old = jnp.where(jnp.logical_and((r_i & h_mask) == (c_i & h_mask), r_i > c_i), 1.0, 0.0).astype(BF16)

    def load(j):
        start = pl.multiple_of(j * SB_DECODE_KEYS, SB_DECODE_KEYS)
        flat = lambda ref: ref[0, pl.ds(start, SB_DECODE_KEYS), :, :].reshape(ncol, D_HB).astype(BF16)
        return flat(kc_ref), flat(vc_ref)

    qa = heads_to_rows(q_ref[0])
    chains = [(0, qa, heads_to_rows(kn_ref[0]).astype(BF16), heads_to_rows(vn_ref[0]).astype(BF16),
               tri_new, keep_new)]
    ahead = min(SB_DECODE_AHEAD, nstep)
    for back in range(1, ahead + 1):
        chains.append((0, qa) + load(nstep - back) + (tri_old, keep_old))
    o, r = _sb_chains(chains, {0: None})
    acc_ref[...] = o[0]
    r_ref[...] = jnp.broadcast_to(r[0], (rows, LANES))
    _sb_tail(qa, load, tri_old, acc_ref, r_ref, jnp.int32(nstep - ahead - 1), keep_old)
    acc = acc_ref[...]
    o_ref[0] = jnp.concatenate([acc[h * t:(h + 1) * t] for h in range(H_B)], axis=1).astype(BF16)
    left_ref[...] = jnp.full(left_ref.shape, jnp.min(r_ref[...]), F32)


def _sb_decode_call(q, k_new, v_new, k_cache, v_cache, *, recent):
    b, t, _ = q.shape
    past = k_cache.shape[1]
    new = pl.BlockSpec((1, t, H_B * D_HB), lambda i: (i, 0, 0))
    old = pl.BlockSpec((1, recent, H_B, D_HB), lambda i: (i, past // recent - 1, 0, 0))
    return pl.pallas_call(
        _sb_decode_kernel,
        grid=(b,),
        in_specs=[new, new, new, old, old],
        out_specs=[new, pl.BlockSpec((1, SUBLANES, LANES), lambda i: (i, 0, 0))],
        out_shape=[jax.ShapeDtypeStruct((b, t, H_B * D_HB), BF16),
                   jax.ShapeDtypeStruct((b, SUBLANES, LANES), F32)],
        scratch_shapes=[pltpu.VMEM((H_B * t, D_HB), F32), pltpu.VMEM((H_B * t, LANES), F32)],
        compiler_params=_params(1),
        name="sb_decode",
    )(q, k_new, v_new, k_cache, v_cache)


def _sb_sample(q, k_new, v_new, k_cache, v_cache, *, tk, recent):
    b, past = k_cache.shape[:2]
    t = q.shape[1]
    flat = lambda c: c.reshape(b, c.shape[1], H_B * D_HB)
    full = lambda: _sb_sample_call(q, k_new, v_new, flat(k_cache), flat(v_cache), tk=tk)[0]
    if not (H_B * t == SB_DECODE_KEYS * H_B and past % recent == 0 and recent % SB_DECODE_KEYS == 0):
        return full()
    o, left = _sb_decode_call(q, k_new, v_new, k_cache, v_cache, recent=recent)
    if recent == past:
        return o
    return lax.cond(jnp.min(left) < SB_DEAD, full, lambda: o)


def _mla_sample_kernel(q_ref, cn_ref, rn_ref, cc_ref, rc_ref, wabs_ref, wuv_ref, o_ref, *, kc, past_len):
    t = q_ref.shape[1]
    past = cc_ref.shape[1]
    qa, qr = [], []
    for hd in range(H_A):
        qh = q_ref[0, :, hd * HEAD_PAD:(hd + 1) * HEAD_PAD]
        qa.append(_dot(qh, wabs_ref[hd]).astype(BF16))
        qr.append(qh[:, :ROPE])
    qa = jnp.concatenate(qa, axis=0)
    qr = jnp.concatenate(qr, axis=0)
    rows = H_A * t

    pieces = [(cc_ref[0, c * kc:(c + 1) * kc, :], rc_ref[0, c * kc:(c + 1) * kc, :], c * kc)
              for c in range(past // kc)]
    pieces.append((cn_ref[0], rn_ref[0], past_len))
    scores, lat = [], []
    for ckv, kr, k0 in pieces:
        ckv = ckv.astype(BF16)
        s = _dot_t(qa, ckv) + _dot_t(qr, kr.astype(BF16))
        n = ckv.shape[0]
        if (k0 + n - 1) // CHUNK > past_len // CHUNK:
            qpos = past_len + lax.broadcasted_iota(jnp.int32, (rows, n), 0) % t
            kpos = k0 + lax.broadcasted_iota(jnp.int32, (rows, n), 1)
            s = jnp.where((kpos >> CHUNK_SHIFT) <= (qpos >> CHUNK_SHIFT), s, NEG)
        scores.append(s)
        lat.append(ckv)
    m = functools.reduce(jnp.maximum, [jnp.max(s, axis=-1, keepdims=True) for s in scores])
    l = jnp.zeros((rows, 1), F32)
    o_lat = jnp.zeros((rows, KV_RANK), F32)
    for s, ckv in zip(scores, lat):
        p = jnp.exp2(s - m)
        l = l + jnp.sum(p, axis=-1, keepdims=True)
        o_lat = o_lat + _dot(p.astype(BF16), ckv)
    o_lat = (o_lat / l).astype(BF16)
    o = jnp.zeros((t, H_A * DV), F32)
    for hd in range(H_A):
        o = o + _dot(o_lat[hd * t:(hd + 1) * t], wuv_ref[hd])
    o_ref[0] = o.astype(BF16)


def _mla_sample(q, ckv_new, kr_new, ckv_cache, kr_cache, w_abs, w_uv_heads, *, past_len, kc):
    b, t, _ = q.shape
    past = ckv_cache.shape[1]
    row = lambda n, s: pl.BlockSpec((1, n, s), lambda i: (i, 0, 0))
    return pl.pallas_call(
        functools.partial(_mla_sample_kernel, kc=kc, past_len=past_len),
        grid=(b,),
        in_specs=[row(t, H_A * HEAD_PAD), row(t, KV_RANK), row(t, ROPE),
                  row(past, KV_RANK), row(past, ROPE),
                  _const_spec((H_A, HEAD_PAD, KV_RANK)), _const_spec((H_A, KV_RANK, H_A * DV))],
        out_specs=row(t, H_A * DV),
        out_shape=jax.ShapeDtypeStruct((b, t, H_A * DV), BF16),
        compiler_params=_params(1),
        name="mla_sample",
    )(q, ckv_new, kr_new, ckv_cache, kr_cache, w_abs, w_uv_heads)


def _merge_kernel(x_ref, ada_ref, g_ref, wg_ref, oa_ref, ob_ref, wpa_ref, wpb_ref, wo_ref, gpost_ref, o_ref):
    tm = x_ref.shape[1]
    parts = _row_parts(tm)
    xs = [x_ref[0, rows, :] for rows in parts]
    gate_in = [_dot(_premix(x, ada_ref, g_ref[...], rows, tm).astype(BF16), wg_ref[...])
               for x, rows in zip(xs, parts)]
    pas = [_dot(oa_ref[0, rows, :], wpa_ref[...]) for rows in parts]
    pbs = [_dot(ob_ref[0, rows, :], wpb_ref[...]) for rows in parts]
    merged = [(_sigmoid(gi[:, :D_MODEL]) * pa + _sigmoid(gi[:, D_MODEL:]) * pb).astype(BF16)
              for gi, pa, pb in zip(gate_in, pas, pbs)]
    mos = [_dot(m, wo_ref[...]) for m in merged]
    for rows, x, mo in zip(parts, xs, mos):
        o_ref[0, rows, :] = x + _mod(ada_ref, 2, rows, tm) * _rms(mo, gpost_ref[...])


def _merge(x, ada, o_a, o_b, wts, *, tm):
    b, t, _ = x.shape
    nseg = ada.shape[0] // b
    tok = lambda n: pl.BlockSpec((1, tm, n), lambda i, j: (i, j, 0))
    return pl.pallas_call(
        _merge_kernel,
        grid=(b, t // tm),
        in_specs=[tok(D_MODEL),
                  pl.BlockSpec((nseg, 6, D_MODEL), lambda i, j: (i, 0, 0)),
                  _const_spec((1, D_MODEL)),
                  _const_spec((D_MODEL, 2 * D_MODEL)),
                  tok(H_A * DV), tok(H_B * D_HB),
                  _const_spec((H_A * DV, D_MODEL)), _const_spec((H_B * D_HB, D_MODEL)),
                  _const_spec((D_MODEL, D_MODEL)),
                  _const_spec((1, D_MODEL))],
        out_specs=tok(D_MODEL),
        out_shape=jax.ShapeDtypeStruct((b, t, D_MODEL), F32),
        compiler_params=_params(2),
        name="merge",
    )(x, ada, wts["g_pre_mix"], wts["w_gate"], o_a, o_b, wts["w_proj_a"], wts["w_proj_b"],
      wts["w_out"], wts["g_post_mix"])


FF_CHUNK = 256
SUB_GROUPS = 4
SUB_ROWS = SUB_GROUPS * SUBLANES
HALO = 2 * SUBLANES


def _gelu_tanh(a):
    return 0.5 * a * (1.0 + jnp.tanh(math.sqrt(2.0 / math.pi) * (a + 0.044715 * (a * a * a))))


def _ffn_kernel(x_ref, ada_ref, g_ref, wup_ref, cw_ref, cb_ref, wdn_ref, gpost_ref, cs_ref,
                o_ref, nc_ref, halo_ref, perm_ref, *, tm):
    ti = pl.program_id(1)
    nt = pl.num_programs(1)
    nsub = tm // SUB_ROWS
    segmented = cs_ref.shape[0] > 1
    assert not segmented or cs_ref.shape[0] == nsub
    all_rows = slice(0, tm)

    if not segmented:
        @pl.when(ti == 0)
        def _():
            halo_ref[...] = jnp.zeros((HALO, 2 * D_FF), F32)
            halo_ref[SUBLANES - 1:SUBLANES, :] = cs_ref[0, 0:1, :]
            halo_ref[HALO - 1:HALO, :] = cs_ref[0, 1:2, :]

    nc = D_MODEL // LANES
    for c in range(nc):
        perm_ref[c] = x_ref[0, :, c * LANES:(c + 1) * LANES]
    x = jnp.concatenate(
        [jnp.concatenate([perm_ref[c, pl.ds(j * SUB_ROWS + i, SUBLANES, stride=SUB_GROUPS), :] for c in range(nc)],
                         axis=1)
         for j in range(nsub) for i in range(SUB_GROUPS)], axis=0)
    h2 = (_rms(x, g_ref[...]) * (1.0 + _mod(ada_ref, 4, all_rows, tm)) + _mod(ada_ref, 3, all_rows, tm)).astype(BF16)
    first = lax.broadcasted_iota(jnp.int32, (SUBLANES, FF_CHUNK), 0) == 0

    def up(col):
        cols = slice(col, col + FF_CHUNK)
        u = _dot(h2, wup_ref[:, cols])
        grp = [u[g * SUBLANES:(g + 1) * SUBLANES] for g in range(tm // SUBLANES)]
        own2 = [pltpu.roll(grp[j * SUB_GROUPS + SUB_GROUPS - 2], 1, axis=0) for j in range(nsub)]
        own3 = [pltpu.roll(grp[j * SUB_GROUPS + SUB_GROUPS - 1], 1, axis=0) for j in range(nsub)]
        if segmented:
            pred2 = [jnp.broadcast_to(cs_ref[j, 0:1, cols], (SUBLANES, FF_CHUNK)) for j in range(nsub)]
            pred3 = [jnp.broadcast_to(cs_ref[j, 1:2, cols], (SUBLANES, FF_CHUNK)) for j in range(nsub)]
            for j in range(nsub):
                last = (j + 1) * SUB_ROWS
                nc_ref[j, 0:1, cols] = u[last - SUBLANES - 1:last - SUBLANES]
                nc_ref[j, 1:2, cols] = u[last - 1:last]
        else:
            pred2 = [pltpu.roll(halo_ref[0:SUBLANES, cols], 1, axis=0)] + own2[:-1]
            pred3 = [pltpu.roll(halo_ref[SUBLANES:HALO, cols], 1, axis=0)] + own3[:-1]
            halo_ref[:, cols] = u[tm - HALO:]
        u1, u2 = [], []
        for j in range(nsub):
            f1 = jnp.where(first, pred3[j], own3[j])
            f2 = jnp.where(first, pred2[j], own2[j])
            g0 = j * SUB_GROUPS
            u1 += [f1] + grp[g0:g0 + SUB_GROUPS - 1]
            u2 += [f2, f1] + grp[g0:g0 + SUB_GROUPS - 2]
        u1 = jnp.concatenate(u1, axis=0)
        u2 = jnp.concatenate(u2, axis=0)
        return (cb_ref[:, cols] + cw_ref[0:1, cols] * u2 + cw_ref[1:2, cols] * u1 + cw_ref[2:3, cols] * u)

    nchunk = D_FF // FF_CHUNK
    acc = jnp.zeros((tm, D_MODEL), F32)
    ya, yb = up(0), up(D_FF)
    for c in range(nchunk):
        if c + 1 < nchunk:
            ya_next, yb_next = up((c + 1) * FF_CHUNK), up(D_FF + (c + 1) * FF_CHUNK)
        g = (_gelu_tanh(ya) * yb).astype(BF16)
        acc = acc + _dot(g, wdn_ref[c * FF_CHUNK:(c + 1) * FF_CHUNK, :])
        if c + 1 < nchunk:
            ya, yb = ya_next, yb_next
    y = x + _mod(ada_ref, 5, all_rows, tm) * _rms(acc, gpost_ref[...])
    for j in range(nsub):
        for i in range(SUB_GROUPS):
            g0 = (j * SUB_GROUPS + i) * SUBLANES
            for c in range(nc):
                perm_ref[c, pl.ds(j * SUB_ROWS + i, SUBLANES, stride=SUB_GROUPS), :] = (
                    y[g0:g0 + SUBLANES, c * LANES:(c + 1) * LANES])
    for c in range(nc):
        o_ref[0, :, c * LANES:(c + 1) * LANES] = perm_ref[c]

    if not segmented:
        @pl.when(ti == nt - 1)
        def _():
            nc_ref[0, 0:1, :] = halo_ref[SUBLANES - 1:SUBLANES, :]
            nc_ref[0, 1:2, :] = halo_ref[HALO - 1:HALO, :]


def _ffn(x, ada, conv_state, wts, *, tm):
    b, t, _ = x.shape
    nseg = ada.shape[0] // b
    assert tm % SUB_ROWS == 0 and t % tm == 0 and (nseg == 1 or (t == tm and t == nseg * SUB_ROWS))
    tok = pl.BlockSpec((1, tm, D_MODEL), lambda i, j: (i, j, 0))
    state = pl.BlockSpec((nseg, CONV_W - 1, 2 * D_FF), lambda i, j: (i, 0, 0))
    return pl.pallas_call(
        functools.partial(_ffn_kernel, tm=tm),
        grid=(b, t // tm),
        in_specs=[tok,
                  pl.BlockSpec((nseg, 6, D_MODEL), lambda i, j: (i, 0, 0)),
                  _const_spec((1, D_MODEL)),
                  _const_spec((D_MODEL, 2 * D_FF)),
                  _const_spec((CONV_W, 2 * D_FF)),
                  _const_spec((1, 2 * D_FF)),
                  _const_spec((D_FF, D_MODEL)),
                  _const_spec((1, D_MODEL)),
                  state],
        out_specs=[tok, state],
        out_shape=[jax.ShapeDtypeStruct((b, t, D_MODEL), F32),
                   jax.ShapeDtypeStruct((b * nseg, CONV_W - 1, 2 * D_FF), F32)],
        scratch_shapes=[pltpu.VMEM((HALO, 2 * D_FF), F32), pltpu.VMEM((D_MODEL // LANES, tm, LANES), F32)],
        compiler_params=_params(1, 1),
        name="conv_ffn",
    )(x, ada, wts["g_pre_ffn"], wts["w_up"], wts["conv_w"], wts["conv_b"], wts["w_down"],
      wts["g_post_ffn"], conv_state)


def _rope_tables(first_pos, t):
    inv = ROPE_BASE ** (-np.arange(0, ROPE, 2, dtype=np.float64) / ROPE)
    ang = np.arange(first_pos, first_pos + t, dtype=np.float64)[:, None] * inv[None, :]
    cos, sin = jnp.asarray(np.cos(ang), F32), jnp.asarray(np.sin(ang), F32)
    cos_t = jnp.concatenate([cos, cos, jnp.ones((t, NOPE), F32), jnp.zeros((t, HEAD_PAD - ROPE - NOPE), F32)], axis=1)
    sin_t = jnp.concatenate([-sin, sin, jnp.zeros((t, HEAD_PAD - ROPE), F32)], axis=1)
    return cos_t, sin_t


def _swap_halves(w):
    return jnp.concatenate([w[..., ROPE // 2:], w[..., :ROPE // 2]], axis=-1)


def _pad_lanes(w, n):
    return jnp.pad(w, [(0, 0)] * (w.ndim - 1) + [(0, n - w.shape[-1])])


def _layer_weights(l, g_pre_mix, g_post_mix, g_pre_ffn, g_post_ffn, w_in, g_q_lat, w_uq, g_kv_lat,
                   w_uk, w_uv, w_proj_a, w_proj_b, w_out, w_up, conv_w, conv_b, w_down):
    splits = np.cumsum([Q_RANK, KV_RANK, ROPE, H_B * D_HB, H_B * D_HB, H_B * D_HB, D_MODEL])
    wi = w_in[l]
    w_ql, w_ckv, w_kr, w_qb, w_kb, w_vb, w_ga, w_gb = jnp.split(wi, splits, axis=1)
    w_in_k1 = jnp.concatenate([w_ql, w_ckv, _pad_lanes(w_kr, LANES), _pad_lanes(_swap_halves(w_kr), LANES),
                               w_qb, w_kb, w_vb], axis=1).astype(BF16)
    wq = w_uq[l].reshape(Q_RANK, H_A, NOPE + ROPE)
    wq_nope, wq_rope = wq[..., :NOPE], wq[..., NOPE:]
    zeros = jnp.zeros((Q_RANK, H_A, HEAD_PAD - ROPE - NOPE), F32)
    q_main = jnp.concatenate([wq_rope, wq_nope, zeros], axis=-1).reshape(Q_RANK, H_A * HEAD_PAD)
    q_rot = _pad_lanes(_swap_halves(wq_rope), HEAD_PAD).reshape(Q_RANK, H_A * HEAD_PAD)
    wk = w_uk[l].reshape(KV_RANK, H_A, NOPE)
    w_uk_pad = jnp.pad(wk, [(0, 0), (0, 0), (ROPE, HEAD_PAD - ROPE - NOPE)])
    w_abs = jnp.transpose(w_uk_pad, (1, 2, 0))
    w_uv_t = _pad_lanes(w_uv[l].reshape(KV_RANK, H_A, DV), VT_ROWS)
    v_ones = np.zeros((H_A, VT_ROWS), np.float32)
    v_ones[:, DV] = 1.0
    head_of_col = jnp.arange(H_A * DV) // DV
    w_uv_heads = jnp.where(head_of_col[None, None, :] == jnp.arange(H_A)[:, None, None], w_uv[l][None], 0.0)
    row = lambda g: g[l].reshape(1, -1)
    return {
        "g_pre_mix": row(g_pre_mix), "g_post_mix": row(g_post_mix),
        "g_pre_ffn": row(g_pre_ffn), "g_post_ffn": row(g_post_ffn),
        "g_q_lat": row(g_q_lat), "g_kv_lat": row(g_kv_lat),
        "w_in_k1": w_in_k1,
        "w_gate": jnp.concatenate([w_ga, w_gb], axis=1).astype(BF16),
        "w_uq_ext": jnp.concatenate([q_main, q_rot], axis=1).astype(BF16),
        "w_uk_pad": w_uk_pad.reshape(KV_RANK, H_A * HEAD_PAD).astype(BF16),
        "w_abs": w_abs.astype(BF16),
        "w_uv_t": w_uv_t.reshape(KV_RANK, H_A * VT_ROWS).astype(BF16),
        "v_ones": jnp.asarray(v_ones.reshape(1, H_A * VT_ROWS)),
        "w_uv_heads": w_uv_heads.astype(BF16),
        "w_proj_a": w_proj_a[l].astype(BF16), "w_proj_b": w_proj_b[l].astype(BF16),
        "w_out": w_out[l].astype(BF16),
        "w_up": w_up[l].astype(BF16), "conv_w": conv_w[l], "conv_b": conv_b[l].reshape(1, -1),
        "w_down": w_down[l].astype(BF16),
    }


def _pick_tile(t, want):
    tm = min(t, want)
    assert t % tm == 0
    return tm


def _layer(x, ada, first_pos, past, wts):
    b, t, _ = x.shape
    cos_t, sin_t = _rope_tables(first_pos, t)
    tm = _pick_tile(t, 512)
    if past is None:
        q, ckv, kr, qb, kb, vb, k_mla, vt_mla = _mixer_inputs(x, ada, cos_t, sin_t, wts, tm=tm, expand_kv=True)
        o_a = _mla_prompt(q, k_mla, vt_mla, blk=_pick_tile(t, 512))
        o_b = _sb_prompt(qb, kb, vb, tq=_pick_tile(t, 256), nsub=2 if t % 512 == 0 else 1)
        conv_state = jnp.zeros((b, CONV_W - 1, 2 * D_FF), F32)
    else:
        past_ckv, past_kr, past_k, past_v, conv_state = past
        past_len = past_ckv.shape[1]
        flat = t == SUB_ROWS and (b * t) % 256 == 0 and b * t <= 512
        if flat:
            x = x.reshape(1, b * t, D_MODEL)
            cos_t, sin_t = jnp.tile(cos_t, (b, 1)), jnp.tile(sin_t, (b, 1))
            tm = b * t
        per_batch = lambda a: a.reshape(b, t, a.shape[-1])
        q, ckv, kr, qb, kb, vb = map(per_batch, _mixer_inputs(x, ada, cos_t, sin_t, wts, tm=tm, expand_kv=False))
        o_a = _mla_sample(q, ckv, kr, past_ckv, past_kr, wts["w_abs"], wts["w_uv_heads"],
                          past_len=past_len, kc=_pick_tile(past_len, 1024))
        o_b = _sb_sample(qb, kb, vb, past_k, past_v, tk=_pick_tile(past_len, 256), recent=512)
        o_a, o_b = (o.reshape(x.shape[0], x.shape[1], o.shape[-1]) for o in (o_a, o_b))
    x1 = _merge(x, ada, o_a, o_b, wts, tm=tm)
    y, new_conv = _ffn(x1, ada, conv_state, wts, tm=_pick_tile(x.shape[1], 256))
    state = (ckv, kr, kb.reshape(b, t, H_B, D_HB), vb.reshape(b, t, H_B, D_HB), new_conv)
    return y.reshape(b, t, D_MODEL), state


def kernel(x_prompt, x_sample, cache_mla_ckv, cache_mla_krope, cache_sb_k, cache_sb_v, state_ffn_conv,
           c_prompt, c_sample, w_ada, b_ada, g_pre_mix, g_post_mix, g_pre_ffn, g_post_ffn,
           w_in, g_q_lat, w_uq, g_kv_lat, w_uk, w_uv, w_proj_a, w_proj_b, w_out,
           w_up, conv_w, conv_b, w_down):
    depth = w_in.shape[0]
    nb_p = x_prompt.shape[0]
    past_len = cache_mla_ckv.shape[2]
    xp, xs = x_prompt, x_sample
    c_all = jnp.concatenate([c_prompt, c_sample], axis=0)
    st_p = [[] for _ in range(5)]
    st_s = [[] for _ in range(5)]
    for l in range(depth):
        wts = _layer_weights(l, g_pre_mix, g_post_mix, g_pre_ffn, g_post_ffn, w_in, g_q_lat, w_uq, g_kv_lat,
                             w_uk, w_uv, w_proj_a, w_proj_b, w_out, w_up, conv_w, conv_b, w_down)
        ada = _ada(c_all, w_ada[l], b_ada[l]).reshape(-1, 6, D_MODEL)
        xp, sp = _layer(xp, ada[:nb_p], 0, None, wts)
        past = (cache_mla_ckv[l], cache_mla_krope[l], cache_sb_k[l], cache_sb_v[l], state_ffn_conv[l])
        xs, ss = _layer(xs, ada[nb_p:], past_len, past, wts)
        for i in range(5):
            st_p[i].append(sp[i])
            st_s[i].append(ss[i])
    p_state = [jnp.stack(a, axis=0) for a in st_p]
    s_state = [jnp.stack(a, axis=0) for a in st_s]
    return (xp, xs, *p_state, *s_state)
```

```python
import functools
import math

import numpy as np
import jax
import jax.numpy as jnp
from jax import lax
from jax.experimental import pallas as pl
from jax.experimental.pallas import tpu as pltpu

D_MODEL = 1024
CHUNK = 64
CHUNK_SHIFT = 6
H_A = 8
NOPE = 64
ROPE = 32
DV = 64
Q_RANK = 384
KV_RANK = 256
ROPE_BASE = 10000.0
H_B = 8
D_HB = 64
D_FF = 2816
CONV_W = 3
EPS = 1e-6
NEG = -1e30
MLA_SCALE = (NOPE + ROPE) ** -0.5
SB_SCALE = D_HB ** -0.5
LOG2E = math.log2(math.e)

LANES = 128
SUBLANES = 8
HEAD_PAD = LANES
VT_ROWS = 80
SB_DEAD = 110.0
SB_AHEAD = 2
VMEM_LIMIT = 56 * 1024 * 1024

F32 = jnp.float32
BF16 = jnp.bfloat16


def _dot(a, b):
    return jnp.dot(a, b, preferred_element_type=F32)


def _dot_t(a, b):
    return lax.dot_general(a, b, (((1,), (1,)), ((), ())), preferred_element_type=F32)


def _rms(x, g):
    return x * lax.rsqrt(jnp.mean(x * x, axis=-1, keepdims=True) + EPS) * g


def _sigmoid(x):
    return 1.0 / (1.0 + jnp.exp(-x))


def _params(n_parallel, n_arbitrary=0):
    return pltpu.CompilerParams(
        dimension_semantics=("parallel",) * n_parallel + ("arbitrary",) * n_arbitrary,
        vmem_limit_bytes=VMEM_LIMIT)


def _const_spec(shape):
    n = len(shape)
    return pl.BlockSpec(shape, lambda *_: (0,) * n, pipeline_mode=pl.Buffered(1))


def _ada_kernel(c_ref, w_ref, b_ref, o_ref):
    c = c_ref[...]
    s = (c * _sigmoid(c)).astype(BF16)
    o_ref[...] = _dot(s, w_ref[...].astype(BF16)) + b_ref[...]


def _ada(c_all, w_ada, b_ada):
    n = c_all.shape[0]
    nchunk = 6
    return pl.pallas_call(
        _ada_kernel,
        grid=(nchunk,),
        in_specs=[pl.BlockSpec((n, D_MODEL), lambda j: (0, 0)),
                  pl.BlockSpec((D_MODEL, D_MODEL), lambda j: (0, j)),
                  pl.BlockSpec((1, D_MODEL), lambda j: (0, j))],
        out_specs=pl.BlockSpec((n, D_MODEL), lambda j: (0, j)),
        out_shape=jax.ShapeDtypeStruct((n, 6 * D_MODEL), F32),
        compiler_params=_params(1),
        name="ada",
    )(c_all, w_ada, b_ada.reshape(1, -1))


_C_QLAT = 0
_C_CKV = _C_QLAT + Q_RANK
_C_KR = _C_CKV + KV_RANK
_C_KRR = _C_KR + LANES
_C_QB = _C_KRR + LANES
_C_KB = _C_QB + H_B * D_HB
_C_VB = _C_KB + H_B * D_HB
_N_K1 = _C_VB + H_B * D_HB


def _mod(ada_ref, comp, rows, tm):
    nseg = ada_ref.shape[0]
    if nseg == 1:
        return ada_ref[0, comp:comp + 1, :]
    seg = tm // nseg
    lo, hi = rows.start // seg, rows.stop // seg
    assert rows.start % seg == 0 and rows.stop % seg == 0
    return jnp.concatenate([jnp.broadcast_to(ada_ref[i, comp:comp + 1, :], (seg, ada_ref.shape[2]))
                            for i in range(lo, hi)], axis=0)


def _premix(x, ada_ref, g, rows, tm):
    return _rms(x, g) * (1.0 + _mod(ada_ref, 1, rows, tm)) + _mod(ada_ref, 0, rows, tm)


def _row_parts(tm):
    n = 2 if tm % (2 * 128) == 0 else 1
    return [slice(i * (tm // n), (i + 1) * (tm // n)) for i in range(n)]


def _mixer_kernel(x_ref, ada_ref, g_ref, win_ref, gq_ref, wuq_ref, gkv_ref, wuk_ref, wuv_ref, vone_ref,
                  cos_ref, sin_ref, *out_refs, expand_kv):
    if expand_kv:
        q_ref, ckv_ref, kr_ref, qb_ref, kb_ref, vb_ref, k_ref, vt_ref = out_refs
    else:
        q_ref, ckv_ref, kr_ref, qb_ref, kb_ref, vb_ref = out_refs
    tm = x_ref.shape[1]
    parts = _row_parts(tm)
    nq = H_A * HEAD_PAD
    ps = [_dot(_premix(x_ref[0, rows, :], ada_ref, g_ref[...], rows, tm).astype(BF16), win_ref[...])
          for rows in parts]
    q_lats = [_rms(p[:, _C_QLAT:_C_QLAT + Q_RANK], gq_ref[...]).astype(BF16) for p in ps]
    c_kvs = [_rms(p[:, _C_CKV:_C_CKV + KV_RANK], gkv_ref[...]) for p in ps]
    q2s = [_dot(q_lat, wuq_ref[...]) for q_lat in q_lats]
    if expand_kv:
        c_bfs = [c_kv.astype(BF16) for c_kv in c_kvs]
        k2s = [_dot(c_bf, wuk_ref[...]) for c_bf in c_bfs]
        v2s = [_dot(c_bf, wuv_ref[...]) for c_bf in c_bfs]
    for i, rows in enumerate(parts):
        p, q2 = ps[i], q2s[i]
        cos = cos_ref[rows, :]
        sin = sin_ref[rows, :]
        for hd in range(H_A):
            lo = hd * HEAD_PAD
            qh = q2[:, lo:lo + HEAD_PAD] * cos + q2[:, nq + lo:nq + lo + HEAD_PAD] * sin
            q_ref[0, rows, lo:lo + HEAD_PAD] = (qh * (MLA_SCALE * LOG2E)).astype(BF16)
        ckv_ref[0, rows, :] = c_kvs[i]
        kr = p[:, _C_KR:_C_KR + LANES] * cos + p[:, _C_KRR:_C_KRR + LANES] * sin
        kr_ref[0, rows, :] = kr[:, :ROPE]
        qb_ref[0, rows, :] = (p[:, _C_QB:_C_QB + H_B * D_HB] * SB_SCALE).astype(BF16)
        kb_ref[0, rows, :] = p[:, _C_KB:_C_KB + H_B * D_HB]
        vb_ref[0, rows, :] = p[:, _C_VB:_C_VB + H_B * D_HB]
        if expand_kv:
            for hd in range(H_A):
                lo = hd * HEAD_PAD
                k_ref[0, rows, lo:lo + HEAD_PAD] = (k2s[i][:, lo:lo + HEAD_PAD] + kr).astype(BF16)
            vt_ref[0, :, rows] = (v2s[i] + vone_ref[...]).T.astype(BF16)


def _mixer_inputs(x, ada, cos_t, sin_t, wts, *, tm, expand_kv):
    b, t, _ = x.shape
    nt = t // tm
    nseg = ada.shape[0] // b
    tok = lambda n: pl.BlockSpec((1, tm, n), lambda i, j: (i, j, 0))
    in_specs = [tok(D_MODEL),
                pl.BlockSpec((nseg, 6, D_MODEL), lambda i, j: (i, 0, 0)),
                _const_spec((1, D_MODEL)),
                _const_spec((D_MODEL, _N_K1)),
                _const_spec((1, Q_RANK)),
                _const_spec((Q_RANK, 2 * H_A * HEAD_PAD)),
                _const_spec((1, KV_RANK)),
                _const_spec((KV_RANK, H_A * HEAD_PAD)),
                _const_spec((KV_RANK, H_A * VT_ROWS)),
                _const_spec((1, H_A * VT_ROWS)),
                pl.BlockSpec((tm, LANES), lambda i, j: (j, 0)),
                pl.BlockSpec((tm, LANES), lambda i, j: (j, 0))]
    shapes = [((b, t, H_A * HEAD_PAD), BF16), ((b, t, KV_RANK), F32), ((b, t, ROPE), F32),
              ((b, t, H_B * D_HB), BF16), ((b, t, H_B * D_HB), F32), ((b, t, H_B * D_HB), F32)]
    if expand_kv:
        shapes += [((b, t, H_A * HEAD_PAD), BF16)]
    out_specs = [tok(s[-1]) for s, _ in shapes]
    if expand_kv:
        shapes += [((b, H_A * VT_ROWS, t), BF16)]
        out_specs += [pl.BlockSpec((1, H_A * VT_ROWS, tm), lambda i, j: (i, 0, j))]
    return pl.pallas_call(
        functools.partial(_mixer_kernel, expand_kv=expand_kv),
        grid=(b, nt),
        in_specs=in_specs,
        out_specs=out_specs,
        out_shape=[jax.ShapeDtypeStruct(s, d) for s, d in shapes],
        compiler_params=_params(2),
        name="mixer_in_kv" if expand_kv else "mixer_in",
    )(x, ada, wts["g_pre_mix"], wts["w_in_k1"], wts["g_q_lat"], wts["w_uq_ext"], wts["g_kv_lat"],
      wts["w_uk_pad"], wts["w_uv_t"], wts["v_ones"], cos_t, sin_t)


MLA_PAIR = 2


def _mla_prompt_kernel(q_ref, k_ref, vt_ref, bias_ref, o_ref, sa_ref, sb_ref, m_ref, acc_ref, *, blk):
    qi = pl.program_id(2)
    lanes = [slice(hh * HEAD_PAD, (hh + 1) * HEAD_PAD) for hh in range(2)]
    m_ref[...] = jnp.full(m_ref.shape, NEG, F32)
    acc_ref[...] = jnp.zeros(acc_ref.shape, F32)

    def start_of(j):
        return pl.multiple_of(j * blk, blk)

    def scores(s_ref, g):
        for hh, ln in enumerate(lanes):
            for b in range(MLA_PAIR):
                k = k_ref[0, pl.ds(start_of(MLA_PAIR * g + b), blk), ln]
                s_ref[hh, b] = _dot_t(k, q_ref[0, :, ln])

    def softmax_pv(s_ref, g, masked):
        for hh in range(2):
            ss = []
            for b in range(MLA_PAIR):
                s = s_ref[hh, b]
                ss.append(s + bias_ref[b] if masked else s)
            m_old = m_ref[hh]
            m_new = m_old
            for s in ss:
                m_new = jnp.maximum(m_new, jnp.max(s, axis=0, keepdims=True))
            acc = jnp.exp2(m_old - m_new) * acc_ref[hh]
            for b, s in enumerate(ss):
                p = jnp.exp2(s - m_new).astype(BF16)
                vt = vt_ref[0, hh * VT_ROWS:(hh + 1) * VT_ROWS, pl.ds(start_of(MLA_PAIR * g + b), blk)]
                acc = acc + _dot(vt, p)
            acc_ref[hh] = acc
            m_ref[hh] = m_new

    scores(sa_ref, 0)

    def body(h, c):
        scores(sb_ref, 2 * h + 1)
        softmax_pv(sa_ref, 2 * h, False)
        scores(sa_ref, 2 * h + 2)
        softmax_pv(sb_ref, 2 * h + 1, False)
        return c

    done = qi // 2 * 2
    lax.fori_loop(0, qi // 2, body, 0)

    @pl.when(done == qi)
    def _():
        softmax_pv(sa_ref, done, True)

    @pl.when(done != qi)
    def _():
        scores(sb_ref, done + 1)
        softmax_pv(sa_ref, done, False)
        softmax_pv(sb_ref, done + 1, True)

    outs = [acc_ref[hh][:DV] / acc_ref[hh][DV:DV + 1] for hh in range(2)]
    o_ref[0] = jnp.concatenate(outs, axis=0).T.astype(BF16)


def _mla_diag_bias(blk, tq):
    kpos = np.arange(MLA_PAIR * blk).reshape(MLA_PAIR, blk, 1)
    qpos = np.arange(tq).reshape(1, 1, tq)
    return np.where(kpos // CHUNK <= qpos // CHUNK, 0.0, NEG).astype(np.float32)


def _mla_prompt(q, k, vt, *, blk):
    b, t, _ = q.shape
    tq = MLA_PAIR * blk
    assert t % tq == 0
    return pl.pallas_call(
        functools.partial(_mla_prompt_kernel, blk=blk),
        grid=(b, H_A // 2, t // tq),
        in_specs=[pl.BlockSpec((1, tq, 2 * HEAD_PAD), lambda i, j, n: (i, n, j)),
                  pl.BlockSpec((1, t, 2 * HEAD_PAD), lambda i, j, n: (i, 0, j)),
                  pl.BlockSpec((1, 2 * VT_ROWS, t), lambda i, j, n: (i, j, 0)),
                  _const_spec((MLA_PAIR, blk, tq))],
        out_specs=pl.BlockSpec((1, tq, 2 * DV), lambda i, j, n: (i, n, j)),
        out_shape=jax.ShapeDtypeStruct((b, t, H_A * DV), BF16),
        scratch_shapes=[pltpu.VMEM((2, MLA_PAIR, blk, tq), F32), pltpu.VMEM((2, MLA_PAIR, blk, tq), F32),
                        pltpu.VMEM((2, 1, tq), F32), pltpu.VMEM((2, VT_ROWS, tq), F32)],
        compiler_params=_params(3),
        name="mla_prompt",
    )(q, k, vt, jnp.asarray(_mla_diag_bias(blk, tq)))


def _tri(n):
    r = lax.broadcasted_iota(jnp.int32, (n, n), 0)
    c = lax.broadcasted_iota(jnp.int32, (n, n), 1)
    return jnp.where(r > c, 1.0, 0.0).astype(BF16)


def _sb_chains(chains, r_init):
    n = len(chains)
    z, sp, sp_bf, later, a = ([None] * n for _ in range(5))
    r_before = [None] * n
    r = dict(r_init)
    out = {}

    def scores(c):
        _, qs, k, _, _, _ = chains[c]
        z[c] = k(qs) if callable(k) else _dot_t(qs, k)

    def softplus(c):
        g, _, _, _, _, keep = chains[c]
        zb = z[c].astype(BF16)
        sp_bf[c] = jnp.maximum(zb, 0.0) + jnp.log(1.0 + jnp.exp(-jnp.abs(zb)))
        sp[c] = sp_bf[c].astype(F32)
        if keep is not None:
            sp[c] = jnp.where(keep, sp[c], 0.0)
            sp_bf[c] = sp[c].astype(BF16)
        rs = jnp.sum(sp[c], axis=-1, keepdims=True)
        r_before[c] = r[g]
        r[g] = rs if r[g] is None else r[g] + rs

    def suffix(c):
        later[c] = _dot(sp_bf[c], chains[c][4])

    def weights(c):
        _, _, _, _, tri, keep = chains[c]
        tk = tri.shape[0]
        w = z[c] - sp[c] - later[c]
        if r_before[c] is not None:
            r_b = r_before[c]
            if r_b.shape[1] == LANES and tk > LANES:
                r_b = jnp.concatenate([r_b] * (tk // LANES), axis=1)
            elif r_b.shape[1] == LANES and tk < LANES:
                r_b = r_b[:, :tk]
            w = w - r_b
        w = jnp.exp2(w * LOG2E)
        a[c] = (w if keep is None else jnp.where(keep, w, 0.0)).astype(BF16)

    def values(c):
        g, _, _, v, _, _ = chains[c]
        o = v(a[c]) if callable(v) else _dot(a[c], v)
        out[g] = o if g not in out else out[g] + o

    for stage in (scores, softplus, suffix, weights, values):
        for c in range(n):
            stage(c)
    return out, r


def _head_mask(x, hh):
    lane = lax.broadcasted_iota(jnp.int32, x.shape, 1)
    mine = jnp.logical_and(lane >= hh * D_HB, lane < (hh + 1) * D_HB)
    return jnp.where(mine, x, jnp.zeros_like(x))


def _stack_heads(q):
    return jnp.concatenate([_head_mask(q, 0), _head_mask(q, 1)], axis=0)


def _causal_pair(t, tk):
    row = lax.broadcasted_iota(jnp.int32, (t, tk), 0)
    col = lax.broadcasted_iota(jnp.int32, (t, tk), 1)
    keep = col < row
    return jnp.concatenate([keep, keep], axis=0)


def _sb_tail(qs, load, tri, acc_ref, r_ref, j_start, keep=None):
    def cond(c):
        j, rmin = c
        return jnp.logical_and(j >= 0, rmin < SB_DEAD)

    def body(c):
        j, _ = c
        o, r_new = _sb_chains([(0, qs) + load(j) + (tri, keep)], {0: r_ref[...]})
        acc_ref[...] += o[0]
        r_ref[...] = r_new[0]
        return j - 1, jnp.min(r_new[0])

    lax.while_loop(cond, body, (j_start, jnp.min(r_ref[...])))


def _unstack_heads(acc, t):
    lane = lax.broadcasted_iota(jnp.int32, (t, LANES), 1)
    return jnp.where(lane < D_HB, acc[:t], acc[t:])


def _sb_prompt_kernel(q_ref, k_ref, v_ref, o_ref, acc_ref, r_ref, *, tq, nsub):
    qi = pl.program_id(2)
    tri = _tri(tq)
    causal = _causal_pair(tq, tq)

    def load(j):
        start = pl.multiple_of(j * tq, tq)
        return (k_ref[0, pl.ds(start, tq), :].astype(BF16),
                v_ref[0, pl.ds(start, tq), :].astype(BF16))

    qss = [_stack_heads(q_ref[0, sub * tq:(sub + 1) * tq, :]) for sub in range(nsub)]
    chains = []
    for back in range(SB_AHEAD + 1):
        for sub in range(nsub):
            blk = qi * nsub + sub
            keep = causal if back == 0 else blk >= back
            chains.append((sub, qss[sub]) + load(jnp.maximum(blk - back, 0)) + (tri, keep))
    o, r = _sb_chains(chains, {sub: None for sub in range(nsub)})
    for sub in range(nsub):
        acc_ref[sub] = o[sub]
        r_ref[sub] = jnp.broadcast_to(r[sub], (2 * tq, LANES))
    for sub in range(nsub):
        _sb_tail(qss[sub], load, tri, acc_ref.at[sub], r_ref.at[sub], qi * nsub + sub - SB_AHEAD - 1)
        o_ref[0, sub * tq:(sub + 1) * tq, :] = _unstack_heads(acc_ref[sub], tq).astype(BF16)


def _sb_prompt(q, k, v, *, tq, nsub):
    b, t, _ = q.shape
    rows = tq * nsub
    return pl.pallas_call(
        functools.partial(_sb_prompt_kernel, tq=tq, nsub=nsub),
        grid=(b, H_B // 2, t // rows),
        in_specs=[pl.BlockSpec((1, rows, LANES), lambda i, j, n: (i, n, j)),
                  pl.BlockSpec((1, t, LANES), lambda i, j, n: (i, 0, j)),
                  pl.BlockSpec((1, t, LANES), lambda i, j, n: (i, 0, j))],
        out_specs=pl.BlockSpec((1, rows, LANES), lambda i, j, n: (i, n, j)),
        out_shape=jax.ShapeDtypeStruct((b, t, H_B * D_HB), BF16),
        scratch_shapes=[pltpu.VMEM((nsub, 2 * tq, LANES), F32), pltpu.VMEM((nsub, 2 * tq, LANES), F32)],
        compiler_params=_params(3),
        name="sb_prompt",
    )(q, k, v)


def _sb_sample_kernel(q_ref, kn_ref, vn_ref, kc_ref, vc_ref, o_ref, left_ref, acc_ref, r_ref, *, tk):
    tq = q_ref.shape[1]
    nblk = kc_ref.shape[1] // tk
    qs = _stack_heads(q_ref[0])
    tri = _tri(tk)

    def load(j):
        start = pl.multiple_of(j * tk, tk)
        return (kc_ref[0, pl.ds(start, tk), :].astype(BF16),
                vc_ref[0, pl.ds(start, tk), :].astype(BF16))

    o, r = _sb_chains([(0, qs, kn_ref[0].astype(BF16), vn_ref[0].astype(BF16), _tri(tq), _causal_pair(tq, tq)),
                       (0, qs) + load(nblk - 1) + (tri, None)], {0: None})
    acc_ref[...] = o[0]
    r_ref[...] = jnp.broadcast_to(r[0], (2 * tq, LANES))
    _sb_tail(qs, load, tri, acc_ref, r_ref, jnp.int32(nblk - 2))
    o_ref[0] = _unstack_heads(acc_ref[...], tq).astype(BF16)
    left_ref[...] = jnp.full(left_ref.shape, jnp.min(r_ref[...]), F32)


def _sb_sample_call(q, k_new, v_new, k_cache, v_cache, *, tk):
    b, t, _ = q.shape
    past = k_cache.shape[1]
    new = pl.BlockSpec((1, t, LANES), lambda i, j: (i, 0, j))
    old = pl.BlockSpec((1, past, LANES), lambda i, j: (i, 0, j))
    return pl.pallas_call(
        functools.partial(_sb_sample_kernel, tk=tk),
        grid=(b, H_B // 2),
        in_specs=[new, new, new, old, old],
        out_specs=[new, pl.BlockSpec((1, 1, SUBLANES, LANES), lambda i, j: (i, j, 0, 0))],
        out_shape=[jax.ShapeDtypeStruct((b, t, H_B * D_HB), BF16),
                   jax.ShapeDtypeStruct((b, H_B // 2, SUBLANES, LANES), F32)],
        scratch_shapes=[pltpu.VMEM((2 * t, LANES), F32), pltpu.VMEM((2 * t, LANES), F32)],
        compiler_params=_params(2),
        name="sb_sample",
    )(q, k_new, v_new, k_cache, v_cache)


SB_DECODE_KEYS = 256
SB_DECODE_AHEAD = 1


def _sb_decode_kernel(q_ref, kn_ref, vn_ref, kct_ref, vct_ref, o_ref, left_ref, acc_ref, r_ref):
    t = q_ref.shape[1]
    rows = H_B * t
    ck = SB_DECODE_KEYS
    nstep = kct_ref.shape[3] // ck
    assert t & (t - 1) == 0
    head = lambda x, h: x[:, h * D_HB:(h + 1) * D_HB]
    q = q_ref[0]
    q3 = jnp.stack([head(q, h) for h in range(H_B)], axis=0)

    def cached(j):
        start = pl.multiple_of(j * ck, ck)
        kt = kct_ref[0, :, :, pl.ds(start, ck)].astype(BF16)
        vt = vct_ref[0, :, :, pl.ds(start, ck)].astype(BF16)

        def scores(_):
            z = lax.dot_general(q3, kt, (((2,), (1,)), ((0,), (0,))), preferred_element_type=F32)
            return z.reshape(rows, ck)

        def values(a):
            o = lax.dot_general(a.reshape(H_B, t, ck), vt, (((2,), (2,)), ((0,), (0,))),
                                preferred_element_type=F32)
            return o.reshape(rows, D_HB)

        return scores, values

    kn, vn = kn_ref[0].astype(BF16), vn_ref[0].astype(BF16)

    def new_scores(_):
        return jnp.concatenate([_dot_t(head(q, h), head(kn, h)) for h in range(H_B)], axis=0)

    def new_values(a):
        return jnp.concatenate([_dot(a[h * t:(h + 1) * t], head(vn, h)) for h in range(H_B)], axis=0)

    r_i = lax.broadcasted_iota(jnp.int32, (rows, t), 0)
    c_i = lax.broadcasted_iota(jnp.int32, (rows, t), 1)
    tri = _tri(ck)
    chains = [(0, None, new_scores, new_values, _tri(t), c_i < (r_i & (t - 1)))]
    ahead = min(SB_DECODE_AHEAD, nstep)
    for back in range(1, ahead + 1):
        chains.append((0, None) + cached(nstep - back) + (tri, None))
    o, r = _sb_chains(chains, {0: None})
    acc_ref[...] = o[0]
    r_ref[...] = jnp.broadcast_to(r[0], (rows, LANES))
    _sb_tail(None, cached, tri, acc_ref, r_ref, jnp.int32(nstep - ahead - 1))
    acc = acc_ref[...]
    o_ref[0] = jnp.concatenate([acc[h * t:(h + 1) * t] for h in range(H_B)], axis=1).astype(BF16)
    left_ref[...] = jnp.full(left_ref.shape, jnp.min(r_ref[...]), F32)


def _sb_decode_call(q, k_new, v_new, k_cache_t, v_cache_t, *, recent):
    b, t, _ = q.shape
    past = k_cache_t.shape[3]
    new = pl.BlockSpec((1, t, H_B * D_HB), lambda i: (i, 0, 0))
    old = pl.BlockSpec((1, H_B, D_HB, recent), lambda i: (i, 0, 0, past // recent - 1))
    return pl.pallas_call(
        _sb_decode_kernel,
        grid=(b,),
        in_specs=[new, new, new, old, old],
        out_specs=[new, pl.BlockSpec((1, SUBLANES, LANES), lambda i: (i, 0, 0))],
        out_shape=[jax.ShapeDtypeStruct((b, t, H_B * D_HB), BF16),
                   jax.ShapeDtypeStruct((b, SUBLANES, LANES), F32)],
        scratch_shapes=[pltpu.VMEM((H_B * t, D_HB), F32), pltpu.VMEM((H_B * t, LANES), F32)],
        compiler_params=_params(1),
        name="sb_decode",
    )(q, k_new, v_new, k_cache_t, v_cache_t)


def _sb_sample(q, k_new, v_new, k_cache, v_cache, *, tk, recent):
    b, past = k_cache.shape[:2]
    t = q.shape[1]
    flat = lambda c: c.reshape(b, c.shape[1], H_B * D_HB)
    full = lambda: _sb_sample_call(q, k_new, v_new, flat(k_cache), flat(v_cache), tk=tk)[0]
    if not (t & (t - 1) == 0 and past % recent == 0 and recent % SB_DECODE_KEYS == 0):
        return full()
    key_minor = lambda c: jnp.transpose(c, (0, 2, 3, 1))
    o, left = _sb_decode_call(q, k_new, v_new, key_minor(k_cache), key_minor(v_cache), recent=recent)
    if recent == past:
        return o
    return lax.cond(jnp.min(left) < SB_DEAD, full, lambda: o)


def _mla_sample_kernel(q_ref, cn_ref, rn_ref, cc_ref, rct_ref, wabs_ref, wuv_ref, o_ref, *, kc, past_len):
    t = q_ref.shape[1]
    past = cc_ref.shape[1]
    qa, qr = [], []
    for hd in range(H_A):
        qh = q_ref[0, :, hd * HEAD_PAD:(hd + 1) * HEAD_PAD]
        qa.append(_dot(qh, wabs_ref[hd]).astype(BF16))
        qr.append(qh[:, :ROPE])
    qa = jnp.concatenate(qa, axis=0)
    qr = jnp.concatenate(qr, axis=0)
    rows = H_A * t

    pieces = [(cc_ref[0, c * kc:(c + 1) * kc, :], rct_ref[0, :, c * kc:(c + 1) * kc], True, c * kc)
              for c in range(past // kc)]
    pieces.append((cn_ref[0], rn_ref[0], False, past_len))
    scores, lat = [], []
    for ckv, kr, key_minor, k0 in pieces:
        ckv = ckv.astype(BF16)
        kr = kr.astype(BF16)
        s = _dot_t(qa, ckv) + (_dot(qr, kr) if key_minor else _dot_t(qr, kr))
        n = ckv.shape[0]
        if (k0 + n - 1) // CHUNK > past_len // CHUNK:
            qpos = past_len + lax.broadcasted_iota(jnp.int32, (rows, n), 0) % t
            kpos = k0 + lax.broadcasted_iota(jnp.int32, (rows, n), 1)
            s = jnp.where((kpos >> CHUNK_SHIFT) <= (qpos >> CHUNK_SHIFT), s, NEG)
        scores.append(s)
        lat.append(ckv)
    m = functools.reduce(jnp.maximum, [jnp.max(s, axis=-1, keepdims=True) for s in scores])
    l = jnp.zeros((rows, 1), F32)
    o_lat = jnp.zeros((rows, KV_RANK), F32)
    for s, ckv in zip(scores, lat):
        p = jnp.exp2(s - m)
        l = l + jnp.sum(p, axis=-1, keepdims=True)
        o_lat = o_lat + _dot(p.astype(BF16), ckv)
    o_lat = (o_lat / l).astype(BF16)
    o = jnp.zeros((t, H_A * DV), F32)
    for hd in range(H_A):
        o = o + _dot(o_lat[hd * t:(hd + 1) * t], wuv_ref[hd])
    o_ref[0] = o.astype(BF16)


def _mla_sample(q, ckv_new, kr_new, ckv_cache, kr_cache, w_abs, w_uv_heads, *, past_len, kc):
    b, t, _ = q.shape
    past = ckv_cache.shape[1]
    row = lambda n, s: pl.BlockSpec((1, n, s), lambda i: (i, 0, 0))
    kr_cache = jnp.transpose(kr_cache, (0, 2, 1))
    return pl.pallas_call(
        functools.partial(_mla_sample_kernel, kc=kc, past_len=past_len),
        grid=(b,),
        in_specs=[row(t, H_A * HEAD_PAD), row(t, KV_RANK), row(t, ROPE),
                  row(past, KV_RANK), row(ROPE, past),
                  _const_spec((H_A, HEAD_PAD, KV_RANK)), _const_spec((H_A, KV_RANK, H_A * DV))],
        out_specs=row(t, H_A * DV),
        out_shape=jax.ShapeDtypeStruct((b, t, H_A * DV), BF16),
        compiler_params=_params(1),
        name="mla_sample",
    )(q, ckv_new, kr_new, ckv_cache, kr_cache, w_abs, w_uv_heads)


def _merge_kernel(x_ref, ada_ref, g_ref, wg_ref, oa_ref, ob_ref, wpa_ref, wpb_ref, wo_ref, gpost_ref, o_ref):
    tm = x_ref.shape[1]
    parts = _row_parts(tm)
    xs = [x_ref[0, rows, :] for rows in parts]
    gate_in = [_dot(_premix(x, ada_ref, g_ref[...], rows, tm).astype(BF16), wg_ref[...])
               for x, rows in zip(xs, parts)]
    pas = [_dot(oa_ref[0, rows, :], wpa_ref[...]) for rows in parts]
    pbs = [_dot(ob_ref[0, rows, :], wpb_ref[...]) for rows in parts]
    merged = [(_sigmoid(gi[:, :D_MODEL]) * pa + _sigmoid(gi[:, D_MODEL:]) * pb).astype(BF16)
              for gi, pa, pb in zip(gate_in, pas, pbs)]
    mos = [_dot(m, wo_ref[...]) for m in merged]
    for rows, x, mo in zip(parts, xs, mos):
        o_ref[0, rows, :] = x + _mod(ada_ref, 2, rows, tm) * _rms(mo, gpost_ref[...])


def _merge(x, ada, o_a, o_b, wts, *, tm):
    b, t, _ = x.shape
    nseg = ada.shape[0] // b
    tok = lambda n: pl.BlockSpec((1, tm, n), lambda i, j: (i, j, 0))
    return pl.pallas_call(
        _merge_kernel,
        grid=(b, t // tm),
        in_specs=[tok(D_MODEL),
                  pl.BlockSpec((nseg, 6, D_MODEL), lambda i, j: (i, 0, 0)),
                  _const_spec((1, D_MODEL)),
                  _const_spec((D_MODEL, 2 * D_MODEL)),
                  tok(H_A * DV), tok(H_B * D_HB),
                  _const_spec((H_A * DV, D_MODEL)), _const_spec((H_B * D_HB, D_MODEL)),
                  _const_spec((D_MODEL, D_MODEL)),
                  _const_spec((1, D_MODEL))],
        out_specs=tok(D_MODEL),
        out_shape=jax.ShapeDtypeStruct((b, t, D_MODEL), F32),
        compiler_params=_params(2),
        name="merge",
    )(x, ada, wts["g_pre_mix"], wts["w_gate"], o_a, o_b, wts["w_proj_a"], wts["w_proj_b"],
      wts["w_out"], wts["g_post_mix"])


FF_CHUNK = 256
SUB_GROUPS = 4
SUB_ROWS = SUB_GROUPS * SUBLANES
HALO = 2 * SUBLANES


def _gelu_tanh(a):
    return 0.5 * a * (1.0 + jnp.tanh(math.sqrt(2.0 / math.pi) * (a + 0.044715 * (a * a * a))))


def _ffn_kernel(x_ref, ada_ref, g_ref, wup_ref, cw_ref, cb_ref, wdn_ref, gpost_ref, cs_ref,
                o_ref, nc_ref, halo_ref, perm_ref, *, tm):
    ti = pl.program_id(1)
    nt = pl.num_programs(1)
    nsub = tm // SUB_ROWS
    segmented = cs_ref.shape[0] > 1
    assert not segmented or cs_ref.shape[0] == nsub
    all_rows = slice(0, tm)

    if not segmented:
        @pl.when(ti == 0)
        def _():
            halo_ref[...] = jnp.zeros((HALO, 2 * D_FF), F32)
            halo_ref[SUBLANES - 1:SUBLANES, :] = cs_ref[0, 0:1, :]
            halo_ref[HALO - 1:HALO, :] = cs_ref[0, 1:2, :]

    nc = D_MODEL // LANES
    for c in range(nc):
        perm_ref[c] = x_ref[0, :, c * LANES:(c + 1) * LANES]
    x = jnp.concatenate(
        [jnp.concatenate([perm_ref[c, pl.ds(j * SUB_ROWS + i, SUBLANES, stride=SUB_GROUPS), :] for c in range(nc)],
                         axis=1)
         for j in range(nsub) for i in range(SUB_GROUPS)], axis=0)
    h2 = (_rms(x, g_ref[...]) * (1.0 + _mod(ada_ref, 4, all_rows, tm)) + _mod(ada_ref, 3, all_rows, tm)).astype(BF16)
    first = lax.broadcasted_iota(jnp.int32, (SUBLANES, FF_CHUNK), 0) == 0

    def up(col):
        cols = slice(col, col + FF_CHUNK)
        u = _dot(h2, wup_ref[:, cols])
        grp = [u[g * SUBLANES:(g + 1) * SUBLANES] for g in range(tm // SUBLANES)]
        own2 = [pltpu.roll(grp[j * SUB_GROUPS + SUB_GROUPS - 2], 1, axis=0) for j in range(nsub)]
        own3 = [pltpu.roll(grp[j * SUB_GROUPS + SUB_GROUPS - 1], 1, axis=0) for j in range(nsub)]
        if segmented:
            pred2 = [jnp.broadcast_to(cs_ref[j, 0:1, cols], (SUBLANES, FF_CHUNK)) for j in range(nsub)]
            pred3 = [jnp.broadcast_to(cs_ref[j, 1:2, cols], (SUBLANES, FF_CHUNK)) for j in range(nsub)]
            for j in range(nsub):
                last = (j + 1) * SUB_ROWS
                nc_ref[j, 0:1, cols] = u[last - SUBLANES - 1:last - SUBLANES]
                nc_ref[j, 1:2, cols] = u[last - 1:last]
        else:
            pred2 = [pltpu.roll(halo_ref[0:SUBLANES, cols], 1, axis=0)] + own2[:-1]
            pred3 = [pltpu.roll(halo_ref[SUBLANES:HALO, cols], 1, axis=0)] + own3[:-1]
            halo_ref[:, cols] = u[tm - HALO:]
        u1, u2 = [], []
        for j in range(nsub):
            f1 = jnp.where(first, pred3[j], own3[j])
            f2 = jnp.where(first, pred2[j], own2[j])
            g0 = j * SUB_GROUPS
            u1 += [f1] + grp[g0:g0 + SUB_GROUPS - 1]
            u2 += [f2, f1] + grp[g0:g0 + SUB_GROUPS - 2]
        u1 = jnp.concatenate(u1, axis=0)
        u2 = jnp.concatenate(u2, axis=0)
        return (cb_ref[:, cols] + cw_ref[0:1, cols] * u2 + cw_ref[1:2, cols] * u1 + cw_ref[2:3, cols] * u)

    nchunk = D_FF // FF_CHUNK
    acc = jnp.zeros((tm, D_MODEL), F32)
    ya, yb = up(0), up(D_FF)
    for c in range(nchunk):
        if c + 1 < nchunk:
            ya_next, yb_next = up((c + 1) * FF_CHUNK), up(D_FF + (c + 1) * FF_CHUNK)
        g = (_gelu_tanh(ya) * yb).astype(BF16)
        acc = acc + _dot(g, wdn_ref[c * FF_CHUNK:(c + 1) * FF_CHUNK, :])
        if c + 1 < nchunk:
            ya, yb = ya_next, yb_next
    y = x + _mod(ada_ref, 5, all_rows, tm) * _rms(acc, gpost_ref[...])
    for j in range(nsub):
        for i in range(SUB_GROUPS):
            g0 = (j * SUB_GROUPS + i) * SUBLANES
            for c in range(nc):
                perm_ref[c, pl.ds(j * SUB_ROWS + i, SUBLANES, stride=SUB_GROUPS), :] = (
                    y[g0:g0 + SUBLANES, c * LANES:(c + 1) * LANES])
    for c in range(nc):
        o_ref[0, :, c * LANES:(c + 1) * LANES] = perm_ref[c]

    if not segmented:
        @pl.when(ti == nt - 1)
        def _():
            nc_ref[0, 0:1, :] = halo_ref[SUBLANES - 1:SUBLANES, :]
            nc_ref[0, 1:2, :] = halo_ref[HALO - 1:HALO, :]


def _ffn(x, ada, conv_state, wts, *, tm):
    b, t, _ = x.shape
    nseg = ada.shape[0] // b
    assert tm % SUB_ROWS == 0 and t % tm == 0 and (nseg == 1 or (t == tm and t == nseg * SUB_ROWS))
    tok = pl.BlockSpec((1, tm, D_MODEL), lambda i, j: (i, j, 0))
    state = pl.BlockSpec((nseg, CONV_W - 1, 2 * D_FF), lambda i, j: (i, 0, 0))
    return pl.pallas_call(
        functools.partial(_ffn_kernel, tm=tm),
        grid=(b, t // tm),
        in_specs=[tok,
                  pl.BlockSpec((nseg, 6, D_MODEL), lambda i, j: (i, 0, 0)),
                  _const_spec((1, D_MODEL)),
                  _const_spec((D_MODEL, 2 * D_FF)),
                  _const_spec((CONV_W, 2 * D_FF)),
                  _const_spec((1, 2 * D_FF)),
                  _const_spec((D_FF, D_MODEL)),
                  _const_spec((1, D_MODEL)),
                  state],
        out_specs=[tok, state],
        out_shape=[jax.ShapeDtypeStruct((b, t, D_MODEL), F32),
                   jax.ShapeDtypeStruct((b * nseg, CONV_W - 1, 2 * D_FF), F32)],
        scratch_shapes=[pltpu.VMEM((HALO, 2 * D_FF), F32), pltpu.VMEM((D_MODEL // LANES, tm, LANES), F32)],
        compiler_params=_params(1, 1),
        name="conv_ffn",
    )(x, ada, wts["g_pre_ffn"], wts["w_up"], wts["conv_w"], wts["conv_b"], wts["w_down"],
      wts["g_post_ffn"], conv_state)


def _rope_tables(first_pos, t):
    inv = ROPE_BASE ** (-np.arange(0, ROPE, 2, dtype=np.float64) / ROPE)
    ang = np.arange(first_pos, first_pos + t, dtype=np.float64)[:, None] * inv[None, :]
    cos, sin = jnp.asarray(np.cos(ang), F32), jnp.asarray(np.sin(ang), F32)
    cos_t = jnp.concatenate([cos, cos, jnp.ones((t, NOPE), F32), jnp.zeros((t, HEAD_PAD - ROPE - NOPE), F32)], axis=1)
    sin_t = jnp.concatenate([-sin, sin, jnp.zeros((t, HEAD_PAD - ROPE), F32)], axis=1)
    return cos_t, sin_t


def _swap_halves(w):
    return jnp.concatenate([w[..., ROPE // 2:], w[..., :ROPE // 2]], axis=-1)


def _pad_lanes(w, n):
    return jnp.pad(w, [(0, 0)] * (w.ndim - 1) + [(0, n - w.shape[-1])])


def _layer_weights(l, g_pre_mix, g_post_mix, g_pre_ffn, g_post_ffn, w_in, g_q_lat, w_uq, g_kv_lat,
                   w_uk, w_uv, w_proj_a, w_proj_b, w_out, w_up, conv_w, conv_b, w_down):
    splits = np.cumsum([Q_RANK, KV_RANK, ROPE, H_B * D_HB, H_B * D_HB, H_B * D_HB, D_MODEL])
    wi = w_in[l]
    w_ql, w_ckv, w_kr, w_qb, w_kb, w_vb, w_ga, w_gb = jnp.split(wi, splits, axis=1)
    w_in_k1 = jnp.concatenate([w_ql, w_ckv, _pad_lanes(w_kr, LANES), _pad_lanes(_swap_halves(w_kr), LANES),
                               w_qb, w_kb, w_vb], axis=1).astype(BF16)
    wq = w_uq[l].reshape(Q_RANK, H_A, NOPE + ROPE)
    wq_nope, wq_rope = wq[..., :NOPE], wq[..., NOPE:]
    zeros = jnp.zeros((Q_RANK, H_A, HEAD_PAD - ROPE - NOPE), F32)
    q_main = jnp.concatenate([wq_rope, wq_nope, zeros], axis=-1).reshape(Q_RANK, H_A * HEAD_PAD)
    q_rot = _pad_lanes(_swap_halves(wq_rope), HEAD_PAD).reshape(Q_RANK, H_A * HEAD_PAD)
    wk = w_uk[l].reshape(KV_RANK, H_A, NOPE)
    w_uk_pad = jnp.pad(wk, [(0, 0), (0, 0), (ROPE, HEAD_PAD - ROPE - NOPE)])
    w_abs = jnp.transpose(w_uk_pad, (1, 2, 0))
    w_uv_t = _pad_lanes(w_uv[l].reshape(KV_RANK, H_A, DV), VT_ROWS)
    v_ones = np.zeros((H_A, VT_ROWS), np.float32)
    v_ones[:, DV] = 1.0
    head_of_col = jnp.arange(H_A * DV) // DV
    w_uv_heads = jnp.where(head_of_col[None, None, :] == jnp.arange(H_A)[:, None, None], w_uv[l][None], 0.0)
    row = lambda g: g[l].reshape(1, -1)
    return {
        "g_pre_mix": row(g_pre_mix), "g_post_mix": row(g_post_mix),
        "g_pre_ffn": row(g_pre_ffn), "g_post_ffn": row(g_post_ffn),
        "g_q_lat": row(g_q_lat), "g_kv_lat": row(g_kv_lat),
        "w_in_k1": w_in_k1,
        "w_gate": jnp.concatenate([w_ga, w_gb], axis=1).astype(BF16),
        "w_uq_ext": jnp.concatenate([q_main, q_rot], axis=1).astype(BF16),
        "w_uk_pad": w_uk_pad.reshape(KV_RANK, H_A * HEAD_PAD).astype(BF16),
        "w_abs": w_abs.astype(BF16),
        "w_uv_t": w_uv_t.reshape(KV_RANK, H_A * VT_ROWS).astype(BF16),
        "v_ones": jnp.asarray(v_ones.reshape(1, H_A * VT_ROWS)),
        "w_uv_heads": w_uv_heads.astype(BF16),
        "w_proj_a": w_proj_a[l].astype(BF16), "w_proj_b": w_proj_b[l].astype(BF16),
        "w_out": w_out[l].astype(BF16),
        "w_up": w_up[l].astype(BF16), "conv_w": conv_w[l], "conv_b": conv_b[l].reshape(1, -1),
        "w_down": w_down[l].astype(BF16),
    }


def _pick_tile(t, want):
    tm = min(t, want)
    assert t % tm == 0
    return tm


def _layer(x, ada, first_pos, past, wts):
    b, t, _ = x.shape
    cos_t, sin_t = _rope_tables(first_pos, t)
    tm = _pick_tile(t, 512)
    if past is None:
        q, ckv, kr, qb, kb, vb, k_mla, vt_mla = _mixer_inputs(x, ada, cos_t, sin_t, wts, tm=tm, expand_kv=True)
        o_a = _mla_prompt(q, k_mla, vt_mla, blk=_pick_tile(t, 512))
        o_b = _sb_prompt(qb, kb, vb, tq=_pick_tile(t, 256), nsub=2 if t % 512 == 0 else 1)
        conv_state = jnp.zeros((b, CONV_W - 1, 2 * D_FF), F32)
    else:
        past_ckv, past_kr, past_k, past_v, conv_state = past
        past_len = past_ckv.shape[1]
        flat = t == SUB_ROWS and (b * t) % 256 == 0 and b * t <= 512
        if flat:
            x = x.reshape(1, b * t, D_MODEL)
            cos_t, sin_t = jnp.tile(cos_t, (b, 1)), jnp.tile(sin_t, (b, 1))
            tm = b * t
        per_batch = lambda a: a.reshape(b, t, a.shape[-1])
        q, ckv, kr, qb, kb, vb = map(per_batch, _mixer_inputs(x, ada, cos_t, sin_t, wts, tm=tm, expand_kv=False))
        o_a = _mla_sample(q, ckv, kr, past_ckv, past_kr, wts["w_abs"], wts["w_uv_heads"],
                          past_len=past_len, kc=_pick_tile(past_len, 1024))
        o_b = _sb_sample(qb, kb, vb, past_k, past_v, tk=_pick_tile(past_len, 256), recent=512)
        o_a, o_b = (o.reshape(x.shape[0], x.shape[1], o.shape[-1]) for o in (o_a, o_b))
    x1 = _merge(x, ada, o_a, o_b, wts, tm=tm)
    y, new_conv = _ffn(x1, ada, conv_state, wts, tm=_pick_tile(x.shape[1], 256))
    state = (ckv, kr, kb.reshape(b, t, H_B, D_HB), vb.reshape(b, t, H_B, D_HB), new_conv)
    return y.reshape(b, t, D_MODEL), state


def kernel(x_prompt, x_sample, cache_mla_ckv, cache_mla_krope, cache_sb_k, cache_sb_v, state_ffn_conv,
           c_prompt, c_sample, w_ada, b_ada, g_pre_mix, g_post_mix, g_pre_ffn, g_post_ffn,
           w_in, g_q_lat, w_uq, g_kv_lat, w_uk, w_uv, w_proj_a, w_proj_b, w_out,
           w_up, conv_w, conv_b, w_down):
    depth = w_in.shape[0]
    nb_p = x_prompt.shape[0]
    past_len = cache_mla_ckv.shape[2]
    xp, xs = x_prompt, x_sample
    c_all = jnp.concatenate([c_prompt, c_sample], axis=0)
    st_p = [[] for _ in range(5)]
    st_s = [[] for _ in range(5)]
    for l in range(depth):
        wts = _layer_weights(l, g_pre_mix, g_post_mix, g_pre_ffn, g_post_ffn, w_in, g_q_lat, w_uq, g_kv_lat,
                             w_uk, w_uv, w_proj_a, w_proj_b, w_out, w_up, conv_w, conv_b, w_down)
        ada = _ada(c_all, w_ada[l], b_ada[l]).reshape(-1, 6, D_MODEL)
        xp, sp = _layer(xp, ada[:nb_p], 0, None, wts)
        past = (cache_mla_ckv[l], cache_mla_krope[l], cache_sb_k[l], cache_sb_v[l], state_ffn_conv[l])
        xs, ss = _layer(xs, ada[nb_p:], past_len, past, wts)
        for i in range(5):
            st_p[i].append(sp[i])
            st_s[i].append(ss[i])
    p_state = [jnp.stack(a, axis=0) for a in st_p]
    s_state = [jnp.stack(a, axis=0) for a in st_s]
    return (xp, xs, *p_state, *s_state)
```

```python
import functools
import math

import numpy as np
import jax
import jax.numpy as jnp
from jax import lax
from jax.experimental import pallas as pl
from jax.experimental.pallas import tpu as pltpu

D_MODEL = 1024
CHUNK = 64
CHUNK_SHIFT = 6
H_A = 8
NOPE = 64
ROPE = 32
DV = 64
Q_RANK = 384
KV_RANK = 256
ROPE_BASE = 10000.0
H_B = 8
D_HB = 64
D_FF = 2816
CONV_W = 3
EPS = 1e-6
NEG = -1e30
MLA_SCALE = (NOPE + ROPE) ** -0.5
SB_SCALE = D_HB ** -0.5
LOG2E = math.log2(math.e)

LANES = 128
SUBLANES = 8
HEAD_PAD = LANES
VT_ROWS = 80
SB_DEAD = 110.0
SB_AHEAD = 2
VMEM_LIMIT = 56 * 1024 * 1024

F32 = jnp.float32
BF16 = jnp.bfloat16


def _dot(a, b):
    return jnp.dot(a, b, preferred_element_type=F32)


def _dot_t(a, b):
    return lax.dot_general(a, b, (((1,), (1,)), ((), ())), preferred_element_type=F32)


def _rms(x, g):
    return x * lax.rsqrt(jnp.mean(x * x, axis=-1, keepdims=True) + EPS) * g


def _sigmoid(x):
    return 1.0 / (1.0 + jnp.exp(-x))


def _params(n_parallel, n_arbitrary=0):
    return pltpu.CompilerParams(
        dimension_semantics=("parallel",) * n_parallel + ("arbitrary",) * n_arbitrary,
        vmem_limit_bytes=VMEM_LIMIT)


def _const_spec(shape):
    n = len(shape)
    return pl.BlockSpec(shape, lambda *_: (0,) * n, pipeline_mode=pl.Buffered(1))


def _ada_kernel(c_ref, w_ref, b_ref, o_ref):
    c = c_ref[...]
    s = (c * _sigmoid(c)).astype(BF16)
    o_ref[...] = _dot(s, w_ref[...].astype(BF16)) + b_ref[...]


def _ada(c_all, w_ada, b_ada):
    n = c_all.shape[0]
    nchunk = 6
    return pl.pallas_call(
        _ada_kernel,
        grid=(nchunk,),
        in_specs=[pl.BlockSpec((n, D_MODEL), lambda j: (0, 0)),
                  pl.BlockSpec((D_MODEL, D_MODEL), lambda j: (0, j)),
                  pl.BlockSpec((1, D_MODEL), lambda j: (0, j))],
        out_specs=pl.BlockSpec((n, D_MODEL), lambda j: (0, j)),
        out_shape=jax.ShapeDtypeStruct((n, 6 * D_MODEL), F32),
        compiler_params=_params(1),
        name="ada",
    )(c_all, w_ada, b_ada.reshape(1, -1))


_C_QLAT = 0
_C_CKV = _C_QLAT + Q_RANK
_C_KR = _C_CKV + KV_RANK
_C_KRR = _C_KR + LANES
_C_QB = _C_KRR + LANES
_C_KB = _C_QB + H_B * D_HB
_C_VB = _C_KB + H_B * D_HB
_N_K1 = _C_VB + H_B * D_HB


def _mod(ada_ref, comp, rows, tm):
    nseg = ada_ref.shape[0]
    if nseg == 1:
        return ada_ref[0, comp:comp + 1, :]
    seg = tm // nseg
    lo, hi = rows.start // seg, rows.stop // seg
    assert rows.start % seg == 0 and rows.stop % seg == 0
    return jnp.concatenate([jnp.broadcast_to(ada_ref[i, comp:comp + 1, :], (seg, ada_ref.shape[2]))
                            for i in range(lo, hi)], axis=0)


def _premix(x, ada_ref, g, rows, tm):
    return _rms(x, g) * (1.0 + _mod(ada_ref, 1, rows, tm)) + _mod(ada_ref, 0, rows, tm)


def _row_parts(tm):
    n = 2 if tm % (2 * 128) == 0 else 1
    return [slice(i * (tm // n), (i + 1) * (tm // n)) for i in range(n)]


def _mixer_kernel(x_ref, ada_ref, g_ref, win_ref, gq_ref, wuq_ref, gkv_ref, wuk_ref, wuv_ref, vone_ref,
                  cos_ref, sin_ref, *out_refs, expand_kv):
    if expand_kv:
        q_ref, ckv_ref, kr_ref, qb_ref, kb_ref, vb_ref, k_ref, vt_ref = out_refs
    else:
        q_ref, ckv_ref, kr_ref, qb_ref, kb_ref, vb_ref = out_refs
    tm = x_ref.shape[1]
    parts = _row_parts(tm)
    nq = H_A * HEAD_PAD
    ps = [_dot(_premix(x_ref[0, rows, :], ada_ref, g_ref[...], rows, tm).astype(BF16), win_ref[...])
          for rows in parts]
    q_lats = [_rms(p[:, _C_QLAT:_C_QLAT + Q_RANK], gq_ref[...]).astype(BF16) for p in ps]
    c_kvs = [_rms(p[:, _C_CKV:_C_CKV + KV_RANK], gkv_ref[...]) for p in ps]
    q2s = [_dot(q_lat, wuq_ref[...]) for q_lat in q_lats]
    if expand_kv:
        c_bfs = [c_kv.astype(BF16) for c_kv in c_kvs]
        k2s = [_dot(c_bf, wuk_ref[...]) for c_bf in c_bfs]
        v2s = [_dot(c_bf, wuv_ref[...]) for c_bf in c_bfs]
    for i, rows in enumerate(parts):
        p, q2 = ps[i], q2s[i]
        cos = cos_ref[rows, :]
        sin = sin_ref[rows, :]
        for hd in range(H_A):
            lo = hd * HEAD_PAD
            qh = q2[:, lo:lo + HEAD_PAD] * cos + q2[:, nq + lo:nq + lo + HEAD_PAD] * sin
            q_ref[0, rows, lo:lo + HEAD_PAD] = (qh * (MLA_SCALE * LOG2E)).astype(BF16)
        ckv_ref[0, rows, :] = c_kvs[i]
        kr = p[:, _C_KR:_C_KR + LANES] * cos + p[:, _C_KRR:_C_KRR + LANES] * sin
        kr_ref[0, rows, :] = kr[:, :ROPE]
        qb_ref[0, rows, :] = (p[:, _C_QB:_C_QB + H_B * D_HB] * SB_SCALE).astype(BF16)
        kb_ref[0, rows, :] = p[:, _C_KB:_C_KB + H_B * D_HB]
        vb_ref[0, rows, :] = p[:, _C_VB:_C_VB + H_B * D_HB]
        if expand_kv:
            for hd in range(H_A):
                lo = hd * HEAD_PAD
                k_ref[0, rows, lo:lo + HEAD_PAD] = (k2s[i][:, lo:lo + HEAD_PAD] + kr).astype(BF16)
            vt_ref[0, :, rows] = (v2s[i] + vone_ref[...]).T.astype(BF16)


def _mixer_inputs(x, ada, cos_t, sin_t, wts, *, tm, expand_kv):
    b, t, _ = x.shape
    nt = t // tm
    nseg = ada.shape[0] // b
    tok = lambda n: pl.BlockSpec((1, tm, n), lambda i, j: (i, j, 0))
    in_specs = [tok(D_MODEL),
                pl.BlockSpec((nseg, 6, D_MODEL), lambda i, j: (i, 0, 0)),
                _const_spec((1, D_MODEL)),
                _const_spec((D_MODEL, _N_K1)),
                _const_spec((1, Q_RANK)),
                _const_spec((Q_RANK, 2 * H_A * HEAD_PAD)),
                _const_spec((1, KV_RANK)),
                _const_spec((KV_RANK, H_A * HEAD_PAD)),
                _const_spec((KV_RANK, H_A * VT_ROWS)),
                _const_spec((1, H_A * VT_ROWS)),
                pl.BlockSpec((tm, LANES), lambda i, j: (j, 0)),
                pl.BlockSpec((tm, LANES), lambda i, j: (j, 0))]
    shapes = [((b, t, H_A * HEAD_PAD), BF16), ((b, t, KV_RANK), F32), ((b, t, ROPE), F32),
              ((b, t, H_B * D_HB), BF16), ((b, t, H_B * D_HB), F32), ((b, t, H_B * D_HB), F32)]
    if expand_kv:
        shapes += [((b, t, H_A * HEAD_PAD), BF16)]
    out_specs = [tok(s[-1]) for s, _ in shapes]
    if expand_kv:
        shapes += [((b, H_A * VT_ROWS, t), BF16)]
        out_specs += [pl.BlockSpec((1, H_A * VT_ROWS, tm), lambda i, j: (i, 0, j))]
    return pl.pallas_call(
        functools.partial(_mixer_kernel, expand_kv=expand_kv),
        grid=(b, nt),
        in_specs=in_specs,
        out_specs=out_specs,
        out_shape=[jax.ShapeDtypeStruct(s, d) for s, d in shapes],
        compiler_params=_params(2),
        name="mixer_in_kv" if expand_kv else "mixer_in",
    )(x, ada, wts["g_pre_mix"], wts["w_in_k1"], wts["g_q_lat"], wts["w_uq_ext"], wts["g_kv_lat"],
      wts["w_uk_pad"], wts["w_uv_t"], wts["v_ones"], cos_t, sin_t)


MLA_PAIR = 2


def _mla_prompt_kernel(q_ref, k_ref, vt_ref, bias_ref, o_ref, sa_ref, sb_ref, m_ref, acc_ref, *, blk, nq):
    tq = MLA_PAIR * blk
    nitems = nq * (nq + 1) // 2
    assert nitems % 2 == 0
    lanes = [slice(hh * HEAD_PAD, (hh + 1) * HEAD_PAD) for hh in range(2)]

    def reset():
        m_ref[...] = jnp.full(m_ref.shape, NEG, F32)
        acc_ref[...] = jnp.zeros(acc_ref.shape, F32)

    def following(item):
        n, g = item
        last = g == n
        return jnp.where(last, n + 1, n), jnp.where(last, 0, g + 1)

    def scores(s_ref, item):
        n, g = item
        diag = (g == n).astype(jnp.int32)
        n = jnp.minimum(n, nq - 1)
        rows = pl.ds(pl.multiple_of(n * tq, tq), tq)
        for hh, ln in enumerate(lanes):
            for b in range(MLA_PAIR):
                k = k_ref[0, pl.ds(pl.multiple_of((MLA_PAIR * g + b) * blk, blk), blk), ln]
                s_ref[hh, b] = _dot_t(k, q_ref[0, rows, ln]) + bias_ref[diag, b]

    def softmax_pv(s_ref, item):
        n, g = item
        outs = []
        for hh in range(2):
            ss = [s_ref[hh, b] for b in range(MLA_PAIR)]
            m_old = m_ref[hh]
            m_new = m_old
            for s in ss:
                m_new = jnp.maximum(m_new, jnp.max(s, axis=0, keepdims=True))
            acc = jnp.exp2(m_old - m_new) * acc_ref[hh]
            for b, s in enumerate(ss):
                p = jnp.exp2(s - m_new).astype(BF16)
                keys = pl.ds(pl.multiple_of((MLA_PAIR * g + b) * blk, blk), blk)
                acc = acc + _dot(vt_ref[0, hh * VT_ROWS:(hh + 1) * VT_ROWS, keys], p)
            outs.append(acc[:DV] / acc[DV:DV + 1])
            acc_ref[hh] = jnp.where(g == n, 0.0, acc)
            m_ref[hh] = jnp.where(g == n, NEG, m_new)
        rows = pl.ds(pl.multiple_of(n * tq, tq), tq)
        o_ref[0, rows, :] = jnp.concatenate(outs, axis=0).T.astype(BF16)

    reset()
    first = (jnp.int32(0), jnp.int32(0))
    scores(sa_ref, first)

    def body(_, item):
        nxt = following(item)
        scores(sb_ref, nxt)
        softmax_pv(sa_ref, item)
        after = following(nxt)
        scores(sa_ref, after)
        softmax_pv(sb_ref, nxt)
        return after

    lax.fori_loop(0, nitems // 2, body, first)


def _mla_diag_bias(blk, tq):
    kpos = np.arange(MLA_PAIR * blk).reshape(MLA_PAIR, blk, 1)
    qpos = np.arange(tq).reshape(1, 1, tq)
    diag = np.where(kpos // CHUNK <= qpos // CHUNK, 0.0, NEG).astype(np.float32)
    return np.stack([np.zeros_like(diag), diag])


def _mla_prompt(q, k, vt, *, blk):
    b, t, _ = q.shape
    tq = MLA_PAIR * blk
    assert t % tq == 0
    seq = lambda n: pl.BlockSpec((1, t, n), lambda i, j: (i, 0, j))
    return pl.pallas_call(
        functools.partial(_mla_prompt_kernel, blk=blk, nq=t // tq),
        grid=(b, H_A // 2),
        in_specs=[seq(2 * HEAD_PAD), seq(2 * HEAD_PAD),
                  pl.BlockSpec((1, 2 * VT_ROWS, t), lambda i, j: (i, j, 0)),
                  _const_spec((2, MLA_PAIR, blk, tq))],
        out_specs=seq(2 * DV),
        out_shape=jax.ShapeDtypeStruct((b, t, H_A * DV), BF16),
        scratch_shapes=[pltpu.VMEM((2, MLA_PAIR, blk, tq), F32), pltpu.VMEM((2, MLA_PAIR, blk, tq), F32),
                        pltpu.VMEM((2, 1, tq), F32), pltpu.VMEM((2, VT_ROWS, tq), F32)],
        compiler_params=_params(2),
        name="mla_prompt",
    )(q, k, vt, jnp.asarray(_mla_diag_bias(blk, tq)))


def _tri(n):
    r = lax.broadcasted_iota(jnp.int32, (n, n), 0)
    c = lax.broadcasted_iota(jnp.int32, (n, n), 1)
    return jnp.where(r > c, 1.0, 0.0).astype(BF16)


def _sb_chains(chains, r_init):
    n = len(chains)
    z, sp, sp_bf, later, a = ([None] * n for _ in range(5))
    r_before = [None] * n
    r = dict(r_init)
    out = {}

    def scores(c):
        _, qs, k, _, _, _ = chains[c]
        z[c] = k(qs) if callable(k) else _dot_t(qs, k)

    def softplus(c):
        g, _, _, _, _, keep = chains[c]
        zb = z[c].astype(BF16)
        sp_bf[c] = jnp.maximum(zb, 0.0) + jnp.log(1.0 + jnp.exp(-jnp.abs(zb)))
        sp[c] = sp_bf[c].astype(F32)
        if keep is not None:
            sp[c] = jnp.where(keep, sp[c], 0.0)
            sp_bf[c] = sp[c].astype(BF16)
        rs = jnp.sum(sp[c], axis=-1, keepdims=True)
        r_before[c] = r[g]
        r[g] = rs if r[g] is None else r[g] + rs

    def suffix(c):
        later[c] = _dot(sp_bf[c], chains[c][4])

    def weights(c):
        _, _, _, _, tri, keep = chains[c]
        tk = tri.shape[0]
        w = z[c] - sp[c] - later[c]
        if r_before[c] is not None:
            r_b = r_before[c]
            if r_b.shape[1] == LANES and tk > LANES:
                r_b = jnp.concatenate([r_b] * (tk // LANES), axis=1)
            elif r_b.shape[1] == LANES and tk < LANES:
                r_b = r_b[:, :tk]
            w = w - r_b
        w = jnp.exp2(w * LOG2E)
        a[c] = (w if keep is None else jnp.where(keep, w, 0.0)).astype(BF16)

    def values(c):
        g, _, _, v, _, _ = chains[c]
        o = v(a[c]) if callable(v) else _dot(a[c], v)
        out[g] = o if g not in out else out[g] + o

    for stage in (scores, softplus, suffix, weights, values):
        for c in range(n):
            stage(c)
    return out, r


def _head_mask(x, hh):
    lane = lax.broadcasted_iota(jnp.int32, x.shape, 1)
    mine = jnp.logical_and(lane >= hh * D_HB, lane < (hh + 1) * D_HB)
    return jnp.where(mine, x, jnp.zeros_like(x))


def _stack_heads(q):
    return jnp.concatenate([_head_mask(q, 0), _head_mask(q, 1)], axis=0)


def _causal_pair(t, tk):
    row = lax.broadcasted_iota(jnp.int32, (t, tk), 0)
    col = lax.broadcasted_iota(jnp.int32, (t, tk), 1)
    keep = col < row
    return jnp.concatenate([keep, keep], axis=0)


def _sb_tail(qs, load, tri, acc_ref, r_ref, j_start, keep=None):
    def cond(c):
        j, rmin = c
        return jnp.logical_and(j >= 0, rmin < SB_DEAD)

    def body(c):
        j, _ = c
        o, r_new = _sb_chains([(0, qs) + load(j) + (tri, keep)], {0: r_ref[...]})
        acc_ref[...] += o[0]
        r_ref[...] = r_new[0]
        return j - 1, jnp.min(r_new[0])

    lax.while_loop(cond, body, (j_start, jnp.min(r_ref[...])))


def _unstack_heads(acc, t):
    lane = lax.broadcasted_iota(jnp.int32, (t, LANES), 1)
    return jnp.where(lane < D_HB, acc[:t], acc[t:])


def _sb_prompt_kernel(q_ref, k_ref, v_ref, o_ref, acc_ref, r_ref, *, tq, nsub):
    qi = pl.program_id(2)
    tri = _tri(tq)
    causal = _causal_pair(tq, tq)

    def load(j):
        start = pl.multiple_of(j * tq, tq)
        return (k_ref[0, pl.ds(start, tq), :].astype(BF16),
                v_ref[0, pl.ds(start, tq), :].astype(BF16))

    qss = [_stack_heads(q_ref[0, sub * tq:(sub + 1) * tq, :]) for sub in range(nsub)]
    chains = []
    for back in range(SB_AHEAD + 1):
        for sub in range(nsub):
            blk = qi * nsub + sub
            keep = causal if back == 0 else blk >= back
            chains.append((sub, qss[sub]) + load(jnp.maximum(blk - back, 0)) + (tri, keep))
    o, r = _sb_chains(chains, {sub: None for sub in range(nsub)})
    for sub in range(nsub):
        acc_ref[sub] = o[sub]
        r_ref[sub] = jnp.broadcast_to(r[sub], (2 * tq, LANES))
    for sub in range(nsub):
        _sb_tail(qss[sub], load, tri, acc_ref.at[sub], r_ref.at[sub], qi * nsub + sub - SB_AHEAD - 1)
        o_ref[0, sub * tq:(sub + 1) * tq, :] = _unstack_heads(acc_ref[sub], tq).astype(BF16)


def _sb_prompt(q, k, v, *, tq, nsub):
    b, t, _ = q.shape
    rows = tq * nsub
    return pl.pallas_call(
        functools.partial(_sb_prompt_kernel, tq=tq, nsub=nsub),
        grid=(b, H_B // 2, t // rows),
        in_specs=[pl.BlockSpec((1, rows, LANES), lambda i, j, n: (i, n, j)),
                  pl.BlockSpec((1, t, LANES), lambda i, j, n: (i, 0, j)),
                  pl.BlockSpec((1, t, LANES), lambda i, j, n: (i, 0, j))],
        out_specs=pl.BlockSpec((1, rows, LANES), lambda i, j, n: (i, n, j)),
        out_shape=jax.ShapeDtypeStruct((b, t, H_B * D_HB), BF16),
        scratch_shapes=[pltpu.VMEM((nsub, 2 * tq, LANES), F32), pltpu.VMEM((nsub, 2 * tq, LANES), F32)],
        compiler_params=_params(3),
        name="sb_prompt",
    )(q, k, v)


def _sb_sample_kernel(q_ref, kn_ref, vn_ref, kc_ref, vc_ref, o_ref, left_ref, acc_ref, r_ref, *, tk):
    tq = q_ref.shape[1]
    nblk = kc_ref.shape[1] // tk
    qs = _stack_heads(q_ref[0])
    tri = _tri(tk)

    def load(j):
        start = pl.multiple_of(j * tk, tk)
        return (kc_ref[0, pl.ds(start, tk), :].astype(BF16),
                vc_ref[0, pl.ds(start, tk), :].astype(BF16))

    o, r = _sb_chains([(0, qs, kn_ref[0].astype(BF16), vn_ref[0].astype(BF16), _tri(tq), _causal_pair(tq, tq)),
                       (0, qs) + load(nblk - 1) + (tri, None)], {0: None})
    acc_ref[...] = o[0]
    r_ref[...] = jnp.broadcast_to(r[0], (2 * tq, LANES))
    _sb_tail(qs, load, tri, acc_ref, r_ref, jnp.int32(nblk - 2))
    o_ref[0] = _unstack_heads(acc_ref[...], tq).astype(BF16)
    left_ref[...] = jnp.full(left_ref.shape, jnp.min(r_ref[...]), F32)


def _sb_sample_call(q, k_new, v_new, k_cache, v_cache, *, tk):
    b, t, _ = q.shape
    past = k_cache.shape[1]
    new = pl.BlockSpec((1, t, LANES), lambda i, j: (i, 0, j))
    old = pl.BlockSpec((1, past, LANES), lambda i, j: (i, 0, j))
    return pl.pallas_call(
        functools.partial(_sb_sample_kernel, tk=tk),
        grid=(b, H_B // 2),
        in_specs=[new, new, new, old, old],
        out_specs=[new, pl.BlockSpec((1, 1, SUBLANES, LANES), lambda i, j: (i, j, 0, 0))],
        out_shape=[jax.ShapeDtypeStruct((b, t, H_B * D_HB), BF16),
                   jax.ShapeDtypeStruct((b, H_B // 2, SUBLANES, LANES), F32)],
        scratch_shapes=[pltpu.VMEM((2 * t, LANES), F32), pltpu.VMEM((2 * t, LANES), F32)],
        compiler_params=_params(2),
        name="sb_sample",
    )(q, k_new, v_new, k_cache, v_cache)


SB_DECODE_KEYS = 256
SB_DECODE_AHEAD = 1


def _sb_decode_kernel(q_ref, kn_ref, vn_ref, kct_ref, vct_ref, o_ref, left_ref, acc_ref, r_ref):
    t = q_ref.shape[1]
    rows = H_B * t
    ck = SB_DECODE_KEYS
    nstep = kct_ref.shape[3] // ck
    assert t & (t - 1) == 0
    head = lambda x, h: x[:, h * D_HB:(h + 1) * D_HB]
    q = q_ref[0]
    q3 = jnp.stack([head(q, h) for h in range(H_B)], axis=0)

    def cached(j):
        start = pl.multiple_of(j * ck, ck)
        kt = kct_ref[0, :, :, pl.ds(start, ck)].astype(BF16)
        vt = vct_ref[0, :, :, pl.ds(start, ck)].astype(BF16)

        def scores(_):
            z = lax.dot_general(q3, kt, (((2,), (1,)), ((0,), (0,))), preferred_element_type=F32)
            return z.reshape(rows, ck)

        def values(a):
            o = lax.dot_general(a.reshape(H_B, t, ck), vt, (((2,), (2,)), ((0,), (0,))),
                                preferred_element_type=F32)
            return o.reshape(rows, D_HB)

        return scores, values

    kn, vn = kn_ref[0].astype(BF16), vn_ref[0].astype(BF16)

    def new_scores(_):
        return jnp.concatenate([_dot_t(head(q, h), head(kn, h)) for h in range(H_B)], axis=0)

    def new_values(a):
        return jnp.concatenate([_dot(a[h * t:(h + 1) * t], head(vn, h)) for h in range(H_B)], axis=0)

    r_i = lax.broadcasted_iota(jnp.int32, (rows, t), 0)
    c_i = lax.broadcasted_iota(jnp.int32, (rows, t), 1)
    tri = _tri(ck)
    chains = [(0, None, new_scores, new_values, _tri(t), c_i < (r_i & (t - 1)))]
    ahead = min(SB_DECODE_AHEAD, nstep)
    for back in range(1, ahead + 1):
        chains.append((0, None) + cached(nstep - back) + (tri, None))
    o, r = _sb_chains(chains, {0: None})
    acc_ref[...] = o[0]
    r_ref[...] = jnp.broadcast_to(r[0], (rows, LANES))
    _sb_tail(None, cached, tri, acc_ref, r_ref, jnp.int32(nstep - ahead - 1))
    acc = acc_ref[...]
    o_ref[0] = jnp.concatenate([acc[h * t:(h + 1) * t] for h in range(H_B)], axis=1).astype(BF16)
    left_ref[...] = jnp.full(left_ref.shape, jnp.min(r_ref[...]), F32)


def _sb_decode_call(q, k_new, v_new, k_cache_t, v_cache_t, *, recent):
    b, t, _ = q.shape
    past = k_cache_t.shape[3]
    new = pl.BlockSpec((1, t, H_B * D_HB), lambda i: (i, 0, 0))
    old = pl.BlockSpec((1, H_B, D_HB, recent), lambda i: (i, 0, 0, past // recent - 1))
    return pl.pallas_call(
        _sb_decode_kernel,
        grid=(b,),
        in_specs=[new, new, new, old, old],
        out_specs=[new, pl.BlockSpec((1, SUBLANES, LANES), lambda i: (i, 0, 0))],
        out_shape=[jax.ShapeDtypeStruct((b, t, H_B * D_HB), BF16),
                   jax.ShapeDtypeStruct((b, SUBLANES, LANES), F32)],
        scratch_shapes=[pltpu.VMEM((H_B * t, D_HB), F32), pltpu.VMEM((H_B * t, LANES), F32)],
        compiler_params=_params(1),
        name="sb_decode",
    )(q, k_new, v_new, k_cache_t, v_cache_t)


def _sb_sample(q, k_new, v_new, k_cache, v_cache, *, tk, recent):
    b, past = k_cache.shape[:2]
    t = q.shape[1]
    flat = lambda c: c.reshape(b, c.shape[1], H_B * D_HB)
    full = lambda: _sb_sample_call(q, k_new, v_new, flat(k_cache), flat(v_cache), tk=tk)[0]
    if not (t & (t - 1) == 0 and past % recent == 0 and recent % SB_DECODE_KEYS == 0):
        return full()
    key_minor = lambda c: jnp.transpose(c, (0, 2, 3, 1))
    o, left = _sb_decode_call(q, k_new, v_new, key_minor(k_cache), key_minor(v_cache), recent=recent)
    if recent == past:
        return o
    return lax.cond(jnp.min(left) < SB_DEAD, full, lambda: o)


def _mla_sample_kernel(q_ref, cn_ref, rn_ref, cc_ref, rct_ref, wabs_ref, wuv_ref, o_ref, *, kc, past_len):
    t = q_ref.shape[1]
    past = cc_ref.shape[1]
    qa, qr = [], []
    for hd in range(H_A):
        qh = q_ref[0, :, hd * HEAD_PAD:(hd + 1) * HEAD_PAD]
        qa.append(_dot(qh, wabs_ref[hd]).astype(BF16))
        qr.append(qh[:, :ROPE])
    qa = jnp.concatenate(qa, axis=0)
    qr = jnp.concatenate(qr, axis=0)
    rows = H_A * t

    pieces = [(cc_ref[0, c * kc:(c + 1) * kc, :], rct_ref[0, :, c * kc:(c + 1) * kc], True, c * kc)
              for c in range(past // kc)]
    pieces.append((cn_ref[0], rn_ref[0], False, past_len))
    scores, lat = [], []
    for ckv, kr, key_minor, k0 in pieces:
        ckv = ckv.astype(BF16)
        kr = kr.astype(BF16)
        s = _dot_t(qa, ckv) + (_dot(qr, kr) if key_minor else _dot_t(qr, kr))
        n = ckv.shape[0]
        if (k0 + n - 1) // CHUNK > past_len // CHUNK:
            qpos = past_len + lax.broadcasted_iota(jnp.int32, (rows, n), 0) % t
            kpos = k0 + lax.broadcasted_iota(jnp.int32, (rows, n), 1)
            s = jnp.where((kpos >> CHUNK_SHIFT) <= (qpos >> CHUNK_SHIFT), s, NEG)
        scores.append(s)
        lat.append(ckv)
    m = functools.reduce(jnp.maximum, [jnp.max(s, axis=-1, keepdims=True) for s in scores])
    l = jnp.zeros((rows, 1), F32)
    o_lat = jnp.zeros((rows, KV_RANK), F32)
    for s, ckv in zip(scores, lat):
        p = jnp.exp2(s - m)
        l = l + jnp.sum(p, axis=-1, keepdims=True)
        o_lat = o_lat + _dot(p.astype(BF16), ckv)
    o_lat = (o_lat / l).astype(BF16)
    o = jnp.zeros((t, H_A * DV), F32)
    for hd in range(H_A):
        o = o + _dot(o_lat[hd * t:(hd + 1) * t], wuv_ref[hd])
    o_ref[0] = o.astype(BF16)


def _mla_sample(q, ckv_new, kr_new, ckv_cache, kr_cache, w_abs, w_uv_heads, *, past_len, kc):
    b, t, _ = q.shape
    past = ckv_cache.shape[1]
    row = lambda n, s: pl.BlockSpec((1, n, s), lambda i: (i, 0, 0))
    kr_cache = jnp.transpose(kr_cache, (0, 2, 1))
    return pl.pallas_call(
        functools.partial(_mla_sample_kernel, kc=kc, past_len=past_len),
        grid=(b,),
        in_specs=[row(t, H_A * HEAD_PAD), row(t, KV_RANK), row(t, ROPE),
                  row(past, KV_RANK), row(ROPE, past),
                  _const_spec((H_A, HEAD_PAD, KV_RANK)), _const_spec((H_A, KV_RANK, H_A * DV))],
        out_specs=row(t, H_A * DV),
        out_shape=jax.ShapeDtypeStruct((b, t, H_A * DV), BF16),
        compiler_params=_params(1),
        name="mla_sample",
    )(q, ckv_new, kr_new, ckv_cache, kr_cache, w_abs, w_uv_heads)


def _merge_kernel(x_ref, ada_ref, g_ref, wg_ref, oa_ref, ob_ref, wpa_ref, wpb_ref, wo_ref, gpost_ref, o_ref):
    tm = x_ref.shape[1]
    parts = _row_parts(tm)
    xs = [x_ref[0, rows, :] for rows in parts]
    gate_in = [_dot(_premix(x, ada_ref, g_ref[...], rows, tm).astype(BF16), wg_ref[...])
               for x, rows in zip(xs, parts)]
    pas = [_dot(oa_ref[0, rows, :], wpa_ref[...]) for rows in parts]
    pbs = [_dot(ob_ref[0, rows, :], wpb_ref[...]) for rows in parts]
    merged = [(_sigmoid(gi[:, :D_MODEL]) * pa + _sigmoid(gi[:, D_MODEL:]) * pb).astype(BF16)
              for gi, pa, pb in zip(gate_in, pas, pbs)]
    mos = [_dot(m, wo_ref[...]) for m in merged]
    for rows, x, mo in zip(parts, xs, mos):
        o_ref[0, rows, :] = x + _mod(ada_ref, 2, rows, tm) * _rms(mo, gpost_ref[...])


def _merge(x, ada, o_a, o_b, wts, *, tm):
    b, t, _ = x.shape
    nseg = ada.shape[0] // b
    tok = lambda n: pl.BlockSpec((1, tm, n), lambda i, j: (i, j, 0))
    return pl.pallas_call(
        _merge_kernel,
        grid=(b, t // tm),
        in_specs=[tok(D_MODEL),
                  pl.BlockSpec((nseg, 6, D_MODEL), lambda i, j: (i, 0, 0)),
                  _const_spec((1, D_MODEL)),
                  _const_spec((D_MODEL, 2 * D_MODEL)),
                  tok(H_A * DV), tok(H_B * D_HB),
                  _const_spec((H_A * DV, D_MODEL)), _const_spec((H_B * D_HB, D_MODEL)),
                  _const_spec((D_MODEL, D_MODEL)),
                  _const_spec((1, D_MODEL))],
        out_specs=tok(D_MODEL),
        out_shape=jax.ShapeDtypeStruct((b, t, D_MODEL), F32),
        compiler_params=_params(2),
        name="merge",
    )(x, ada, wts["g_pre_mix"], wts["w_gate"], o_a, o_b, wts["w_proj_a"], wts["w_proj_b"],
      wts["w_out"], wts["g_post_mix"])


FF_CHUNK = 256
SUB_GROUPS = 4
SUB_ROWS = SUB_GROUPS * SUBLANES
HALO = 2 * SUBLANES


def _gelu_tanh(a):
    return 0.5 * a * (1.0 + jnp.tanh(math.sqrt(2.0 / math.pi) * (a + 0.044715 * (a * a * a))))


def _ffn_kernel(x_ref, ada_ref, g_ref, wup_ref, cw_ref, cb_ref, wdn_ref, gpost_ref, cs_ref,
                o_ref, nc_ref, halo_ref, perm_ref, *, tm):
    ti = pl.program_id(1)
    nt = pl.num_programs(1)
    nsub = tm // SUB_ROWS
    segmented = cs_ref.shape[0] > 1
    assert not segmented or cs_ref.shape[0] == nsub
    all_rows = slice(0, tm)

    if not segmented:
        @pl.when(ti == 0)
        def _():
            halo_ref[...] = jnp.zeros((HALO, 2 * D_FF), F32)
            halo_ref[SUBLANES - 1:SUBLANES, :] = cs_ref[0, 0:1, :]
            halo_ref[HALO - 1:HALO, :] = cs_ref[0, 1:2, :]

    nc = D_MODEL // LANES
    for c in range(nc):
        perm_ref[c] = x_ref[0, :, c * LANES:(c + 1) * LANES]
    x = jnp.concatenate(
        [jnp.concatenate([perm_ref[c, pl.ds(j * SUB_ROWS + i, SUBLANES, stride=SUB_GROUPS), :] for c in range(nc)],
                         axis=1)
         for j in range(nsub) for i in range(SUB_GROUPS)], axis=0)
    h2 = (_rms(x, g_ref[...]) * (1.0 + _mod(ada_ref, 4, all_rows, tm)) + _mod(ada_ref, 3, all_rows, tm)).astype(BF16)
    first = lax.broadcasted_iota(jnp.int32, (SUBLANES, FF_CHUNK), 0) == 0

    def up(col):
        cols = slice(col, col + FF_CHUNK)
        u = _dot(h2, wup_ref[:, cols])
        grp = [u[g * SUBLANES:(g + 1) * SUBLANES] for g in range(tm // SUBLANES)]
        own2 = [pltpu.roll(grp[j * SUB_GROUPS + SUB_GROUPS - 2], 1, axis=0) for j in range(nsub)]
        own3 = [pltpu.roll(grp[j * SUB_GROUPS + SUB_GROUPS - 1], 1, axis=0) for j in range(nsub)]
        if segmented:
            pred2 = [jnp.broadcast_to(cs_ref[j, 0:1, cols], (SUBLANES, FF_CHUNK)) for j in range(nsub)]
            pred3 = [jnp.broadcast_to(cs_ref[j, 1:2, cols], (SUBLANES, FF_CHUNK)) for j in range(nsub)]
            for j in range(nsub):
                last = (j + 1) * SUB_ROWS
                nc_ref[j, 0:1, cols] = u[last - SUBLANES - 1:last - SUBLANES]
                nc_ref[j, 1:2, cols] = u[last - 1:last]
        else:
            pred2 = [pltpu.roll(halo_ref[0:SUBLANES, cols], 1, axis=0)] + own2[:-1]
            pred3 = [pltpu.roll(halo_ref[SUBLANES:HALO, cols], 1, axis=0)] + own3[:-1]
            halo_ref[:, cols] = u[tm - HALO:]
        u1, u2 = [], []
        for j in range(nsub):
            f1 = jnp.where(first, pred3[j], own3[j])
            f2 = jnp.where(first, pred2[j], own2[j])
            g0 = j * SUB_GROUPS
            u1 += [f1] + grp[g0:g0 + SUB_GROUPS - 1]
            u2 += [f2, f1] + grp[g0:g0 + SUB_GROUPS - 2]
        u1 = jnp.concatenate(u1, axis=0)
        u2 = jnp.concatenate(u2, axis=0)
        return (cb_ref[:, cols] + cw_ref[0:1, cols] * u2 + cw_ref[1:2, cols] * u1 + cw_ref[2:3, cols] * u)

    nchunk = D_FF // FF_CHUNK
    acc = jnp.zeros((tm, D_MODEL), F32)
    ya, yb = up(0), up(D_FF)
    for c in range(nchunk):
        if c + 1 < nchunk:
            ya_next, yb_next = up((c + 1) * FF_CHUNK), up(D_FF + (c + 1) * FF_CHUNK)
        g = (_gelu_tanh(ya) * yb).astype(BF16)
        acc = acc + _dot(g, wdn_ref[c * FF_CHUNK:(c + 1) * FF_CHUNK, :])
        if c + 1 < nchunk:
            ya, yb = ya_next, yb_next
    y = x + _mod(ada_ref, 5, all_rows, tm) * _rms(acc, gpost_ref[...])
    for j in range(nsub):
        for i in range(SUB_GROUPS):
            g0 = (j * SUB_GROUPS + i) * SUBLANES
            for c in range(nc):
                perm_ref[c, pl.ds(j * SUB_ROWS + i, SUBLANES, stride=SUB_GROUPS), :] = (
                    y[g0:g0 + SUBLANES, c * LANES:(c + 1) * LANES])
    for c in range(nc):
        o_ref[0, :, c * LANES:(c + 1) * LANES] = perm_ref[c]

    if not segmented:
        @pl.when(ti == nt - 1)
        def _():
            nc_ref[0, 0:1, :] = halo_ref[SUBLANES - 1:SUBLANES, :]
            nc_ref[0, 1:2, :] = halo_ref[HALO - 1:HALO, :]


def _ffn(x, ada, conv_state, wts, *, tm):
    b, t, _ = x.shape
    nseg = ada.shape[0] // b
    assert tm % SUB_ROWS == 0 and t % tm == 0 and (nseg == 1 or (t == tm and t == nseg * SUB_ROWS))
    tok = pl.BlockSpec((1, tm, D_MODEL), lambda i, j: (i, j, 0))
    state = pl.BlockSpec((nseg, CONV_W - 1, 2 * D_FF), lambda i, j: (i, 0, 0))
    return pl.pallas_call(
        functools.partial(_ffn_kernel, tm=tm),
        grid=(b, t // tm),
        in_specs=[tok,
                  pl.BlockSpec((nseg, 6, D_MODEL), lambda i, j: (i, 0, 0)),
                  _const_spec((1, D_MODEL)),
                  _const_spec((D_MODEL, 2 * D_FF)),
                  _const_spec((CONV_W, 2 * D_FF)),
                  _const_spec((1, 2 * D_FF)),
                  _const_spec((D_FF, D_MODEL)),
                  _const_spec((1, D_MODEL)),
                  state],
        out_specs=[tok, state],
        out_shape=[jax.ShapeDtypeStruct((b, t, D_MODEL), F32),
                   jax.ShapeDtypeStruct((b * nseg, CONV_W - 1, 2 * D_FF), F32)],
        scratch_shapes=[pltpu.VMEM((HALO, 2 * D_FF), F32), pltpu.VMEM((D_MODEL // LANES, tm, LANES), F32)],
        compiler_params=_params(1, 1),
        name="conv_ffn",
    )(x, ada, wts["g_pre_ffn"], wts["w_up"], wts["conv_w"], wts["conv_b"], wts["w_down"],
      wts["g_post_ffn"], conv_state)


def _rope_tables(first_pos, t):
    inv = ROPE_BASE ** (-np.arange(0, ROPE, 2, dtype=np.float64) / ROPE)
    ang = np.arange(first_pos, first_pos + t, dtype=np.float64)[:, None] * inv[None, :]
    cos, sin = jnp.asarray(np.cos(ang), F32), jnp.asarray(np.sin(ang), F32)
    cos_t = jnp.concatenate([cos, cos, jnp.ones((t, NOPE), F32), jnp.zeros((t, HEAD_PAD - ROPE - NOPE), F32)], axis=1)
    sin_t = jnp.concatenate([-sin, sin, jnp.zeros((t, HEAD_PAD - ROPE), F32)], axis=1)
    return cos_t, sin_t


def _swap_halves(w):
    return jnp.concatenate([w[..., ROPE // 2:], w[..., :ROPE // 2]], axis=-1)


def _pad_lanes(w, n):
    return jnp.pad(w, [(0, 0)] * (w.ndim - 1) + [(0, n - w.shape[-1])])


def _layer_weights(l, g_pre_mix, g_post_mix, g_pre_ffn, g_post_ffn, w_in, g_q_lat, w_uq, g_kv_lat,
                   w_uk, w_uv, w_proj_a, w_proj_b, w_out, w_up, conv_w, conv_b, w_down):
    splits = np.cumsum([Q_RANK, KV_RANK, ROPE, H_B * D_HB, H_B * D_HB, H_B * D_HB, D_MODEL])
    wi = w_in[l]
    w_ql, w_ckv, w_kr, w_qb, w_kb, w_vb, w_ga, w_gb = jnp.split(wi, splits, axis=1)
    w_in_k1 = jnp.concatenate([w_ql, w_ckv, _pad_lanes(w_kr, LANES), _pad_lanes(_swap_halves(w_kr), LANES),
                               w_qb, w_kb, w_vb], axis=1).astype(BF16)
    wq = w_uq[l].reshape(Q_RANK, H_A, NOPE + ROPE)
    wq_nope, wq_rope = wq[..., :NOPE], wq[..., NOPE:]
    zeros = jnp.zeros((Q_RANK, H_A, HEAD_PAD - ROPE - NOPE), F32)
    q_main = jnp.concatenate([wq_rope, wq_nope, zeros], axis=-1).reshape(Q_RANK, H_A * HEAD_PAD)
    q_rot = _pad_lanes(_swap_halves(wq_rope), HEAD_PAD).reshape(Q_RANK, H_A * HEAD_PAD)
    wk = w_uk[l].reshape(KV_RANK, H_A, NOPE)
    w_uk_pad = jnp.pad(wk, [(0, 0), (0, 0), (ROPE, HEAD_PAD - ROPE - NOPE)])
    w_abs = jnp.transpose(w_uk_pad, (1, 2, 0))
    w_uv_t = _pad_lanes(w_uv[l].reshape(KV_RANK, H_A, DV), VT_ROWS)
    v_ones = np.zeros((H_A, VT_ROWS), np.float32)
    v_ones[:, DV] = 1.0
    head_of_col = jnp.arange(H_A * DV) // DV
    w_uv_heads = jnp.where(head_of_col[None, None, :] == jnp.arange(H_A)[:, None, None], w_uv[l][None], 0.0)
    row = lambda g: g[l].reshape(1, -1)
    return {
        "g_pre_mix": row(g_pre_mix), "g_post_mix": row(g_post_mix),
        "g_pre_ffn": row(g_pre_ffn), "g_post_ffn": row(g_post_ffn),
        "g_q_lat": row(g_q_lat), "g_kv_lat": row(g_kv_lat),
        "w_in_k1": w_in_k1,
        "w_gate": jnp.concatenate([w_ga, w_gb], axis=1).astype(BF16),
        "w_uq_ext": jnp.concatenate([q_main, q_rot], axis=1).astype(BF16),
        "w_uk_pad": w_uk_pad.reshape(KV_RANK, H_A * HEAD_PAD).astype(BF16),
        "w_abs": w_abs.astype(BF16),
        "w_uv_t": w_uv_t.reshape(KV_RANK, H_A * VT_ROWS).astype(BF16),
        "v_ones": jnp.asarray(v_ones.reshape(1, H_A * VT_ROWS)),
        "w_uv_heads": w_uv_heads.astype(BF16),
        "w_proj_a": w_proj_a[l].astype(BF16), "w_proj_b": w_proj_b[l].astype(BF16),
        "w_out": w_out[l].astype(BF16),
        "w_up": w_up[l].astype(BF16), "conv_w": conv_w[l], "conv_b": conv_b[l].reshape(1, -1),
        "w_down": w_down[l].astype(BF16),
    }


def _pick_tile(t, want):
    tm = min(t, want)
    assert t % tm == 0
    return tm


def _layer(x, ada, first_pos, past, wts):
    b, t, _ = x.shape
    cos_t, sin_t = _rope_tables(first_pos, t)
    tm = _pick_tile(t, 512)
    if past is None:
        q, ckv, kr, qb, kb, vb, k_mla, vt_mla = _mixer_inputs(x, ada, cos_t, sin_t, wts, tm=tm, expand_kv=True)
        o_a = _mla_prompt(q, k_mla, vt_mla, blk=_pick_tile(t, 512))
        o_b = _sb_prompt(qb, kb, vb, tq=_pick_tile(t, 256), nsub=2 if t % 512 == 0 else 1)
        conv_state = jnp.zeros((b, CONV_W - 1, 2 * D_FF), F32)
    else:
        past_ckv, past_kr, past_k, past_v, conv_state = past
        past_len = past_ckv.shape[1]
        flat = t == SUB_ROWS and (b * t) % 256 == 0 and b * t <= 512
        if flat:
            x = x.reshape(1, b * t, D_MODEL)
            cos_t, sin_t = jnp.tile(cos_t, (b, 1)), jnp.tile(sin_t, (b, 1))
            tm = b * t
        per_batch = lambda a: a.reshape(b, t, a.shape[-1])
        q, ckv, kr, qb, kb, vb = map(per_batch, _mixer_inputs(x, ada, cos_t, sin_t, wts, tm=tm, expand_kv=False))
        o_a = _mla_sample(q, ckv, kr, past_ckv, past_kr, wts["w_abs"], wts["w_uv_heads"],
                          past_len=past_len, kc=_pick_tile(past_len, 1024))
        o_b = _sb_sample(qb, kb, vb, past_k, past_v, tk=_pick_tile(past_len, 256), recent=512)
        o_a, o_b = (o.reshape(x.shape[0], x.shape[1], o.shape[-1]) for o in (o_a, o_b))
    x1 = _merge(x, ada, o_a, o_b, wts, tm=tm)
    y, new_conv = _ffn(x1, ada, conv_state, wts, tm=_pick_tile(x.shape[1], 256))
    state = (ckv, kr, kb.reshape(b, t, H_B, D_HB), vb.reshape(b, t, H_B, D_HB), new_conv)
    return y.reshape(b, t, D_MODEL), state


def kernel(x_prompt, x_sample, cache_mla_ckv, cache_mla_krope, cache_sb_k, cache_sb_v, state_ffn_conv,
           c_prompt, c_sample, w_ada, b_ada, g_pre_mix, g_post_mix, g_pre_ffn, g_post_ffn,
           w_in, g_q_lat, w_uq, g_kv_lat, w_uk, w_uv, w_proj_a, w_proj_b, w_out,
           w_up, conv_w, conv_b, w_down):
    depth = w_in.shape[0]
    nb_p = x_prompt.shape[0]
    past_len = cache_mla_ckv.shape[2]
    xp, xs = x_prompt, x_sample
    c_all = jnp.concatenate([c_prompt, c_sample], axis=0)
    st_p = [[] for _ in range(5)]
    st_s = [[] for _ in range(5)]
    for l in range(depth):
        wts = _layer_weights(l, g_pre_mix, g_post_mix, g_pre_ffn, g_post_ffn, w_in, g_q_lat, w_uq, g_kv_lat,
                             w_uk, w_uv, w_proj_a, w_proj_b, w_out, w_up, conv_w, conv_b, w_down)
        ada = _ada(c_all, w_ada[l], b_ada[l]).reshape(-1, 6, D_MODEL)
        xp, sp = _layer(xp, ada[:nb_p], 0, None, wts)
        past = (cache_mla_ckv[l], cache_mla_krope[l], cache_sb_k[l], cache_sb_v[l], state_ffn_conv[l])
        xs, ss = _layer(xs, ada[nb_p:], past_len, past, wts)
        for i in range(5):
            st_p[i].append(sp[i])
            st_s[i].append(ss[i])
    p_state = [jnp.stack(a, axis=0) for a in st_p]
    s_state = [jnp.stack(a, axis=0) for a in st_s]
    return (xp, xs, *p_state, *s_state)
```

```python
import functools
import math

import numpy as np
import jax
import jax.numpy as jnp
from jax import lax
from jax.experimental import pallas as pl
from jax.experimental.pallas import tpu as pltpu

D_MODEL = 1024
CHUNK = 64
CHUNK_SHIFT = 6
H_A = 8
NOPE = 64
ROPE = 32
DV = 64
Q_RANK = 384
KV_RANK = 256
ROPE_BASE = 10000.0
H_B = 8
D_HB = 64
D_FF = 2816
CONV_W = 3
EPS = 1e-6
NEG = -1e30
MLA_SCALE = (NOPE + ROPE) ** -0.5
SB_SCALE = D_HB ** -0.5
LOG2E = math.log2(math.e)

LANES = 128
SUBLANES = 8
HEAD_PAD = LANES
VT_ROWS = 80
SB_DEAD = 110.0
SB_AHEAD = 2
VMEM_LIMIT = 56 * 1024 * 1024

F32 = jnp.float32
BF16 = jnp.bfloat16


def _dot(a, b):
    return jnp.dot(a, b, preferred_element_type=F32)


def _dot_t(a, b):
    return lax.dot_general(a, b, (((1,), (1,)), ((), ())), preferred_element_type=F32)


def _rms(x, g):
    return x * lax.rsqrt(jnp.mean(x * x, axis=-1, keepdims=True) + EPS) * g


def _sigmoid(x):
    return 1.0 / (1.0 + jnp.exp(-x))


def _params(n_parallel, n_arbitrary=0):
    return pltpu.CompilerParams(
        dimension_semantics=("parallel",) * n_parallel + ("arbitrary",) * n_arbitrary,
        vmem_limit_bytes=VMEM_LIMIT)


def _const_spec(shape):
    n = len(shape)
    return pl.BlockSpec(shape, lambda *_: (0,) * n, pipeline_mode=pl.Buffered(1))


def _ada_kernel(c_ref, w_ref, b_ref, o_ref):
    c = c_ref[...]
    s = (c * _sigmoid(c)).astype(BF16)
    o_ref[...] = _dot(s, w_ref[...].astype(BF16)) + b_ref[...]


def _ada(c_all, w_ada, b_ada):
    n = c_all.shape[0]
    nchunk = 6
    return pl.pallas_call(
        _ada_kernel,
        grid=(nchunk,),
        in_specs=[pl.BlockSpec((n, D_MODEL), lambda j: (0, 0)),
                  pl.BlockSpec((D_MODEL, D_MODEL), lambda j: (0, j)),
                  pl.BlockSpec((1, D_MODEL), lambda j: (0, j))],
        out_specs=pl.BlockSpec((n, D_MODEL), lambda j: (0, j)),
        out_shape=jax.ShapeDtypeStruct((n, 6 * D_MODEL), F32),
        compiler_params=_params(1),
        name="ada",
    )(c_all, w_ada, b_ada.reshape(1, -1))


_C_QLAT = 0
_C_CKV = _C_QLAT + Q_RANK
_C_KR = _C_CKV + KV_RANK
_C_KRR = _C_KR + LANES
_C_QB = _C_KRR + LANES
_C_KB = _C_QB + H_B * D_HB
_C_VB = _C_KB + H_B * D_HB
_N_K1 = _C_VB + H_B * D_HB


def _mod(ada_ref, comp, rows, tm):
    nseg = ada_ref.shape[0]
    if nseg == 1:
        return ada_ref[0, comp:comp + 1, :]
    seg = tm // nseg
    lo, hi = rows.start // seg, rows.stop // seg
    assert rows.start % seg == 0 and rows.stop % seg == 0
    return jnp.concatenate([jnp.broadcast_to(ada_ref[i, comp:comp + 1, :], (seg, ada_ref.shape[2]))
                            for i in range(lo, hi)], axis=0)


def _premix(x, ada_ref, g, rows, tm):
    return _rms(x, g) * (1.0 + _mod(ada_ref, 1, rows, tm)) + _mod(ada_ref, 0, rows, tm)


def _row_parts(tm):
    n = 2 if tm % (2 * 128) == 0 else 1
    return [slice(i * (tm // n), (i + 1) * (tm // n)) for i in range(n)]


def _mixer_kernel(x_ref, ada_ref, g_ref, win_ref, gq_ref, wuq_ref, gkv_ref, wuk_ref, wuv_ref, vone_ref,
                  cos_ref, sin_ref, *out_refs, expand_kv):
    if expand_kv:
        q_ref, ckv_ref, kr_ref, qb_ref, kb_ref, vb_ref, k_ref, vt_ref = out_refs
    else:
        q_ref, ckv_ref, kr_ref, qb_ref, kb_ref, vb_ref = out_refs
    tm = x_ref.shape[1]
    parts = _row_parts(tm)
    nq = H_A * HEAD_PAD
    ps = [_dot(_premix(x_ref[0, rows, :], ada_ref, g_ref[...], rows, tm).astype(BF16), win_ref[...])
          for rows in parts]
    q_lats = [_rms(p[:, _C_QLAT:_C_QLAT + Q_RANK], gq_ref[...]).astype(BF16) for p in ps]
    c_kvs = [_rms(p[:, _C_CKV:_C_CKV + KV_RANK], gkv_ref[...]) for p in ps]
    q2s = [_dot(q_lat, wuq_ref[...]) for q_lat in q_lats]
    if expand_kv:
        c_bfs = [c_kv.astype(BF16) for c_kv in c_kvs]
        k2s = [_dot(c_bf, wuk_ref[...]) for c_bf in c_bfs]
        v2s = [_dot(c_bf, wuv_ref[...]) for c_bf in c_bfs]
    for i, rows in enumerate(parts):
        p, q2 = ps[i], q2s[i]
        cos = cos_ref[rows, :]
        sin = sin_ref[rows, :]
        for hd in range(H_A):
            lo = hd * HEAD_PAD
            qh = q2[:, lo:lo + HEAD_PAD] * cos + q2[:, nq + lo:nq + lo + HEAD_PAD] * sin
            q_ref[0, rows, lo:lo + HEAD_PAD] = (qh * (MLA_SCALE * LOG2E)).astype(BF16)
        ckv_ref[0, rows, :] = c_kvs[i]
        kr = p[:, _C_KR:_C_KR + LANES] * cos + p[:, _C_KRR:_C_KRR + LANES] * sin
        if expand_kv:
            kr_ref[0, :, rows] = kr.T[:ROPE]
        else:
            kr_ref[0, rows, :] = kr[:, :ROPE]
        qb_ref[0, rows, :] = (p[:, _C_QB:_C_QB + H_B * D_HB] * SB_SCALE).astype(BF16)
        kb_ref[0, rows, :] = p[:, _C_KB:_C_KB + H_B * D_HB]
        vb_ref[0, rows, :] = p[:, _C_VB:_C_VB + H_B * D_HB]
        if expand_kv:
            for hd in range(H_A):
                lo = hd * HEAD_PAD
                k_ref[0, rows, lo:lo + HEAD_PAD] = (k2s[i][:, lo:lo + HEAD_PAD] + kr).astype(BF16)
            vt_ref[0, :, rows] = (v2s[i] + vone_ref[...]).T.astype(BF16)


def _mixer_inputs(x, ada, cos_t, sin_t, wts, *, tm, expand_kv):
    b, t, _ = x.shape
    nt = t // tm
    nseg = ada.shape[0] // b
    tok = lambda n: pl.BlockSpec((1, tm, n), lambda i, j: (i, j, 0))
    in_specs = [tok(D_MODEL),
                pl.BlockSpec((nseg, 6, D_MODEL), lambda i, j: (i, 0, 0)),
                _const_spec((1, D_MODEL)),
                _const_spec((D_MODEL, _N_K1)),
                _const_spec((1, Q_RANK)),
                _const_spec((Q_RANK, 2 * H_A * HEAD_PAD)),
                _const_spec((1, KV_RANK)),
                _const_spec((KV_RANK, H_A * HEAD_PAD)),
                _const_spec((KV_RANK, H_A * VT_ROWS)),
                _const_spec((1, H_A * VT_ROWS)),
                pl.BlockSpec((tm, LANES), lambda i, j: (j, 0)),
                pl.BlockSpec((tm, LANES), lambda i, j: (j, 0))]
    shapes = [((b, t, H_A * HEAD_PAD), BF16), ((b, t, KV_RANK), F32), ((b, t, ROPE), F32),
              ((b, t, H_B * D_HB), BF16), ((b, t, H_B * D_HB), F32), ((b, t, H_B * D_HB), F32)]
    if expand_kv:
        shapes += [((b, t, H_A * HEAD_PAD), BF16)]
    out_specs = [tok(s[-1]) for s, _ in shapes]
    if expand_kv:
        t_minor = lambda n: pl.BlockSpec((1, n, tm), lambda i, j: (i, 0, j))
        shapes[2], out_specs[2] = ((b, ROPE, t), F32), t_minor(ROPE)
        shapes += [((b, H_A * VT_ROWS, t), BF16)]
        out_specs += [t_minor(H_A * VT_ROWS)]
    return pl.pallas_call(
        functools.partial(_mixer_kernel, expand_kv=expand_kv),
        grid=(b, nt),
        in_specs=in_specs,
        out_specs=out_specs,
        out_shape=[jax.ShapeDtypeStruct(s, d) for s, d in shapes],
        compiler_params=_params(2),
        name="mixer_in_kv" if expand_kv else "mixer_in",
    )(x, ada, wts["g_pre_mix"], wts["w_in_k1"], wts["g_q_lat"], wts["w_uq_ext"], wts["g_kv_lat"],
      wts["w_uk_pad"], wts["w_uv_t"], wts["v_ones"], cos_t, sin_t)


MLA_PAIR = 2


def _mla_prompt_kernel(q_ref, k_ref, vt_ref, bias_ref, o_ref, sa_ref, sb_ref, m_ref, acc_ref, *, blk):
    qi = pl.program_id(2)
    lanes = [slice(hh * HEAD_PAD, (hh + 1) * HEAD_PAD) for hh in range(2)]
    m_ref[...] = jnp.full(m_ref.shape, NEG, F32)
    acc_ref[...] = jnp.zeros(acc_ref.shape, F32)

    def start_of(j):
        return pl.multiple_of(j * blk, blk)

    def scores(s_ref, g):
        for hh, ln in enumerate(lanes):
            for b in range(MLA_PAIR):
                k = k_ref[0, pl.ds(start_of(MLA_PAIR * g + b), blk), ln]
                s_ref[hh, b] = _dot_t(k, q_ref[0, :, ln])

    def softmax_pv(s_ref, g, masked):
        for hh in range(2):
            ss = []
            for b in range(MLA_PAIR):
                s = s_ref[hh, b]
                ss.append(s + bias_ref[b] if masked else s)
            m_old = m_ref[hh]
            m_new = m_old
            for s in ss:
                m_new = jnp.maximum(m_new, jnp.max(s, axis=0, keepdims=True))
            acc = jnp.exp2(m_old - m_new) * acc_ref[hh]
            for b, s in enumerate(ss):
                p = jnp.exp2(s - m_new).astype(BF16)
                vt = vt_ref[0, hh * VT_ROWS:(hh + 1) * VT_ROWS, pl.ds(start_of(MLA_PAIR * g + b), blk)]
                acc = acc + _dot(vt, p)
            acc_ref[hh] = acc
            m_ref[hh] = m_new

    scores(sa_ref, 0)

    def body(h, c):
        scores(sb_ref, 2 * h + 1)
        softmax_pv(sa_ref, 2 * h, False)
        scores(sa_ref, 2 * h + 2)
        softmax_pv(sb_ref, 2 * h + 1, False)
        return c

    done = qi // 2 * 2
    lax.fori_loop(0, qi // 2, body, 0)

    @pl.when(done == qi)
    def _():
        softmax_pv(sa_ref, done, True)

    @pl.when(done != qi)
    def _():
        scores(sb_ref, done + 1)
        softmax_pv(sa_ref, done, False)
        softmax_pv(sb_ref, done + 1, True)

    outs = [acc_ref[hh][:DV] / acc_ref[hh][DV:DV + 1] for hh in range(2)]
    o_ref[0] = jnp.concatenate(outs, axis=0).T.astype(BF16)


def _mla_diag_bias(blk, tq):
    kpos = np.arange(MLA_PAIR * blk).reshape(MLA_PAIR, blk, 1)
    qpos = np.arange(tq).reshape(1, 1, tq)
    return np.where(kpos // CHUNK <= qpos // CHUNK, 0.0, NEG).astype(np.float32)


def _mla_prompt(q, k, vt, *, blk):
    b, t, _ = q.shape
    tq = MLA_PAIR * blk
    assert t % tq == 0
    return pl.pallas_call(
        functools.partial(_mla_prompt_kernel, blk=blk),
        grid=(b, H_A // 2, t // tq),
        in_specs=[pl.BlockSpec((1, tq, 2 * HEAD_PAD), lambda i, j, n: (i, n, j)),
                  pl.BlockSpec((1, t, 2 * HEAD_PAD), lambda i, j, n: (i, 0, j)),
                  pl.BlockSpec((1, 2 * VT_ROWS, t), lambda i, j, n: (i, j, 0)),
                  _const_spec((MLA_PAIR, blk, tq))],
        out_specs=pl.BlockSpec((1, tq, 2 * DV), lambda i, j, n: (i, n, j)),
        out_shape=jax.ShapeDtypeStruct((b, t, H_A * DV), BF16),
        scratch_shapes=[pltpu.VMEM((2, MLA_PAIR, blk, tq), F32), pltpu.VMEM((2, MLA_PAIR, blk, tq), F32),
                        pltpu.VMEM((2, 1, tq), F32), pltpu.VMEM((2, VT_ROWS, tq), F32)],
        compiler_params=_params(3),
        name="mla_prompt",
    )(q, k, vt, jnp.asarray(_mla_diag_bias(blk, tq)))


def _tri(n):
    r = lax.broadcasted_iota(jnp.int32, (n, n), 0)
    c = lax.broadcasted_iota(jnp.int32, (n, n), 1)
    return jnp.where(r > c, 1.0, 0.0).astype(BF16)


def _sb_chains(chains, r_init):
    n = len(chains)
    z, sp, sp_bf, later, a = ([None] * n for _ in range(5))
    r_before = [None] * n
    r = dict(r_init)
    out = {}

    def scores(c):
        _, qs, k, _, _, _ = chains[c]
        z[c] = k(qs) if callable(k) else _dot_t(qs, k)

    def softplus(c):
        g, _, _, _, _, keep = chains[c]
        zb = z[c].astype(BF16)
        sp_bf[c] = jnp.maximum(zb, 0.0) + jnp.log(1.0 + jnp.exp(-jnp.abs(zb)))
        sp[c] = sp_bf[c].astype(F32)
        if keep is not None:
            sp[c] = jnp.where(keep, sp[c], 0.0)
            sp_bf[c] = sp[c].astype(BF16)
        rs = jnp.sum(sp[c], axis=-1, keepdims=True)
        r_before[c] = r[g]
        r[g] = rs if r[g] is None else r[g] + rs

    def suffix(c):
        later[c] = _dot(sp_bf[c], chains[c][4])

    def weights(c):
        _, _, _, _, tri, keep = chains[c]
        tk = tri.shape[0]
        w = z[c] - sp[c] - later[c]
        if r_before[c] is not None:
            r_b = r_before[c]
            if r_b.shape[1] == LANES and tk > LANES:
                r_b = jnp.concatenate([r_b] * (tk // LANES), axis=1)
            elif r_b.shape[1] == LANES and tk < LANES:
                r_b = r_b[:, :tk]
            w = w - r_b
        w = jnp.exp2(w * LOG2E)
        a[c] = (w if keep is None else jnp.where(keep, w, 0.0)).astype(BF16)

    def values(c):
        g, _, _, v, _, _ = chains[c]
        o = v(a[c]) if callable(v) else _dot(a[c], v)
        out[g] = o if g not in out else out[g] + o

    for stage in (scores, softplus, suffix, weights, values):
        for c in range(n):
            stage(c)
    return out, r


def _head_mask(x, hh):
    lane = lax.broadcasted_iota(jnp.int32, x.shape, 1)
    mine = jnp.logical_and(lane >= hh * D_HB, lane < (hh + 1) * D_HB)
    return jnp.where(mine, x, jnp.zeros_like(x))


def _stack_heads(q):
    return jnp.concatenate([_head_mask(q, 0), _head_mask(q, 1)], axis=0)


def _causal_pair(t, tk):
    row = lax.broadcasted_iota(jnp.int32, (t, tk), 0)
    col = lax.broadcasted_iota(jnp.int32, (t, tk), 1)
    keep = col < row
    return jnp.concatenate([keep, keep], axis=0)


def _sb_tail(qs, load, tri, acc_ref, r_ref, j_start, keep=None):
    def cond(c):
        j, rmin = c
        return jnp.logical_and(j >= 0, rmin < SB_DEAD)

    def body(c):
        j, _ = c
        o, r_new = _sb_chains([(0, qs) + load(j) + (tri, keep)], {0: r_ref[...]})
        acc_ref[...] += o[0]
        r_ref[...] = r_new[0]
        return j - 1, jnp.min(r_new[0])

    lax.while_loop(cond, body, (j_start, jnp.min(r_ref[...])))


def _unstack_heads(acc, t):
    lane = lax.broadcasted_iota(jnp.int32, (t, LANES), 1)
    return jnp.where(lane < D_HB, acc[:t], acc[t:])


def _sb_prompt_kernel(q_ref, k_ref, v_ref, o_ref, acc_ref, r_ref, *, tq, nsub):
    qi = pl.program_id(2)
    tri = _tri(tq)
    causal = _causal_pair(tq, tq)

    def load(j):
        start = pl.multiple_of(j * tq, tq)
        return (k_ref[0, pl.ds(start, tq), :].astype(BF16),
                v_ref[0, pl.ds(start, tq), :].astype(BF16))

    qss = [_stack_heads(q_ref[0, sub * tq:(sub + 1) * tq, :]) for sub in range(nsub)]
    chains = []
    for back in range(SB_AHEAD + 1):
        for sub in range(nsub):
            blk = qi * nsub + sub
            keep = causal if back == 0 else blk >= back
            chains.append((sub, qss[sub]) + load(jnp.maximum(blk - back, 0)) + (tri, keep))
    o, r = _sb_chains(chains, {sub: None for sub in range(nsub)})
    for sub in range(nsub):
        acc_ref[sub] = o[sub]
        r_ref[sub] = jnp.broadcast_to(r[sub], (2 * tq, LANES))
    for sub in range(nsub):
        _sb_tail(qss[sub], load, tri, acc_ref.at[sub], r_ref.at[sub], qi * nsub + sub - SB_AHEAD - 1)
        o_ref[0, sub * tq:(sub + 1) * tq, :] = _unstack_heads(acc_ref[sub], tq).astype(BF16)


def _sb_prompt(q, k, v, *, tq, nsub):
    b, t, _ = q.shape
    rows = tq * nsub
    return pl.pallas_call(
        functools.partial(_sb_prompt_kernel, tq=tq, nsub=nsub),
        grid=(b, H_B // 2, t // rows),
        in_specs=[pl.BlockSpec((1, rows, LANES), lambda i, j, n: (i, n, j)),
                  pl.BlockSpec((1, t, LANES), lambda i, j, n: (i, 0, j)),
                  pl.BlockSpec((1, t, LANES), lambda i, j, n: (i, 0, j))],
        out_specs=pl.BlockSpec((1, rows, LANES), lambda i, j, n: (i, n, j)),
        out_shape=jax.ShapeDtypeStruct((b, t, H_B * D_HB), BF16),
        scratch_shapes=[pltpu.VMEM((nsub, 2 * tq, LANES), F32), pltpu.VMEM((nsub, 2 * tq, LANES), F32)],
        compiler_params=_params(3),
        name="sb_prompt",
    )(q, k, v)


def _sb_sample_kernel(q_ref, kn_ref, vn_ref, kc_ref, vc_ref, o_ref, left_ref, acc_ref, r_ref, *, tk):
    tq = q_ref.shape[1]
    nblk = kc_ref.shape[1] // tk
    qs = _stack_heads(q_ref[0])
    tri = _tri(tk)

    def load(j):
        start = pl.multiple_of(j * tk, tk)
        return (kc_ref[0, pl.ds(start, tk), :].astype(BF16),
                vc_ref[0, pl.ds(start, tk), :].astype(BF16))

    o, r = _sb_chains([(0, qs, kn_ref[0].astype(BF16), vn_ref[0].astype(BF16), _tri(tq), _causal_pair(tq, tq)),
                       (0, qs) + load(nblk - 1) + (tri, None)], {0: None})
    acc_ref[...] = o[0]
    r_ref[...] = jnp.broadcast_to(r[0], (2 * tq, LANES))
    _sb_tail(qs, load, tri, acc_ref, r_ref, jnp.int32(nblk - 2))
    o_ref[0] = _unstack_heads(acc_ref[...], tq).astype(BF16)
    left_ref[...] = jnp.full(left_ref.shape, jnp.min(r_ref[...]), F32)


def _sb_sample_call(q, k_new, v_new, k_cache, v_cache, *, tk):
    b, t, _ = q.shape
    past = k_cache.shape[1]
    new = pl.BlockSpec((1, t, LANES), lambda i, j: (i, 0, j))
    old = pl.BlockSpec((1, past, LANES), lambda i, j: (i, 0, j))
    return pl.pallas_call(
        functools.partial(_sb_sample_kernel, tk=tk),
        grid=(b, H_B // 2),
        in_specs=[new, new, new, old, old],
        out_specs=[new, pl.BlockSpec((1, 1, SUBLANES, LANES), lambda i, j: (i, j, 0, 0))],
        out_shape=[jax.ShapeDtypeStruct((b, t, H_B * D_HB), BF16),
                   jax.ShapeDtypeStruct((b, H_B // 2, SUBLANES, LANES), F32)],
        scratch_shapes=[pltpu.VMEM((2 * t, LANES), F32), pltpu.VMEM((2 * t, LANES), F32)],
        compiler_params=_params(2),
        name="sb_sample",
    )(q, k_new, v_new, k_cache, v_cache)


SB_DECODE_KEYS = 256
SB_DECODE_AHEAD = 1


def _sb_decode_kernel(q_ref, kn_ref, vn_ref, kct_ref, vct_ref, o_ref, left_ref, acc_ref, r_ref):
    t = q_ref.shape[1]
    rows = H_B * t
    ck = SB_DECODE_KEYS
    nstep = kct_ref.shape[3] // ck
    assert t & (t - 1) == 0
    head = lambda x, h: x[:, h * D_HB:(h + 1) * D_HB]
    q = q_ref[0]
    q3 = jnp.stack([head(q, h) for h in range(H_B)], axis=0)

    def cached(j):
        start = pl.multiple_of(j * ck, ck)
        kt = kct_ref[0, :, :, pl.ds(start, ck)].astype(BF16)
        vt = vct_ref[0, :, :, pl.ds(start, ck)].astype(BF16)

        def scores(_):
            z = lax.dot_general(q3, kt, (((2,), (1,)), ((0,), (0,))), preferred_element_type=F32)
            return z.reshape(rows, ck)

        def values(a):
            o = lax.dot_general(a.reshape(H_B, t, ck), vt, (((2,), (2,)), ((0,), (0,))),
                                preferred_element_type=F32)
            return o.reshape(rows, D_HB)

        return scores, values

    kn, vn = kn_ref[0].astype(BF16), vn_ref[0].astype(BF16)

    def new_scores(_):
        return jnp.concatenate([_dot_t(head(q, h), head(kn, h)) for h in range(H_B)], axis=0)

    def new_values(a):
        return jnp.concatenate([_dot(a[h * t:(h + 1) * t], head(vn, h)) for h in range(H_B)], axis=0)

    r_i = lax.broadcasted_iota(jnp.int32, (rows, t), 0)
    c_i = lax.broadcasted_iota(jnp.int32, (rows, t), 1)
    tri = _tri(ck)
    chains = [(0, None, new_scores, new_values, _tri(t), c_i < (r_i & (t - 1)))]
    ahead = min(SB_DECODE_AHEAD, nstep)
    for back in range(1, ahead + 1):
        chains.append((0, None) + cached(nstep - back) + (tri, None))
    o, r = _sb_chains(chains, {0: None})
    acc_ref[...] = o[0]
    r_ref[...] = jnp.broadcast_to(r[0], (rows, LANES))
    _sb_tail(None, cached, tri, acc_ref, r_ref, jnp.int32(nstep - ahead - 1))
    acc = acc_ref[...]
    o_ref[0] = jnp.concatenate([acc[h * t:(h + 1) * t] for h in range(H_B)], axis=1).astype(BF16)
    left_ref[...] = jnp.full(left_ref.shape, jnp.min(r_ref[...]), F32)


def _sb_decode_call(q, k_new, v_new, k_cache_t, v_cache_t, *, recent):
    b, t, _ = q.shape
    past = k_cache_t.shape[3]
    new = pl.BlockSpec((1, t, H_B * D_HB), lambda i: (i, 0, 0))
    old = pl.BlockSpec((1, H_B, D_HB, recent), lambda i: (i, 0, 0, past // recent - 1))
    return pl.pallas_call(
        _sb_decode_kernel,
        grid=(b,),
        in_specs=[new, new, new, old, old],
        out_specs=[new, pl.BlockSpec((1, SUBLANES, LANES), lambda i: (i, 0, 0))],
        out_shape=[jax.ShapeDtypeStruct((b, t, H_B * D_HB), BF16),
                   jax.ShapeDtypeStruct((b, SUBLANES, LANES), F32)],
        scratch_shapes=[pltpu.VMEM((H_B * t, D_HB), F32), pltpu.VMEM((H_B * t, LANES), F32)],
        compiler_params=_params(1),
        name="sb_decode",
    )(q, k_new, v_new, k_cache_t, v_cache_t)


def _sb_sample(q, k_new, v_new, k_cache, v_cache, *, tk, recent):
    b, past = k_cache.shape[:2]
    t = q.shape[1]
    flat = lambda c: c.reshape(b, c.shape[1], H_B * D_HB)
    full = lambda: _sb_sample_call(q, k_new, v_new, flat(k_cache), flat(v_cache), tk=tk)[0]
    if not (t & (t - 1) == 0 and past % recent == 0 and recent % SB_DECODE_KEYS == 0):
        return full()
    key_minor = lambda c: jnp.transpose(c, (0, 2, 3, 1))
    o, left = _sb_decode_call(q, k_new, v_new, key_minor(k_cache), key_minor(v_cache), recent=recent)
    if recent == past:
        return o
    return lax.cond(jnp.min(left) < SB_DEAD, full, lambda: o)


def _mla_sample_kernel(q_ref, cn_ref, rn_ref, cc_ref, rct_ref, wabs_ref, wuv_ref, o_ref, *, kc, past_len):
    t = q_ref.shape[1]
    past = cc_ref.shape[1]
    qa, qr = [], []
    for hd in range(H_A):
        qh = q_ref[0, :, hd * HEAD_PAD:(hd + 1) * HEAD_PAD]
        qa.append(_dot(qh, wabs_ref[hd]).astype(BF16))
        qr.append(qh[:, :ROPE])
    qa = jnp.concatenate(qa, axis=0)
    qr = jnp.concatenate(qr, axis=0)
    rows = H_A * t

    pieces = [(cc_ref[0, c * kc:(c + 1) * kc, :], rct_ref[0, :, c * kc:(c + 1) * kc], True, c * kc)
              for c in range(past // kc)]
    pieces.append((cn_ref[0], rn_ref[0], False, past_len))
    scores, lat = [], []
    for ckv, kr, key_minor, k0 in pieces:
        ckv = ckv.astype(BF16)
        kr = kr.astype(BF16)
        s = _dot_t(qa, ckv) + (_dot(qr, kr) if key_minor else _dot_t(qr, kr))
        n = ckv.shape[0]
        if (k0 + n - 1) // CHUNK > past_len // CHUNK:
            qpos = past_len + lax.broadcasted_iota(jnp.int32, (rows, n), 0) % t
            kpos = k0 + lax.broadcasted_iota(jnp.int32, (rows, n), 1)
            s = jnp.where((kpos >> CHUNK_SHIFT) <= (qpos >> CHUNK_SHIFT), s, NEG)
        scores.append(s)
        lat.append(ckv)
    m = functools.reduce(jnp.maximum, [jnp.max(s, axis=-1, keepdims=True) for s in scores])
    l = jnp.zeros((rows, 1), F32)
    o_lat = jnp.zeros((rows, KV_RANK), F32)
    for s, ckv in zip(scores, lat):
        p = jnp.exp2(s - m)
        l = l + jnp.sum(p, axis=-1, keepdims=True)
        o_lat = o_lat + _dot(p.astype(BF16), ckv)
    o_lat = (o_lat / l).astype(BF16)
    o = jnp.zeros((t, H_A * DV), F32)
    for hd in range(H_A):
        o = o + _dot(o_lat[hd * t:(hd + 1) * t], wuv_ref[hd])
    o_ref[0] = o.astype(BF16)


def _mla_sample(q, ckv_new, kr_new, ckv_cache, kr_cache, w_abs, w_uv_heads, *, past_len, kc):
    b, t, _ = q.shape
    past = ckv_cache.shape[1]
    row = lambda n, s: pl.BlockSpec((1, n, s), lambda i: (i, 0, 0))
    kr_cache = jnp.transpose(kr_cache, (0, 2, 1))
    return pl.pallas_call(
        functools.partial(_mla_sample_kernel, kc=kc, past_len=past_len),
        grid=(b,),
        in_specs=[row(t, H_A * HEAD_PAD), row(t, KV_RANK), row(t, ROPE),
                  row(past, KV_RANK), row(ROPE, past),
                  _const_spec((H_A, HEAD_PAD, KV_RANK)), _const_spec((H_A, KV_RANK, H_A * DV))],
        out_specs=row(t, H_A * DV),
        out_shape=jax.ShapeDtypeStruct((b, t, H_A * DV), BF16),
        compiler_params=_params(1),
        name="mla_sample",
    )(q, ckv_new, kr_new, ckv_cache, kr_cache, w_abs, w_uv_heads)


def _merge_kernel(x_ref, ada_ref, g_ref, wg_ref, oa_ref, ob_ref, wpa_ref, wpb_ref, wo_ref, gpost_ref, o_ref):
    tm = x_ref.shape[1]
    parts = _row_parts(tm)
    xs = [x_ref[0, rows, :] for rows in parts]
    gate_in = [_dot(_premix(x, ada_ref, g_ref[...], rows, tm).astype(BF16), wg_ref[...])
               for x, rows in zip(xs, parts)]
    pas = [_dot(oa_ref[0, rows, :], wpa_ref[...]) for rows in parts]
    pbs = [_dot(ob_ref[0, rows, :], wpb_ref[...]) for rows in parts]
    merged = [(_sigmoid(gi[:, :D_MODEL]) * pa + _sigmoid(gi[:, D_MODEL:]) * pb).astype(BF16)
              for gi, pa, pb in zip(gate_in, pas, pbs)]
    mos = [_dot(m, wo_ref[...]) for m in merged]
    for rows, x, mo in zip(parts, xs, mos):
        o_ref[0, rows, :] = x + _mod(ada_ref, 2, rows, tm) * _rms(mo, gpost_ref[...])


def _merge(x, ada, o_a, o_b, wts, *, tm):
    b, t, _ = x.shape
    nseg = ada.shape[0] // b
    tok = lambda n: pl.BlockSpec((1, tm, n), lambda i, j: (i, j, 0))
    return pl.pallas_call(
        _merge_kernel,
        grid=(b, t // tm),
        in_specs=[tok(D_MODEL),
                  pl.BlockSpec((nseg, 6, D_MODEL), lambda i, j: (i, 0, 0)),
                  _const_spec((1, D_MODEL)),
                  _const_spec((D_MODEL, 2 * D_MODEL)),
                  tok(H_A * DV), tok(H_B * D_HB),
                  _const_spec((H_A * DV, D_MODEL)), _const_spec((H_B * D_HB, D_MODEL)),
                  _const_spec((D_MODEL, D_MODEL)),
                  _const_spec((1, D_MODEL))],
        out_specs=tok(D_MODEL),
        out_shape=jax.ShapeDtypeStruct((b, t, D_MODEL), F32),
        compiler_params=_params(2),
        name="merge",
    )(x, ada, wts["g_pre_mix"], wts["w_gate"], o_a, o_b, wts["w_proj_a"], wts["w_proj_b"],
      wts["w_out"], wts["g_post_mix"])


FF_CHUNK = 256
SUB_GROUPS = 4
SUB_ROWS = SUB_GROUPS * SUBLANES
HALO = 2 * SUBLANES


def _gelu_tanh(a):
    return 0.5 * a * (1.0 + jnp.tanh(math.sqrt(2.0 / math.pi) * (a + 0.044715 * (a * a * a))))


def _ffn_kernel(x_ref, ada_ref, g_ref, wup_ref, cw_ref, cb_ref, wdn_ref, gpost_ref, cs_ref,
                o_ref, nc_ref, halo_ref, perm_ref, *, tm):
    ti = pl.program_id(1)
    nt = pl.num_programs(1)
    nsub = tm // SUB_ROWS
    segmented = cs_ref.shape[0] > 1
    assert not segmented or cs_ref.shape[0] == nsub
    all_rows = slice(0, tm)

    if not segmented:
        @pl.when(ti == 0)
        def _():
            halo_ref[...] = jnp.zeros((HALO, 2 * D_FF), F32)
            halo_ref[SUBLANES - 1:SUBLANES, :] = cs_ref[0, 0:1, :]
            halo_ref[HALO - 1:HALO, :] = cs_ref[0, 1:2, :]

    nc = D_MODEL // LANES
    for c in range(nc):
        perm_ref[c] = x_ref[0, :, c * LANES:(c + 1) * LANES]
    x = jnp.concatenate(
        [jnp.concatenate([perm_ref[c, pl.ds(j * SUB_ROWS + i, SUBLANES, stride=SUB_GROUPS), :] for c in range(nc)],
                         axis=1)
         for j in range(nsub) for i in range(SUB_GROUPS)], axis=0)
    h2 = (_rms(x, g_ref[...]) * (1.0 + _mod(ada_ref, 4, all_rows, tm)) + _mod(ada_ref, 3, all_rows, tm)).astype(BF16)
    first = lax.broadcasted_iota(jnp.int32, (SUBLANES, FF_CHUNK), 0) == 0

    def up(col):
        cols = slice(col, col + FF_CHUNK)
        u = _dot(h2, wup_ref[:, cols])
        grp = [u[g * SUBLANES:(g + 1) * SUBLANES] for g in range(tm // SUBLANES)]
        own2 = [pltpu.roll(grp[j * SUB_GROUPS + SUB_GROUPS - 2], 1, axis=0) for j in range(nsub)]
        own3 = [pltpu.roll(grp[j * SUB_GROUPS + SUB_GROUPS - 1], 1, axis=0) for j in range(nsub)]
        if segmented:
            pred2 = [jnp.broadcast_to(cs_ref[j, 0:1, cols], (SUBLANES, FF_CHUNK)) for j in range(nsub)]
            pred3 = [jnp.broadcast_to(cs_ref[j, 1:2, cols], (SUBLANES, FF_CHUNK)) for j in range(nsub)]
            for j in range(nsub):
                last = (j + 1) * SUB_ROWS
                nc_ref[j, 0:1, cols] = u[last - SUBLANES - 1:last - SUBLANES]
                nc_ref[j, 1:2, cols] = u[last - 1:last]
        else:
            pred2 = [pltpu.roll(halo_ref[0:SUBLANES, cols], 1, axis=0)] + own2[:-1]
            pred3 = [pltpu.roll(halo_ref[SUBLANES:HALO, cols], 1, axis=0)] + own3[:-1]
            halo_ref[:, cols] = u[tm - HALO:]
        u1, u2 = [], []
        for j in range(nsub):
            f1 = jnp.where(first, pred3[j], own3[j])
            f2 = jnp.where(first, pred2[j], own2[j])
            g0 = j * SUB_GROUPS
            u1 += [f1] + grp[g0:g0 + SUB_GROUPS - 1]
            u2 += [f2, f1] + grp[g0:g0 + SUB_GROUPS - 2]
        u1 = jnp.concatenate(u1, axis=0)
        u2 = jnp.concatenate(u2, axis=0)
        return (cb_ref[:, cols] + cw_ref[0:1, cols] * u2 + cw_ref[1:2, cols] * u1 + cw_ref[2:3, cols] * u)

    nchunk = D_FF // FF_CHUNK
    acc = jnp.zeros((tm, D_MODEL), F32)
    ya, yb = up(0), up(D_FF)
    for c in range(nchunk):
        if c + 1 < nchunk:
            ya_next, yb_next = up((c + 1) * FF_CHUNK), up(D_FF + (c + 1) * FF_CHUNK)
        g = (_gelu_tanh(ya) * yb).astype(BF16)
        acc = acc + _dot(g, wdn_ref[c * FF_CHUNK:(c + 1) * FF_CHUNK, :])
        if c + 1 < nchunk:
            ya, yb = ya_next, yb_next
    y = x + _mod(ada_ref, 5, all_rows, tm) * _rms(acc, gpost_ref[...])
    for j in range(nsub):
        for i in range(SUB_GROUPS):
            g0 = (j * SUB_GROUPS + i) * SUBLANES
            for c in range(nc):
                perm_ref[c, pl.ds(j * SUB_ROWS + i, SUBLANES, stride=SUB_GROUPS), :] = (
                    y[g0:g0 + SUBLANES, c * LANES:(c + 1) * LANES])
    for c in range(nc):
        o_ref[0, :, c * LANES:(c + 1) * LANES] = perm_ref[c]

    if not segmented:
        @pl.when(ti == nt - 1)
        def _():
            nc_ref[0, 0:1, :] = halo_ref[SUBLANES - 1:SUBLANES, :]
            nc_ref[0, 1:2, :] = halo_ref[HALO - 1:HALO, :]


def _ffn(x, ada, conv_state, wts, *, tm):
    b, t, _ = x.shape
    nseg = ada.shape[0] // b
    assert tm % SUB_ROWS == 0 and t % tm == 0 and (nseg == 1 or (t == tm and t == nseg * SUB_ROWS))
    tok = pl.BlockSpec((1, tm, D_MODEL), lambda i, j: (i, j, 0))
    state = pl.BlockSpec((nseg, CONV_W - 1, 2 * D_FF), lambda i, j: (i, 0, 0))
    return pl.pallas_call(
        functools.partial(_ffn_kernel, tm=tm),
        grid=(b, t // tm),
        in_specs=[tok,
                  pl.BlockSpec((nseg, 6, D_MODEL), lambda i, j: (i, 0, 0)),
                  _const_spec((1, D_MODEL)),
                  _const_spec((D_MODEL, 2 * D_FF)),
                  _const_spec((CONV_W, 2 * D_FF)),
                  _const_spec((1, 2 * D_FF)),
                  _const_spec((D_FF, D_MODEL)),
                  _const_spec((1, D_MODEL)),
                  state],
        out_specs=[tok, state],
        out_shape=[jax.ShapeDtypeStruct((b, t, D_MODEL), F32),
                   jax.ShapeDtypeStruct((b * nseg, CONV_W - 1, 2 * D_FF), F32)],
        scratch_shapes=[pltpu.VMEM((HALO, 2 * D_FF), F32), pltpu.VMEM((D_MODEL // LANES, tm, LANES), F32)],
        compiler_params=_params(1, 1),
        name="conv_ffn",
    )(x, ada, wts["g_pre_ffn"], wts["w_up"], wts["conv_w"], wts["conv_b"], wts["w_down"],
      wts["g_post_ffn"], conv_state)


def _rope_tables(first_pos, t):
    inv = ROPE_BASE ** (-np.arange(0, ROPE, 2, dtype=np.float64) / ROPE)
    ang = np.arange(first_pos, first_pos + t, dtype=np.float64)[:, None] * inv[None, :]
    cos, sin = jnp.asarray(np.cos(ang), F32), jnp.asarray(np.sin(ang), F32)
    cos_t = jnp.concatenate([cos, cos, jnp.ones((t, NOPE), F32), jnp.zeros((t, HEAD_PAD - ROPE - NOPE), F32)], axis=1)
    sin_t = jnp.concatenate([-sin, sin, jnp.zeros((t, HEAD_PAD - ROPE), F32)], axis=1)
    return cos_t, sin_t


def _swap_halves(w):
    return jnp.concatenate([w[..., ROPE // 2:], w[..., :ROPE // 2]], axis=-1)


def _pad_lanes(w, n):
    return jnp.pad(w, [(0, 0)] * (w.ndim - 1) + [(0, n - w.shape[-1])])


def _layer_weights(l, g_pre_mix, g_post_mix, g_pre_ffn, g_post_ffn, w_in, g_q_lat, w_uq, g_kv_lat,
                   w_uk, w_uv, w_proj_a, w_proj_b, w_out, w_up, conv_w, conv_b, w_down):
    splits = np.cumsum([Q_RANK, KV_RANK, ROPE, H_B * D_HB, H_B * D_HB, H_B * D_HB, D_MODEL])
    wi = w_in[l]
    w_ql, w_ckv, w_kr, w_qb, w_kb, w_vb, w_ga, w_gb = jnp.split(wi, splits, axis=1)
    w_in_k1 = jnp.concatenate([w_ql, w_ckv, _pad_lanes(w_kr, LANES), _pad_lanes(_swap_halves(w_kr), LANES),
                               w_qb, w_kb, w_vb], axis=1).astype(BF16)
    wq = w_uq[l].reshape(Q_RANK, H_A, NOPE + ROPE)
    wq_nope, wq_rope = wq[..., :NOPE], wq[..., NOPE:]
    zeros = jnp.zeros((Q_RANK, H_A, HEAD_PAD - ROPE - NOPE), F32)
    q_main = jnp.concatenate([wq_rope, wq_nope, zeros], axis=-1).reshape(Q_RANK, H_A * HEAD_PAD)
    q_rot = _pad_lanes(_swap_halves(wq_rope), HEAD_PAD).reshape(Q_RANK, H_A * HEAD_PAD)
    wk = w_uk[l].reshape(KV_RANK, H_A, NOPE)
    w_uk_pad = jnp.pad(wk, [(0, 0), (0, 0), (ROPE, HEAD_PAD - ROPE - NOPE)])
    w_abs = jnp.transpose(w_uk_pad, (1, 2, 0))
    w_uv_t = _pad_lanes(w_uv[l].reshape(KV_RANK, H_A, DV), VT_ROWS)
    v_ones = np.zeros((H_A, VT_ROWS), np.float32)
    v_ones[:, DV] = 1.0
    head_of_col = jnp.arange(H_A * DV) // DV
    w_uv_heads = jnp.where(head_of_col[None, None, :] == jnp.arange(H_A)[:, None, None], w_uv[l][None], 0.0)
    row = lambda g: g[l].reshape(1, -1)
    return {
        "g_pre_mix": row(g_pre_mix), "g_post_mix": row(g_post_mix),
        "g_pre_ffn": row(g_pre_ffn), "g_post_ffn": row(g_post_ffn),
        "g_q_lat": row(g_q_lat), "g_kv_lat": row(g_kv_lat),
        "w_in_k1": w_in_k1,
        "w_gate": jnp.concatenate([w_ga, w_gb], axis=1).astype(BF16),
        "w_uq_ext": jnp.concatenate([q_main, q_rot], axis=1).astype(BF16),
        "w_uk_pad": w_uk_pad.reshape(KV_RANK, H_A * HEAD_PAD).astype(BF16),
        "w_abs": w_abs.astype(BF16),
        "w_uv_t": w_uv_t.reshape(KV_RANK, H_A * VT_ROWS).astype(BF16),
        "v_ones": jnp.asarray(v_ones.reshape(1, H_A * VT_ROWS)),
        "w_uv_heads": w_uv_heads.astype(BF16),
        "w_proj_a": w_proj_a[l].astype(BF16), "w_proj_b": w_proj_b[l].astype(BF16),
        "w_out": w_out[l].astype(BF16),
        "w_up": w_up[l].astype(BF16), "conv_w": conv_w[l], "conv_b": conv_b[l].reshape(1, -1),
        "w_down": w_down[l].astype(BF16),
    }


def _pick_tile(t, want):
    tm = min(t, want)
    assert t % tm == 0
    return tm


def _layer(x, ada, first_pos, past, wts):
    b, t, _ = x.shape
    cos_t, sin_t = _rope_tables(first_pos, t)
    tm = _pick_tile(t, 512)
    if past is None:
        q, ckv, kr_t, qb, kb, vb, k_mla, vt_mla = _mixer_inputs(x, ada, cos_t, sin_t, wts, tm=tm, expand_kv=True)
        kr = jnp.transpose(kr_t, (0, 2, 1))
        o_a = _mla_prompt(q, k_mla, vt_mla, blk=_pick_tile(t, 512))
        o_b = _sb_prompt(qb, kb, vb, tq=_pick_tile(t, 256), nsub=max(n for n in (1, 2, 4) if t % (256 * n) == 0))
        conv_state = jnp.zeros((b, CONV_W - 1, 2 * D_FF), F32)
    else:
        past_ckv, past_kr, past_k, past_v, conv_state = past
        past_len = past_ckv.shape[1]
        flat = t == SUB_ROWS and (b * t) % 256 == 0 and b * t <= 512
        if flat:
            x = x.reshape(1, b * t, D_MODEL)
            cos_t, sin_t = jnp.tile(cos_t, (b, 1)), jnp.tile(sin_t, (b, 1))
            tm = b * t
        per_batch = lambda a: a.reshape(b, t, a.shape[-1])
        q, ckv, kr, qb, kb, vb = map(per_batch, _mixer_inputs(x, ada, cos_t, sin_t, wts, tm=tm, expand_kv=False))
        o_a = _mla_sample(q, ckv, kr, past_ckv, past_kr, wts["w_abs"], wts["w_uv_heads"],
                          past_len=past_len, kc=_pick_tile(past_len, 1024))
        o_b = _sb_sample(qb, kb, vb, past_k, past_v, tk=_pick_tile(past_len, 256), recent=512)
        o_a, o_b = (o.reshape(x.shape[0], x.shape[1], o.shape[-1]) for o in (o_a, o_b))
    x1 = _merge(x, ada, o_a, o_b, wts, tm=tm)
    y, new_conv = _ffn(x1, ada, conv_state, wts, tm=_pick_tile(x.shape[1], 256))
    state = (ckv, kr, kb.reshape(b, t, H_B, D_HB), vb.reshape(b, t, H_B, D_HB), new_conv)
    return y.reshape(b, t, D_MODEL), state


def kernel(x_prompt, x_sample, cache_mla_ckv, cache_mla_krope, cache_sb_k, cache_sb_v, state_ffn_conv,
           c_prompt, c_sample, w_ada, b_ada, g_pre_mix, g_post_mix, g_pre_ffn, g_post_ffn,
           w_in, g_q_lat, w_uq, g_kv_lat, w_uk, w_uv, w_proj_a, w_proj_b, w_out,
           w_up, conv_w, conv_b, w_down):
    depth = w_in.shape[0]
    nb_p = x_prompt.shape[0]
    past_len = cache_mla_ckv.shape[2]
    xp, xs = x_prompt, x_sample
    c_all = jnp.concatenate([c_prompt, c_sample], axis=0)
    st_p = [[] for _ in range(5)]
    st_s = [[] for _ in range(5)]
    for l in range(depth):
        wts = _layer_weights(l, g_pre_mix, g_post_mix, g_pre_ffn, g_post_ffn, w_in, g_q_lat, w_uq, g_kv_lat,
                             w_uk, w_uv, w_proj_a, w_proj_b, w_out, w_up, conv_w, conv_b, w_down)
        ada = _ada(c_all, w_ada[l], b_ada[l]).reshape(-1, 6, D_MODEL)
        xp, sp = _layer(xp, ada[:nb_p], 0, None, wts)
        past = (cache_mla_ckv[l], cache_mla_krope[l], cache_sb_k[l], cache_sb_v[l], state_ffn_conv[l])
        xs, ss = _layer(xs, ada[nb_p:], past_len, past, wts)
        for i in range(5):
            st_p[i].append(sp[i])
            st_s[i].append(ss[i])
    p_state = [jnp.stack(a, axis=0) for a in st_p]
    s_state = [jnp.stack(a, axis=0) for a in st_s]
    return (xp, xs, *p_state, *s_state)
```

```python
import functools
import math

import numpy as np
import jax
import jax.numpy as jnp
from jax import lax
from jax.experimental import pallas as pl
from jax.experimental.pallas import tpu as pltpu

D_MODEL = 1024
CHUNK = 64
CHUNK_SHIFT = 6
H_A = 8
NOPE = 64
ROPE = 32
DV = 64
Q_RANK = 384
KV_RANK = 256
ROPE_BASE = 10000.0
H_B = 8
D_HB = 64
D_FF = 2816
CONV_W = 3
EPS = 1e-6
NEG = -1e30
MLA_SCALE = (NOPE + ROPE) ** -0.5
SB_SCALE = D_HB ** -0.5
LOG2E = math.log2(math.e)

LANES = 128
SUBLANES = 8
HEAD_PAD = LANES
VT_ROWS = 80
SB_DEAD = 110.0
SB_AHEAD = 2
VMEM_LIMIT = 56 * 1024 * 1024

F32 = jnp.float32
BF16 = jnp.bfloat16


def _dot(a, b):
    return jnp.dot(a, b, preferred_element_type=F32)


def _dot_t(a, b):
    return lax.dot_general(a, b, (((1,), (1,)), ((), ())), preferred_element_type=F32)


def _rms(x, g):
    return x * lax.rsqrt(jnp.mean(x * x, axis=-1, keepdims=True) + EPS) * g


def _sigmoid(x):
    return 1.0 / (1.0 + jnp.exp(-x))


def _params(n_parallel, n_arbitrary=0):
    return pltpu.CompilerParams(
        dimension_semantics=("parallel",) * n_parallel + ("arbitrary",) * n_arbitrary,
        vmem_limit_bytes=VMEM_LIMIT)


def _const_spec(shape):
    n = len(shape)
    return pl.BlockSpec(shape, lambda *_: (0,) * n, pipeline_mode=pl.Buffered(1))


def _ada_kernel(c_ref, w_ref, b_ref, o_ref):
    c = c_ref[...]
    s = (c * _sigmoid(c)).astype(BF16)
    o_ref[...] = _dot(s, w_ref[...].astype(BF16)) + b_ref[...]


def _ada(c_all, w_ada, b_ada):
    n = c_all.shape[0]
    nchunk = 6
    return pl.pallas_call(
        _ada_kernel,
        grid=(nchunk,),
        in_specs=[pl.BlockSpec((n, D_MODEL), lambda j: (0, 0)),
                  pl.BlockSpec((D_MODEL, D_MODEL), lambda j: (0, j)),
                  pl.BlockSpec((1, D_MODEL), lambda j: (0, j))],
        out_specs=pl.BlockSpec((n, D_MODEL), lambda j: (0, j)),
        out_shape=jax.ShapeDtypeStruct((n, 6 * D_MODEL), F32),
        compiler_params=_params(1),
        name="ada",
    )(c_all, w_ada, b_ada.reshape(1, -1))


_C_QLAT = 0
_C_CKV = _C_QLAT + Q_RANK
_C_KR = _C_CKV + KV_RANK
_C_KRR = _C_KR + LANES
_C_QB = _C_KRR + LANES
_C_KB = _C_QB + H_B * D_HB
_C_VB = _C_KB + H_B * D_HB
_N_K1 = _C_VB + H_B * D_HB


def _mod(ada_ref, comp, rows, tm):
    nseg = ada_ref.shape[0]
    if nseg == 1:
        return ada_ref[0, comp:comp + 1, :]
    seg = tm // nseg
    lo, hi = rows.start // seg, rows.stop // seg
    assert rows.start % seg == 0 and rows.stop % seg == 0
    return jnp.concatenate([jnp.broadcast_to(ada_ref[i, comp:comp + 1, :], (seg, ada_ref.shape[2]))
                            for i in range(lo, hi)], axis=0)


def _premix(x, ada_ref, g, rows, tm):
    return _rms(x, g) * (1.0 + _mod(ada_ref, 1, rows, tm)) + _mod(ada_ref, 0, rows, tm)


def _row_parts(tm):
    n = 2 if tm % (2 * 128) == 0 else 1
    return [slice(i * (tm // n), (i + 1) * (tm // n)) for i in range(n)]


def _mixer_kernel(x_ref, ada_ref, g_ref, win_ref, gq_ref, wuq_ref, gkv_ref, wuk_ref, wuv_ref, vone_ref,
                  cos_ref, sin_ref, *out_refs, expand_kv):
    if expand_kv:
        q_ref, ckv_ref, kr_ref, qb_ref, kb_ref, vb_ref, k_ref, vt_ref = out_refs
    else:
        q_ref, ckv_ref, kr_ref, qb_ref, kb_ref, vb_ref = out_refs
    tm = x_ref.shape[1]
    parts = _row_parts(tm)
    nq = H_A * HEAD_PAD
    ps = [_dot(_premix(x_ref[0, rows, :], ada_ref, g_ref[...], rows, tm).astype(BF16), win_ref[...])
          for rows in parts]
    q_lats = [_rms(p[:, _C_QLAT:_C_QLAT + Q_RANK], gq_ref[...]).astype(BF16) for p in ps]
    c_kvs = [_rms(p[:, _C_CKV:_C_CKV + KV_RANK], gkv_ref[...]) for p in ps]
    q2s = [_dot(q_lat, wuq_ref[...]) for q_lat in q_lats]
    if expand_kv:
        c_bfs = [c_kv.astype(BF16) for c_kv in c_kvs]
        k2s = [_dot(c_bf, wuk_ref[...]) for c_bf in c_bfs]
        v2s = [_dot(c_bf, wuv_ref[...]) for c_bf in c_bfs]
    for i, rows in enumerate(parts):
        p, q2 = ps[i], q2s[i]
        cos = cos_ref[rows, :]
        sin = sin_ref[rows, :]
        for hd in range(H_A):
            lo = hd * HEAD_PAD
            qh = q2[:, lo:lo + HEAD_PAD] * cos + q2[:, nq + lo:nq + lo + HEAD_PAD] * sin
            q_ref[0, rows, lo:lo + HEAD_PAD] = (qh * (MLA_SCALE * LOG2E)).astype(BF16)
        ckv_ref[0, rows, :] = c_kvs[i]
        kr = p[:, _C_KR:_C_KR + LANES] * cos + p[:, _C_KRR:_C_KRR + LANES] * sin
        if expand_kv:
            kr_ref[0, :, rows] = kr.T[:ROPE]
        else:
            kr_ref[0, rows, :] = kr[:, :ROPE]
        qb_ref[0, rows, :] = (p[:, _C_QB:_C_QB + H_B * D_HB] * SB_SCALE).astype(BF16)
        kb_ref[0, rows, :] = p[:, _C_KB:_C_KB + H_B * D_HB]
        vb_ref[0, rows, :] = p[:, _C_VB:_C_VB + H_B * D_HB]
        if expand_kv:
            for hd in range(H_A):
                lo = hd * HEAD_PAD
                k_ref[0, rows, lo:lo + HEAD_PAD] = (k2s[i][:, lo:lo + HEAD_PAD] + kr).astype(BF16)
            vt_ref[0, :, rows] = (v2s[i] + vone_ref[...]).T.astype(BF16)


def _mixer_inputs(x, ada, cos_t, sin_t, wts, *, tm, expand_kv):
    b, t, _ = x.shape
    nt = t // tm
    nseg = ada.shape[0] // b
    tok = lambda n: pl.BlockSpec((1, tm, n), lambda i, j: (i, j, 0))
    in_specs = [tok(D_MODEL),
                pl.BlockSpec((nseg, 6, D_MODEL), lambda i, j: (i, 0, 0)),
                _const_spec((1, D_MODEL)),
                _const_spec((D_MODEL, _N_K1)),
                _const_spec((1, Q_RANK)),
                _const_spec((Q_RANK, 2 * H_A * HEAD_PAD)),
                _const_spec((1, KV_RANK)),
                _const_spec((KV_RANK, H_A * HEAD_PAD)),
                _const_spec((KV_RANK, H_A * VT_ROWS)),
                _const_spec((1, H_A * VT_ROWS)),
                pl.BlockSpec((tm, LANES), lambda i, j: (j, 0)),
                pl.BlockSpec((tm, LANES), lambda i, j: (j, 0))]
    shapes = [((b, t, H_A * HEAD_PAD), BF16), ((b, t, KV_RANK), F32), ((b, t, ROPE), F32),
              ((b, t, H_B * D_HB), BF16), ((b, t, H_B * D_HB), F32), ((b, t, H_B * D_HB), F32)]
    if expand_kv:
        shapes += [((b, t, H_A * HEAD_PAD), BF16)]
    out_specs = [tok(s[-1]) for s, _ in shapes]
    if expand_kv:
        t_minor = lambda n: pl.BlockSpec((1, n, tm), lambda i, j: (i, 0, j))
        shapes[2], out_specs[2] = ((b, ROPE, t), F32), t_minor(ROPE)
        shapes += [((b, H_A * VT_ROWS, t), BF16)]
        out_specs += [t_minor(H_A * VT_ROWS)]
    return pl.pallas_call(
        functools.partial(_mixer_kernel, expand_kv=expand_kv),
        grid=(b, nt),
        in_specs=in_specs,
        out_specs=out_specs,
        out_shape=[jax.ShapeDtypeStruct(s, d) for s, d in shapes],
        compiler_params=_params(2),
        name="mixer_in_kv" if expand_kv else "mixer_in",
    )(x, ada, wts["g_pre_mix"], wts["w_in_k1"], wts["g_q_lat"], wts["w_uq_ext"], wts["g_kv_lat"],
      wts["w_uk_pad"], wts["w_uv_t"], wts["v_ones"], cos_t, sin_t)


MLA_PAIR = 2


MLA_TILES = 2


def _mla_prompt_kernel(q_ref, k_ref, vt_ref, bias_ref, o_ref, sa_ref, sb_ref, m_ref, acc_ref, *, blk):
    tq = MLA_PAIR * blk
    lanes = [slice(hh * HEAD_PAD, (hh + 1) * HEAD_PAD) for hh in range(2)]

    def start_of(j):
        return pl.multiple_of(j * blk, blk)

    def scores(s_ref, rows, g):
        for hh, ln in enumerate(lanes):
            for b in range(MLA_PAIR):
                k = k_ref[0, pl.ds(start_of(MLA_PAIR * g + b), blk), ln]
                s_ref[hh, b] = _dot_t(k, q_ref[0, rows, ln])

    def softmax_pv(s_ref, g, masked):
        for hh in range(2):
            ss = []
            for b in range(MLA_PAIR):
                s = s_ref[hh, b]
                ss.append(s + bias_ref[b] if masked else s)
            m_old = m_ref[hh]
            m_new = m_old
            for s in ss:
                m_new = jnp.maximum(m_new, jnp.max(s, axis=0, keepdims=True))
            acc = jnp.exp2(m_old - m_new) * acc_ref[hh]
            for b, s in enumerate(ss):
                p = jnp.exp2(s - m_new).astype(BF16)
                vt = vt_ref[0, hh * VT_ROWS:(hh + 1) * VT_ROWS, pl.ds(start_of(MLA_PAIR * g + b), blk)]
                acc = acc + _dot(vt, p)
            acc_ref[hh] = acc
            m_ref[hh] = m_new

    def finish(rows):
        outs = [acc_ref[hh][:DV] / acc_ref[hh][DV:DV + 1] for hh in range(2)]
        o_ref[0, rows, :] = jnp.concatenate(outs, axis=0).T.astype(BF16)

    def tile(n, rows, even_ref, odd_ref, n_is_even, first_scores_done):
        if not first_scores_done:
            scores(even_ref, rows, 0)
        m_ref[...] = jnp.full(m_ref.shape, NEG, F32)
        acc_ref[...] = jnp.zeros(acc_ref.shape, F32)

        def body(h, c):
            scores(odd_ref, rows, 2 * h + 1)
            softmax_pv(even_ref, 2 * h, False)
            scores(even_ref, rows, 2 * h + 2)
            softmax_pv(odd_ref, 2 * h + 1, False)
            return c

        lax.fori_loop(0, n // 2, body, 0)
        if not n_is_even:
            scores(odd_ref, rows, n)
            softmax_pv(even_ref, n - 1, False)
        return odd_ref if not n_is_even else even_ref

    step = pl.program_id(2)
    n0 = MLA_TILES * step
    rows0, rows1 = slice(0, tq), slice(tq, 2 * tq)
    diag0 = tile(n0, rows0, sa_ref, sb_ref, True, False)
    scores(sb_ref, rows1, 0)
    softmax_pv(diag0, n0, True)
    finish(rows0)
    diag1 = tile(n0 + 1, rows1, sb_ref, sa_ref, False, True)
    softmax_pv(diag1, n0 + 1, True)
    finish(rows1)


def _mla_diag_bias(blk, tq):
    kpos = np.arange(MLA_PAIR * blk).reshape(MLA_PAIR, blk, 1)
    qpos = np.arange(tq).reshape(1, 1, tq)
    return np.where(kpos // CHUNK <= qpos // CHUNK, 0.0, NEG).astype(np.float32)


def _mla_prompt(q, k, vt, *, blk):
    b, t, _ = q.shape
    tq = MLA_PAIR * blk
    rows = MLA_TILES * tq
    assert t % rows == 0
    return pl.pallas_call(
        functools.partial(_mla_prompt_kernel, blk=blk),
        grid=(b, H_A // 2, t // rows),
        in_specs=[pl.BlockSpec((1, rows, 2 * HEAD_PAD), lambda i, j, n: (i, n, j)),
                  pl.BlockSpec((1, t, 2 * HEAD_PAD), lambda i, j, n: (i, 0, j)),
                  pl.BlockSpec((1, 2 * VT_ROWS, t), lambda i, j, n: (i, j, 0)),
                  _const_spec((MLA_PAIR, blk, tq))],
        out_specs=pl.BlockSpec((1, rows, 2 * DV), lambda i, j, n: (i, n, j)),
        out_shape=jax.ShapeDtypeStruct((b, t, H_A * DV), BF16),
        scratch_shapes=[pltpu.VMEM((2, MLA_PAIR, blk, tq), F32), pltpu.VMEM((2, MLA_PAIR, blk, tq), F32),
                        pltpu.VMEM((2, 1, tq), F32), pltpu.VMEM((2, VT_ROWS, tq), F32)],
        compiler_params=_params(3),
        name="mla_prompt",
    )(q, k, vt, jnp.asarray(_mla_diag_bias(blk, tq)))


def _tri(n):
    r = lax.broadcasted_iota(jnp.int32, (n, n), 0)
    c = lax.broadcasted_iota(jnp.int32, (n, n), 1)
    return jnp.where(r > c, 1.0, 0.0).astype(BF16)


def _sb_chains(chains, r_init):
    n = len(chains)
    z, sp, sp_bf, later, a = ([None] * n for _ in range(5))
    r_before = [None] * n
    r = dict(r_init)
    out = {}

    def scores(c):
        _, qs, k, _, _, _ = chains[c]
        z[c] = k(qs) if callable(k) else _dot_t(qs, k)

    def softplus(c):
        g, _, _, _, _, keep = chains[c]
        zb = z[c].astype(BF16)
        sp_bf[c] = jnp.maximum(zb, 0.0) + jnp.log(1.0 + jnp.exp(-jnp.abs(zb)))
        sp[c] = sp_bf[c].astype(F32)
        if keep is not None:
            sp[c] = jnp.where(keep, sp[c], 0.0)
            sp_bf[c] = sp[c].astype(BF16)
        rs = jnp.sum(sp[c], axis=-1, keepdims=True)
        r_before[c] = r[g]
        r[g] = rs if r[g] is None else r[g] + rs

    def suffix(c):
        later[c] = _dot(sp_bf[c], chains[c][4])

    def weights(c):
        _, _, _, _, tri, keep = chains[c]
        tk = tri.shape[0]
        w = z[c] - sp[c] - later[c]
        if r_before[c] is not None:
            r_b = r_before[c]
            if r_b.shape[1] == LANES and tk > LANES:
                r_b = jnp.concatenate([r_b] * (tk // LANES), axis=1)
            elif r_b.shape[1] == LANES and tk < LANES:
                r_b = r_b[:, :tk]
            w = w - r_b
        w = jnp.exp2(w * LOG2E)
        a[c] = (w if keep is None else jnp.where(keep, w, 0.0)).astype(BF16)

    def values(c):
        g, _, _, v, _, _ = chains[c]
        o = v(a[c]) if callable(v) else _dot(a[c], v)
        out[g] = o if g not in out else out[g] + o

    for stage in (scores, softplus, suffix, weights, values):
        for c in range(n):
            stage(c)
    return out, r


def _head_mask(x, hh):
    lane = lax.broadcasted_iota(jnp.int32, x.shape, 1)
    mine = jnp.logical_and(lane >= hh * D_HB, lane < (hh + 1) * D_HB)
    return jnp.where(mine, x, jnp.zeros_like(x))


def _stack_heads(q):
    return jnp.concatenate([_head_mask(q, 0), _head_mask(q, 1)], axis=0)


def _causal_pair(t, tk):
    row = lax.broadcasted_iota(jnp.int32, (t, tk), 0)
    col = lax.broadcasted_iota(jnp.int32, (t, tk), 1)
    keep = col < row
    return jnp.concatenate([keep, keep], axis=0)


def _sb_tail(qs, load, tri, acc_ref, r_ref, j_start, keep=None):
    def cond(c):
        j, rmin = c
        return jnp.logical_and(j >= 0, rmin < SB_DEAD)

    def body(c):
        j, _ = c
        o, r_new = _sb_chains([(0, qs) + load(j) + (tri, keep)], {0: r_ref[...]})
        acc_ref[...] += o[0]
        r_ref[...] = r_new[0]
        return j - 1, jnp.min(r_new[0])

    lax.while_loop(cond, body, (j_start, jnp.min(r_ref[...])))


def _unstack_heads(acc, t):
    lane = lax.broadcasted_iota(jnp.int32, (t, LANES), 1)
    return jnp.where(lane < D_HB, acc[:t], acc[t:])


def _sb_prompt_kernel(q_ref, k_ref, v_ref, o_ref, acc_ref, r_ref, *, tq, nsub):
    qi = pl.program_id(2)
    tri = _tri(tq)
    causal = _causal_pair(tq, tq)

    def load(j):
        start = pl.multiple_of(j * tq, tq)
        return (k_ref[0, pl.ds(start, tq), :].astype(BF16),
                v_ref[0, pl.ds(start, tq), :].astype(BF16))

    qss = [_stack_heads(q_ref[0, sub * tq:(sub + 1) * tq, :]) for sub in range(nsub)]
    chains = []
    for back in range(SB_AHEAD + 1):
        for sub in range(nsub):
            blk = qi * nsub + sub
            keep = causal if back == 0 else blk >= back
            chains.append((sub, qss[sub]) + load(jnp.maximum(blk - back, 0)) + (tri, keep))
    o, r = _sb_chains(chains, {sub: None for sub in range(nsub)})
    for sub in range(nsub):
        acc_ref[sub] = o[sub]
        r_ref[sub] = jnp.broadcast_to(r[sub], (2 * tq, LANES))
    for sub in range(nsub):
        _sb_tail(qss[sub], load, tri, acc_ref.at[sub], r_ref.at[sub], qi * nsub + sub - SB_AHEAD - 1)
        o_ref[0, sub * tq:(sub + 1) * tq, :] = _unstack_heads(acc_ref[sub], tq).astype(BF16)


def _sb_prompt(q, k, v, *, tq, nsub):
    b, t, _ = q.shape
    rows = tq * nsub
    return pl.pallas_call(
        functools.partial(_sb_prompt_kernel, tq=tq, nsub=nsub),
        grid=(b, H_B // 2, t // rows),
        in_specs=[pl.BlockSpec((1, rows, LANES), lambda i, j, n: (i, n, j)),
                  pl.BlockSpec((1, t, LANES), lambda i, j, n: (i, 0, j)),
                  pl.BlockSpec((1, t, LANES), lambda i, j, n: (i, 0, j))],
        out_specs=pl.BlockSpec((1, rows, LANES), lambda i, j, n: (i, n, j)),
        out_shape=jax.ShapeDtypeStruct((b, t, H_B * D_HB), BF16),
        scratch_shapes=[pltpu.VMEM((nsub, 2 * tq, LANES), F32), pltpu.VMEM((nsub, 2 * tq, LANES), F32)],
        compiler_params=_params(3),
        name="sb_prompt",
    )(q, k, v)


def _sb_sample_kernel(q_ref, kn_ref, vn_ref, kc_ref, vc_ref, o_ref, left_ref, acc_ref, r_ref, *, tk):
    tq = q_ref.shape[1]
    nblk = kc_ref.shape[1] // tk
    qs = _stack_heads(q_ref[0])
    tri = _tri(tk)

    def load(j):
        start = pl.multiple_of(j * tk, tk)
        return (kc_ref[0, pl.ds(start, tk), :].astype(BF16),
                vc_ref[0, pl.ds(start, tk), :].astype(BF16))

    o, r = _sb_chains([(0, qs, kn_ref[0].astype(BF16), vn_ref[0].astype(BF16), _tri(tq), _causal_pair(tq, tq)),
                       (0, qs) + load(nblk - 1) + (tri, None)], {0: None})
    acc_ref[...] = o[0]
    r_ref[...] = jnp.broadcast_to(r[0], (2 * tq, LANES))
    _sb_tail(qs, load, tri, acc_ref, r_ref, jnp.int32(nblk - 2))
    o_ref[0] = _unstack_heads(acc_ref[...], tq).astype(BF16)
    left_ref[...] = jnp.full(left_ref.shape, jnp.min(r_ref[...]), F32)


def _sb_sample_call(q, k_new, v_new, k_cache, v_cache, *, tk):
    b, t, _ = q.shape
    past = k_cache.shape[1]
    new = pl.BlockSpec((1, t, LANES), lambda i, j: (i, 0, j))
    old = pl.BlockSpec((1, past, LANES), lambda i, j: (i, 0, j))
    return pl.pallas_call(
        functools.partial(_sb_sample_kernel, tk=tk),
        grid=(b, H_B // 2),
        in_specs=[new, new, new, old, old],
        out_specs=[new, pl.BlockSpec((1, 1, SUBLANES, LANES), lambda i, j: (i, j, 0, 0))],
        out_shape=[jax.ShapeDtypeStruct((b, t, H_B * D_HB), BF16),
                   jax.ShapeDtypeStruct((b, H_B // 2, SUBLANES, LANES), F32)],
        scratch_shapes=[pltpu.VMEM((2 * t, LANES), F32), pltpu.VMEM((2 * t, LANES), F32)],
        compiler_params=_params(2),
        name="sb_sample",
    )(q, k_new, v_new, k_cache, v_cache)


SB_DECODE_KEYS = 256
SB_DECODE_AHEAD = 1


def _sb_decode_kernel(q_ref, kn_ref, vn_ref, kct_ref, vct_ref, o_ref, left_ref, acc_ref, r_ref):
    t = q_ref.shape[1]
    rows = H_B * t
    ck = SB_DECODE_KEYS
    nstep = kct_ref.shape[3] // ck
    assert t & (t - 1) == 0
    head = lambda x, h: x[:, h * D_HB:(h + 1) * D_HB]
    q = q_ref[0]
    q3 = jnp.stack([head(q, h) for h in range(H_B)], axis=0)

    def cached(j):
        start = pl.multiple_of(j * ck, ck)
        kt = kct_ref[0, :, :, pl.ds(start, ck)].astype(BF16)
        vt = vct_ref[0, :, :, pl.ds(start, ck)].astype(BF16)

        def scores(_):
            z = lax.dot_general(q3, kt, (((2,), (1,)), ((0,), (0,))), preferred_element_type=F32)
            return z.reshape(rows, ck)

        def values(a):
            o = lax.dot_general(a.reshape(H_B, t, ck), vt, (((2,), (2,)), ((0,), (0,))),
                                preferred_element_type=F32)
            return o.reshape(rows, D_HB)

        return scores, values

    kn, vn = kn_ref[0].astype(BF16), vn_ref[0].astype(BF16)

    def new_scores(_):
        return jnp.concatenate([_dot_t(head(q, h), head(kn, h)) for h in range(H_B)], axis=0)

    def new_values(a):
        return jnp.concatenate([_dot(a[h * t:(h + 1) * t], head(vn, h)) for h in range(H_B)], axis=0)

    r_i = lax.broadcasted_iota(jnp.int32, (rows, t), 0)
    c_i = lax.broadcasted_iota(jnp.int32, (rows, t), 1)
    tri = _tri(ck)
    chains = [(0, None, new_scores, new_values, _tri(t), c_i < (r_i & (t - 1)))]
    ahead = min(SB_DECODE_AHEAD, nstep)
    for back in range(1, ahead + 1):
        chains.append((0, None) + cached(nstep - back) + (tri, None))
    o, r = _sb_chains(chains, {0: None})
    acc_ref[...] = o[0]
    r_ref[...] = jnp.broadcast_to(r[0], (rows, LANES))
    _sb_tail(None, cached, tri, acc_ref, r_ref, jnp.int32(nstep - ahead - 1))
    acc = acc_ref[...]
    o_ref[0] = jnp.concatenate([acc[h * t:(h + 1) * t] for h in range(H_B)], axis=1).astype(BF16)
    left_ref[...] = jnp.full(left_ref.shape, jnp.min(r_ref[...]), F32)


def _sb_decode_call(q, k_new, v_new, k_cache_t, v_cache_t, *, recent):
    b, t, _ = q.shape
    past = k_cache_t.shape[3]
    new = pl.BlockSpec((1, t, H_B * D_HB), lambda i: (i, 0, 0))
    old = pl.BlockSpec((1, H_B, D_HB, recent), lambda i: (i, 0, 0, past // recent - 1))
    return pl.pallas_call(
        _sb_decode_kernel,
        grid=(b,),
        in_specs=[new, new, new, old, old],
        out_specs=[new, pl.BlockSpec((1, SUBLANES, LANES), lambda i: (i, 0, 0))],
        out_shape=[jax.ShapeDtypeStruct((b, t, H_B * D_HB), BF16),
                   jax.ShapeDtypeStruct((b, SUBLANES, LANES), F32)],
        scratch_shapes=[pltpu.VMEM((H_B * t, D_HB), F32), pltpu.VMEM((H_B * t, LANES), F32)],
        compiler_params=_params(1),
        name="sb_decode",
    )(q, k_new, v_new, k_cache_t, v_cache_t)


def _sb_sample(q, k_new, v_new, k_cache, v_cache, *, tk, recent):
    b, past = k_cache.shape[:2]
    t = q.shape[1]
    flat = lambda c: c.reshape(b, c.shape[1], H_B * D_HB)
    full = lambda: _sb_sample_call(q, k_new, v_new, flat(k_cache), flat(v_cache), tk=tk)[0]
    if not (t & (t - 1) == 0 and past % recent == 0 and recent % SB_DECODE_KEYS == 0):
        return full()
    key_minor = lambda c: jnp.transpose(c, (0, 2, 3, 1))
    o, left = _sb_decode_call(q, k_new, v_new, key_minor(k_cache), key_minor(v_cache), recent=recent)
    if recent == past:
        return o
    return lax.cond(jnp.min(left) < SB_DEAD, full, lambda: o)


def _mla_sample_kernel(q_ref, cn_ref, rn_ref, cc_ref, rct_ref, wabs_ref, wuv_ref, o_ref, *, kc, past_len):
    t = q_ref.shape[1]
    past = cc_ref.shape[1]
    qa, qr = [], []
    for hd in range(H_A):
        qh = q_ref[0, :, hd * HEAD_PAD:(hd + 1) * HEAD_PAD]
        qa.append(_dot(qh, wabs_ref[hd]).astype(BF16))
        qr.append(qh[:, :ROPE])
    qa = jnp.concatenate(qa, axis=0)
    qr = jnp.concatenate(qr, axis=0)
    rows = H_A * t

    pieces = [(cc_ref[0, c * kc:(c + 1) * kc, :], rct_ref[0, :, c * kc:(c + 1) * kc], True, c * kc)
              for c in range(past // kc)]
    pieces.append((cn_ref[0], rn_ref[0], False, past_len))
    scores, lat = [], []
    for ckv, kr, key_minor, k0 in pieces:
        ckv = ckv.astype(BF16)
        kr = kr.astype(BF16)
        s = _dot_t(qa, ckv) + (_dot(qr, kr) if key_minor else _dot_t(qr, kr))
        n = ckv.shape[0]
        if (k0 + n - 1) // CHUNK > past_len // CHUNK:
            qpos = past_len + lax.broadcasted_iota(jnp.int32, (rows, n), 0) % t
            kpos = k0 + lax.broadcasted_iota(jnp.int32, (rows, n), 1)
            s = jnp.where((kpos >> CHUNK_SHIFT) <= (qpos >> CHUNK_SHIFT), s, NEG)
        scores.append(s)
        lat.append(ckv)
    m = functools.reduce(jnp.maximum, [jnp.max(s, axis=-1, keepdims=True) for s in scores])
    l = jnp.zeros((rows, 1), F32)
    o_lat = jnp.zeros((rows, KV_RANK), F32)
    for s, ckv in zip(scores, lat):
        p = jnp.exp2(s - m)
        l = l + jnp.sum(p, axis=-1, keepdims=True)
        o_lat = o_lat + _dot(p.astype(BF16), ckv)
    o_lat = (o_lat / l).astype(BF16)
    o = jnp.zeros((t, H_A * DV), F32)
    for hd in range(H_A):
        o = o + _dot(o_lat[hd * t:(hd + 1) * t], wuv_ref[hd])
    o_ref[0] = o.astype(BF16)


def _mla_sample(q, ckv_new, kr_new, ckv_cache, kr_cache, w_abs, w_uv_heads, *, past_len, kc):
    b, t, _ = q.shape
    past = ckv_cache.shape[1]
    row = lambda n, s: pl.BlockSpec((1, n, s), lambda i: (i, 0, 0))
    kr_cache = jnp.transpose(kr_cache, (0, 2, 1))
    return pl.pallas_call(
        functools.partial(_mla_sample_kernel, kc=kc, past_len=past_len),
        grid=(b,),
        in_specs=[row(t, H_A * HEAD_PAD), row(t, KV_RANK), row(t, ROPE),
                  row(past, KV_RANK), row(ROPE, past),
                  _const_spec((H_A, HEAD_PAD, KV_RANK)), _const_spec((H_A, KV_RANK, H_A * DV))],
        out_specs=row(t, H_A * DV),
        out_shape=jax.ShapeDtypeStruct((b, t, H_A * DV), BF16),
        compiler_params=_params(1),
        name="mla_sample",
    )(q, ckv_new, kr_new, ckv_cache, kr_cache, w_abs, w_uv_heads)


def _merge_kernel(x_ref, ada_ref, g_ref, wg_ref, oa_ref, ob_ref, wpa_ref, wpb_ref, wo_ref, gpost_ref, o_ref):
    tm = x_ref.shape[1]
    parts = _row_parts(tm)
    xs = [x_ref[0, rows, :] for rows in parts]
    gate_in = [_dot(_premix(x, ada_ref, g_ref[...], rows, tm).astype(BF16), wg_ref[...])
               for x, rows in zip(xs, parts)]
    pas = [_dot(oa_ref[0, rows, :], wpa_ref[...]) for rows in parts]
    pbs = [_dot(ob_ref[0, rows, :], wpb_ref[...]) for rows in parts]
    merged = [(_sigmoid(gi[:, :D_MODEL]) * pa + _sigmoid(gi[:, D_MODEL:]) * pb).astype(BF16)
              for gi, pa, pb in zip(gate_in, pas, pbs)]
    mos = [_dot(m, wo_ref[...]) for m in merged]
    for rows, x, mo in zip(parts, xs, mos):
        o_ref[0, rows, :] = x + _mod(ada_ref, 2, rows, tm) * _rms(mo, gpost_ref[...])


def _merge(x, ada, o_a, o_b, wts, *, tm):
    b, t, _ = x.shape
    nseg = ada.shape[0] // b
    tok = lambda n: pl.BlockSpec((1, tm, n), lambda i, j: (i, j, 0))
    return pl.pallas_call(
        _merge_kernel,
        grid=(b, t // tm),
        in_specs=[tok(D_MODEL),
                  pl.BlockSpec((nseg, 6, D_MODEL), lambda i, j: (i, 0, 0)),
                  _const_spec((1, D_MODEL)),
                  _const_spec((D_MODEL, 2 * D_MODEL)),
                  tok(H_A * DV), tok(H_B * D_HB),
                  _const_spec((H_A * DV, D_MODEL)), _const_spec((H_B * D_HB, D_MODEL)),
                  _const_spec((D_MODEL, D_MODEL)),
                  _const_spec((1, D_MODEL))],
        out_specs=tok(D_MODEL),
        out_shape=jax.ShapeDtypeStruct((b, t, D_MODEL), F32),
        compiler_params=_params(2),
        name="merge",
    )(x, ada, wts["g_pre_mix"], wts["w_gate"], o_a, o_b, wts["w_proj_a"], wts["w_proj_b"],
      wts["w_out"], wts["g_post_mix"])


FF_CHUNK = 256
SUB_GROUPS = 4
SUB_ROWS = SUB_GROUPS * SUBLANES
HALO = 2 * SUBLANES


def _gelu_tanh(a):
    return 0.5 * a * (1.0 + jnp.tanh(math.sqrt(2.0 / math.pi) * (a + 0.044715 * (a * a * a))))


def _ffn_kernel(x_ref, ada_ref, g_ref, wup_ref, cw_ref, cb_ref, wdn_ref, gpost_ref, cs_ref,
                o_ref, nc_ref, halo_ref, perm_ref, *, tm, ntile):
    ti = pl.program_id(1)
    nt = pl.num_programs(1)
    segmented = cs_ref.shape[0] > 1
    assert not segmented or (ntile == 1 and cs_ref.shape[0] == tm // SUB_ROWS)

    if not segmented:
        @pl.when(ti == 0)
        def _():
            halo_ref[...] = jnp.zeros((HALO, 2 * D_FF), F32)
            halo_ref[SUBLANES - 1:SUBLANES, :] = cs_ref[0, 0:1, :]
            halo_ref[HALO - 1:HALO, :] = cs_ref[0, 1:2, :]

    first = lax.broadcasted_iota(jnp.int32, (SUBLANES, FF_CHUNK), 0) == 0
    for tix in range(ntile):
        _ffn_tile(x_ref, ada_ref, g_ref, wup_ref, cw_ref, cb_ref, wdn_ref, gpost_ref, cs_ref, o_ref, nc_ref,
                  halo_ref, perm_ref.at[tix], slice(tix * tm, (tix + 1) * tm), first, segmented)

    if not segmented:
        @pl.when(ti == nt - 1)
        def _():
            nc_ref[0, 0:1, :] = halo_ref[SUBLANES - 1:SUBLANES, :]
            nc_ref[0, 1:2, :] = halo_ref[HALO - 1:HALO, :]


def _ffn_tile(x_ref, ada_ref, g_ref, wup_ref, cw_ref, cb_ref, wdn_ref, gpost_ref, cs_ref, o_ref, nc_ref,
              halo_ref, perm_ref, tile_rows, first, segmented):
    tm = tile_rows.stop - tile_rows.start
    nsub = tm // SUB_ROWS
    nc = D_MODEL // LANES
    all_rows = slice(0, tm)
    for c in range(nc):
        perm_ref[c] = x_ref[0, tile_rows, c * LANES:(c + 1) * LANES]
    x = jnp.concatenate(
        [jnp.concatenate([perm_ref[c, pl.ds(j * SUB_ROWS + i, SUBLANES, stride=SUB_GROUPS), :] for c in range(nc)],
                         axis=1)
         for j in range(nsub) for i in range(SUB_GROUPS)], axis=0)
    h2 = (_rms(x, g_ref[...]) * (1.0 + _mod(ada_ref, 4, all_rows, tm)) + _mod(ada_ref, 3, all_rows, tm)).astype(BF16)

    def up(col):
        cols = slice(col, col + FF_CHUNK)
        u = _dot(h2, wup_ref[:, cols])
        grp = [u[g * SUBLANES:(g + 1) * SUBLANES] for g in range(tm // SUBLANES)]
        own2 = [pltpu.roll(grp[j * SUB_GROUPS + SUB_GROUPS - 2], 1, axis=0) for j in range(nsub)]
        own3 = [pltpu.roll(grp[j * SUB_GROUPS + SUB_GROUPS - 1], 1, axis=0) for j in range(nsub)]
        if segmented:
            pred2 = [jnp.broadcast_to(cs_ref[j, 0:1, cols], (SUBLANES, FF_CHUNK)) for j in range(nsub)]
            pred3 = [jnp.broadcast_to(cs_ref[j, 1:2, cols], (SUBLANES, FF_CHUNK)) for j in range(nsub)]
            for j in range(nsub):
                last = (j + 1) * SUB_ROWS
                nc_ref[j, 0:1, cols] = u[last - SUBLANES - 1:last - SUBLANES]
                nc_ref[j, 1:2, cols] = u[last - 1:last]
        else:
            pred2 = [pltpu.roll(halo_ref[0:SUBLANES, cols], 1, axis=0)] + own2[:-1]
            pred3 = [pltpu.roll(halo_ref[SUBLANES:HALO, cols], 1, axis=0)] + own3[:-1]
            halo_ref[:, cols] = u[tm - HALO:]
        u1, u2 = [], []
        for j in range(nsub):
            f1 = jnp.where(first, pred3[j], own3[j])
            f2 = jnp.where(first, pred2[j], own2[j])
            g0 = j * SUB_GROUPS
            u1 += [f1] + grp[g0:g0 + SUB_GROUPS - 1]
            u2 += [f2, f1] + grp[g0:g0 + SUB_GROUPS - 2]
        u1 = jnp.concatenate(u1, axis=0)
        u2 = jnp.concatenate(u2, axis=0)
        return (cb_ref[:, cols] + cw_ref[0:1, cols] * u2 + cw_ref[1:2, cols] * u1 + cw_ref[2:3, cols] * u)

    nchunk = D_FF // FF_CHUNK
    acc = jnp.zeros((tm, D_MODEL), F32)
    ya, yb = up(0), up(D_FF)
    for c in range(nchunk):
        if c + 1 < nchunk:
            ya_next, yb_next = up((c + 1) * FF_CHUNK), up(D_FF + (c + 1) * FF_CHUNK)
        g = (_gelu_tanh(ya) * yb).astype(BF16)
        acc = acc + _dot(g, wdn_ref[c * FF_CHUNK:(c + 1) * FF_CHUNK, :])
        if c + 1 < nchunk:
            ya, yb = ya_next, yb_next
    y = x + _mod(ada_ref, 5, all_rows, tm) * _rms(acc, gpost_ref[...])
    for j in range(nsub):
        for i in range(SUB_GROUPS):
            g0 = (j * SUB_GROUPS + i) * SUBLANES
            for c in range(nc):
                perm_ref[c, pl.ds(j * SUB_ROWS + i, SUBLANES, stride=SUB_GROUPS), :] = (
                    y[g0:g0 + SUBLANES, c * LANES:(c + 1) * LANES])
    for c in range(nc):
        o_ref[0, tile_rows, c * LANES:(c + 1) * LANES] = perm_ref[c]


def _ffn(x, ada, conv_state, wts, *, tm):
    b, t, _ = x.shape
    nseg = ada.shape[0] // b
    assert tm % SUB_ROWS == 0 and t % tm == 0 and (nseg == 1 or (t == tm and t == nseg * SUB_ROWS))
    ntile = 2 if nseg == 1 and t % (2 * tm) == 0 else 1
    tok = pl.BlockSpec((1, ntile * tm, D_MODEL), lambda i, j: (i, j, 0))
    state = pl.BlockSpec((nseg, CONV_W - 1, 2 * D_FF), lambda i, j: (i, 0, 0))
    return pl.pallas_call(
        functools.partial(_ffn_kernel, tm=tm, ntile=ntile),
        grid=(b, t // (ntile * tm)),
        in_specs=[tok,
                  pl.BlockSpec((nseg, 6, D_MODEL), lambda i, j: (i, 0, 0)),
                  _const_spec((1, D_MODEL)),
                  _const_spec((D_MODEL, 2 * D_FF)),
                  _const_spec((CONV_W, 2 * D_FF)),
                  _const_spec((1, 2 * D_FF)),
                  _const_spec((D_FF, D_MODEL)),
                  _const_spec((1, D_MODEL)),
                  state],
        out_specs=[tok, state],
        out_shape=[jax.ShapeDtypeStruct((b, t, D_MODEL), F32),
                   jax.ShapeDtypeStruct((b * nseg, CONV_W - 1, 2 * D_FF), F32)],
        scratch_shapes=[pltpu.VMEM((HALO, 2 * D_FF), F32),
                        pltpu.VMEM((ntile, D_MODEL // LANES, tm, LANES), F32)],
        compiler_params=_params(1, 1),
        name="conv_ffn",
    )(x, ada, wts["g_pre_ffn"], wts["w_up"], wts["conv_w"], wts["conv_b"], wts["w_down"],
      wts["g_post_ffn"], conv_state)


def _rope_tables(first_pos, t):
    inv = ROPE_BASE ** (-np.arange(0, ROPE, 2, dtype=np.float64) / ROPE)
    ang = np.arange(first_pos, first_pos + t, dtype=np.float64)[:, None] * inv[None, :]
    cos, sin = jnp.asarray(np.cos(ang), F32), jnp.asarray(np.sin(ang), F32)
    cos_t = jnp.concatenate([cos, cos, jnp.ones((t, NOPE), F32), jnp.zeros((t, HEAD_PAD - ROPE - NOPE), F32)], axis=1)
    sin_t = jnp.concatenate([-sin, sin, jnp.zeros((t, HEAD_PAD - ROPE), F32)], axis=1)
    return cos_t, sin_t


def _swap_halves(w):
    return jnp.concatenate([w[..., ROPE // 2:], w[..., :ROPE // 2]], axis=-1)


def _pad_lanes(w, n):
    return jnp.pad(w, [(0, 0)] * (w.ndim - 1) + [(0, n - w.shape[-1])])


def _layer_weights(l, g_pre_mix, g_post_mix, g_pre_ffn, g_post_ffn, w_in, g_q_lat, w_uq, g_kv_lat,
                   w_uk, w_uv, w_proj_a, w_proj_b, w_out, w_up, conv_w, conv_b, w_down):
    splits = np.cumsum([Q_RANK, KV_RANK, ROPE, H_B * D_HB, H_B * D_HB, H_B * D_HB, D_MODEL])
    wi = w_in[l]
    w_ql, w_ckv, w_kr, w_qb, w_kb, w_vb, w_ga, w_gb = jnp.split(wi, splits, axis=1)
    w_in_k1 = jnp.concatenate([w_ql, w_ckv, _pad_lanes(w_kr, LANES), _pad_lanes(_swap_halves(w_kr), LANES),
                               w_qb, w_kb, w_vb], axis=1).astype(BF16)
    wq = w_uq[l].reshape(Q_RANK, H_A, NOPE + ROPE)
    wq_nope, wq_rope = wq[..., :NOPE], wq[..., NOPE:]
    zeros = jnp.zeros((Q_RANK, H_A, HEAD_PAD - ROPE - NOPE), F32)
    q_main = jnp.concatenate([wq_rope, wq_nope, zeros], axis=-1).reshape(Q_RANK, H_A * HEAD_PAD)
    q_rot = _pad_lanes(_swap_halves(wq_rope), HEAD_PAD).reshape(Q_RANK, H_A * HEAD_PAD)
    wk = w_uk[l].reshape(KV_RANK, H_A, NOPE)
    w_uk_pad = jnp.pad(wk, [(0, 0), (0, 0), (ROPE, HEAD_PAD - ROPE - NOPE)])
    w_abs = jnp.transpose(w_uk_pad, (1, 2, 0))
    w_uv_t = _pad_lanes(w_uv[l].reshape(KV_RANK, H_A, DV), VT_ROWS)
    v_ones = np.zeros((H_A, VT_ROWS), np.float32)
    v_ones[:, DV] = 1.0
    head_of_col = jnp.arange(H_A * DV) // DV
    w_uv_heads = jnp.where(head_of_col[None, None, :] == jnp.arange(H_A)[:, None, None], w_uv[l][None], 0.0)
    row = lambda g: g[l].reshape(1, -1)
    return {
        "g_pre_mix": row(g_pre_mix), "g_post_mix": row(g_post_mix),
        "g_pre_ffn": row(g_pre_ffn), "g_post_ffn": row(g_post_ffn),
        "g_q_lat": row(g_q_lat), "g_kv_lat": row(g_kv_lat),
        "w_in_k1": w_in_k1,
        "w_gate": jnp.concatenate([w_ga, w_gb], axis=1).astype(BF16),
        "w_uq_ext": jnp.concatenate([q_main, q_rot], axis=1).astype(BF16),
        "w_uk_pad": w_uk_pad.reshape(KV_RANK, H_A * HEAD_PAD).astype(BF16),
        "w_abs": w_abs.astype(BF16),
        "w_uv_t": w_uv_t.reshape(KV_RANK, H_A * VT_ROWS).astype(BF16),
        "v_ones": jnp.asarray(v_ones.reshape(1, H_A * VT_ROWS)),
        "w_uv_heads": w_uv_heads.astype(BF16),
        "w_proj_a": w_proj_a[l].astype(BF16), "w_proj_b": w_proj_b[l].astype(BF16),
        "w_out": w_out[l].astype(BF16),
        "w_up": w_up[l].astype(BF16), "conv_w": conv_w[l], "conv_b": conv_b[l].reshape(1, -1),
        "w_down": w_down[l].astype(BF16),
    }


def _pick_tile(t, want):
    tm = min(t, want)
    assert t % tm == 0
    return tm


def _layer(x, ada, first_pos, past, wts):
    b, t, _ = x.shape
    cos_t, sin_t = _rope_tables(first_pos, t)
    tm = _pick_tile(t, 512)
    if past is None:
        q, ckv, kr_t, qb, kb, vb, k_mla, vt_mla = _mixer_inputs(x, ada, cos_t, sin_t, wts, tm=tm, expand_kv=True)
        kr = jnp.transpose(kr_t, (0, 2, 1))
        o_a = _mla_prompt(q, k_mla, vt_mla, blk=_pick_tile(t, 512))
        o_b = _sb_prompt(qb, kb, vb, tq=_pick_tile(t, 256), nsub=max(n for n in (1, 2, 4) if t % (256 * n) == 0))
        conv_state = jnp.zeros((b, CONV_W - 1, 2 * D_FF), F32)
    else:
        past_ckv, past_kr, past_k, past_v, conv_state = past
        past_len = past_ckv.shape[1]
        flat = t == SUB_ROWS and (b * t) % 256 == 0 and b * t <= 512
        if flat:
            x = x.reshape(1, b * t, D_MODEL)
            cos_t, sin_t = jnp.tile(cos_t, (b, 1)), jnp.tile(sin_t, (b, 1))
            tm = b * t
        per_batch = lambda a: a.reshape(b, t, a.shape[-1])
        q, ckv, kr, qb, kb, vb = map(per_batch, _mixer_inputs(x, ada, cos_t, sin_t, wts, tm=tm, expand_kv=False))
        o_a = _mla_sample(q, ckv, kr, past_ckv, past_kr, wts["w_abs"], wts["w_uv_heads"],
                          past_len=past_len, kc=_pick_tile(past_len, 1024))
        o_b = _sb_sample(qb, kb, vb, past_k, past_v, tk=_pick_tile(past_len, 256), recent=512)
        o_a, o_b = (o.reshape(x.shape[0], x.shape[1], o.shape[-1]) for o in (o_a, o_b))
    x1 = _merge(x, ada, o_a, o_b, wts, tm=tm)
    y, new_conv = _ffn(x1, ada, conv_state, wts, tm=_pick_tile(x.shape[1], 256))
    state = (ckv, kr, kb.reshape(b, t, H_B, D_HB), vb.reshape(b, t, H_B, D_HB), new_conv)
    return y.reshape(b, t, D_MODEL), state


def kernel(x_prompt, x_sample, cache_mla_ckv, cache_mla_krope, cache_sb_k, cache_sb_v, state_ffn_conv,
           c_prompt, c_sample, w_ada, b_ada, g_pre_mix, g_post_mix, g_pre_ffn, g_post_ffn,
           w_in, g_q_lat, w_uq, g_kv_lat, w_uk, w_uv, w_proj_a, w_proj_b, w_out,
           w_up, conv_w, conv_b, w_down):
    depth = w_in.shape[0]
    nb_p = x_prompt.shape[0]
    past_len = cache_mla_ckv.shape[2]
    xp, xs = x_prompt, x_sample
    c_all = jnp.concatenate([c_prompt, c_sample], axis=0)
    st_p = [[] for _ in range(5)]
    st_s = [[] for _ in range(5)]
    for l in range(depth):
        wts = _layer_weights(l, g_pre_mix, g_post_mix, g_pre_ffn, g_post_ffn, w_in, g_q_lat, w_uq, g_kv_lat,
                             w_uk, w_uv, w_proj_a, w_proj_b, w_out, w_up, conv_w, conv_b, w_down)
        ada = _ada(c_all, w_ada[l], b_ada[l]).reshape(-1, 6, D_MODEL)
        xp, sp = _layer(xp, ada[:nb_p], 0, None, wts)
        past = (cache_mla_ckv[l], cache_mla_krope[l], cache_sb_k[l], cache_sb_v[l], state_ffn_conv[l])
        xs, ss = _layer(xs, ada[nb_p:], past_len, past, wts)
        for i in range(5):
            st_p[i].append(sp[i])
            st_s[i].append(ss[i])
    p_state = [jnp.stack(a, axis=0) for a in st_p]
    s_state = [jnp.stack(a, axis=0) for a in st_s]
    return (xp, xs, *p_state, *s_state)
```

```python
import functools
import math

import numpy as np
import jax
import jax.numpy as jnp
from jax import lax
from jax.experimental import pallas as pl
from jax.experimental.pallas import tpu as pltpu

D_MODEL = 1024
CHUNK = 64
CHUNK_SHIFT = 6
H_A = 8
NOPE = 64
ROPE = 32
DV = 64
Q_RANK = 384
KV_RANK = 256
ROPE_BASE = 10000.0
H_B = 8
D_HB = 64
D_FF = 2816
CONV_W = 3
EPS = 1e-6
NEG = -1e30
MLA_SCALE = (NOPE + ROPE) ** -0.5
SB_SCALE = D_HB ** -0.5
LOG2E = math.log2(math.e)

LANES = 128
SUBLANES = 8
HEAD_PAD = LANES
VT_ROWS = 80
SB_DEAD = 110.0
SB_AHEAD = 2
VMEM_LIMIT = 56 * 1024 * 1024
MXU_TILE = 256
TOKEN_TILE = 2 * MXU_TILE
FFN_TILE = MXU_TILE
MLA_BLOCK = 2 * MXU_TILE
SB_BLOCK = MXU_TILE
SB_SUBS = (1, 2, 4)
SB_RECENT = 2 * MXU_TILE
MLA_DECODE_KEYS = 4 * MXU_TILE

F32 = jnp.float32
BF16 = jnp.bfloat16


def _dot(a, b):
    return jnp.dot(a, b, preferred_element_type=F32)


def _dot_t(a, b):
    return lax.dot_general(a, b, (((1,), (1,)), ((), ())), preferred_element_type=F32)


def _rms(x, g):
    return x * lax.rsqrt(jnp.mean(x * x, axis=-1, keepdims=True) + EPS) * g


def _sigmoid(x):
    return 1.0 / (1.0 + jnp.exp(-x))


def _params(n_parallel, n_arbitrary=0):
    return pltpu.CompilerParams(
        dimension_semantics=("parallel",) * n_parallel + ("arbitrary",) * n_arbitrary,
        vmem_limit_bytes=VMEM_LIMIT)


def _const_spec(shape):
    return _fixed_spec(shape, (0,) * len(shape))


def _fixed_spec(shape, index):
    return pl.BlockSpec(shape, lambda *_: index, pipeline_mode=pl.Buffered(1))


def _ada_kernel(c_ref, w_ref, b_ref, o_ref):
    c = c_ref[...]
    s = (c * _sigmoid(c)).astype(BF16)
    o_ref[...] = _dot(s, w_ref[...].astype(BF16)) + b_ref[...]


def _ada(c_all, w_ada, b_ada):
    n = c_all.shape[0]
    nchunk = 6
    return pl.pallas_call(
        _ada_kernel,
        grid=(nchunk,),
        in_specs=[pl.BlockSpec((n, D_MODEL), lambda j: (0, 0)),
                  pl.BlockSpec((D_MODEL, D_MODEL), lambda j: (0, j)),
                  pl.BlockSpec((1, D_MODEL), lambda j: (0, j))],
        out_specs=pl.BlockSpec((n, D_MODEL), lambda j: (0, j)),
        out_shape=jax.ShapeDtypeStruct((n, 6 * D_MODEL), F32),
        compiler_params=_params(1),
        name="ada",
    )(c_all, w_ada, b_ada.reshape(1, -1))


_C_QLAT = 0
_C_CKV = _C_QLAT + Q_RANK
_C_KR = _C_CKV + KV_RANK
_C_KRR = _C_KR + LANES
_C_QB = _C_KRR + LANES
_C_KB = _C_QB + H_B * D_HB
_C_VB = _C_KB + H_B * D_HB
_N_K1 = _C_VB + H_B * D_HB
_N_GATE = 2 * D_MODEL
_GATE_BLOCK = -(-_N_K1 // _N_GATE)
_N_IN_ALL = (_GATE_BLOCK + 1) * _N_GATE


def _mod(ada_ref, comp, rows, tm):
    nseg = ada_ref.shape[0]
    if nseg == 1:
        return ada_ref[0, comp:comp + 1, :]
    seg = tm // nseg
    lo, hi = rows.start // seg, rows.stop // seg
    assert rows.start % seg == 0 and rows.stop % seg == 0
    return jnp.concatenate([jnp.broadcast_to(ada_ref[i, comp:comp + 1, :], (seg, ada_ref.shape[2]))
                            for i in range(lo, hi)], axis=0)


def _premix(x, ada_ref, g, rows, tm):
    return _rms(x, g) * (1.0 + _mod(ada_ref, 1, rows, tm)) + _mod(ada_ref, 0, rows, tm)


def _row_parts(tm):
    n = 2 if tm % (2 * LANES) == 0 else 1
    return [slice(i * (tm // n), (i + 1) * (tm // n)) for i in range(n)]


def _mixer_kernel(x_ref, ada_ref, g_ref, win_ref, gq_ref, wuq_ref, gkv_ref, wuk_ref, wuv_ref, vone_ref,
                  cos_ref, sin_ref, *out_refs, expand_kv):
    if expand_kv:
        q_ref, ckv_ref, kr_ref, qb_ref, kb_ref, vb_ref, k_ref, vt_ref = out_refs
    else:
        q_ref, ckv_ref, kr_ref, qb_ref, kb_ref, vb_ref = out_refs
    tm = x_ref.shape[1]
    parts = _row_parts(tm)
    nq = H_A * HEAD_PAD
    ps = [_dot_t(_premix(x_ref[0, rows, :], ada_ref, g_ref[...], rows, tm).astype(BF16), win_ref[...])
          for rows in parts]
    q_lats = [_rms(p[:, _C_QLAT:_C_QLAT + Q_RANK], gq_ref[...]).astype(BF16) for p in ps]
    c_kvs = [_rms(p[:, _C_CKV:_C_CKV + KV_RANK], gkv_ref[...]) for p in ps]
    q2s = [_dot(q_lat, wuq_ref[...]) for q_lat in q_lats]
    if expand_kv:
        c_bfs = [c_kv.astype(BF16) for c_kv in c_kvs]
        k2s = [_dot(c_bf, wuk_ref[...]) for c_bf in c_bfs]
        v2s = [_dot(c_bf, wuv_ref[...]) for c_bf in c_bfs]
    for i, rows in enumerate(parts):
        p, q2 = ps[i], q2s[i]
        cos = cos_ref[rows, :]
        sin = sin_ref[rows, :]
        for hd in range(H_A):
            lo = hd * HEAD_PAD
            qh = q2[:, lo:lo + HEAD_PAD] * cos + q2[:, nq + lo:nq + lo + HEAD_PAD] * sin
            q_ref[0, rows, lo:lo + HEAD_PAD] = (qh * (MLA_SCALE * LOG2E)).astype(BF16)
        ckv_ref[0, rows, :] = c_kvs[i]
        kr = p[:, _C_KR:_C_KR + LANES] * cos + p[:, _C_KRR:_C_KRR + LANES] * sin
        if expand_kv:
            kr_ref[0, :, rows] = kr.T[:ROPE]
        else:
            kr_ref[0, rows, :] = kr[:, :ROPE]
        qb_ref[0, rows, :] = (p[:, _C_QB:_C_QB + H_B * D_HB] * SB_SCALE).astype(BF16)
        kb_ref[0, rows, :] = p[:, _C_KB:_C_KB + H_B * D_HB]
        vb_ref[0, rows, :] = p[:, _C_VB:_C_VB + H_B * D_HB]
        if expand_kv:
            for hd in range(H_A):
                lo = hd * HEAD_PAD
                k_ref[0, rows, lo:lo + HEAD_PAD] = (k2s[i][:, lo:lo + HEAD_PAD] + kr).astype(BF16)
            vt_ref[0, :, rows] = (v2s[i] + vone_ref[...]).T.astype(BF16)


def _mixer_inputs(x, ada, cos_t, sin_t, wts, *, tm, expand_kv):
    b, t, _ = x.shape
    nt = t // tm
    nseg = ada.shape[0] // b
    tok = lambda n: pl.BlockSpec((1, tm, n), lambda i, j: (i, j, 0))
    in_specs = [tok(D_MODEL),
                pl.BlockSpec((nseg, 6, D_MODEL), lambda i, j: (i, 0, 0)),
                _const_spec((1, D_MODEL)),
                _const_spec((_N_K1, D_MODEL)),
                _const_spec((1, Q_RANK)),
                _const_spec((Q_RANK, 2 * H_A * HEAD_PAD)),
                _const_spec((1, KV_RANK)),
                _const_spec((KV_RANK, H_A * HEAD_PAD)),
                _const_spec((KV_RANK, H_A * VT_ROWS)),
                _const_spec((1, H_A * VT_ROWS)),
                pl.BlockSpec((tm, LANES), lambda i, j: (j, 0)),
                pl.BlockSpec((tm, LANES), lambda i, j: (j, 0))]
    shapes = [((b, t, H_A * HEAD_PAD), BF16), ((b, t, KV_RANK), F32), ((b, t, ROPE), F32),
              ((b, t, H_B * D_HB), BF16), ((b, t, H_B * D_HB), F32), ((b, t, H_B * D_HB), F32)]
    if expand_kv:
        shapes += [((b, t, H_A * HEAD_PAD), BF16)]
    out_specs = [tok(s[-1]) for s, _ in shapes]
    if expand_kv:
        t_minor = lambda n: pl.BlockSpec((1, n, tm), lambda i, j: (i, 0, j))
        shapes[2], out_specs[2] = ((b, ROPE, t), F32), t_minor(ROPE)
        shapes += [((b, H_A * VT_ROWS, t), BF16)]
        out_specs += [t_minor(H_A * VT_ROWS)]
    return pl.pallas_call(
        functools.partial(_mixer_kernel, expand_kv=expand_kv),
        grid=(b, nt),
        in_specs=in_specs,
        out_specs=out_specs,
        out_shape=[jax.ShapeDtypeStruct(s, d) for s, d in shapes],
        compiler_params=_params(2),
        name="mixer_in_kv" if expand_kv else "mixer_in",
    )(x, ada, wts["g_pre_mix"], wts["w_in_all"], wts["g_q_lat"], wts["w_uq_ext"], wts["g_kv_lat"],
      wts["w_uk_pad"], wts["w_uv_t"], wts["v_ones"], cos_t, sin_t)


MLA_PAIR = 2


MLA_TILES = 2


def _mla_prompt_kernel(q_ref, k_ref, vt_ref, bias_ref, o_ref, sa_ref, sb_ref, m_ref, acc_ref, *, blk):
    tq = MLA_PAIR * blk
    lanes = [slice(hh * HEAD_PAD, (hh + 1) * HEAD_PAD) for hh in range(2)]

    def start_of(j):
        return pl.multiple_of(j * blk, blk)

    def scores(s_ref, rows, g):
        for hh, ln in enumerate(lanes):
            for b in range(MLA_PAIR):
                k = k_ref[0, pl.ds(start_of(MLA_PAIR * g + b), blk), ln]
                s_ref[hh, b] = _dot_t(k, q_ref[0, rows, ln])

    def softmax_pv(s_ref, g, masked):
        for hh in range(2):
            ss = []
            for b in range(MLA_PAIR):
                s = s_ref[hh, b]
                ss.append(s + bias_ref[b] if masked else s)
            m_old = m_ref[hh]
            m_new = m_old
            for s in ss:
                m_new = jnp.maximum(m_new, jnp.max(s, axis=0, keepdims=True))
            acc = jnp.exp2(m_old - m_new) * acc_ref[hh]
            for b, s in enumerate(ss):
                p = jnp.exp2(s - m_new).astype(BF16)
                vt = vt_ref[0, hh * VT_ROWS:(hh + 1) * VT_ROWS, pl.ds(start_of(MLA_PAIR * g + b), blk)]
                acc = acc + _dot(vt, p)
            acc_ref[hh] = acc
            m_ref[hh] = m_new

    def finish(rows):
        outs = [acc_ref[hh][:DV] / acc_ref[hh][DV:DV + 1] for hh in range(2)]
        o_ref[0, rows, :] = jnp.concatenate(outs, axis=0).T.astype(BF16)

    def tile(n, rows, even_ref, odd_ref, n_is_even, first_scores_done):
        if not first_scores_done:
            scores(even_ref, rows, 0)
        m_ref[...] = jnp.full(m_ref.shape, NEG, F32)
        acc_ref[...] = jnp.zeros(acc_ref.shape, F32)

        def body(h, c):
            scores(odd_ref, rows, 2 * h + 1)
            softmax_pv(even_ref, 2 * h, False)
            scores(even_ref, rows, 2 * h + 2)
            softmax_pv(odd_ref, 2 * h + 1, False)
            return c

        lax.fori_loop(0, n // 2, body, 0)
        if not n_is_even:
            scores(odd_ref, rows, n)
            softmax_pv(even_ref, n - 1, False)
        return odd_ref if not n_is_even else even_ref

    step = pl.program_id(2)
    n0 = MLA_TILES * step
    rows0, rows1 = slice(0, tq), slice(tq, 2 * tq)
    diag0 = tile(n0, rows0, sa_ref, sb_ref, True, False)
    scores(sb_ref, rows1, 0)
    softmax_pv(diag0, n0, True)
    finish(rows0)
    diag1 = tile(n0 + 1, rows1, sb_ref, sa_ref, False, True)
    softmax_pv(diag1, n0 + 1, True)
    finish(rows1)


def _mla_diag_bias(blk, tq):
    kpos = np.arange(MLA_PAIR * blk).reshape(MLA_PAIR, blk, 1)
    qpos = np.arange(tq).reshape(1, 1, tq)
    return np.where(kpos // CHUNK <= qpos // CHUNK, 0.0, NEG).astype(np.float32)


def _mla_prompt(q, k, vt, *, blk):
    b, t, _ = q.shape
    tq = MLA_PAIR * blk
    rows = MLA_TILES * tq
    assert t % rows == 0
    return pl.pallas_call(
        functools.partial(_mla_prompt_kernel, blk=blk),
        grid=(b, H_A // 2, t // rows),
        in_specs=[pl.BlockSpec((1, rows, 2 * HEAD_PAD), lambda i, j, n: (i, n, j)),
                  pl.BlockSpec((1, t, 2 * HEAD_PAD), lambda i, j, n: (i, 0, j)),
                  pl.BlockSpec((1, 2 * VT_ROWS, t), lambda i, j, n: (i, j, 0)),
                  _const_spec((MLA_PAIR, blk, tq))],
        out_specs=pl.BlockSpec((1, rows, 2 * DV), lambda i, j, n: (i, n, j)),
        out_shape=jax.ShapeDtypeStruct((b, t, H_A * DV), BF16),
        scratch_shapes=[pltpu.VMEM((2, MLA_PAIR, blk, tq), F32), pltpu.VMEM((2, MLA_PAIR, blk, tq), F32),
                        pltpu.VMEM((2, 1, tq), F32), pltpu.VMEM((2, VT_ROWS, tq), F32)],
        compiler_params=_params(3),
        name="mla_prompt",
    )(q, k, vt, jnp.asarray(_mla_diag_bias(blk, tq)))


def _tri(n):
    r = lax.broadcasted_iota(jnp.int32, (n, n), 0)
    c = lax.broadcasted_iota(jnp.int32, (n, n), 1)
    return jnp.where(r > c, 1.0, 0.0).astype(BF16)


def _sb_chains(chains, r_init):
    n = len(chains)
    z, sp, sp_bf, later, a = ([None] * n for _ in range(5))
    r_before = [None] * n
    r = dict(r_init)
    out = {}

    def scores(c):
        _, qs, k, _, _, _ = chains[c]
        z[c] = k(qs) if callable(k) else _dot_t(qs, k)

    def softplus(c):
        g, _, _, _, _, keep = chains[c]
        zb = z[c].astype(BF16)
        sp_bf[c] = jnp.maximum(zb, 0.0) + jnp.log(1.0 + jnp.exp(-jnp.abs(zb)))
        sp[c] = sp_bf[c].astype(F32)
        if keep is not None:
            sp[c] = jnp.where(keep, sp[c], 0.0)
            sp_bf[c] = sp[c].astype(BF16)
        rs = jnp.sum(sp[c], axis=-1, keepdims=True)
        r_before[c] = r[g]
        r[g] = rs if r[g] is None else r[g] + rs

    def suffix(c):
        later[c] = _dot(sp_bf[c], chains[c][4])

    def weights(c):
        _, _, _, _, tri, keep = chains[c]
        tk = tri.shape[0]
        w = z[c] - sp[c] - later[c]
        if r_before[c] is not None:
            r_b = r_before[c]
            if r_b.shape[1] == LANES and tk > LANES:
                r_b = jnp.concatenate([r_b] * (tk // LANES), axis=1)
            elif r_b.shape[1] == LANES and tk < LANES:
                r_b = r_b[:, :tk]
            w = w - r_b
        w = jnp.exp2(w * LOG2E)
        a[c] = (w if keep is None else jnp.where(keep, w, 0.0)).astype(BF16)

    def values(c):
        g, _, _, v, _, _ = chains[c]
        o = v(a[c]) if callable(v) else _dot(a[c], v)
        out[g] = o if g not in out else out[g] + o

    for stage in (scores, softplus, suffix, weights, values):
        for c in range(n):
            stage(c)
    return out, r


def _head_mask(x, hh):
    lane = lax.broadcasted_iota(jnp.int32, x.shape, 1)
    mine = jnp.logical_and(lane >= hh * D_HB, lane < (hh + 1) * D_HB)
    return jnp.where(mine, x, jnp.zeros_like(x))


def _stack_heads(q):
    return jnp.concatenate([_head_mask(q, 0), _head_mask(q, 1)], axis=0)


def _causal_pair(t, tk):
    row = lax.broadcasted_iota(jnp.int32, (t, tk), 0)
    col = lax.broadcasted_iota(jnp.int32, (t, tk), 1)
    keep = col < row
    return jnp.concatenate([keep, keep], axis=0)


def _sb_tail(qs, load, tri, acc_ref, r_ref, j_start, keep=None):
    def cond(c):
        j, rmin = c
        return jnp.logical_and(j >= 0, rmin < SB_DEAD)

    def body(c):
        j, _ = c
        o, r_new = _sb_chains([(0, qs) + load(j) + (tri, keep)], {0: r_ref[...]})
        acc_ref[...] += o[0]
        r_ref[...] = r_new[0]
        return j - 1, jnp.min(r_new[0])

    lax.while_loop(cond, body, (j_start, jnp.min(r_ref[...])))


def _unstack_heads(acc, t):
    lane = lax.broadcasted_iota(jnp.int32, (t, LANES), 1)
    return jnp.where(lane < D_HB, acc[:t], acc[t:])


def _sb_prompt_kernel(q_ref, k_ref, v_ref, o_ref, acc_ref, r_ref, *, tq, nsub):
    qi = pl.program_id(2)
    tri = _tri(tq)
    causal = _causal_pair(tq, tq)

    def load(j):
        start = pl.multiple_of(j * tq, tq)
        return (k_ref[0, pl.ds(start, tq), :].astype(BF16),
                v_ref[0, pl.ds(start, tq), :].astype(BF16))

    qss = [_stack_heads(q_ref[0, sub * tq:(sub + 1) * tq, :]) for sub in range(nsub)]
    chains = []
    for back in range(SB_AHEAD + 1):
        for sub in range(nsub):
            blk = qi * nsub + sub
            keep = causal if back == 0 else blk >= back
            chains.append((sub, qss[sub]) + load(jnp.maximum(blk - back, 0)) + (tri, keep))
    o, r = _sb_chains(chains, {sub: None for sub in range(nsub)})
    for sub in range(nsub):
        acc_ref[sub] = o[sub]
        r_ref[sub] = jnp.broadcast_to(r[sub], (2 * tq, LANES))
    for sub in range(nsub):
        _sb_tail(qss[sub], load, tri, acc_ref.at[sub], r_ref.at[sub], qi * nsub + sub - SB_AHEAD - 1)
        o_ref[0, sub * tq:(sub + 1) * tq, :] = _unstack_heads(acc_ref[sub], tq).astype(BF16)


def _sb_prompt(q, k, v, *, tq, nsub):
    b, t, _ = q.shape
    rows = tq * nsub
    return pl.pallas_call(
        functools.partial(_sb_prompt_kernel, tq=tq, nsub=nsub),
        grid=(b, H_B // 2, t // rows),
        in_specs=[pl.BlockSpec((1, rows, LANES), lambda i, j, n: (i, n, j)),
                  pl.BlockSpec((1, t, LANES), lambda i, j, n: (i, 0, j)),
                  pl.BlockSpec((1, t, LANES), lambda i, j, n: (i, 0, j))],
        out_specs=pl.BlockSpec((1, rows, LANES), lambda i, j, n: (i, n, j)),
        out_shape=jax.ShapeDtypeStruct((b, t, H_B * D_HB), BF16),
        scratch_shapes=[pltpu.VMEM((nsub, 2 * tq, LANES), F32), pltpu.VMEM((nsub, 2 * tq, LANES), F32)],
        compiler_params=_params(3),
        name="sb_prompt",
    )(q, k, v)


def _sb_sample_kernel(q_ref, kn_ref, vn_ref, kc_ref, vc_ref, o_ref, left_ref, acc_ref, r_ref, *, tk):
    tq = q_ref.shape[1]
    nblk = kc_ref.shape[1] // tk
    qs = _stack_heads(q_ref[0])
    tri = _tri(tk)

    def load(j):
        start = pl.multiple_of(j * tk, tk)
        return (kc_ref[0, pl.ds(start, tk), :].astype(BF16),
                vc_ref[0, pl.ds(start, tk), :].astype(BF16))

    o, r = _sb_chains([(0, qs, kn_ref[0].astype(BF16), vn_ref[0].astype(BF16), _tri(tq), _causal_pair(tq, tq)),
                       (0, qs) + load(nblk - 1) + (tri, None)], {0: None})
    acc_ref[...] = o[0]
    r_ref[...] = jnp.broadcast_to(r[0], (2 * tq, LANES))
    _sb_tail(qs, load, tri, acc_ref, r_ref, jnp.int32(nblk - 2))
    o_ref[0] = _unstack_heads(acc_ref[...], tq).astype(BF16)
    left_ref[...] = jnp.full(left_ref.shape, jnp.min(r_ref[...]), F32)


def _sb_sample_call(q, k_new, v_new, k_cache, v_cache, *, tk):
    b, t, _ = q.shape
    past = k_cache.shape[1]
    new = pl.BlockSpec((1, t, LANES), lambda i, j: (i, 0, j))
    old = pl.BlockSpec((1, past, LANES), lambda i, j: (i, 0, j))
    return pl.pallas_call(
        functools.partial(_sb_sample_kernel, tk=tk),
        grid=(b, H_B // 2),
        in_specs=[new, new, new, old, old],
        out_specs=[new, pl.BlockSpec((1, 1, SUBLANES, LANES), lambda i, j: (i, j, 0, 0))],
        out_shape=[jax.ShapeDtypeStruct((b, t, H_B * D_HB), BF16),
                   jax.ShapeDtypeStruct((b, H_B // 2, SUBLANES, LANES), F32)],
        scratch_shapes=[pltpu.VMEM((2 * t, LANES), F32), pltpu.VMEM((2 * t, LANES), F32)],
        compiler_params=_params(2),
        name="sb_sample",
    )(q, k_new, v_new, k_cache, v_cache)


SB_DECODE_KEYS = MXU_TILE
SB_DECODE_AHEAD = 1


def _sb_decode_kernel(q_ref, kn_ref, vn_ref, kct_ref, vct_ref, o_ref, left_ref, acc_ref, r_ref):
    t = q_ref.shape[1]
    rows = H_B * t
    ck = SB_DECODE_KEYS
    nstep = kct_ref.shape[3] // ck
    assert t & (t - 1) == 0
    head = lambda x, h: x[:, h * D_HB:(h + 1) * D_HB]
    q = q_ref[0]
    q3 = jnp.stack([head(q, h) for h in range(H_B)], axis=0)

    def cached(j):
        start = pl.multiple_of(j * ck, ck)
        kt = kct_ref[0, :, :, pl.ds(start, ck)].astype(BF16)
        vt = vct_ref[0, :, :, pl.ds(start, ck)].astype(BF16)

        def scores(_):
            z = lax.dot_general(q3, kt, (((2,), (1,)), ((0,), (0,))), preferred_element_type=F32)
            return z.reshape(rows, ck)

        def values(a):
            o = lax.dot_general(a.reshape(H_B, t, ck), vt, (((2,), (2,)), ((0,), (0,))),
                                preferred_element_type=F32)
            return o.reshape(rows, D_HB)

        return scores, values

    kn, vn = kn_ref[0].astype(BF16), vn_ref[0].astype(BF16)

    def new_scores(_):
        return jnp.concatenate([_dot_t(head(q, h), head(kn, h)) for h in range(H_B)], axis=0)

    def new_values(a):
        return jnp.concatenate([_dot(a[h * t:(h + 1) * t], head(vn, h)) for h in range(H_B)], axis=0)

    r_i = lax.broadcasted_iota(jnp.int32, (rows, t), 0)
    c_i = lax.broadcasted_iota(jnp.int32, (rows, t), 1)
    tri = _tri(ck)
    chains = [(0, None, new_scores, new_values, _tri(t), c_i < (r_i & (t - 1)))]
    ahead = min(SB_DECODE_AHEAD, nstep)
    for back in range(1, ahead + 1):
        chains.append((0, None) + cached(nstep - back) + (tri, None))
    o, r = _sb_chains(chains, {0: None})
    acc_ref[...] = o[0]
    r_ref[...] = jnp.broadcast_to(r[0], (rows, LANES))
    _sb_tail(None, cached, tri, acc_ref, r_ref, jnp.int32(nstep - ahead - 1))
    acc = acc_ref[...]
    o_ref[0] = jnp.concatenate([acc[h * t:(h + 1) * t] for h in range(H_B)], axis=1).astype(BF16)
    left_ref[...] = jnp.full(left_ref.shape, jnp.min(r_ref[...]), F32)


def _sb_decode_call(q, k_new, v_new, k_cache_t, v_cache_t, *, recent):
    b, t, _ = q.shape
    past = k_cache_t.shape[3]
    new = pl.BlockSpec((1, t, H_B * D_HB), lambda i: (i, 0, 0))
    old = pl.BlockSpec((1, H_B, D_HB, recent), lambda i: (i, 0, 0, past // recent - 1))
    return pl.pallas_call(
        _sb_decode_kernel,
        grid=(b,),
        in_specs=[new, new, new, old, old],
        out_specs=[new, pl.BlockSpec((1, SUBLANES, LANES), lambda i: (i, 0, 0))],
        out_shape=[jax.ShapeDtypeStruct((b, t, H_B * D_HB), BF16),
                   jax.ShapeDtypeStruct((b, SUBLANES, LANES), F32)],
        scratch_shapes=[pltpu.VMEM((H_B * t, D_HB), F32), pltpu.VMEM((H_B * t, LANES), F32)],
        compiler_params=_params(1),
        name="sb_decode",
    )(q, k_new, v_new, k_cache_t, v_cache_t)


def _sb_sample(q, k_new, v_new, k_cache, v_cache, *, tk, recent):
    b, past = k_cache.shape[:2]
    t = q.shape[1]
    flat = lambda c: c.reshape(b, c.shape[1], H_B * D_HB)
    full = lambda: _sb_sample_call(q, k_new, v_new, flat(k_cache), flat(v_cache), tk=tk)[0]
    if not (t & (t - 1) == 0 and past % recent == 0 and recent % SB_DECODE_KEYS == 0):
        return full()
    key_minor = lambda c: jnp.transpose(c, (0, 2, 3, 1))
    o, left = _sb_decode_call(q, k_new, v_new, key_minor(k_cache), key_minor(v_cache), recent=recent)
    if recent == past:
        return o
    return lax.cond(jnp.min(left) < SB_DEAD, full, lambda: o)


def _mla_sample_kernel(q_ref, cn_ref, rn_ref, cc_ref, rct_ref, wabs_ref, wuv_ref, o_ref, *, kc, past_len):
    t = q_ref.shape[1]
    past = cc_ref.shape[1]
    qa, qr = [], []
    for hd in range(H_A):
        qh = q_ref[0, :, hd * HEAD_PAD:(hd + 1) * HEAD_PAD]
        qa.append(_dot(qh, wabs_ref[hd]).astype(BF16))
        qr.append(qh[:, :ROPE])
    qa = jnp.concatenate(qa, axis=0)
    qr = jnp.concatenate(qr, axis=0)
    rows = H_A * t

    pieces = [(cc_ref[0, c * kc:(c + 1) * kc, :], rct_ref[0, :, c * kc:(c + 1) * kc], True, c * kc)
              for c in range(past // kc)]
    pieces.append((cn_ref[0], rn_ref[0], False, past_len))
    scores, lat = [], []
    for ckv, kr, key_minor, k0 in pieces:
        ckv = ckv.astype(BF16)
        kr = kr.astype(BF16)
        s = _dot_t(qa, ckv) + (_dot(qr, kr) if key_minor else _dot_t(qr, kr))
        n = ckv.shape[0]
        if (k0 + n - 1) // CHUNK > past_len // CHUNK:
            qpos = past_len + lax.broadcasted_iota(jnp.int32, (rows, n), 0) % t
            kpos = k0 + lax.broadcasted_iota(jnp.int32, (rows, n), 1)
            s = jnp.where((kpos >> CHUNK_SHIFT) <= (qpos >> CHUNK_SHIFT), s, NEG)
        scores.append(s)
        lat.append(ckv)
    m = functools.reduce(jnp.maximum, [jnp.max(s, axis=-1, keepdims=True) for s in scores])
    l = jnp.zeros((rows, 1), F32)
    o_lat = jnp.zeros((rows, KV_RANK), F32)
    for s, ckv in zip(scores, lat):
        p = jnp.exp2(s - m)
        l = l + jnp.sum(p, axis=-1, keepdims=True)
        o_lat = o_lat + _dot(p.astype(BF16), ckv)
    o_lat = (o_lat / l).astype(BF16)
    o = jnp.zeros((t, H_A * DV), F32)
    for hd in range(H_A):
        o = o + _dot(o_lat[hd * t:(hd + 1) * t], wuv_ref[hd])
    o_ref[0] = o.astype(BF16)


def _mla_sample(q, ckv_new, kr_new, ckv_cache, kr_cache, w_abs, w_uv_heads, *, past_len, kc):
    b, t, _ = q.shape
    past = ckv_cache.shape[1]
    row = lambda n, s: pl.BlockSpec((1, n, s), lambda i: (i, 0, 0))
    kr_cache = jnp.transpose(kr_cache, (0, 2, 1))
    return pl.pallas_call(
        functools.partial(_mla_sample_kernel, kc=kc, past_len=past_len),
        grid=(b,),
        in_specs=[row(t, H_A * HEAD_PAD), row(t, KV_RANK), row(t, ROPE),
                  row(past, KV_RANK), row(ROPE, past),
                  _const_spec((H_A, HEAD_PAD, KV_RANK)), _const_spec((H_A, KV_RANK, H_A * DV))],
        out_specs=row(t, H_A * DV),
        out_shape=jax.ShapeDtypeStruct((b, t, H_A * DV), BF16),
        compiler_params=_params(1),
        name="mla_sample",
    )(q, ckv_new, kr_new, ckv_cache, kr_cache, w_abs, w_uv_heads)


def _merge_kernel(x_ref, ada_ref, g_ref, wg_ref, oa_ref, ob_ref, wpa_ref, wpb_ref, wo_ref, gpost_ref, o_ref):
    tm = x_ref.shape[1]
    parts = _row_parts(tm)
    xs = [x_ref[0, rows, :] for rows in parts]
    gate_in = [_dot_t(_premix(x, ada_ref, g_ref[...], rows, tm).astype(BF16), wg_ref[...])
               for x, rows in zip(xs, parts)]
    pas = [_dot(oa_ref[0, rows, :], wpa_ref[...]) for rows in parts]
    pbs = [_dot(ob_ref[0, rows, :], wpb_ref[...]) for rows in parts]
    merged = [(_sigmoid(gi[:, :D_MODEL]) * pa + _sigmoid(gi[:, D_MODEL:]) * pb).astype(BF16)
              for gi, pa, pb in zip(gate_in, pas, pbs)]
    mos = [_dot(m, wo_ref[...]) for m in merged]
    for rows, x, mo in zip(parts, xs, mos):
        o_ref[0, rows, :] = x + _mod(ada_ref, 2, rows, tm) * _rms(mo, gpost_ref[...])


def _merge(x, ada, o_a, o_b, wts, *, tm):
    b, t, _ = x.shape
    nseg = ada.shape[0] // b
    tok = lambda n: pl.BlockSpec((1, tm, n), lambda i, j: (i, j, 0))
    return pl.pallas_call(
        _merge_kernel,
        grid=(b, t // tm),
        in_specs=[tok(D_MODEL),
                  pl.BlockSpec((nseg, 6, D_MODEL), lambda i, j: (i, 0, 0)),
                  _const_spec((1, D_MODEL)),
                  _fixed_spec((_N_GATE, D_MODEL), (_GATE_BLOCK, 0)),
                  tok(H_A * DV), tok(H_B * D_HB),
                  _const_spec((H_A * DV, D_MODEL)), _const_spec((H_B * D_HB, D_MODEL)),
                  _const_spec((D_MODEL, D_MODEL)),
                  _const_spec((1, D_MODEL))],
        out_specs=tok(D_MODEL),
        out_shape=jax.ShapeDtypeStruct((b, t, D_MODEL), F32),
        compiler_params=_params(2),
        name="merge",
    )(x, ada, wts["g_pre_mix"], wts["w_in_all"], o_a, o_b, wts["w_proj_a"], wts["w_proj_b"],
      wts["w_out"], wts["g_post_mix"])


FF_CHUNK = MXU_TILE
SUB_GROUPS = 4
SUB_ROWS = SUB_GROUPS * SUBLANES
HALO = 2 * SUBLANES


def _gelu_tanh(a):
    return 0.5 * a * (1.0 + jnp.tanh(math.sqrt(2.0 / math.pi) * (a + 0.044715 * (a * a * a))))


def _ffn_kernel(x_ref, ada_ref, g_ref, wup_ref, cw_ref, cb_ref, wdn_ref, gpost_ref, cs_ref,
                o_ref, nc_ref, halo_ref, perm_ref, *, tm, ntile):
    ti = pl.program_id(1)
    nt = pl.num_programs(1)
    segmented = cs_ref.shape[0] > 1
    assert not segmented or (ntile == 1 and cs_ref.shape[0] == tm // SUB_ROWS)

    if not segmented:
        @pl.when(ti == 0)
        def _():
            halo_ref[...] = jnp.zeros((HALO, 2 * D_FF), F32)
            halo_ref[SUBLANES - 1:SUBLANES, :] = cs_ref[0, 0:1, :]
            halo_ref[HALO - 1:HALO, :] = cs_ref[0, 1:2, :]

    first = lax.broadcasted_iota(jnp.int32, (SUBLANES, FF_CHUNK), 0) == 0
    for tix in range(ntile):
        _ffn_tile(x_ref, ada_ref, g_ref, wup_ref, cw_ref, cb_ref, wdn_ref, gpost_ref, cs_ref, o_ref, nc_ref,
                  halo_ref, perm_ref.at[tix], slice(tix * tm, (tix + 1) * tm), first, segmented)

    if not segmented:
        @pl.when(ti == nt - 1)
        def _():
            nc_ref[0, 0:1, :] = halo_ref[SUBLANES - 1:SUBLANES, :]
            nc_ref[0, 1:2, :] = halo_ref[HALO - 1:HALO, :]


def _ffn_tile(x_ref, ada_ref, g_ref, wup_ref, cw_ref, cb_ref, wdn_ref, gpost_ref, cs_ref, o_ref, nc_ref,
              halo_ref, perm_ref, tile_rows, first, segmented):
    tm = tile_rows.stop - tile_rows.start
    nsub = tm // SUB_ROWS
    nc = D_MODEL // LANES
    all_rows = slice(0, tm)
    for c in range(nc):
        perm_ref[c] = x_ref[0, tile_rows, c * LANES:(c + 1) * LANES]
    x = jnp.concatenate(
        [jnp.concatenate([perm_ref[c, pl.ds(j * SUB_ROWS + i, SUBLANES, stride=SUB_GROUPS), :] for c in range(nc)],
                         axis=1)
         for j in range(nsub) for i in range(SUB_GROUPS)], axis=0)
    h2 = (_rms(x, g_ref[...]) * (1.0 + _mod(ada_ref, 4, all_rows, tm)) + _mod(ada_ref, 3, all_rows, tm)).astype(BF16)

    def up(col):
        cols = slice(col, col + FF_CHUNK)
        u = _dot(h2, wup_ref[:, cols])
        grp = [u[g * SUBLANES:(g + 1) * SUBLANES] for g in range(tm // SUBLANES)]
        own2 = [pltpu.roll(grp[j * SUB_GROUPS + SUB_GROUPS - 2], 1, axis=0) for j in range(nsub)]
        own3 = [pltpu.roll(grp[j * SUB_GROUPS + SUB_GROUPS - 1], 1, axis=0) for j in range(nsub)]
        if segmented:
            pred2 = [jnp.broadcast_to(cs_ref[j, 0:1, cols], (SUBLANES, FF_CHUNK)) for j in range(nsub)]
            pred3 = [jnp.broadcast_to(cs_ref[j, 1:2, cols], (SUBLANES, FF_CHUNK)) for j in range(nsub)]
            for j in range(nsub):
                last = (j + 1) * SUB_ROWS
                nc_ref[j, 0:1, cols] = u[last - SUBLANES - 1:last - SUBLANES]
                nc_ref[j, 1:2, cols] = u[last - 1:last]
        else:
            pred2 = [pltpu.roll(halo_ref[0:SUBLANES, cols], 1, axis=0)] + own2[:-1]
            pred3 = [pltpu.roll(halo_ref[SUBLANES:HALO, cols], 1, axis=0)] + own3[:-1]
            halo_ref[:, cols] = u[tm - HALO:]
        u1, u2 = [], []
        for j in range(nsub):
            f1 = jnp.where(first, pred3[j], own3[j])
            f2 = jnp.where(first, pred2[j], own2[j])
            g0 = j * SUB_GROUPS
            u1 += [f1] + grp[g0:g0 + SUB_GROUPS - 1]
            u2 += [f2, f1] + grp[g0:g0 + SUB_GROUPS - 2]
        u1 = jnp.concatenate(u1, axis=0)
        u2 = jnp.concatenate(u2, axis=0)
        return (cb_ref[:, cols] + cw_ref[0:1, cols] * u2 + cw_ref[1:2, cols] * u1 + cw_ref[2:3, cols] * u)

    nchunk = D_FF // FF_CHUNK
    acc = jnp.zeros((tm, D_MODEL), F32)
    ya, yb = up(0), up(D_FF)
    for c in range(nchunk):
        if c + 1 < nchunk:
            ya_next, yb_next = up((c + 1) * FF_CHUNK), up(D_FF + (c + 1) * FF_CHUNK)
        g = (_gelu_tanh(ya) * yb).astype(BF16)
        acc = acc + _dot(g, wdn_ref[c * FF_CHUNK:(c + 1) * FF_CHUNK, :])
        if c + 1 < nchunk:
            ya, yb = ya_next, yb_next
    y = x + _mod(ada_ref, 5, all_rows, tm) * _rms(acc, gpost_ref[...])
    for j in range(nsub):
        for i in range(SUB_GROUPS):
            g0 = (j * SUB_GROUPS + i) * SUBLANES
            for c in range(nc):
                perm_ref[c, pl.ds(j * SUB_ROWS + i, SUBLANES, stride=SUB_GROUPS), :] = (
                    y[g0:g0 + SUBLANES, c * LANES:(c + 1) * LANES])
    for c in range(nc):
        o_ref[0, tile_rows, c * LANES:(c + 1) * LANES] = perm_ref[c]


def _ffn(x, ada, conv_state, wts, *, tm):
    b, t, _ = x.shape
    nseg = ada.shape[0] // b
    assert tm % SUB_ROWS == 0 and t % tm == 0 and (nseg == 1 or (t == tm and t == nseg * SUB_ROWS))
    ntile = 2 if nseg == 1 and t % (2 * tm) == 0 else 1
    tok = pl.BlockSpec((1, ntile * tm, D_MODEL), lambda i, j: (i, j, 0))
    state = pl.BlockSpec((nseg, CONV_W - 1, 2 * D_FF), lambda i, j: (i, 0, 0))
    return pl.pallas_call(
        functools.partial(_ffn_kernel, tm=tm, ntile=ntile),
        grid=(b, t // (ntile * tm)),
        in_specs=[tok,
                  pl.BlockSpec((nseg, 6, D_MODEL), lambda i, j: (i, 0, 0)),
                  _const_spec((1, D_MODEL)),
                  _const_spec((D_MODEL, 2 * D_FF)),
                  _const_spec((CONV_W, 2 * D_FF)),
                  _const_spec((1, 2 * D_FF)),
                  _const_spec((D_FF, D_MODEL)),
                  _const_spec((1, D_MODEL)),
                  state],
        out_specs=[tok, state],
        out_shape=[jax.ShapeDtypeStruct((b, t, D_MODEL), F32),
                   jax.ShapeDtypeStruct((b * nseg, CONV_W - 1, 2 * D_FF), F32)],
        scratch_shapes=[pltpu.VMEM((HALO, 2 * D_FF), F32),
                        pltpu.VMEM((ntile, D_MODEL // LANES, tm, LANES), F32)],
        compiler_params=_params(1, 1),
        name="conv_ffn",
    )(x, ada, wts["g_pre_ffn"], wts["w_up"], wts["conv_w"], wts["conv_b"], wts["w_down"],
      wts["g_post_ffn"], conv_state)


def _rope_tables(first_pos, t):
    inv = ROPE_BASE ** (-np.arange(0, ROPE, 2, dtype=np.float64) / ROPE)
    ang = np.arange(first_pos, first_pos + t, dtype=np.float64)[:, None] * inv[None, :]
    cos, sin = jnp.asarray(np.cos(ang), F32), jnp.asarray(np.sin(ang), F32)
    cos_t = jnp.concatenate([cos, cos, jnp.ones((t, NOPE), F32), jnp.zeros((t, HEAD_PAD - ROPE - NOPE), F32)], axis=1)
    sin_t = jnp.concatenate([-sin, sin, jnp.zeros((t, HEAD_PAD - ROPE), F32)], axis=1)
    return cos_t, sin_t


def _swap_halves(w):
    return jnp.concatenate([w[..., ROPE // 2:], w[..., :ROPE // 2]], axis=-1)


def _pad_lanes(w, n):
    return jnp.pad(w, [(0, 0)] * (w.ndim - 1) + [(0, n - w.shape[-1])])


def _layer_weights(l, g_pre_mix, g_post_mix, g_pre_ffn, g_post_ffn, w_in, g_q_lat, w_uq, g_kv_lat,
                   w_uk, w_uv, w_proj_a, w_proj_b, w_out, w_up, conv_w, conv_b, w_down):
    wt = jnp.transpose(w_in[l])
    c_kr = Q_RANK + KV_RANK
    c_qb = c_kr + ROPE
    c_gate = c_qb + 3 * H_B * D_HB
    kr_pad = jnp.zeros((LANES - ROPE, D_MODEL), F32)
    w_in_all = jnp.concatenate([wt[:c_kr], wt[c_kr:c_qb], kr_pad,
                                wt[c_kr + ROPE // 2:c_qb], wt[c_kr:c_kr + ROPE // 2], kr_pad,
                                wt[c_qb:c_gate], jnp.zeros((_GATE_BLOCK * _N_GATE - _N_K1, D_MODEL), F32),
                                wt[c_gate:]], axis=0).astype(BF16)
    wq = w_uq[l].reshape(Q_RANK, H_A, NOPE + ROPE)
    wq_nope, wq_rope = wq[..., :NOPE], wq[..., NOPE:]
    zeros = jnp.zeros((Q_RANK, H_A, HEAD_PAD - ROPE - NOPE), F32)
    q_main = jnp.concatenate([wq_rope, wq_nope, zeros], axis=-1).reshape(Q_RANK, H_A * HEAD_PAD)
    q_rot = _pad_lanes(_swap_halves(wq_rope), HEAD_PAD).reshape(Q_RANK, H_A * HEAD_PAD)
    wk = w_uk[l].reshape(KV_RANK, H_A, NOPE)
    w_uk_pad = jnp.pad(wk, [(0, 0), (0, 0), (ROPE, HEAD_PAD - ROPE - NOPE)])
    w_abs = jnp.transpose(w_uk_pad, (1, 2, 0))
    w_uv_t = _pad_lanes(w_uv[l].reshape(KV_RANK, H_A, DV), VT_ROWS)
    v_ones = np.zeros((H_A, VT_ROWS), np.float32)
    v_ones[:, DV] = 1.0
    head_of_col = jnp.arange(H_A * DV) // DV
    w_uv_heads = jnp.where(head_of_col[None, None, :] == jnp.arange(H_A)[:, None, None], w_uv[l][None], 0.0)
    row = lambda g: g[l].reshape(1, -1)
    return {
        "g_pre_mix": row(g_pre_mix), "g_post_mix": row(g_post_mix),
        "g_pre_ffn": row(g_pre_ffn), "g_post_ffn": row(g_post_ffn),
        "g_q_lat": row(g_q_lat), "g_kv_lat": row(g_kv_lat),
        "w_in_all": w_in_all,
        "w_uq_ext": jnp.concatenate([q_main, q_rot], axis=1).astype(BF16),
        "w_uk_pad": w_uk_pad.reshape(KV_RANK, H_A * HEAD_PAD).astype(BF16),
        "w_abs": w_abs.astype(BF16),
        "w_uv_t": w_uv_t.reshape(KV_RANK, H_A * VT_ROWS).astype(BF16),
        "v_ones": jnp.asarray(v_ones.reshape(1, H_A * VT_ROWS)),
        "w_uv_heads": w_uv_heads.astype(BF16),
        "w_proj_a": w_proj_a[l].astype(BF16), "w_proj_b": w_proj_b[l].astype(BF16),
        "w_out": w_out[l].astype(BF16),
        "w_up": w_up[l].astype(BF16), "conv_w": conv_w[l], "conv_b": conv_b[l].reshape(1, -1),
        "w_down": w_down[l].astype(BF16),
    }


def _pick_tile(t, want):
    tm = min(t, want)
    assert t % tm == 0
    return tm


def _layer(x, ada, first_pos, past, wts):
    b, t, _ = x.shape
    cos_t, sin_t = _rope_tables(first_pos, t)
    tm = _pick_tile(t, TOKEN_TILE)
    if past is None:
        q, ckv, kr_t, qb, kb, vb, k_mla, vt_mla = _mixer_inputs(x, ada, cos_t, sin_t, wts, tm=tm, expand_kv=True)
        kr = jnp.transpose(kr_t, (0, 2, 1))
        o_a = _mla_prompt(q, k_mla, vt_mla, blk=_pick_tile(t, MLA_BLOCK))
        tq = _pick_tile(t, SB_BLOCK)
        o_b = _sb_prompt(qb, kb, vb, tq=tq, nsub=max(n for n in SB_SUBS if t % (tq * n) == 0))
        conv_state = jnp.zeros((b, CONV_W - 1, 2 * D_FF), F32)
    else:
        past_ckv, past_kr, past_k, past_v, conv_state = past
        past_len = past_ckv.shape[1]
        flat = t == SUB_ROWS and (b * t) % FFN_TILE == 0 and b * t <= TOKEN_TILE
        if flat:
            x = x.reshape(1, b * t, D_MODEL)
            cos_t, sin_t = jnp.tile(cos_t, (b, 1)), jnp.tile(sin_t, (b, 1))
            tm = b * t
        per_batch = lambda a: a.reshape(b, t, a.shape[-1])
        q, ckv, kr, qb, kb, vb = map(per_batch, _mixer_inputs(x, ada, cos_t, sin_t, wts, tm=tm, expand_kv=False))
        o_a = _mla_sample(q, ckv, kr, past_ckv, past_kr, wts["w_abs"], wts["w_uv_heads"],
                          past_len=past_len, kc=_pick_tile(past_len, MLA_DECODE_KEYS))
        o_b = _sb_sample(qb, kb, vb, past_k, past_v, tk=_pick_tile(past_len, SB_BLOCK),
                         recent=_pick_tile(past_len, SB_RECENT))
        o_a, o_b = (o.reshape(x.shape[0], x.shape[1], o.shape[-1]) for o in (o_a, o_b))
    x1 = _merge(x, ada, o_a, o_b, wts, tm=tm)
    y, new_conv = _ffn(x1, ada, conv_state, wts, tm=_pick_tile(x.shape[1], FFN_TILE))
    state = (ckv, kr, kb.reshape(b, t, H_B, D_HB), vb.reshape(b, t, H_B, D_HB), new_conv)
    return y.reshape(b, t, D_MODEL), state


def kernel(x_prompt, x_sample, cache_mla_ckv, cache_mla_krope, cache_sb_k, cache_sb_v, state_ffn_conv,
           c_prompt, c_sample, w_ada, b_ada, g_pre_mix, g_post_mix, g_pre_ffn, g_post_ffn,
           w_in, g_q_lat, w_uq, g_kv_lat, w_uk, w_uv, w_proj_a, w_proj_b, w_out,
           w_up, conv_w, conv_b, w_down):
    depth = w_in.shape[0]
    nb_p = x_prompt.shape[0]
    past_len = cache_mla_ckv.shape[2]
    xp, xs = x_prompt, x_sample
    c_all = jnp.concatenate([c_prompt, c_sample], axis=0)
    st_p = [[] for _ in range(5)]
    st_s = [[] for _ in range(5)]
    for l in range(depth):
        wts = _layer_weights(l, g_pre_mix, g_post_mix, g_pre_ffn, g_post_ffn, w_in, g_q_lat, w_uq, g_kv_lat,
                             w_uk, w_uv, w_proj_a, w_proj_b, w_out, w_up, conv_w, conv_b, w_down)
        ada = _ada(c_all, w_ada[l], b_ada[l]).reshape(-1, 6, D_MODEL)
        xp, sp = _layer(xp, ada[:nb_p], 0, None, wts)
        past = (cache_mla_ckv[l], cache_mla_krope[l], cache_sb_k[l], cache_sb_v[l], state_ffn_conv[l])
        xs, ss = _layer(xs, ada[nb_p:], past_len, past, wts)
        for i in range(5):
            st_p[i].append(sp[i])
            st_s[i].append(ss[i])
    p_state = [jnp.stack(a, axis=0) for a in st_p]
    s_state = [jnp.stack(a, axis=0) for a in st_s]
    return (xp, xs, *p_state, *s_state)
```

```python
import functools
import math

import numpy as np
import jax
import jax.numpy as jnp
from jax import lax
from jax.experimental import pallas as pl
from jax.experimental.pallas import tpu as pltpu

D_MODEL = 1024
CHUNK = 64
CHUNK_SHIFT = 6
H_A = 8
NOPE = 64
ROPE = 32
DV = 64
Q_RANK = 384
KV_RANK = 256
ROPE_BASE = 10000.0
H_B = 8
D_HB = 64
D_FF = 2816
CONV_W = 3
EPS = 1e-6
NEG = -1e30
MLA_SCALE = (NOPE + ROPE) ** -0.5
SB_SCALE = D_HB ** -0.5
LOG2E = math.log2(math.e)

LANES = 128
SUBLANES = 8
HEAD_PAD = LANES
VT_ROWS = 80
SB_DEAD = 110.0
SB_AHEAD = 2
VMEM_LIMIT = 56 * 1024 * 1024
MXU_TILE = 256
TOKEN_TILE = 2 * MXU_TILE
FFN_TILE = MXU_TILE
MLA_BLOCK = 2 * MXU_TILE
SB_BLOCK = MXU_TILE
SB_SUBS = (1, 2, 4)
SB_RECENT = 2 * MXU_TILE
MLA_DECODE_KEYS = 4 * MXU_TILE

F32 = jnp.float32
BF16 = jnp.bfloat16


def _dot(a, b):
    return jnp.dot(a, b, preferred_element_type=F32)


def _dot_t(a, b):
    return lax.dot_general(a, b, (((1,), (1,)), ((), ())), preferred_element_type=F32)


def _rms(x, g):
    return x * lax.rsqrt(jnp.mean(x * x, axis=-1, keepdims=True) + EPS) * g


def _sigmoid(x):
    return 1.0 / (1.0 + jnp.exp(-x))


def _params(n_parallel, n_arbitrary=0):
    return pltpu.CompilerParams(
        dimension_semantics=("parallel",) * n_parallel + ("arbitrary",) * n_arbitrary,
        vmem_limit_bytes=VMEM_LIMIT)


def _const_spec(shape):
    return _fixed_spec(shape, (0,) * len(shape))


def _fixed_spec(shape, index):
    return pl.BlockSpec(shape, lambda *_: index, pipeline_mode=pl.Buffered(1))


def _ada_kernel(c_ref, w_ref, b_ref, o_ref):
    c = c_ref[...]
    s = (c * _sigmoid(c)).astype(BF16)
    o_ref[...] = _dot(s, w_ref[...].astype(BF16)) + b_ref[...]


def _ada(c_all, w_ada, b_ada):
    n = c_all.shape[0]
    nchunk = 6
    return pl.pallas_call(
        _ada_kernel,
        grid=(nchunk,),
        in_specs=[pl.BlockSpec((n, D_MODEL), lambda j: (0, 0)),
                  pl.BlockSpec((D_MODEL, D_MODEL), lambda j: (0, j)),
                  pl.BlockSpec((1, D_MODEL), lambda j: (0, j))],
        out_specs=pl.BlockSpec((n, D_MODEL), lambda j: (0, j)),
        out_shape=jax.ShapeDtypeStruct((n, 6 * D_MODEL), F32),
        compiler_params=_params(1),
        name="ada",
    )(c_all, w_ada, b_ada.reshape(1, -1))


_C_QLAT = 0
_C_CKV = _C_QLAT + Q_RANK
_C_KR = _C_CKV + KV_RANK
_C_KRR = _C_KR + LANES
_C_QB = _C_KRR + LANES
_C_KB = _C_QB + H_B * D_HB
_C_VB = _C_KB + H_B * D_HB
_N_K1 = _C_VB + H_B * D_HB
_N_GATE = 2 * D_MODEL
_GATE_BLOCK = -(-_N_K1 // _N_GATE)
_N_IN_ALL = (_GATE_BLOCK + 1) * _N_GATE


def _mod(ada_ref, comp, rows, tm):
    nseg = ada_ref.shape[0]
    if nseg == 1:
        return ada_ref[0, comp:comp + 1, :]
    seg = tm // nseg
    lo, hi = rows.start // seg, rows.stop // seg
    assert rows.start % seg == 0 and rows.stop % seg == 0
    return jnp.concatenate([jnp.broadcast_to(ada_ref[i, comp:comp + 1, :], (seg, ada_ref.shape[2]))
                            for i in range(lo, hi)], axis=0)


def _premix(x, ada_ref, g, rows, tm):
    return _rms(x, g) * (1.0 + _mod(ada_ref, 1, rows, tm)) + _mod(ada_ref, 0, rows, tm)


def _row_parts(tm):
    n = 2 if tm % (2 * LANES) == 0 else 1
    return [slice(i * (tm // n), (i + 1) * (tm // n)) for i in range(n)]


def _mixer_kernel(x_ref, ada_ref, g_ref, win_ref, gq_ref, wuq_ref, gkv_ref, wuk_ref, wuv_ref, vone_ref,
                  cos_ref, sin_ref, *out_refs, expand_kv):
    if expand_kv:
        q_ref, ckv_ref, kr_ref, qb_ref, kb_ref, vb_ref, k_ref, vt_ref = out_refs
    else:
        q_ref, ckv_ref, kr_ref, qb_ref, kb_ref, vb_ref = out_refs
    tm = x_ref.shape[1]
    parts = _row_parts(tm)
    nq = H_A * HEAD_PAD
    ps = [_dot_t(_premix(x_ref[0, rows, :], ada_ref, g_ref[...], rows, tm).astype(BF16), win_ref[...])
          for rows in parts]
    q_lats = [_rms(p[:, _C_QLAT:_C_QLAT + Q_RANK], gq_ref[...]).astype(BF16) for p in ps]
    c_kvs = [_rms(p[:, _C_CKV:_C_CKV + KV_RANK], gkv_ref[...]) for p in ps]
    q2s = [_dot(q_lat, wuq_ref[...]) for q_lat in q_lats]
    if expand_kv:
        c_bfs = [c_kv.astype(BF16) for c_kv in c_kvs]
        k2s = [_dot(c_bf, wuk_ref[...]) for c_bf in c_bfs]
        v2s = [_dot(c_bf, wuv_ref[...]) for c_bf in c_bfs]
    for i, rows in enumerate(parts):
        p, q2 = ps[i], q2s[i]
        cos = cos_ref[rows, :]
        sin = sin_ref[rows, :]
        for hd in range(H_A):
            lo = hd * HEAD_PAD
            qh = q2[:, lo:lo + HEAD_PAD] * cos + q2[:, nq + lo:nq + lo + HEAD_PAD] * sin
            q_ref[0, rows, lo:lo + HEAD_PAD] = (qh * (MLA_SCALE * LOG2E)).astype(BF16)
        ckv_ref[0, rows, :] = c_kvs[i]
        kr = p[:, _C_KR:_C_KR + LANES] * cos + p[:, _C_KRR:_C_KRR + LANES] * sin
        if expand_kv:
            kr_ref[0, :, rows] = kr.T[:ROPE]
        else:
            kr_ref[0, rows, :] = kr[:, :ROPE]
        qb_ref[0, rows, :] = (p[:, _C_QB:_C_QB + H_B * D_HB] * SB_SCALE).astype(BF16)
        kb_ref[0, rows, :] = p[:, _C_KB:_C_KB + H_B * D_HB]
        vb_ref[0, rows, :] = p[:, _C_VB:_C_VB + H_B * D_HB]
        if expand_kv:
            for hd in range(H_A):
                lo = hd * HEAD_PAD
                k_ref[0, rows, lo:lo + HEAD_PAD] = (k2s[i][:, lo:lo + HEAD_PAD] + kr).astype(BF16)
            vt_ref[0, :, rows] = (v2s[i] + vone_ref[...]).T.astype(BF16)


def _mixer_inputs(x, ada, cos_t, sin_t, wts, *, tm, expand_kv):
    b, t, _ = x.shape
    nt = t // tm
    nseg = ada.shape[0] // b
    tok = lambda n: pl.BlockSpec((1, tm, n), lambda i, j: (i, j, 0))
    in_specs = [tok(D_MODEL),
                pl.BlockSpec((nseg, 6, D_MODEL), lambda i, j: (i, 0, 0)),
                _const_spec((1, D_MODEL)),
                _const_spec((_N_K1, D_MODEL)),
                _const_spec((1, Q_RANK)),
                _const_spec((Q_RANK, 2 * H_A * HEAD_PAD)),
                _const_spec((1, KV_RANK)),
                _const_spec((KV_RANK, H_A * HEAD_PAD)),
                _const_spec((KV_RANK, H_A * VT_ROWS)),
                _const_spec((1, H_A * VT_ROWS)),
                pl.BlockSpec((tm, LANES), lambda i, j: (j, 0)),
                pl.BlockSpec((tm, LANES), lambda i, j: (j, 0))]
    shapes = [((b, t, H_A * HEAD_PAD), BF16), ((b, t, KV_RANK), F32), ((b, t, ROPE), F32),
              ((b, t, H_B * D_HB), BF16), ((b, t, H_B * D_HB), F32), ((b, t, H_B * D_HB), F32)]
    if expand_kv:
        shapes += [((b, t, H_A * HEAD_PAD), BF16)]
    out_specs = [tok(s[-1]) for s, _ in shapes]
    if expand_kv:
        t_minor = lambda n: pl.BlockSpec((1, n, tm), lambda i, j: (i, 0, j))
        shapes[2], out_specs[2] = ((b, ROPE, t), F32), t_minor(ROPE)
        shapes += [((b, H_A * VT_ROWS, t), BF16)]
        out_specs += [t_minor(H_A * VT_ROWS)]
    return pl.pallas_call(
        functools.partial(_mixer_kernel, expand_kv=expand_kv),
        grid=(b, nt),
        in_specs=in_specs,
        out_specs=out_specs,
        out_shape=[jax.ShapeDtypeStruct(s, d) for s, d in shapes],
        compiler_params=_params(2),
        name="mixer_in_kv" if expand_kv else "mixer_in",
    )(x, ada, wts["g_pre_mix"], wts["w_in_all"], wts["g_q_lat"], wts["w_uq_ext"], wts["g_kv_lat"],
      wts["w_uk_pad"], wts["w_uv_t"], wts["v_ones"], cos_t, sin_t)


MLA_PAIR = 2


MLA_TILES = 2


def _mla_prompt_kernel(q_ref, k_ref, vt_ref, bias_ref, o_ref, sa_ref, sb_ref, m_ref, acc_ref, *, blk):
    tq = MLA_PAIR * blk
    lanes = [slice(hh * HEAD_PAD, (hh + 1) * HEAD_PAD) for hh in range(2)]

    def start_of(j):
        return pl.multiple_of(j * blk, blk)

    def scores(s_ref, rows, g):
        for hh, ln in enumerate(lanes):
            for b in range(MLA_PAIR):
                k = k_ref[0, pl.ds(start_of(MLA_PAIR * g + b), blk), ln]
                s_ref[hh, b] = _dot_t(k, q_ref[0, rows, ln])

    def softmax_pv(s_ref, g, masked):
        for hh in range(2):
            ss = []
            for b in range(MLA_PAIR):
                s = s_ref[hh, b]
                ss.append(s + bias_ref[b] if masked else s)
            m_old = m_ref[hh]
            m_new = m_old
            for s in ss:
                m_new = jnp.maximum(m_new, jnp.max(s, axis=0, keepdims=True))
            acc = jnp.exp2(m_old - m_new) * acc_ref[hh]
            for b, s in enumerate(ss):
                p = jnp.exp2(s - m_new).astype(BF16)
                vt = vt_ref[0, hh * VT_ROWS:(hh + 1) * VT_ROWS, pl.ds(start_of(MLA_PAIR * g + b), blk)]
                acc = acc + _dot(vt, p)
            acc_ref[hh] = acc
            m_ref[hh] = m_new

    def finish(rows):
        outs = [acc_ref[hh][:DV] / acc_ref[hh][DV:DV + 1] for hh in range(2)]
        o_ref[0, rows, :] = jnp.concatenate(outs, axis=0).T.astype(BF16)

    def tile(n, rows, even_ref, odd_ref, n_is_even, first_scores_done):
        if not first_scores_done:
            scores(even_ref, rows, 0)
        m_ref[...] = jnp.full(m_ref.shape, NEG, F32)
        acc_ref[...] = jnp.zeros(acc_ref.shape, F32)

        def body(h, c):
            scores(odd_ref, rows, 2 * h + 1)
            softmax_pv(even_ref, 2 * h, False)
            scores(even_ref, rows, 2 * h + 2)
            softmax_pv(odd_ref, 2 * h + 1, False)
            return c

        lax.fori_loop(0, n // 2, body, 0)
        if not n_is_even:
            scores(odd_ref, rows, n)
            softmax_pv(even_ref, n - 1, False)
        return odd_ref if not n_is_even else even_ref

    step = pl.program_id(2)
    n0 = MLA_TILES * step
    rows0, rows1 = slice(0, tq), slice(tq, 2 * tq)
    diag0 = tile(n0, rows0, sa_ref, sb_ref, True, False)
    scores(sb_ref, rows1, 0)
    softmax_pv(diag0, n0, True)
    finish(rows0)
    diag1 = tile(n0 + 1, rows1, sb_ref, sa_ref, False, True)
    softmax_pv(diag1, n0 + 1, True)
    finish(rows1)


def _mla_diag_bias(blk, tq):
    kpos = np.arange(MLA_PAIR * blk).reshape(MLA_PAIR, blk, 1)
    qpos = np.arange(tq).reshape(1, 1, tq)
    return np.where(kpos // CHUNK <= qpos // CHUNK, 0.0, NEG).astype(np.float32)


def _mla_prompt(q, k, vt, *, blk):
    b, t, _ = q.shape
    tq = MLA_PAIR * blk
    rows = MLA_TILES * tq
    assert t % rows == 0
    return pl.pallas_call(
        functools.partial(_mla_prompt_kernel, blk=blk),
        grid=(b, H_A // 2, t // rows),
        in_specs=[pl.BlockSpec((1, rows, 2 * HEAD_PAD), lambda i, j, n: (i, n, j)),
                  pl.BlockSpec((1, t, 2 * HEAD_PAD), lambda i, j, n: (i, 0, j)),
                  pl.BlockSpec((1, 2 * VT_ROWS, t), lambda i, j, n: (i, j, 0)),
                  _const_spec((MLA_PAIR, blk, tq))],
        out_specs=pl.BlockSpec((1, rows, 2 * DV), lambda i, j, n: (i, n, j)),
        out_shape=jax.ShapeDtypeStruct((b, t, H_A * DV), BF16),
        scratch_shapes=[pltpu.VMEM((2, MLA_PAIR, blk, tq), F32), pltpu.VMEM((2, MLA_PAIR, blk, tq), F32),
                        pltpu.VMEM((2, 1, tq), F32), pltpu.VMEM((2, VT_ROWS, tq), F32)],
        compiler_params=_params(3),
        name="mla_prompt",
    )(q, k, vt, jnp.asarray(_mla_diag_bias(blk, tq)))


def _tri(n):
    r = lax.broadcasted_iota(jnp.int32, (n, n), 0)
    c = lax.broadcasted_iota(jnp.int32, (n, n), 1)
    return jnp.where(r > c, 1.0, 0.0).astype(BF16)


def _sb_chains(chains, r_init):
    n = len(chains)
    z, sp, sp_bf, later, a = ([None] * n for _ in range(5))
    r_before = [None] * n
    r = dict(r_init)
    out = {}

    def scores(c):
        _, qs, k, _, _, _ = chains[c]
        z[c] = k(qs) if callable(k) else _dot_t(qs, k)

    def softplus(c):
        g, _, _, _, _, keep = chains[c]
        zb = z[c].astype(BF16)
        sp_bf[c] = jnp.maximum(zb, 0.0) + jnp.log(1.0 + jnp.exp(-jnp.abs(zb)))
        sp[c] = sp_bf[c].astype(F32)
        if keep is not None:
            sp[c] = jnp.where(keep, sp[c], 0.0)
            sp_bf[c] = sp[c].astype(BF16)
        rs = jnp.sum(sp[c], axis=-1, keepdims=True)
        r_before[c] = r[g]
        r[g] = rs if r[g] is None else r[g] + rs

    def suffix(c):
        later[c] = _dot(sp_bf[c], chains[c][4])

    def weights(c):
        _, _, _, _, tri, keep = chains[c]
        tk = tri.shape[0]
        w = z[c] - sp[c] - later[c]
        if r_before[c] is not None:
            r_b = r_before[c]
            if r_b.shape[1] == LANES and tk > LANES:
                r_b = jnp.concatenate([r_b] * (tk // LANES), axis=1)
            elif r_b.shape[1] == LANES and tk < LANES:
                r_b = r_b[:, :tk]
            w = w - r_b
        w = jnp.exp2(w * LOG2E)
        a[c] = (w if keep is None else jnp.where(keep, w, 0.0)).astype(BF16)

    def values(c):
        g, _, _, v, _, _ = chains[c]
        o = v(a[c]) if callable(v) else _dot(a[c], v)
        out[g] = o if g not in out else out[g] + o

    for stage in (scores, softplus, suffix, weights, values):
        for c in range(n):
            stage(c)
    return out, r


def _head_mask(x, hh):
    lane = lax.broadcasted_iota(jnp.int32, x.shape, 1)
    mine = jnp.logical_and(lane >= hh * D_HB, lane < (hh + 1) * D_HB)
    return jnp.where(mine, x, jnp.zeros_like(x))


def _stack_heads(q):
    return jnp.concatenate([_head_mask(q, 0), _head_mask(q, 1)], axis=0)


def _causal_pair(t, tk):
    row = lax.broadcasted_iota(jnp.int32, (t, tk), 0)
    col = lax.broadcasted_iota(jnp.int32, (t, tk), 1)
    keep = col < row
    return jnp.concatenate([keep, keep], axis=0)


def _sb_tail(qs, load, tri, acc_ref, r_ref, j_start, keep=None):
    def cond(c):
        j, rmin = c
        return jnp.logical_and(j >= 0, rmin < SB_DEAD)

    def body(c):
        j, _ = c
        o, r_new = _sb_chains([(0, qs) + load(j) + (tri, keep)], {0: r_ref[...]})
        acc_ref[...] += o[0]
        r_ref[...] = r_new[0]
        return j - 1, jnp.min(r_new[0])

    lax.while_loop(cond, body, (j_start, jnp.min(r_ref[...])))


def _unstack_heads(acc, t):
    lane = lax.broadcasted_iota(jnp.int32, (t, LANES), 1)
    return jnp.where(lane < D_HB, acc[:t], acc[t:])


def _sb_prompt_kernel(q_ref, k_ref, v_ref, o_ref, acc_ref, r_ref, *, tq, nsub):
    qi = pl.program_id(2)
    tri = _tri(tq)
    causal = _causal_pair(tq, tq)

    def load(j):
        start = pl.multiple_of(j * tq, tq)
        return (k_ref[0, pl.ds(start, tq), :].astype(BF16),
                v_ref[0, pl.ds(start, tq), :].astype(BF16))

    qss = [_stack_heads(q_ref[0, sub * tq:(sub + 1) * tq, :]) for sub in range(nsub)]
    chains = []
    for back in range(SB_AHEAD + 1):
        for sub in range(nsub):
            blk = qi * nsub + sub
            keep = causal if back == 0 else blk >= back
            chains.append((sub, qss[sub]) + load(jnp.maximum(blk - back, 0)) + (tri, keep))
    o, r = _sb_chains(chains, {sub: None for sub in range(nsub)})
    for sub in range(nsub):
        acc_ref[sub] = o[sub]
        r_ref[sub] = jnp.broadcast_to(r[sub], (2 * tq, LANES))
    for sub in range(nsub):
        _sb_tail(qss[sub], load, tri, acc_ref.at[sub], r_ref.at[sub], qi * nsub + sub - SB_AHEAD - 1)
        o_ref[0, sub * tq:(sub + 1) * tq, :] = _unstack_heads(acc_ref[sub], tq).astype(BF16)


def _sb_prompt(q, k, v, *, tq, nsub):
    b, t, _ = q.shape
    rows = tq * nsub
    return pl.pallas_call(
        functools.partial(_sb_prompt_kernel, tq=tq, nsub=nsub),
        grid=(b, H_B // 2, t // rows),
        in_specs=[pl.BlockSpec((1, rows, LANES), lambda i, j, n: (i, n, j)),
                  pl.BlockSpec((1, t, LANES), lambda i, j, n: (i, 0, j)),
                  pl.BlockSpec((1, t, LANES), lambda i, j, n: (i, 0, j))],
        out_specs=pl.BlockSpec((1, rows, LANES), lambda i, j, n: (i, n, j)),
        out_shape=jax.ShapeDtypeStruct((b, t, H_B * D_HB), BF16),
        scratch_shapes=[pltpu.VMEM((nsub, 2 * tq, LANES), F32), pltpu.VMEM((nsub, 2 * tq, LANES), F32)],
        compiler_params=_params(3),
        name="sb_prompt",
    )(q, k, v)


def _sb_sample_kernel(q_ref, kn_ref, vn_ref, kc_ref, vc_ref, o_ref, left_ref, acc_ref, r_ref, *, tk):
    tq = q_ref.shape[1]
    nblk = kc_ref.shape[1] // tk
    qs = _stack_heads(q_ref[0])
    tri = _tri(tk)

    def load(j):
        start = pl.multiple_of(j * tk, tk)
        return (kc_ref[0, pl.ds(start, tk), :].astype(BF16),
                vc_ref[0, pl.ds(start, tk), :].astype(BF16))

    o, r = _sb_chains([(0, qs, kn_ref[0].astype(BF16), vn_ref[0].astype(BF16), _tri(tq), _causal_pair(tq, tq)),
                       (0, qs) + load(nblk - 1) + (tri, None)], {0: None})
    acc_ref[...] = o[0]
    r_ref[...] = jnp.broadcast_to(r[0], (2 * tq, LANES))
    _sb_tail(qs, load, tri, acc_ref, r_ref, jnp.int32(nblk - 2))
    o_ref[0] = _unstack_heads(acc_ref[...], tq).astype(BF16)
    left_ref[...] = jnp.full(left_ref.shape, jnp.min(r_ref[...]), F32)


def _sb_sample_call(q, k_new, v_new, k_cache, v_cache, *, tk):
    b, t, _ = q.shape
    past = k_cache.shape[1]
    new = pl.BlockSpec((1, t, LANES), lambda i, j: (i, 0, j))
    old = pl.BlockSpec((1, past, LANES), lambda i, j: (i, 0, j))
    return pl.pallas_call(
        functools.partial(_sb_sample_kernel, tk=tk),
        grid=(b, H_B // 2),
        in_specs=[new, new, new, old, old],
        out_specs=[new, pl.BlockSpec((1, 1, SUBLANES, LANES), lambda i, j: (i, j, 0, 0))],
        out_shape=[jax.ShapeDtypeStruct((b, t, H_B * D_HB), BF16),
                   jax.ShapeDtypeStruct((b, H_B // 2, SUBLANES, LANES), F32)],
        scratch_shapes=[pltpu.VMEM((2 * t, LANES), F32), pltpu.VMEM((2 * t, LANES), F32)],
        compiler_params=_params(2),
        name="sb_sample",
    )(q, k_new, v_new, k_cache, v_cache)


SB_DECODE_KEYS = MXU_TILE
SB_DECODE_AHEAD = 1


def _sb_decode_kernel(q_ref, kn_ref, vn_ref, kct_ref, vct_ref, o_ref, left_ref, acc_ref, r_ref):
    t = q_ref.shape[1]
    rows = H_B * t
    ck = SB_DECODE_KEYS
    nstep = kct_ref.shape[3] // ck
    assert t & (t - 1) == 0
    head = lambda x, h: x[:, h * D_HB:(h + 1) * D_HB]
    q = q_ref[0]
    q3 = jnp.stack([head(q, h) for h in range(H_B)], axis=0)

    def cached(j):
        start = pl.multiple_of(j * ck, ck)
        kt = kct_ref[0, :, :, pl.ds(start, ck)].astype(BF16)
        vt = vct_ref[0, :, :, pl.ds(start, ck)].astype(BF16)

        def scores(_):
            z = lax.dot_general(q3, kt, (((2,), (1,)), ((0,), (0,))), preferred_element_type=F32)
            return z.reshape(rows, ck)

        def values(a):
            o = lax.dot_general(a.reshape(H_B, t, ck), vt, (((2,), (2,)), ((0,), (0,))),
                                preferred_element_type=F32)
            return o.reshape(rows, D_HB)

        return scores, values

    kn, vn = kn_ref[0].astype(BF16), vn_ref[0].astype(BF16)

    def new_scores(_):
        return jnp.concatenate([_dot_t(head(q, h), head(kn, h)) for h in range(H_B)], axis=0)

    def new_values(a):
        return jnp.concatenate([_dot(a[h * t:(h + 1) * t], head(vn, h)) for h in range(H_B)], axis=0)

    r_i = lax.broadcasted_iota(jnp.int32, (rows, t), 0)
    c_i = lax.broadcasted_iota(jnp.int32, (rows, t), 1)
    tri = _tri(ck)
    chains = [(0, None, new_scores, new_values, _tri(t), c_i < (r_i & (t - 1)))]
    ahead = min(SB_DECODE_AHEAD, nstep)
    for back in range(1, ahead + 1):
        chains.append((0, None) + cached(nstep - back) + (tri, None))
    o, r = _sb_chains(chains, {0: None})
    acc_ref[...] = o[0]
    r_ref[...] = jnp.broadcast_to(r[0], (rows, LANES))
    _sb_tail(None, cached, tri, acc_ref, r_ref, jnp.int32(nstep - ahead - 1))
    acc = acc_ref[...]
    o_ref[0] = jnp.concatenate([acc[h * t:(h + 1) * t] for h in range(H_B)], axis=1).astype(BF16)
    left_ref[...] = jnp.full(left_ref.shape, jnp.min(r_ref[...]), F32)


def _sb_decode_call(q, k_new, v_new, k_cache_t, v_cache_t, *, recent):
    b, t, _ = q.shape
    past = k_cache_t.shape[3]
    new = pl.BlockSpec((1, t, H_B * D_HB), lambda i: (i, 0, 0))
    old = pl.BlockSpec((1, H_B, D_HB, recent), lambda i: (i, 0, 0, past // recent - 1))
    return pl.pallas_call(
        _sb_decode_kernel,
        grid=(b,),
        in_specs=[new, new, new, old, old],
        out_specs=[new, pl.BlockSpec((1, SUBLANES, LANES), lambda i: (i, 0, 0))],
        out_shape=[jax.ShapeDtypeStruct((b, t, H_B * D_HB), BF16),
                   jax.ShapeDtypeStruct((b, SUBLANES, LANES), F32)],
        scratch_shapes=[pltpu.VMEM((H_B * t, D_HB), F32), pltpu.VMEM((H_B * t, LANES), F32)],
        compiler_params=_params(1),
        name="sb_decode",
    )(q, k_new, v_new, k_cache_t, v_cache_t)


def _sb_sample(q, k_new, v_new, k_cache, v_cache, *, tk, recent):
    b, past = k_cache.shape[:2]
    t = q.shape[1]
    flat = lambda c: c.reshape(b, c.shape[1], H_B * D_HB)
    full = lambda: _sb_sample_call(q, k_new, v_new, flat(k_cache), flat(v_cache), tk=tk)[0]
    if not (t & (t - 1) == 0 and past % recent == 0 and recent % SB_DECODE_KEYS == 0):
        return full()
    key_minor = lambda c: jnp.transpose(c, (0, 2, 3, 1))
    o, left = _sb_decode_call(q, k_new, v_new, key_minor(k_cache), key_minor(v_cache), recent=recent)
    if recent == past:
        return o
    return lax.cond(jnp.min(left) < SB_DEAD, full, lambda: o)


def _mla_sample_kernel(q_ref, cn_ref, rn_ref, cc_ref, rct_ref, wabs_ref, wuv_ref, o_ref, *, kc, past_len):
    parked = [_mla_sample_scores(bi, q_ref, cn_ref, rn_ref, cc_ref, rct_ref, wabs_ref, kc, past_len)
              for bi in range(q_ref.shape[0])]
    for bi, (scores, lat) in enumerate(parked):
        _mla_sample_finish(bi, scores, lat, wuv_ref, o_ref)


def _mla_sample_scores(bi, q_ref, cn_ref, rn_ref, cc_ref, rct_ref, wabs_ref, kc, past_len):
    t = q_ref.shape[1]
    past = cc_ref.shape[1]
    qa, qr = [], []
    for hd in range(H_A):
        qh = q_ref[bi, :, hd * HEAD_PAD:(hd + 1) * HEAD_PAD]
        qa.append(_dot(qh, wabs_ref[hd]).astype(BF16))
        qr.append(qh[:, :ROPE])
    qa = jnp.concatenate(qa, axis=0)
    qr = jnp.concatenate(qr, axis=0)
    rows = H_A * t

    pieces = [(cc_ref[bi, c * kc:(c + 1) * kc, :], rct_ref[bi, :, c * kc:(c + 1) * kc], True, c * kc)
              for c in range(past // kc)]
    pieces.append((cn_ref[bi], rn_ref[bi], False, past_len))
    scores, lat = [], []
    for ckv, kr, key_minor, k0 in pieces:
        ckv = ckv.astype(BF16)
        kr = kr.astype(BF16)
        s = _dot_t(qa, ckv) + (_dot(qr, kr) if key_minor else _dot_t(qr, kr))
        n = ckv.shape[0]
        if (k0 + n - 1) // CHUNK > past_len // CHUNK:
            qpos = past_len + lax.broadcasted_iota(jnp.int32, (rows, n), 0) % t
            kpos = k0 + lax.broadcasted_iota(jnp.int32, (rows, n), 1)
            s = jnp.where((kpos >> CHUNK_SHIFT) <= (qpos >> CHUNK_SHIFT), s, NEG)
        scores.append(s)
        lat.append(ckv)
    return scores, lat


def _mla_sample_finish(bi, scores, lat, wuv_ref, o_ref):
    rows = scores[0].shape[0]
    t = rows // H_A
    m = functools.reduce(jnp.maximum, [jnp.max(s, axis=-1, keepdims=True) for s in scores])
    l = jnp.zeros((rows, 1), F32)
    o_lat = jnp.zeros((rows, KV_RANK), F32)
    for s, ckv in zip(scores, lat):
        p = jnp.exp2(s - m)
        l = l + jnp.sum(p, axis=-1, keepdims=True)
        o_lat = o_lat + _dot(p.astype(BF16), ckv)
    o_lat = (o_lat / l).astype(BF16)
    o = jnp.zeros((t, H_A * DV), F32)
    for hd in range(H_A):
        o = o + _dot(o_lat[hd * t:(hd + 1) * t], wuv_ref[hd])
    o_ref[bi] = o.astype(BF16)


def _mla_sample(q, ckv_new, kr_new, ckv_cache, kr_cache, w_abs, w_uv_heads, *, past_len, kc):
    b, t, _ = q.shape
    past = ckv_cache.shape[1]
    nb = 2 if b % 2 == 0 else 1
    row = lambda n, s: pl.BlockSpec((nb, n, s), lambda i: (i, 0, 0))
    kr_cache = jnp.transpose(kr_cache, (0, 2, 1))
    return pl.pallas_call(
        functools.partial(_mla_sample_kernel, kc=kc, past_len=past_len),
        grid=(b // nb,),
        in_specs=[row(t, H_A * HEAD_PAD), row(t, KV_RANK), row(t, ROPE),
                  row(past, KV_RANK), row(ROPE, past),
                  _const_spec((H_A, HEAD_PAD, KV_RANK)), _const_spec((H_A, KV_RANK, H_A * DV))],
        out_specs=row(t, H_A * DV),
        out_shape=jax.ShapeDtypeStruct((b, t, H_A * DV), BF16),
        compiler_params=_params(1),
        name="mla_sample",
    )(q, ckv_new, kr_new, ckv_cache, kr_cache, w_abs, w_uv_heads)


def _merge_kernel(x_ref, ada_ref, g_ref, wg_ref, oa_ref, ob_ref, wpa_ref, wpb_ref, wo_ref, gpost_ref, o_ref):
    tm = x_ref.shape[1]
    parts = _row_parts(tm)
    xs = [x_ref[0, rows, :] for rows in parts]
    gate_in = [_dot_t(_premix(x, ada_ref, g_ref[...], rows, tm).astype(BF16), wg_ref[...])
               for x, rows in zip(xs, parts)]
    pas = [_dot(oa_ref[0, rows, :], wpa_ref[...]) for rows in parts]
    pbs = [_dot(ob_ref[0, rows, :], wpb_ref[...]) for rows in parts]
    merged = [(_sigmoid(gi[:, :D_MODEL]) * pa + _sigmoid(gi[:, D_MODEL:]) * pb).astype(BF16)
              for gi, pa, pb in zip(gate_in, pas, pbs)]
    mos = [_dot(m, wo_ref[...]) for m in merged]
    for rows, x, mo in zip(parts, xs, mos):
        o_ref[0, rows, :] = x + _mod(ada_ref, 2, rows, tm) * _rms(mo, gpost_ref[...])


def _merge(x, ada, o_a, o_b, wts, *, tm):
    b, t, _ = x.shape
    nseg = ada.shape[0] // b
    tok = lambda n: pl.BlockSpec((1, tm, n), lambda i, j: (i, j, 0))
    return pl.pallas_call(
        _merge_kernel,
        grid=(b, t // tm),
        in_specs=[tok(D_MODEL),
                  pl.BlockSpec((nseg, 6, D_MODEL), lambda i, j: (i, 0, 0)),
                  _const_spec((1, D_MODEL)),
                  _fixed_spec((_N_GATE, D_MODEL), (_GATE_BLOCK, 0)),
                  tok(H_A * DV), tok(H_B * D_HB),
                  _const_spec((H_A * DV, D_MODEL)), _const_spec((H_B * D_HB, D_MODEL)),
                  _const_spec((D_MODEL, D_MODEL)),
                  _const_spec((1, D_MODEL))],
        out_specs=tok(D_MODEL),
        out_shape=jax.ShapeDtypeStruct((b, t, D_MODEL), F32),
        compiler_params=_params(2),
        name="merge",
    )(x, ada, wts["g_pre_mix"], wts["w_in_all"], o_a, o_b, wts["w_proj_a"], wts["w_proj_b"],
      wts["w_out"], wts["g_post_mix"])


FF_CHUNK = MXU_TILE
SUB_GROUPS = 4
SUB_ROWS = SUB_GROUPS * SUBLANES
HALO = 2 * SUBLANES


def _gelu_tanh(a):
    return 0.5 * a * (1.0 + jnp.tanh(math.sqrt(2.0 / math.pi) * (a + 0.044715 * (a * a * a))))


def _ffn_kernel(x_ref, ada_ref, g_ref, wup_ref, cw_ref, cb_ref, wdn_ref, gpost_ref, cs_ref,
                o_ref, nc_ref, halo_ref, perm_ref, *, tm, ntile):
    ti = pl.program_id(1)
    nt = pl.num_programs(1)
    segmented = cs_ref.shape[0] > 1
    assert not segmented or (ntile == 1 and cs_ref.shape[0] == tm // SUB_ROWS)

    if not segmented:
        @pl.when(ti == 0)
        def _():
            halo_ref[...] = jnp.zeros((HALO, 2 * D_FF), F32)
            halo_ref[SUBLANES - 1:SUBLANES, :] = cs_ref[0, 0:1, :]
            halo_ref[HALO - 1:HALO, :] = cs_ref[0, 1:2, :]

    first = lax.broadcasted_iota(jnp.int32, (SUBLANES, FF_CHUNK), 0) == 0
    for tix in range(ntile):
        _ffn_tile(x_ref, ada_ref, g_ref, wup_ref, cw_ref, cb_ref, wdn_ref, gpost_ref, cs_ref, o_ref, nc_ref,
                  halo_ref, perm_ref.at[tix], slice(tix * tm, (tix + 1) * tm), first, segmented)

    if not segmented:
        @pl.when(ti == nt - 1)
        def _():
            nc_ref[0, 0:1, :] = halo_ref[SUBLANES - 1:SUBLANES, :]
            nc_ref[0, 1:2, :] = halo_ref[HALO - 1:HALO, :]


def _ffn_tile(x_ref, ada_ref, g_ref, wup_ref, cw_ref, cb_ref, wdn_ref, gpost_ref, cs_ref, o_ref, nc_ref,
              halo_ref, perm_ref, tile_rows, first, segmented):
    tm = tile_rows.stop - tile_rows.start
    nsub = tm // SUB_ROWS
    nc = D_MODEL // LANES
    all_rows = slice(0, tm)
    for c in range(nc):
        perm_ref[c] = x_ref[0, tile_rows, c * LANES:(c + 1) * LANES]
    x = jnp.concatenate(
        [jnp.concatenate([perm_ref[c, pl.ds(j * SUB_ROWS + i, SUBLANES, stride=SUB_GROUPS), :] for c in range(nc)],
                         axis=1)
         for j in range(nsub) for i in range(SUB_GROUPS)], axis=0)
    h2 = (_rms(x, g_ref[...]) * (1.0 + _mod(ada_ref, 4, all_rows, tm)) + _mod(ada_ref, 3, all_rows, tm)).astype(BF16)

    def up(col):
        cols = slice(col, col + FF_CHUNK)
        u = _dot(h2, wup_ref[:, cols])
        grp = [u[g * SUBLANES:(g + 1) * SUBLANES] for g in range(tm // SUBLANES)]
        own2 = [pltpu.roll(grp[j * SUB_GROUPS + SUB_GROUPS - 2], 1, axis=0) for j in range(nsub)]
        own3 = [pltpu.roll(grp[j * SUB_GROUPS + SUB_GROUPS - 1], 1, axis=0) for j in range(nsub)]
        if segmented:
            pred2 = [jnp.broadcast_to(cs_ref[j, 0:1, cols], (SUBLANES, FF_CHUNK)) for j in range(nsub)]
            pred3 = [jnp.broadcast_to(cs_ref[j, 1:2, cols], (SUBLANES, FF_CHUNK)) for j in range(nsub)]
            for j in range(nsub):
                last = (j + 1) * SUB_ROWS
                nc_ref[j, 0:1, cols] = u[last - SUBLANES - 1:last - SUBLANES]
                nc_ref[j, 1:2, cols] = u[last - 1:last]
        else:
            pred2 = [pltpu.roll(halo_ref[0:SUBLANES, cols], 1, axis=0)] + own2[:-1]
            pred3 = [pltpu.roll(halo_ref[SUBLANES:HALO, cols], 1, axis=0)] + own3[:-1]
            halo_ref[:, cols] = u[tm - HALO:]
        u1, u2 = [], []
        for j in range(nsub):
            f1 = jnp.where(first, pred3[j], own3[j])
            f2 = jnp.where(first, pred2[j], own2[j])
            g0 = j * SUB_GROUPS
            u1 += [f1] + grp[g0:g0 + SUB_GROUPS - 1]
            u2 += [f2, f1] + grp[g0:g0 + SUB_GROUPS - 2]
        u1 = jnp.concatenate(u1, axis=0)
        u2 = jnp.concatenate(u2, axis=0)
        return (cb_ref[:, cols] + cw_ref[0:1, cols] * u2 + cw_ref[1:2, cols] * u1 + cw_ref[2:3, cols] * u)

    nchunk = D_FF // FF_CHUNK
    acc = jnp.zeros((tm, D_MODEL), F32)
    ya, yb = up(0), up(D_FF)
    for c in range(nchunk):
        if c + 1 < nchunk:
            ya_next, yb_next = up((c + 1) * FF_CHUNK), up(D_FF + (c + 1) * FF_CHUNK)
        g = (_gelu_tanh(ya) * yb).astype(BF16)
        acc = acc + _dot(g, wdn_ref[c * FF_CHUNK:(c + 1) * FF_CHUNK, :])
        if c + 1 < nchunk:
            ya, yb = ya_next, yb_next
    y = x + _mod(ada_ref, 5, all_rows, tm) * _rms(acc, gpost_ref[...])
    for j in range(nsub):
        for i in range(SUB_GROUPS):
            g0 = (j * SUB_GROUPS + i) * SUBLANES
            for c in range(nc):
                perm_ref[c, pl.ds(j * SUB_ROWS + i, SUBLANES, stride=SUB_GROUPS), :] = (
                    y[g0:g0 + SUBLANES, c * LANES:(c + 1) * LANES])
    for c in range(nc):
        o_ref[0, tile_rows, c * LANES:(c + 1) * LANES] = perm_ref[c]


def _ffn(x, ada, conv_state, wts, *, tm):
    b, t, _ = x.shape
    nseg = ada.shape[0] // b
    assert tm % SUB_ROWS == 0 and t % tm == 0 and (nseg == 1 or (t == tm and t == nseg * SUB_ROWS))
    ntile = 2 if nseg == 1 and t % (2 * tm) == 0 else 1
    tok = pl.BlockSpec((1, ntile * tm, D_MODEL), lambda i, j: (i, j, 0))
    state = pl.BlockSpec((nseg, CONV_W - 1, 2 * D_FF), lambda i, j: (i, 0, 0))
    return pl.pallas_call(
        functools.partial(_ffn_kernel, tm=tm, ntile=ntile),
        grid=(b, t // (ntile * tm)),
        in_specs=[tok,
                  pl.BlockSpec((nseg, 6, D_MODEL), lambda i, j: (i, 0, 0)),
                  _const_spec((1, D_MODEL)),
                  _const_spec((D_MODEL, 2 * D_FF)),
                  _const_spec((CONV_W, 2 * D_FF)),
                  _const_spec((1, 2 * D_FF)),
                  _const_spec((D_FF, D_MODEL)),
                  _const_spec((1, D_MODEL)),
                  state],
        out_specs=[tok, state],
        out_shape=[jax.ShapeDtypeStruct((b, t, D_MODEL), F32),
                   jax.ShapeDtypeStruct((b * nseg, CONV_W - 1, 2 * D_FF), F32)],
        scratch_shapes=[pltpu.VMEM((HALO, 2 * D_FF), F32),
                        pltpu.VMEM((ntile, D_MODEL // LANES, tm, LANES), F32)],
        compiler_params=_params(1, 1),
        name="conv_ffn",
    )(x, ada, wts["g_pre_ffn"], wts["w_up"], wts["conv_w"], wts["conv_b"], wts["w_down"],
      wts["g_post_ffn"], conv_state)


def _rope_tables(first_pos, t):
    inv = ROPE_BASE ** (-np.arange(0, ROPE, 2, dtype=np.float64) / ROPE)
    ang = np.arange(first_pos, first_pos + t, dtype=np.float64)[:, None] * inv[None, :]
    cos, sin = jnp.asarray(np.cos(ang), F32), jnp.asarray(np.sin(ang), F32)
    cos_t = jnp.concatenate([cos, cos, jnp.ones((t, NOPE), F32), jnp.zeros((t, HEAD_PAD - ROPE - NOPE), F32)], axis=1)
    sin_t = jnp.concatenate([-sin, sin, jnp.zeros((t, HEAD_PAD - ROPE), F32)], axis=1)
    return cos_t, sin_t


def _swap_halves(w):
    return jnp.concatenate([w[..., ROPE // 2:], w[..., :ROPE // 2]], axis=-1)


def _pad_lanes(w, n):
    return jnp.pad(w, [(0, 0)] * (w.ndim - 1) + [(0, n - w.shape[-1])])


def _layer_weights(l, g_pre_mix, g_post_mix, g_pre_ffn, g_post_ffn, w_in, g_q_lat, w_uq, g_kv_lat,
                   w_uk, w_uv, w_proj_a, w_proj_b, w_out, w_up, conv_w, conv_b, w_down):
    wt = jnp.transpose(w_in[l])
    c_kr = Q_RANK + KV_RANK
    c_qb = c_kr + ROPE
    c_gate = c_qb + 3 * H_B * D_HB
    kr_pad = jnp.zeros((LANES - ROPE, D_MODEL), F32)
    w_in_all = jnp.concatenate([wt[:c_kr], wt[c_kr:c_qb], kr_pad,
                                wt[c_kr + ROPE // 2:c_qb], wt[c_kr:c_kr + ROPE // 2], kr_pad,
                                wt[c_qb:c_gate], jnp.zeros((_GATE_BLOCK * _N_GATE - _N_K1, D_MODEL), F32),
                                wt[c_gate:]], axis=0).astype(BF16)
    wq = w_uq[l].reshape(Q_RANK, H_A, NOPE + ROPE)
    wq_nope, wq_rope = wq[..., :NOPE], wq[..., NOPE:]
    zeros = jnp.zeros((Q_RANK, H_A, HEAD_PAD - ROPE - NOPE), F32)
    q_main = jnp.concatenate([wq_rope, wq_nope, zeros], axis=-1).reshape(Q_RANK, H_A * HEAD_PAD)
    q_rot = _pad_lanes(_swap_halves(wq_rope), HEAD_PAD).reshape(Q_RANK, H_A * HEAD_PAD)
    wk = w_uk[l].reshape(KV_RANK, H_A, NOPE)
    w_uk_pad = jnp.pad(wk, [(0, 0), (0, 0), (ROPE, HEAD_PAD - ROPE - NOPE)])
    w_abs = jnp.transpose(w_uk_pad, (1, 2, 0))
    w_uv_t = _pad_lanes(w_uv[l].reshape(KV_RANK, H_A, DV), VT_ROWS)
    v_ones = np.zeros((H_A, VT_ROWS), np.float32)
    v_ones[:, DV] = 1.0
    head_of_col = jnp.arange(H_A * DV) // DV
    w_uv_heads = jnp.where(head_of_col[None, None, :] == jnp.arange(H_A)[:, None, None], w_uv[l][None], 0.0)
    row = lambda g: g[l].reshape(1, -1)
    return {
        "g_pre_mix": row(g_pre_mix), "g_post_mix": row(g_post_mix),
        "g_pre_ffn": row(g_pre_ffn), "g_post_ffn": row(g_post_ffn),
        "g_q_lat": row(g_q_lat), "g_kv_lat": row(g_kv_lat),
        "w_in_all": w_in_all,
        "w_uq_ext": jnp.concatenate([q_main, q_rot], axis=1).astype(BF16),
        "w_uk_pad": w_uk_pad.reshape(KV_RANK, H_A * HEAD_PAD).astype(BF16),
        "w_abs": w_abs.astype(BF16),
        "w_uv_t": w_uv_t.reshape(KV_RANK, H_A * VT_ROWS).astype(BF16),
        "v_ones": jnp.asarray(v_ones.reshape(1, H_A * VT_ROWS)),
        "w_uv_heads": w_uv_heads.astype(BF16),
        "w_proj_a": w_proj_a[l].astype(BF16), "w_proj_b": w_proj_b[l].astype(BF16),
        "w_out": w_out[l].astype(BF16),
        "w_up": w_up[l].astype(BF16), "conv_w": conv_w[l], "conv_b": conv_b[l].reshape(1, -1),
        "w_down": w_down[l].astype(BF16),
    }


def _pick_tile(t, want):
    tm = min(t, want)
    assert t % tm == 0
    return tm


def _layer(x, ada, first_pos, past, wts):
    b, t, _ = x.shape
    cos_t, sin_t = _rope_tables(first_pos, t)
    tm = _pick_tile(t, TOKEN_TILE)
    if past is None:
        q, ckv, kr_t, qb, kb, vb, k_mla, vt_mla = _mixer_inputs(x, ada, cos_t, sin_t, wts, tm=tm, expand_kv=True)
        kr = jnp.transpose(kr_t, (0, 2, 1))
        o_a = _mla_prompt(q, k_mla, vt_mla, blk=_pick_tile(t, MLA_BLOCK))
        tq = _pick_tile(t, SB_BLOCK)
        o_b = _sb_prompt(qb, kb, vb, tq=tq, nsub=max(n for n in SB_SUBS if t % (tq * n) == 0))
        conv_state = jnp.zeros((b, CONV_W - 1, 2 * D_FF), F32)
    else:
        past_ckv, past_kr, past_k, past_v, conv_state = past
        past_len = past_ckv.shape[1]
        flat = t == SUB_ROWS and (b * t) % FFN_TILE == 0 and b * t <= TOKEN_TILE
        if flat:
            x = x.reshape(1, b * t, D_MODEL)
            cos_t, sin_t = jnp.tile(cos_t, (b, 1)), jnp.tile(sin_t, (b, 1))
            tm = b * t
        per_batch = lambda a: a.reshape(b, t, a.shape[-1])
        q, ckv, kr, qb, kb, vb = map(per_batch, _mixer_inputs(x, ada, cos_t, sin_t, wts, tm=tm, expand_kv=False))
        o_a = _mla_sample(q, ckv, kr, past_ckv, past_kr, wts["w_abs"], wts["w_uv_heads"],
                          past_len=past_len, kc=_pick_tile(past_len, MLA_DECODE_KEYS))
        o_b = _sb_sample(qb, kb, vb, past_k, past_v, tk=_pick_tile(past_len, SB_BLOCK),
                         recent=_pick_tile(past_len, SB_RECENT))
        o_a, o_b = (o.reshape(x.shape[0], x.shape[1], o.shape[-1]) for o in (o_a, o_b))
    x1 = _merge(x, ada, o_a, o_b, wts, tm=tm)
    y, new_conv = _ffn(x1, ada, conv_state, wts, tm=_pick_tile(x.shape[1], FFN_TILE))
    state = (ckv, kr, kb.reshape(b, t, H_B, D_HB), vb.reshape(b, t, H_B, D_HB), new_conv)
    return y.reshape(b, t, D_MODEL), state


def kernel(x_prompt, x_sample, cache_mla_ckv, cache_mla_krope, cache_sb_k, cache_sb_v, state_ffn_conv,
           c_prompt, c_sample, w_ada, b_ada, g_pre_mix, g_post_mix, g_pre_ffn, g_post_ffn,
           w_in, g_q_lat, w_uq, g_kv_lat, w_uk, w_uv, w_proj_a, w_proj_b, w_out,
           w_up, conv_w, conv_b, w_down):
    depth = w_in.shape[0]
    nb_p = x_prompt.shape[0]
    past_len = cache_mla_ckv.shape[2]
    xp, xs = x_prompt, x_sample
    c_all = jnp.concatenate([c_prompt, c_sample], axis=0)
    st_p = [[] for _ in range(5)]
    st_s = [[] for _ in range(5)]
    for l in range(depth):
        wts = _layer_weights(l, g_pre_mix, g_post_mix, g_pre_ffn, g_post_ffn, w_in, g_q_lat, w_uq, g_kv_lat,
                             w_uk, w_uv, w_proj_a, w_proj_b, w_out, w_up, conv_w, conv_b, w_down)
        ada = _ada(c_all, w_ada[l], b_ada[l]).reshape(-1, 6, D_MODEL)
        xp, sp = _layer(xp, ada[:nb_p], 0, None, wts)
        past = (cache_mla_ckv[l], cache_mla_krope[l], cache_sb_k[l], cache_sb_v[l], state_ffn_conv[l])
        xs, ss = _layer(xs, ada[nb_p:], past_len, past, wts)
        for i in range(5):
            st_p[i].append(sp[i])
            st_s[i].append(ss[i])
    p_state = [jnp.stack(a, axis=0) for a in st_p]
    s_state = [jnp.stack(a, axis=0) for a in st_s]
    return (xp, xs, *p_state, *s_state)
```

```python
import functools
import math

import numpy as np
import jax
import jax.numpy as jnp
from jax import lax
from jax.experimental import pallas as pl
from jax.experimental.pallas import tpu as pltpu

D_MODEL = 1024
CHUNK = 64
CHUNK_SHIFT = 6
H_A = 8
NOPE = 64
ROPE = 32
DV = 64
Q_RANK = 384
KV_RANK = 256
ROPE_BASE = 10000.0
H_B = 8
D_HB = 64
D_FF = 2816
CONV_W = 3
EPS = 1e-6
NEG = -1e30
MLA_SCALE = (NOPE + ROPE) ** -0.5
SB_SCALE = D_HB ** -0.5
LOG2E = math.log2(math.e)

LANES = 128
SUBLANES = 8
HEAD_PAD = LANES
VT_ROWS = 80
SB_DEAD = 110.0
SB_AHEAD = 2
VMEM_LIMIT = 56 * 1024 * 1024
MXU_TILE = 256
TOKEN_TILE = 2 * MXU_TILE
MERGE_TILE = 4 * MXU_TILE
FFN_TILE = MXU_TILE
MLA_BLOCK = 2 * MXU_TILE
SB_BLOCK = MXU_TILE
SB_SUBS = (1, 2, 4)
SB_RECENT = 2 * MXU_TILE
MLA_DECODE_KEYS = 4 * MXU_TILE

F32 = jnp.float32
BF16 = jnp.bfloat16


def _dot(a, b):
    return jnp.dot(a, b, preferred_element_type=F32)


def _dot_t(a, b):
    return lax.dot_general(a, b, (((1,), (1,)), ((), ())), preferred_element_type=F32)


def _rms(x, g):
    return x * lax.rsqrt(jnp.mean(x * x, axis=-1, keepdims=True) + EPS) * g


def _sigmoid(x):
    return 1.0 / (1.0 + jnp.exp(-x))


def _params(n_parallel, n_arbitrary=0):
    return pltpu.CompilerParams(
        dimension_semantics=("parallel",) * n_parallel + ("arbitrary",) * n_arbitrary,
        vmem_limit_bytes=VMEM_LIMIT)


def _const_spec(shape):
    return _fixed_spec(shape, (0,) * len(shape))


def _fixed_spec(shape, index):
    return pl.BlockSpec(shape, lambda *_: index, pipeline_mode=pl.Buffered(1))


def _ada_kernel(c_ref, w_ref, b_ref, o_ref):
    c = c_ref[...]
    s = (c * _sigmoid(c)).astype(BF16)
    o_ref[...] = _dot(s, w_ref[...].astype(BF16)) + b_ref[...]


def _ada(c_all, w_ada, b_ada):
    n = c_all.shape[0]
    nchunk = 6
    return pl.pallas_call(
        _ada_kernel,
        grid=(nchunk,),
        in_specs=[pl.BlockSpec((n, D_MODEL), lambda j: (0, 0)),
                  pl.BlockSpec((D_MODEL, D_MODEL), lambda j: (0, j)),
                  pl.BlockSpec((1, D_MODEL), lambda j: (0, j))],
        out_specs=pl.BlockSpec((n, D_MODEL), lambda j: (0, j)),
        out_shape=jax.ShapeDtypeStruct((n, 6 * D_MODEL), F32),
        compiler_params=_params(1),
        name="ada",
    )(c_all, w_ada, b_ada.reshape(1, -1))


_C_QLAT = 0
_C_CKV = _C_QLAT + Q_RANK
_C_KR = _C_CKV + KV_RANK
_C_KRR = _C_KR + LANES
_C_QB = _C_KRR + LANES
_C_KB = _C_QB + H_B * D_HB
_C_VB = _C_KB + H_B * D_HB
_N_K1 = _C_VB + H_B * D_HB
_N_GATE = 2 * D_MODEL
_GATE_BLOCK = -(-_N_K1 // _N_GATE)
_N_IN_ALL = (_GATE_BLOCK + 1) * _N_GATE


def _mod(ada_ref, comp, rows, tm):
    nseg = ada_ref.shape[0]
    if nseg == 1:
        return ada_ref[0, comp:comp + 1, :]
    seg = tm // nseg
    lo, hi = rows.start // seg, rows.stop // seg
    assert rows.start % seg == 0 and rows.stop % seg == 0
    return jnp.concatenate([jnp.broadcast_to(ada_ref[i, comp:comp + 1, :], (seg, ada_ref.shape[2]))
                            for i in range(lo, hi)], axis=0)


def _premix(x, ada_ref, g, rows, tm):
    return _rms(x, g) * (1.0 + _mod(ada_ref, 1, rows, tm)) + _mod(ada_ref, 0, rows, tm)


def _row_parts(tm):
    if tm % MXU_TILE == 0 and tm > MXU_TILE:
        n = tm // MXU_TILE
    else:
        n = 2 if tm % (2 * LANES) == 0 else 1
    return [slice(i * (tm // n), (i + 1) * (tm // n)) for i in range(n)]


def _mixer_kernel(x_ref, ada_ref, g_ref, win_ref, gq_ref, wuq_ref, gkv_ref, wuk_ref, wuv_ref, vone_ref,
                  cos_ref, sin_ref, *out_refs, expand_kv):
    if expand_kv:
        q_ref, ckv_ref, kr_ref, qb_ref, kb_ref, vb_ref, k_ref, vt_ref = out_refs
    else:
        q_ref, ckv_ref, kr_ref, qb_ref, kb_ref, vb_ref = out_refs
    tm = x_ref.shape[1]
    parts = _row_parts(tm)
    nq = H_A * HEAD_PAD
    ps = [_dot_t(_premix(x_ref[0, rows, :], ada_ref, g_ref[...], rows, tm).astype(BF16), win_ref[...])
          for rows in parts]
    q_lats = [_rms(p[:, _C_QLAT:_C_QLAT + Q_RANK], gq_ref[...]).astype(BF16) for p in ps]
    c_kvs = [_rms(p[:, _C_CKV:_C_CKV + KV_RANK], gkv_ref[...]) for p in ps]
    q2s = [_dot(q_lat, wuq_ref[...]) for q_lat in q_lats]
    if expand_kv:
        c_bfs = [c_kv.astype(BF16) for c_kv in c_kvs]
        k2s = [_dot(c_bf, wuk_ref[...]) for c_bf in c_bfs]
        v2s = [_dot(c_bf, wuv_ref[...]) for c_bf in c_bfs]
    for i, rows in enumerate(parts):
        p, q2 = ps[i], q2s[i]
        cos = cos_ref[rows, :]
        sin = sin_ref[rows, :]
        for hd in range(H_A):
            lo = hd * HEAD_PAD
            qh = q2[:, lo:lo + HEAD_PAD] * cos + q2[:, nq + lo:nq + lo + HEAD_PAD] * sin
            q_ref[0, rows, lo:lo + HEAD_PAD] = (qh * (MLA_SCALE * LOG2E)).astype(BF16)
        ckv_ref[0, rows, :] = c_kvs[i]
        kr = p[:, _C_KR:_C_KR + LANES] * cos + p[:, _C_KRR:_C_KRR + LANES] * sin
        if expand_kv:
            kr_ref[0, :, rows] = kr.T[:ROPE]
        else:
            kr_ref[0, rows, :] = kr[:, :ROPE]
        qb_ref[0, rows, :] = (p[:, _C_QB:_C_QB + H_B * D_HB] * SB_SCALE).astype(BF16)
        kb_ref[0, rows, :] = p[:, _C_KB:_C_KB + H_B * D_HB]
        vb_ref[0, rows, :] = p[:, _C_VB:_C_VB + H_B * D_HB]
        if expand_kv:
            for hd in range(H_A):
                lo = hd * HEAD_PAD
                k_ref[0, rows, lo:lo + HEAD_PAD] = (k2s[i][:, lo:lo + HEAD_PAD] + kr).astype(BF16)
            vt_ref[0, :, rows] = (v2s[i] + vone_ref[...]).T.astype(BF16)


def _mixer_inputs(x, ada, cos_t, sin_t, wts, *, tm, expand_kv):
    b, t, _ = x.shape
    nt = t // tm
    nseg = ada.shape[0] // b
    tok = lambda n: pl.BlockSpec((1, tm, n), lambda i, j: (i, j, 0))
    in_specs = [tok(D_MODEL),
                pl.BlockSpec((nseg, 6, D_MODEL), lambda i, j: (i, 0, 0)),
                _const_spec((1, D_MODEL)),
                _const_spec((_N_K1, D_MODEL)),
                _const_spec((1, Q_RANK)),
                _const_spec((Q_RANK, 2 * H_A * HEAD_PAD)),
                _const_spec((1, KV_RANK)),
                _const_spec((KV_RANK, H_A * HEAD_PAD)),
                _const_spec((KV_RANK, H_A * VT_ROWS)),
                _const_spec((1, H_A * VT_ROWS)),
                pl.BlockSpec((tm, LANES), lambda i, j: (j, 0)),
                pl.BlockSpec((tm, LANES), lambda i, j: (j, 0))]
    shapes = [((b, t, H_A * HEAD_PAD), BF16), ((b, t, KV_RANK), F32), ((b, t, ROPE), F32),
              ((b, t, H_B * D_HB), BF16), ((b, t, H_B * D_HB), F32), ((b, t, H_B * D_HB), F32)]
    if expand_kv:
        shapes += [((b, t, H_A * HEAD_PAD), BF16)]
    out_specs = [tok(s[-1]) for s, _ in shapes]
    if expand_kv:
        t_minor = lambda n: pl.BlockSpec((1, n, tm), lambda i, j: (i, 0, j))
        shapes[2], out_specs[2] = ((b, ROPE, t), F32), t_minor(ROPE)
        shapes += [((b, H_A * VT_ROWS, t), BF16)]
        out_specs += [t_minor(H_A * VT_ROWS)]
    return pl.pallas_call(
        functools.partial(_mixer_kernel, expand_kv=expand_kv),
        grid=(b, nt),
        in_specs=in_specs,
        out_specs=out_specs,
        out_shape=[jax.ShapeDtypeStruct(s, d) for s, d in shapes],
        compiler_params=_params(2),
        name="mixer_in_kv" if expand_kv else "mixer_in",
    )(x, ada, wts["g_pre_mix"], wts["w_in_all"], wts["g_q_lat"], wts["w_uq_ext"], wts["g_kv_lat"],
      wts["w_uk_pad"], wts["w_uv_t"], wts["v_ones"], cos_t, sin_t)


MLA_PAIR = 2


MLA_TILES = 2


def _mla_prompt_kernel(q_ref, k_ref, vt_ref, bias_ref, o_ref, sa_ref, sb_ref, m_ref, acc_ref, *, blk):
    tq = MLA_PAIR * blk
    lanes = [slice(hh * HEAD_PAD, (hh + 1) * HEAD_PAD) for hh in range(2)]

    def start_of(j):
        return pl.multiple_of(j * blk, blk)

    def scores(s_ref, rows, g):
        for hh, ln in enumerate(lanes):
            for b in range(MLA_PAIR):
                k = k_ref[0, pl.ds(start_of(MLA_PAIR * g + b), blk), ln]
                s_ref[hh, b] = _dot_t(k, q_ref[0, rows, ln])

    def softmax_pv(s_ref, g, masked):
        for hh in range(2):
            ss = []
            for b in range(MLA_PAIR):
                s = s_ref[hh, b]
                ss.append(s + bias_ref[b] if masked else s)
            m_old = m_ref[hh]
            m_new = m_old
            for s in ss:
                m_new = jnp.maximum(m_new, jnp.max(s, axis=0, keepdims=True))
            acc = jnp.exp2(m_old - m_new) * acc_ref[hh]
            for b, s in enumerate(ss):
                p = jnp.exp2(s - m_new).astype(BF16)
                vt = vt_ref[0, hh * VT_ROWS:(hh + 1) * VT_ROWS, pl.ds(start_of(MLA_PAIR * g + b), blk)]
                acc = acc + _dot(vt, p)
            acc_ref[hh] = acc
            m_ref[hh] = m_new

    def finish(rows):
        outs = [acc_ref[hh][:DV] / acc_ref[hh][DV:DV + 1] for hh in range(2)]
        o_ref[0, rows, :] = jnp.concatenate(outs, axis=0).T.astype(BF16)

    def tile(n, rows, even_ref, odd_ref, n_is_even, first_scores_done):
        if not first_scores_done:
            scores(even_ref, rows, 0)
        m_ref[...] = jnp.full(m_ref.shape, NEG, F32)
        acc_ref[...] = jnp.zeros(acc_ref.shape, F32)

        def body(h, c):
            scores(odd_ref, rows, 2 * h + 1)
            softmax_pv(even_ref, 2 * h, False)
            scores(even_ref, rows, 2 * h + 2)
            softmax_pv(odd_ref, 2 * h + 1, False)
            return c

        lax.fori_loop(0, n // 2, body, 0)
        if not n_is_even:
            scores(odd_ref, rows, n)
            softmax_pv(even_ref, n - 1, False)
        return odd_ref if not n_is_even else even_ref

    step = pl.program_id(2)
    n0 = MLA_TILES * step
    rows0, rows1 = slice(0, tq), slice(tq, 2 * tq)
    diag0 = tile(n0, rows0, sa_ref, sb_ref, True, False)
    scores(sb_ref, rows1, 0)
    softmax_pv(diag0, n0, True)
    finish(rows0)
    diag1 = tile(n0 + 1, rows1, sb_ref, sa_ref, False, True)
    softmax_pv(diag1, n0 + 1, True)
    finish(rows1)


def _mla_diag_bias(blk, tq):
    kpos = np.arange(MLA_PAIR * blk).reshape(MLA_PAIR, blk, 1)
    qpos = np.arange(tq).reshape(1, 1, tq)
    return np.where(kpos // CHUNK <= qpos // CHUNK, 0.0, NEG).astype(np.float32)


def _mla_prompt(q, k, vt, *, blk):
    b, t, _ = q.shape
    tq = MLA_PAIR * blk
    rows = MLA_TILES * tq
    assert t % rows == 0
    return pl.pallas_call(
        functools.partial(_mla_prompt_kernel, blk=blk),
        grid=(b, H_A // 2, t // rows),
        in_specs=[pl.BlockSpec((1, rows, 2 * HEAD_PAD), lambda i, j, n: (i, n, j)),
                  pl.BlockSpec((1, t, 2 * HEAD_PAD), lambda i, j, n: (i, 0, j)),
                  pl.BlockSpec((1, 2 * VT_ROWS, t), lambda i, j, n: (i, j, 0)),
                  _const_spec((MLA_PAIR, blk, tq))],
        out_specs=pl.BlockSpec((1, rows, 2 * DV), lambda i, j, n: (i, n, j)),
        out_shape=jax.ShapeDtypeStruct((b, t, H_A * DV), BF16),
        scratch_shapes=[pltpu.VMEM((2, MLA_PAIR, blk, tq), F32), pltpu.VMEM((2, MLA_PAIR, blk, tq), F32),
                        pltpu.VMEM((2, 1, tq), F32), pltpu.VMEM((2, VT_ROWS, tq), F32)],
        compiler_params=_params(3),
        name="mla_prompt",
    )(q, k, vt, jnp.asarray(_mla_diag_bias(blk, tq)))


def _tri(n):
    r = lax.broadcasted_iota(jnp.int32, (n, n), 0)
    c = lax.broadcasted_iota(jnp.int32, (n, n), 1)
    return jnp.where(r > c, 1.0, 0.0).astype(BF16)


def _sb_chains(chains, r_init):
    n = len(chains)
    z, sp, sp_bf, later, a = ([None] * n for _ in range(5))
    r_before = [None] * n
    r = dict(r_init)
    out = {}

    def scores(c):
        _, qs, k, _, _, _ = chains[c]
        z[c] = k(qs) if callable(k) else _dot_t(qs, k)

    def softplus(c):
        g, _, _, _, _, keep = chains[c]
        zb = z[c].astype(BF16)
        sp_bf[c] = jnp.maximum(zb, 0.0) + jnp.log(1.0 + jnp.exp(-jnp.abs(zb)))
        sp[c] = sp_bf[c].astype(F32)
        if keep is not None:
            sp[c] = jnp.where(keep, sp[c], 0.0)
            sp_bf[c] = sp[c].astype(BF16)
        rs = jnp.sum(sp[c], axis=-1, keepdims=True)
        r_before[c] = r[g]
        r[g] = rs if r[g] is None else r[g] + rs

    def suffix(c):
        later[c] = _dot(sp_bf[c], chains[c][4])

    def weights(c):
        _, _, _, _, tri, keep = chains[c]
        tk = tri.shape[0]
        w = z[c] - sp[c] - later[c]
        if r_before[c] is not None:
            r_b = r_before[c]
            if r_b.shape[1] == LANES and tk > LANES:
                r_b = jnp.concatenate([r_b] * (tk // LANES), axis=1)
            elif r_b.shape[1] == LANES and tk < LANES:
                r_b = r_b[:, :tk]
            w = w - r_b
        w = jnp.exp2(w * LOG2E)
        a[c] = (w if keep is None else jnp.where(keep, w, 0.0)).astype(BF16)

    def values(c):
        g, _, _, v, _, _ = chains[c]
        o = v(a[c]) if callable(v) else _dot(a[c], v)
        out[g] = o if g not in out else out[g] + o

    for stage in (scores, softplus, suffix, weights, values):
        for c in range(n):
            stage(c)
    return out, r


def _head_mask(x, hh):
    lane = lax.broadcasted_iota(jnp.int32, x.shape, 1)
    mine = jnp.logical_and(lane >= hh * D_HB, lane < (hh + 1) * D_HB)
    return jnp.where(mine, x, jnp.zeros_like(x))


def _stack_heads(q):
    return jnp.concatenate([_head_mask(q, 0), _head_mask(q, 1)], axis=0)


def _causal_pair(t, tk):
    row = lax.broadcasted_iota(jnp.int32, (t, tk), 0)
    col = lax.broadcasted_iota(jnp.int32, (t, tk), 1)
    keep = col < row
    return jnp.concatenate([keep, keep], axis=0)


def _sb_tail(qs, load, tri, acc_ref, r_ref, j_start, keep=None):
    def cond(c):
        j, rmin = c
        return jnp.logical_and(j >= 0, rmin < SB_DEAD)

    def body(c):
        j, _ = c
        o, r_new = _sb_chains([(0, qs) + load(j) + (tri, keep)], {0: r_ref[...]})
        acc_ref[...] += o[0]
        r_ref[...] = r_new[0]
        return j - 1, jnp.min(r_new[0])

    lax.while_loop(cond, body, (j_start, jnp.min(r_ref[...])))


def _unstack_heads(acc, t):
    lane = lax.broadcasted_iota(jnp.int32, (t, LANES), 1)
    return jnp.where(lane < D_HB, acc[:t], acc[t:])


def _sb_prompt_kernel(q_ref, k_ref, v_ref, o_ref, acc_ref, r_ref, *, tq, nsub):
    qi = pl.program_id(2)
    tri = _tri(tq)
    causal = _causal_pair(tq, tq)

    def load(j):
        start = pl.multiple_of(j * tq, tq)
        return (k_ref[0, pl.ds(start, tq), :].astype(BF16),
                v_ref[0, pl.ds(start, tq), :].astype(BF16))

    qss = [_stack_heads(q_ref[0, sub * tq:(sub + 1) * tq, :]) for sub in range(nsub)]
    chains = []
    for back in range(SB_AHEAD + 1):
        for sub in range(nsub):
            blk = qi * nsub + sub
            keep = causal if back == 0 else blk >= back
            chains.append((sub, qss[sub]) + load(jnp.maximum(blk - back, 0)) + (tri, keep))
    o, r = _sb_chains(chains, {sub: None for sub in range(nsub)})
    for sub in range(nsub):
        acc_ref[sub] = o[sub]
        r_ref[sub] = jnp.broadcast_to(r[sub], (2 * tq, LANES))
    for sub in range(nsub):
        _sb_tail(qss[sub], load, tri, acc_ref.at[sub], r_ref.at[sub], qi * nsub + sub - SB_AHEAD - 1)
        o_ref[0, sub * tq:(sub + 1) * tq, :] = _unstack_heads(acc_ref[sub], tq).astype(BF16)


def _sb_prompt(q, k, v, *, tq, nsub):
    b, t, _ = q.shape
    rows = tq * nsub
    return pl.pallas_call(
        functools.partial(_sb_prompt_kernel, tq=tq, nsub=nsub),
        grid=(b, H_B // 2, t // rows),
        in_specs=[pl.BlockSpec((1, rows, LANES), lambda i, j, n: (i, n, j)),
                  pl.BlockSpec((1, t, LANES), lambda i, j, n: (i, 0, j)),
                  pl.BlockSpec((1, t, LANES), lambda i, j, n: (i, 0, j))],
        out_specs=pl.BlockSpec((1, rows, LANES), lambda i, j, n: (i, n, j)),
        out_shape=jax.ShapeDtypeStruct((b, t, H_B * D_HB), BF16),
        scratch_shapes=[pltpu.VMEM((nsub, 2 * tq, LANES), F32), pltpu.VMEM((nsub, 2 * tq, LANES), F32)],
        compiler_params=_params(3),
        name="sb_prompt",
    )(q, k, v)


def _sb_sample_kernel(q_ref, kn_ref, vn_ref, kc_ref, vc_ref, o_ref, left_ref, acc_ref, r_ref, *, tk):
    tq = q_ref.shape[1]
    nblk = kc_ref.shape[1] // tk
    qs = _stack_heads(q_ref[0])
    tri = _tri(tk)

    def load(j):
        start = pl.multiple_of(j * tk, tk)
        return (kc_ref[0, pl.ds(start, tk), :].astype(BF16),
                vc_ref[0, pl.ds(start, tk), :].astype(BF16))

    o, r = _sb_chains([(0, qs, kn_ref[0].astype(BF16), vn_ref[0].astype(BF16), _tri(tq), _causal_pair(tq, tq)),
                       (0, qs) + load(nblk - 1) + (tri, None)], {0: None})
    acc_ref[...] = o[0]
    r_ref[...] = jnp.broadcast_to(r[0], (2 * tq, LANES))
    _sb_tail(qs, load, tri, acc_ref, r_ref, jnp.int32(nblk - 2))
    o_ref[0] = _unstack_heads(acc_ref[...], tq).astype(BF16)
    left_ref[...] = jnp.full(left_ref.shape, jnp.min(r_ref[...]), F32)


def _sb_sample_call(q, k_new, v_new, k_cache, v_cache, *, tk):
    b, t, _ = q.shape
    past = k_cache.shape[1]
    new = pl.BlockSpec((1, t, LANES), lambda i, j: (i, 0, j))
    old = pl.BlockSpec((1, past, LANES), lambda i, j: (i, 0, j))
    return pl.pallas_call(
        functools.partial(_sb_sample_kernel, tk=tk),
        grid=(b, H_B // 2),
        in_specs=[new, new, new, old, old],
        out_specs=[new, pl.BlockSpec((1, 1, SUBLANES, LANES), lambda i, j: (i, j, 0, 0))],
        out_shape=[jax.ShapeDtypeStruct((b, t, H_B * D_HB), BF16),
                   jax.ShapeDtypeStruct((b, H_B // 2, SUBLANES, LANES), F32)],
        scratch_shapes=[pltpu.VMEM((2 * t, LANES), F32), pltpu.VMEM((2 * t, LANES), F32)],
        compiler_params=_params(2),
        name="sb_sample",
    )(q, k_new, v_new, k_cache, v_cache)


SB_DECODE_KEYS = MXU_TILE
SB_DECODE_AHEAD = 1


def _sb_decode_kernel(q_ref, kn_ref, vn_ref, kct_ref, vct_ref, o_ref, left_ref, acc_ref, r_ref):
    t = q_ref.shape[1]
    rows = H_B * t
    ck = SB_DECODE_KEYS
    nstep = kct_ref.shape[3] // ck
    assert t & (t - 1) == 0
    head = lambda x, h: x[:, h * D_HB:(h + 1) * D_HB]
    q = q_ref[0]
    q3 = jnp.stack([head(q, h) for h in range(H_B)], axis=0)

    def cached(j):
        start = pl.multiple_of(j * ck, ck)
        kt = kct_ref[0, :, :, pl.ds(start, ck)].astype(BF16)
        vt = vct_ref[0, :, :, pl.ds(start, ck)].astype(BF16)

        def scores(_):
            z = lax.dot_general(q3, kt, (((2,), (1,)), ((0,), (0,))), preferred_element_type=F32)
            return z.reshape(rows, ck)

        def values(a):
            o = lax.dot_general(a.reshape(H_B, t, ck), vt, (((2,), (2,)), ((0,), (0,))),
                                preferred_element_type=F32)
            return o.reshape(rows, D_HB)

        return scores, values

    kn, vn = kn_ref[0].astype(BF16), vn_ref[0].astype(BF16)

    def new_scores(_):
        return jnp.concatenate([_dot_t(head(q, h), head(kn, h)) for h in range(H_B)], axis=0)

    def new_values(a):
        return jnp.concatenate([_dot(a[h * t:(h + 1) * t], head(vn, h)) for h in range(H_B)], axis=0)

    r_i = lax.broadcasted_iota(jnp.int32, (rows, t), 0)
    c_i = lax.broadcasted_iota(jnp.int32, (rows, t), 1)
    tri = _tri(ck)
    chains = [(0, None, new_scores, new_values, _tri(t), c_i < (r_i & (t - 1)))]
    ahead = min(SB_DECODE_AHEAD, nstep)
    for back in range(1, ahead + 1):
        chains.append((0, None) + cached(nstep - back) + (tri, None))
    o, r = _sb_chains(chains, {0: None})
    acc_ref[...] = o[0]
    r_ref[...] = jnp.broadcast_to(r[0], (rows, LANES))
    _sb_tail(None, cached, tri, acc_ref, r_ref, jnp.int32(nstep - ahead - 1))
    acc = acc_ref[...]
    o_ref[0] = jnp.concatenate([acc[h * t:(h + 1) * t] for h in range(H_B)], axis=1).astype(BF16)
    left_ref[...] = jnp.full(left_ref.shape, jnp.min(r_ref[...]), F32)


def _sb_decode_call(q, k_new, v_new, k_cache_t, v_cache_t, *, recent):
    b, t, _ = q.shape
    past = k_cache_t.shape[3]
    new = pl.BlockSpec((1, t, H_B * D_HB), lambda i: (i, 0, 0))
    old = pl.BlockSpec((1, H_B, D_HB, recent), lambda i: (i, 0, 0, past // recent - 1))
    return pl.pallas_call(
        _sb_decode_kernel,
        grid=(b,),
        in_specs=[new, new, new, old, old],
        out_specs=[new, pl.BlockSpec((1, SUBLANES, LANES), lambda i: (i, 0, 0))],
        out_shape=[jax.ShapeDtypeStruct((b, t, H_B * D_HB), BF16),
                   jax.ShapeDtypeStruct((b, SUBLANES, LANES), F32)],
        scratch_shapes=[pltpu.VMEM((H_B * t, D_HB), F32), pltpu.VMEM((H_B * t, LANES), F32)],
        compiler_params=_params(1),
        name="sb_decode",
    )(q, k_new, v_new, k_cache_t, v_cache_t)


def _sb_sample(q, k_new, v_new, k_cache, v_cache, *, tk, recent):
    b, past = k_cache.shape[:2]
    t = q.shape[1]
    flat = lambda c: c.reshape(b, c.shape[1], H_B * D_HB)
    full = lambda: _sb_sample_call(q, k_new, v_new, flat(k_cache), flat(v_cache), tk=tk)[0]
    if not (t & (t - 1) == 0 and past % recent == 0 and recent % SB_DECODE_KEYS == 0):
        return full()
    key_minor = lambda c: jnp.transpose(c, (0, 2, 3, 1))
    o, left = _sb_decode_call(q, k_new, v_new, key_minor(k_cache), key_minor(v_cache), recent=recent)
    if recent == past:
        return o
    return lax.cond(jnp.min(left) < SB_DEAD, full, lambda: o)


def _mla_sample_kernel(q_ref, cn_ref, rn_ref, cc_ref, rct_ref, wabs_ref, wuv_ref, o_ref, *, kc, past_len):
    parked = [_mla_sample_scores(bi, q_ref, cn_ref, rn_ref, cc_ref, rct_ref, wabs_ref, kc, past_len)
              for bi in range(q_ref.shape[0])]
    for bi, (scores, lat) in enumerate(parked):
        _mla_sample_finish(bi, scores, lat, wuv_ref, o_ref)


def _mla_sample_scores(bi, q_ref, cn_ref, rn_ref, cc_ref, rct_ref, wabs_ref, kc, past_len):
    t = q_ref.shape[1]
    past = cc_ref.shape[1]
    qa, qr = [], []
    for hd in range(H_A):
        qh = q_ref[bi, :, hd * HEAD_PAD:(hd + 1) * HEAD_PAD]
        qa.append(_dot(qh, wabs_ref[hd]).astype(BF16))
        qr.append(qh[:, :ROPE])
    qa = jnp.concatenate(qa, axis=0)
    qr = jnp.concatenate(qr, axis=0)
    rows = H_A * t

    pieces = [(cc_ref[bi, c * kc:(c + 1) * kc, :], rct_ref[bi, :, c * kc:(c + 1) * kc], True, c * kc)
              for c in range(past // kc)]
    pieces.append((cn_ref[bi], rn_ref[bi], False, past_len))
    scores, lat = [], []
    for ckv, kr, key_minor, k0 in pieces:
        ckv = ckv.astype(BF16)
        kr = kr.astype(BF16)
        s = _dot_t(qa, ckv) + (_dot(qr, kr) if key_minor else _dot_t(qr, kr))
        n = ckv.shape[0]
        if (k0 + n - 1) // CHUNK > past_len // CHUNK:
            qpos = past_len + lax.broadcasted_iota(jnp.int32, (rows, n), 0) % t
            kpos = k0 + lax.broadcasted_iota(jnp.int32, (rows, n), 1)
            s = jnp.where((kpos >> CHUNK_SHIFT) <= (qpos >> CHUNK_SHIFT), s, NEG)
        scores.append(s)
        lat.append(ckv)
    return scores, lat


def _mla_sample_finish(bi, scores, lat, wuv_ref, o_ref):
    rows = scores[0].shape[0]
    t = rows // H_A
    m = functools.reduce(jnp.maximum, [jnp.max(s, axis=-1, keepdims=True) for s in scores])
    l = jnp.zeros((rows, 1), F32)
    o_lat = jnp.zeros((rows, KV_RANK), F32)
    for s, ckv in zip(scores, lat):
        p = jnp.exp2(s - m)
        l = l + jnp.sum(p, axis=-1, keepdims=True)
        o_lat = o_lat + _dot(p.astype(BF16), ckv)
    o_lat = (o_lat / l).astype(BF16)
    o = jnp.zeros((t, H_A * DV), F32)
    for hd in range(H_A):
        o = o + _dot(o_lat[hd * t:(hd + 1) * t], wuv_ref[hd])
    o_ref[bi] = o.astype(BF16)


def _mla_sample(q, ckv_new, kr_new, ckv_cache, kr_cache, w_abs, w_uv_heads, *, past_len, kc):
    b, t, _ = q.shape
    past = ckv_cache.shape[1]
    nb = 2 if b % 2 == 0 else 1
    row = lambda n, s: pl.BlockSpec((nb, n, s), lambda i: (i, 0, 0))
    kr_cache = jnp.transpose(kr_cache, (0, 2, 1))
    return pl.pallas_call(
        functools.partial(_mla_sample_kernel, kc=kc, past_len=past_len),
        grid=(b // nb,),
        in_specs=[row(t, H_A * HEAD_PAD), row(t, KV_RANK), row(t, ROPE),
                  row(past, KV_RANK), row(ROPE, past),
                  _const_spec((H_A, HEAD_PAD, KV_RANK)), _const_spec((H_A, KV_RANK, H_A * DV))],
        out_specs=row(t, H_A * DV),
        out_shape=jax.ShapeDtypeStruct((b, t, H_A * DV), BF16),
        compiler_params=_params(1),
        name="mla_sample",
    )(q, ckv_new, kr_new, ckv_cache, kr_cache, w_abs, w_uv_heads)


def _merge_kernel(x_ref, ada_ref, g_ref, wg_ref, oa_ref, ob_ref, wpa_ref, wpb_ref, wo_ref, gpost_ref, o_ref):
    tm = x_ref.shape[1]
    parts = _row_parts(tm)
    xs = [x_ref[0, rows, :] for rows in parts]
    gate_in = [_dot_t(_premix(x, ada_ref, g_ref[...], rows, tm).astype(BF16), wg_ref[...])
               for x, rows in zip(xs, parts)]
    pas = [_dot(oa_ref[0, rows, :], wpa_ref[...]) for rows in parts]
    pbs = [_dot(ob_ref[0, rows, :], wpb_ref[...]) for rows in parts]
    merged = [(_sigmoid(gi[:, :D_MODEL]) * pa + _sigmoid(gi[:, D_MODEL:]) * pb).astype(BF16)
              for gi, pa, pb in zip(gate_in, pas, pbs)]
    mos = [_dot(m, wo_ref[...]) for m in merged]
    for rows, x, mo in zip(parts, xs, mos):
        o_ref[0, rows, :] = x + _mod(ada_ref, 2, rows, tm) * _rms(mo, gpost_ref[...])


def _merge(x, ada, o_a, o_b, wts, *, tm):
    b, t, _ = x.shape
    nseg = ada.shape[0] // b
    tok = lambda n: pl.BlockSpec((1, tm, n), lambda i, j: (i, j, 0))
    return pl.pallas_call(
        _merge_kernel,
        grid=(b, t // tm),
        in_specs=[tok(D_MODEL),
                  pl.BlockSpec((nseg, 6, D_MODEL), lambda i, j: (i, 0, 0)),
                  _const_spec((1, D_MODEL)),
                  _fixed_spec((_N_GATE, D_MODEL), (_GATE_BLOCK, 0)),
                  tok(H_A * DV), tok(H_B * D_HB),
                  _const_spec((H_A * DV, D_MODEL)), _const_spec((H_B * D_HB, D_MODEL)),
                  _const_spec((D_MODEL, D_MODEL)),
                  _const_spec((1, D_MODEL))],
        out_specs=tok(D_MODEL),
        out_shape=jax.ShapeDtypeStruct((b, t, D_MODEL), F32),
        compiler_params=_params(2),
        name="merge",
    )(x, ada, wts["g_pre_mix"], wts["w_in_all"], o_a, o_b, wts["w_proj_a"], wts["w_proj_b"],
      wts["w_out"], wts["g_post_mix"])


FF_CHUNK = MXU_TILE
SUB_GROUPS = 4
SUB_ROWS = SUB_GROUPS * SUBLANES
HALO = 2 * SUBLANES


def _gelu_tanh(a):
    return 0.5 * a * (1.0 + jnp.tanh(math.sqrt(2.0 / math.pi) * (a + 0.044715 * (a * a * a))))


def _ffn_kernel(x_ref, ada_ref, g_ref, wup_ref, cw_ref, cb_ref, wdn_ref, gpost_ref, cs_ref,
                o_ref, nc_ref, halo_ref, perm_ref, *, tm, ntile):
    ti = pl.program_id(1)
    nt = pl.num_programs(1)
    segmented = cs_ref.shape[0] > 1
    assert not segmented or (ntile == 1 and cs_ref.shape[0] == tm // SUB_ROWS)

    if not segmented:
        @pl.when(ti == 0)
        def _():
            halo_ref[...] = jnp.zeros((HALO, 2 * D_FF), F32)
            halo_ref[SUBLANES - 1:SUBLANES, :] = cs_ref[0, 0:1, :]
            halo_ref[HALO - 1:HALO, :] = cs_ref[0, 1:2, :]

    first = lax.broadcasted_iota(jnp.int32, (SUBLANES, FF_CHUNK), 0) == 0
    for tix in range(ntile):
        _ffn_tile(x_ref, ada_ref, g_ref, wup_ref, cw_ref, cb_ref, wdn_ref, gpost_ref, cs_ref, o_ref, nc_ref,
                  halo_ref, perm_ref.at[tix], slice(tix * tm, (tix + 1) * tm), first, segmented)

    if not segmented:
        @pl.when(ti == nt - 1)
        def _():
            nc_ref[0, 0:1, :] = halo_ref[SUBLANES - 1:SUBLANES, :]
            nc_ref[0, 1:2, :] = halo_ref[HALO - 1:HALO, :]


def _ffn_tile(x_ref, ada_ref, g_ref, wup_ref, cw_ref, cb_ref, wdn_ref, gpost_ref, cs_ref, o_ref, nc_ref,
              halo_ref, perm_ref, tile_rows, first, segmented):
    tm = tile_rows.stop - tile_rows.start
    nsub = tm // SUB_ROWS
    nc = D_MODEL // LANES
    all_rows = slice(0, tm)
    for c in range(nc):
        perm_ref[c] = x_ref[0, tile_rows, c * LANES:(c + 1) * LANES]
    x = jnp.concatenate(
        [jnp.concatenate([perm_ref[c, pl.ds(j * SUB_ROWS + i, SUBLANES, stride=SUB_GROUPS), :] for c in range(nc)],
                         axis=1)
         for j in range(nsub) for i in range(SUB_GROUPS)], axis=0)
    h2 = (_rms(x, g_ref[...]) * (1.0 + _mod(ada_ref, 4, all_rows, tm)) + _mod(ada_ref, 3, all_rows, tm)).astype(BF16)

    def up(col):
        cols = slice(col, col + FF_CHUNK)
        u = _dot(h2, wup_ref[:, cols])
        grp = [u[g * SUBLANES:(g + 1) * SUBLANES] for g in range(tm // SUBLANES)]
        own2 = [pltpu.roll(grp[j * SUB_GROUPS + SUB_GROUPS - 2], 1, axis=0) for j in range(nsub)]
        own3 = [pltpu.roll(grp[j * SUB_GROUPS + SUB_GROUPS - 1], 1, axis=0) for j in range(nsub)]
        if segmented:
            pred2 = [jnp.broadcast_to(cs_ref[j, 0:1, cols], (SUBLANES, FF_CHUNK)) for j in range(nsub)]
            pred3 = [jnp.broadcast_to(cs_ref[j, 1:2, cols], (SUBLANES, FF_CHUNK)) for j in range(nsub)]
            for j in range(nsub):
                last = (j + 1) * SUB_ROWS
                nc_ref[j, 0:1, cols] = u[last - SUBLANES - 1:last - SUBLANES]
                nc_ref[j, 1:2, cols] = u[last - 1:last]
        else:
            pred2 = [pltpu.roll(halo_ref[0:SUBLANES, cols], 1, axis=0)] + own2[:-1]
            pred3 = [pltpu.roll(halo_ref[SUBLANES:HALO, cols], 1, axis=0)] + own3[:-1]
            halo_ref[:, cols] = u[tm - HALO:]
        u1, u2 = [], []
        for j in range(nsub):
            f1 = jnp.where(first, pred3[j], own3[j])
            f2 = jnp.where(first, pred2[j], own2[j])
            g0 = j * SUB_GROUPS
            u1 += [f1] + grp[g0:g0 + SUB_GROUPS - 1]
            u2 += [f2, f1] + grp[g0:g0 + SUB_GROUPS - 2]
        u1 = jnp.concatenate(u1, axis=0)
        u2 = jnp.concatenate(u2, axis=0)
        return (cb_ref[:, cols] + cw_ref[0:1, cols] * u2 + cw_ref[1:2, cols] * u1 + cw_ref[2:3, cols] * u)

    nchunk = D_FF // FF_CHUNK
    acc = jnp.zeros((tm, D_MODEL), F32)
    ya, yb = up(0), up(D_FF)
    for c in range(nchunk):
        if c + 1 < nchunk:
            ya_next, yb_next = up((c + 1) * FF_CHUNK), up(D_FF + (c + 1) * FF_CHUNK)
        g = (_gelu_tanh(ya) * yb).astype(BF16)
        acc = acc + _dot(g, wdn_ref[c * FF_CHUNK:(c + 1) * FF_CHUNK, :])
        if c + 1 < nchunk:
            ya, yb = ya_next, yb_next
    y = x + _mod(ada_ref, 5, all_rows, tm) * _rms(acc, gpost_ref[...])
    for j in range(nsub):
        for i in range(SUB_GROUPS):
            g0 = (j * SUB_GROUPS + i) * SUBLANES
            for c in range(nc):
                perm_ref[c, pl.ds(j * SUB_ROWS + i, SUBLANES, stride=SUB_GROUPS), :] = (
                    y[g0:g0 + SUBLANES, c * LANES:(c + 1) * LANES])
    for c in range(nc):
        o_ref[0, tile_rows, c * LANES:(c + 1) * LANES] = perm_ref[c]


def _ffn(x, ada, conv_state, wts, *, tm):
    b, t, _ = x.shape
    nseg = ada.shape[0] // b
    assert tm % SUB_ROWS == 0 and t % tm == 0 and (nseg == 1 or (t == tm and t == nseg * SUB_ROWS))
    ntile = 2 if nseg == 1 and t % (2 * tm) == 0 else 1
    tok = pl.BlockSpec((1, ntile * tm, D_MODEL), lambda i, j: (i, j, 0))
    state = pl.BlockSpec((nseg, CONV_W - 1, 2 * D_FF), lambda i, j: (i, 0, 0))
    return pl.pallas_call(
        functools.partial(_ffn_kernel, tm=tm, ntile=ntile),
        grid=(b, t // (ntile * tm)),
        in_specs=[tok,
                  pl.BlockSpec((nseg, 6, D_MODEL), lambda i, j: (i, 0, 0)),
                  _const_spec((1, D_MODEL)),
                  _const_spec((D_MODEL, 2 * D_FF)),
                  _const_spec((CONV_W, 2 * D_FF)),
                  _const_spec((1, 2 * D_FF)),
                  _const_spec((D_FF, D_MODEL)),
                  _const_spec((1, D_MODEL)),
                  state],
        out_specs=[tok, state],
        out_shape=[jax.ShapeDtypeStruct((b, t, D_MODEL), F32),
                   jax.ShapeDtypeStruct((b * nseg, CONV_W - 1, 2 * D_FF), F32)],
        scratch_shapes=[pltpu.VMEM((HALO, 2 * D_FF), F32),
                        pltpu.VMEM((ntile, D_MODEL // LANES, tm, LANES), F32)],
        compiler_params=_params(1, 1),
        name="conv_ffn",
    )(x, ada, wts["g_pre_ffn"], wts["w_up"], wts["conv_w"], wts["conv_b"], wts["w_down"],
      wts["g_post_ffn"], conv_state)


def _rope_tables(first_pos, t):
    inv = ROPE_BASE ** (-np.arange(0, ROPE, 2, dtype=np.float64) / ROPE)
    ang = np.arange(first_pos, first_pos + t, dtype=np.float64)[:, None] * inv[None, :]
    cos, sin = jnp.asarray(np.cos(ang), F32), jnp.asarray(np.sin(ang), F32)
    cos_t = jnp.concatenate([cos, cos, jnp.ones((t, NOPE), F32), jnp.zeros((t, HEAD_PAD - ROPE - NOPE), F32)], axis=1)
    sin_t = jnp.concatenate([-sin, sin, jnp.zeros((t, HEAD_PAD - ROPE), F32)], axis=1)
    return cos_t, sin_t


def _swap_halves(w):
    return jnp.concatenate([w[..., ROPE // 2:], w[..., :ROPE // 2]], axis=-1)


def _pad_lanes(w, n):
    return jnp.pad(w, [(0, 0)] * (w.ndim - 1) + [(0, n - w.shape[-1])])


def _layer_weights(l, g_pre_mix, g_post_mix, g_pre_ffn, g_post_ffn, w_in, g_q_lat, w_uq, g_kv_lat,
                   w_uk, w_uv, w_proj_a, w_proj_b, w_out, w_up, conv_w, conv_b, w_down):
    wt = jnp.transpose(w_in[l])
    c_kr = Q_RANK + KV_RANK
    c_qb = c_kr + ROPE
    c_gate = c_qb + 3 * H_B * D_HB
    kr_pad = jnp.zeros((LANES - ROPE, D_MODEL), F32)
    w_in_all = jnp.concatenate([wt[:c_kr], wt[c_kr:c_qb], kr_pad,
                                wt[c_kr + ROPE // 2:c_qb], wt[c_kr:c_kr + ROPE // 2], kr_pad,
                                wt[c_qb:c_gate], jnp.zeros((_GATE_BLOCK * _N_GATE - _N_K1, D_MODEL), F32),
                                wt[c_gate:]], axis=0).astype(BF16)
    wq = w_uq[l].reshape(Q_RANK, H_A, NOPE + ROPE)
    wq_nope, wq_rope = wq[..., :NOPE], wq[..., NOPE:]
    zeros = jnp.zeros((Q_RANK, H_A, HEAD_PAD - ROPE - NOPE), F32)
    q_main = jnp.concatenate([wq_rope, wq_nope, zeros], axis=-1).reshape(Q_RANK, H_A * HEAD_PAD)
    q_rot = _pad_lanes(_swap_halves(wq_rope), HEAD_PAD).reshape(Q_RANK, H_A * HEAD_PAD)
    wk = w_uk[l].reshape(KV_RANK, H_A, NOPE)
    w_uk_pad = jnp.pad(wk, [(0, 0), (0, 0), (ROPE, HEAD_PAD - ROPE - NOPE)])
    w_abs = jnp.transpose(w_uk_pad, (1, 2, 0))
    w_uv_t = _pad_lanes(w_uv[l].reshape(KV_RANK, H_A, DV), VT_ROWS)
    v_ones = np.zeros((H_A, VT_ROWS), np.float32)
    v_ones[:, DV] = 1.0
    head_of_col = jnp.arange(H_A * DV) // DV
    w_uv_heads = jnp.where(head_of_col[None, None, :] == jnp.arange(H_A)[:, None, None], w_uv[l][None], 0.0)
    row = lambda g: g[l].reshape(1, -1)
    return {
        "g_pre_mix": row(g_pre_mix), "g_post_mix": row(g_post_mix),
        "g_pre_ffn": row(g_pre_ffn), "g_post_ffn": row(g_post_ffn),
        "g_q_lat": row(g_q_lat), "g_kv_lat": row(g_kv_lat),
        "w_in_all": w_in_all,
        "w_uq_ext": jnp.concatenate([q_main, q_rot], axis=1).astype(BF16),
        "w_uk_pad": w_uk_pad.reshape(KV_RANK, H_A * HEAD_PAD).astype(BF16),
        "w_abs": w_abs.astype(BF16),
        "w_uv_t": w_uv_t.reshape(KV_RANK, H_A * VT_ROWS).astype(BF16),
        "v_ones": jnp.asarray(v_ones.reshape(1, H_A * VT_ROWS)),
        "w_uv_heads": w_uv_heads.astype(BF16),
        "w_proj_a": w_proj_a[l].astype(BF16), "w_proj_b": w_proj_b[l].astype(BF16),
        "w_out": w_out[l].astype(BF16),
        "w_up": w_up[l].astype(BF16), "conv_w": conv_w[l], "conv_b": conv_b[l].reshape(1, -1),
        "w_down": w_down[l].astype(BF16),
    }


def _pick_tile(t, want):
    tm = min(t, want)
    assert t % tm == 0
    return tm


def _layer(x, ada, first_pos, past, wts):
    b, t, _ = x.shape
    cos_t, sin_t = _rope_tables(first_pos, t)
    tm = _pick_tile(t, TOKEN_TILE)
    if past is None:
        q, ckv, kr_t, qb, kb, vb, k_mla, vt_mla = _mixer_inputs(x, ada, cos_t, sin_t, wts, tm=tm, expand_kv=True)
        kr = jnp.transpose(kr_t, (0, 2, 1))
        o_a = _mla_prompt(q, k_mla, vt_mla, blk=_pick_tile(t, MLA_BLOCK))
        tq = _pick_tile(t, SB_BLOCK)
        o_b = _sb_prompt(qb, kb, vb, tq=tq, nsub=max(n for n in SB_SUBS if t % (tq * n) == 0))
        conv_state = jnp.zeros((b, CONV_W - 1, 2 * D_FF), F32)
    else:
        past_ckv, past_kr, past_k, past_v, conv_state = past
        past_len = past_ckv.shape[1]
        flat = t == SUB_ROWS and (b * t) % FFN_TILE == 0 and b * t <= TOKEN_TILE
        if flat:
            x = x.reshape(1, b * t, D_MODEL)
            cos_t, sin_t = jnp.tile(cos_t, (b, 1)), jnp.tile(sin_t, (b, 1))
            tm = b * t
        per_batch = lambda a: a.reshape(b, t, a.shape[-1])
        q, ckv, kr, qb, kb, vb = map(per_batch, _mixer_inputs(x, ada, cos_t, sin_t, wts, tm=tm, expand_kv=False))
        o_a = _mla_sample(q, ckv, kr, past_ckv, past_kr, wts["w_abs"], wts["w_uv_heads"],
                          past_len=past_len, kc=_pick_tile(past_len, MLA_DECODE_KEYS))
        o_b = _sb_sample(qb, kb, vb, past_k, past_v, tk=_pick_tile(past_len, SB_BLOCK),
                         recent=_pick_tile(past_len, SB_RECENT))
        o_a, o_b = (o.reshape(x.shape[0], x.shape[1], o.shape[-1]) for o in (o_a, o_b))
    x1 = _merge(x, ada, o_a, o_b, wts, tm=_pick_tile(x.shape[1], MERGE_TILE))
    y, new_conv = _ffn(x1, ada, conv_state, wts, tm=_pick_tile(x.shape[1], FFN_TILE))
    state = (ckv, kr, kb.reshape(b, t, H_B, D_HB), vb.reshape(b, t, H_B, D_HB), new_conv)
    return y.reshape(b, t, D_MODEL), state


def kernel(x_prompt, x_sample, cache_mla_ckv, cache_mla_krope, cache_sb_k, cache_sb_v, state_ffn_conv,
           c_prompt, c_sample, w_ada, b_ada, g_pre_mix, g_post_mix, g_pre_ffn, g_post_ffn,
           w_in, g_q_lat, w_uq, g_kv_lat, w_uk, w_uv, w_proj_a, w_proj_b, w_out,
           w_up, conv_w, conv_b, w_down):
    depth = w_in.shape[0]
    nb_p = x_prompt.shape[0]
    past_len = cache_mla_ckv.shape[2]
    xp, xs = x_prompt, x_sample
    c_all = jnp.concatenate([c_prompt, c_sample], axis=0)
    st_p = [[] for _ in range(5)]
    st_s = [[] for _ in range(5)]
    for l in range(depth):
        wts = _layer_weights(l, g_pre_mix, g_post_mix, g_pre_ffn, g_post_ffn, w_in, g_q_lat, w_uq, g_kv_lat,
                             w_uk, w_uv, w_proj_a, w_proj_b, w_out, w_up, conv_w, conv_b, w_down)
        ada = _ada(c_all, w_ada[l], b_ada[l]).reshape(-1, 6, D_MODEL)
        xp, sp = _layer(xp, ada[:nb_p], 0, None, wts)
        past = (cache_mla_ckv[l], cache_mla_krope[l], cache_sb_k[l], cache_sb_v[l], state_ffn_conv[l])
        xs, ss = _layer(xs, ada[nb_p:], past_len, past, wts)
        for i in range(5):
            st_p[i].append(sp[i])
            st_s[i].append(ss[i])
    p_state = [jnp.stack(a, axis=0) for a in st_p]
    s_state = [jnp.stack(a, axis=0) for a in st_s]
    return (xp, xs, *p_state, *s_state)
```
